```python
import jax
import jax.numpy as jnp
from jax import lax
import numpy as np

D_MODEL = 2048
BATCH = 1
SEQ = 8192
DEPTH = 1

MEM_LEN = 256
D_MIX = D_MODEL
MLSTM_HEADS = 4
MLSTM_V_DIM = (D_MIX // 2) // MLSTM_HEADS
MLSTM_QK_DIM = MLSTM_V_DIM // 2
MLSTM_CHUNK = 128
MLSTM_CONV = 4
MOBA_HEADS = 8
MOBA_HEAD_DIM = (D_MIX // 2) // MOBA_HEADS
MOBA_BLOCK = 256
MOBA_TOPK = 3
MOBA_QCHUNK = 64
ROPE_DIM = MOBA_HEAD_DIM // 4
ROPE_THETA = 500000.0
XA_HEADS = 4
XA_HEAD_DIM = D_MODEL // XA_HEADS
MOE_GROUPS = 8
MOE_EXPERTS_PER_GROUP = 8
MOE_EXPERTS = MOE_GROUPS * MOE_EXPERTS_PER_GROUP
MOE_TOPK = 2
MOE_FF = D_MODEL // 4
MOE_BLOCK_ROWS = 128
EPS = 1e-6

MLSTM_QK_W = MLSTM_HEADS * MLSTM_QK_DIM
MLSTM_V_W = MLSTM_HEADS * MLSTM_V_DIM
MOBA_W = MOBA_HEADS * MOBA_HEAD_DIM
IN_WIDTHS = (MLSTM_QK_W, MLSTM_QK_W, MLSTM_V_W, MLSTM_V_W, 2 * MLSTM_HEADS, MOBA_W, MOBA_W, MOBA_W)
D_IN = 2 * MLSTM_QK_W + 2 * MLSTM_V_W + 2 * MLSTM_HEADS + 3 * MOBA_W

kernel_name = 'hybrid_mlstm_moba_hmoe_layer'


def rms_norm(x, g):
    xf = x.astype(jnp.float32)
    y = xf * lax.rsqrt(jnp.mean(xf * xf, axis=-1, keepdims=True) + EPS)
    return (y * g.astype(jnp.float32)).astype(x.dtype)


def split_heads(a, n_heads):
    b, s, w = a.shape
    return a.reshape(b, s, n_heads, w // n_heads).transpose(0, 2, 1, 3)


def partial_rotary(x, positions):
    half = ROPE_DIM // 2
    inv_freq = ROPE_THETA ** (-jnp.arange(half, dtype=jnp.float32) * 2.0 / ROPE_DIM)
    ang = positions.astype(jnp.float32)[:, None, :, None] * inv_freq
    cos, sin = jnp.cos(ang), jnp.sin(ang)
    xr = x[..., :ROPE_DIM].astype(jnp.float32)
    x1, x2 = xr[..., :half], xr[..., half:]
    rot = jnp.concatenate([x1 * cos - x2 * sin, x2 * cos + x1 * sin], axis=-1)
    return jnp.concatenate([rot.astype(x.dtype), x[..., ROPE_DIM:]], axis=-1)


def causal_depthwise_conv(x, w, b):
    c = x.shape[-1]
    y = lax.conv_general_dilated(x, w[:, None, :].astype(x.dtype), window_strides=(1,),
                                 padding=((MLSTM_CONV - 1, 0),),
                                 dimension_numbers=('NWC', 'WIO', 'NWC'), feature_group_count=c)
    return y + b.astype(x.dtype)


def mlstm_chunkwise(q, k, v, i_pre, f_pre):
    bsz, nh, s, dk = q.shape
    dv = v.shape[-1]
    L = MLSTM_CHUNK
    nc = s // L
    q = q * (dk ** -0.5)
    log_f = jax.nn.log_sigmoid(f_pre)

    def to_chunks(a):
        return jnp.moveaxis(a.reshape(bsz, nh, nc, L, *a.shape[3:]), 2, 0)

    causal = jnp.tril(jnp.ones((L, L), dtype=bool))

    def chunk_step(carry, inp):
        c_prev, n_prev, m_prev = carry
        qb, kb, vb, ib, fb = inp
        b = jnp.cumsum(fb, axis=-1)
        g = b[..., -1]
        d = jnp.where(causal, b[..., :, None] - b[..., None, :] + ib[..., None, :], -jnp.inf)
        inter = b + m_prev[..., None]
        m_t = jnp.maximum(inter, jnp.max(d, axis=-1))
        w_inter = jnp.exp(inter - m_t)
        p = jnp.exp(d - m_t[..., None]) * jnp.einsum('bhtd,bhsd->bhts', qb, kb)
        num = (w_inter[..., None] * jnp.einsum('bhtd,bhde->bhte', qb, c_prev)
               + jnp.einsum('bhts,bhse->bhte', p, vb))
        den = w_inter * jnp.einsum('bhtd,bhd->bht', qb, n_prev) + jnp.sum(p, axis=-1)
        h = num / jnp.maximum(jnp.abs(den), jnp.exp(-m_t))[..., None]
        a = g[..., None] - b + ib
        m_new = jnp.maximum(g + m_prev, jnp.max(a, axis=-1))
        w_k = jnp.exp(a - m_new[..., None])
        decay = jnp.exp(g + m_prev - m_new)
        c_new = decay[..., None, None] * c_prev + jnp.einsum('bhs,bhsd,bhse->bhde', w_k, kb, vb)
        n_new = decay[..., None] * n_prev + jnp.einsum('bhs,bhsd->bhd', w_k, kb)
        return (c_new, n_new, m_new), h

    init = (jnp.zeros((bsz, nh, dk, dv), jnp.float32),
            jnp.zeros((bsz, nh, dk), jnp.float32),
            jnp.zeros((bsz, nh), jnp.float32))
    xs = (to_chunks(q), to_chunks(k), to_chunks(v), to_chunks(i_pre), to_chunks(log_f))
    _, hs = lax.scan(chunk_step, init, xs)
    return jnp.moveaxis(hs, 0, 2).reshape(bsz, nh, s, dv)


def moba_attention(q, k, v):
    bsz, nh, s, hd = q.shape
    nb = -(-s // MOBA_BLOCK)
    n_sel = min(MOBA_TOPK, nb)
    pad = nb * MOBA_BLOCK - s
    kp = jnp.pad(k, ((0, 0), (0, 0), (0, pad), (0, 0)))
    vp = jnp.pad(v, ((0, 0), (0, 0), (0, pad), (0, 0)))
    k_blocks = kp.reshape(bsz, nh, nb, MOBA_BLOCK, hd)
    v_blocks = vp.reshape(bsz, nh, nb, MOBA_BLOCK, hd)
    k_mean = jnp.mean(k_blocks.astype(jnp.float32), axis=3)
    scale = hd ** -0.5
    bi = jnp.arange(bsz)[:, None, None, None]
    hi = jnp.arange(nh)[None, :, None, None]

    def query_chunk(c):
        t0 = c * MOBA_QCHUNK
        cur = t0 // MOBA_BLOCK
        qc = lax.dynamic_slice_in_dim(q, t0, MOBA_QCHUNK, axis=2)
        gate = jnp.einsum('bhtd,bhnd->bhtn', qc.astype(jnp.float32), k_mean)
        gate = jnp.where(jnp.arange(nb) < cur, gate, -jnp.inf)
        _, idx = lax.top_k(gate, n_sel)
        valid = jnp.arange(n_sel) < cur
        k_sel = k_blocks[bi, hi, idx]
        v_sel = v_blocks[bi, hi, idx]
        s_past = jnp.einsum('bhtd,bhtjsd->bhtjs', qc, k_sel).astype(jnp.float32) * scale
        s_past = jnp.where(valid[:, None], s_past, -jnp.inf)
        s_past = s_past.reshape(bsz, nh, MOBA_QCHUNK, n_sel * MOBA_BLOCK)
        k_own = lax.dynamic_slice_in_dim(kp, cur * MOBA_BLOCK, MOBA_BLOCK, axis=2)
        v_own = lax.dynamic_slice_in_dim(vp, cur * MOBA_BLOCK, MOBA_BLOCK, axis=2)
        s_own = jnp.einsum('bhtd,bhsd->bhts', qc, k_own).astype(jnp.float32) * scale
        q_pos = t0 + jnp.arange(MOBA_QCHUNK)
        k_pos = cur * MOBA_BLOCK + jnp.arange(MOBA_BLOCK)
        s_own = jnp.where(k_pos[None, :] <= q_pos[:, None], s_own, -jnp.inf)
        p = jax.nn.softmax(jnp.concatenate([s_past, s_own], axis=-1), axis=-1).astype(v.dtype)
        p_past = p[..., :n_sel * MOBA_BLOCK].reshape(bsz, nh, MOBA_QCHUNK, n_sel, MOBA_BLOCK)
        p_own = p[..., n_sel * MOBA_BLOCK:]
        return (jnp.einsum('bhtjs,bhtjsd->bhtd', p_past, v_sel)
                + jnp.einsum('bhts,bhsd->bhtd', p_own, v_own))

    out = lax.map(query_chunk, jnp.arange(s // MOBA_QCHUNK))
    return jnp.moveaxis(out, 0, 2).reshape(bsz, nh, s, hd)


def hybrid_mixer(xn, positions, w_in, gate_b, conv_w, conv_b, out_g, q_g, k_g, w_out):
    bsz, s, _ = xn.shape
    dt = xn.dtype
    cuts = [int(c) for c in np.cumsum(IN_WIDTHS)[:-1]]
    mq, mk, mv, mo, mgate, aq, ak, av = jnp.split(xn @ w_in, cuts, axis=-1)
    qk = jax.nn.silu(causal_depthwise_conv(jnp.concatenate([mq, mk], axis=-1), conv_w, conv_b))
    mq, mk = jnp.split(qk, [MLSTM_QK_W], axis=-1)
    gates = mgate.astype(jnp.float32) + gate_b.astype(jnp.float32)
    i_pre = jnp.transpose(gates[..., :MLSTM_HEADS], (0, 2, 1))
    f_pre = jnp.transpose(gates[..., MLSTM_HEADS:], (0, 2, 1))
    hm = mlstm_chunkwise(split_heads(mq, MLSTM_HEADS).astype(jnp.float32),
                         split_heads(mk, MLSTM_HEADS).astype(jnp.float32),
                         split_heads(mv, MLSTM_HEADS).astype(jnp.float32), i_pre, f_pre)
    hm = hm * lax.rsqrt(jnp.mean(hm * hm, axis=-1, keepdims=True) + EPS)
    hm = hm.transpose(0, 2, 1, 3).reshape(bsz, s, MLSTM_V_W) * out_g.astype(jnp.float32)
    hm = (hm * jax.nn.sigmoid(mo.astype(jnp.float32))).astype(dt)
    aq = partial_rotary(rms_norm(split_heads(aq, MOBA_HEADS), q_g), positions)
    ak = partial_rotary(rms_norm(split_heads(ak, MOBA_HEADS), k_g), positions)
    ha = moba_attention(aq, ak, split_heads(av, MOBA_HEADS))
    ha = ha.transpose(0, 2, 1, 3).reshape(bsz, s, MOBA_W)
    return jnp.concatenate([hm, ha], axis=-1) @ w_out


def cross_attention(hn, memn, wq, wkv, q_g, k_g, wo):
    bsz, s, d = hn.shape
    q = rms_norm((hn @ wq).reshape(bsz, s, XA_HEADS, XA_HEAD_DIM), q_g)
    k, v = jnp.split(memn @ wkv, 2, axis=-1)
    k = rms_norm(k.reshape(bsz, -1, XA_HEADS, XA_HEAD_DIM), k_g)
    v = v.reshape(bsz, -1, XA_HEADS, XA_HEAD_DIM)
    sc = jnp.einsum('bshd,bmhd->bhsm', q, k).astype(jnp.float32) * (XA_HEAD_DIM ** -0.5)
    p = jax.nn.softmax(sc, axis=-1).astype(v.dtype)
    o = jnp.einsum('bhsm,bmhd->bshd', p, v).reshape(bsz, s, d)
    return o @ wo


def hier_moe(hn, wg, bg, we, be, w_gate, w_up, w_down):
    bsz, s, d = hn.shape
    n_tok = bsz * s
    xt = hn.reshape(n_tok, d)
    gprob = jax.nn.softmax((xt @ wg).astype(jnp.float32) + bg.astype(jnp.float32), axis=-1)
    gsel = jnp.argmax(gprob, axis=-1)
    pg = jnp.max(gprob, axis=-1)
    elog = ((xt @ we).astype(jnp.float32) + be.astype(jnp.float32)).reshape(n_tok, MOE_GROUPS, MOE_EXPERTS_PER_GROUP)
    gidx = jnp.broadcast_to(gsel[:, None, None], (n_tok, 1, MOE_EXPERTS_PER_GROUP))
    eprob = jax.nn.softmax(jnp.take_along_axis(elog, gidx, axis=1)[:, 0], axis=-1)
    ptop, ltop = lax.top_k(eprob, MOE_TOPK)
    wts = pg[:, None] * ptop / jnp.sum(ptop, axis=-1, keepdims=True)
    eid = gsel[:, None] * MOE_EXPERTS_PER_GROUP + ltop
    n_pair = n_tok * MOE_TOPK
    flat_e = eid.reshape(-1)
    flat_t = jnp.repeat(jnp.arange(n_tok, dtype=jnp.int32), MOE_TOPK)
    flat_w = wts.reshape(-1)
    order = jnp.argsort(flat_e)
    e_s, t_s, w_s = flat_e[order], flat_t[order], flat_w[order]
    counts = jnp.zeros((MOE_EXPERTS,), jnp.int32).at[flat_e].add(1)
    padded = ((counts + MOE_BLOCK_ROWS - 1) // MOE_BLOCK_ROWS) * MOE_BLOCK_ROWS
    start = jnp.cumsum(counts) - counts
    pend = jnp.cumsum(padded)
    pstart = pend - padded
    dest = pstart[e_s] + (jnp.arange(n_pair) - start[e_s])
    n_blk = -(-n_pair // MOE_BLOCK_ROWS) + MOE_EXPERTS
    n_rows = n_blk * MOE_BLOCK_ROWS
    row_tok = jnp.zeros((n_rows,), jnp.int32).at[dest].set(t_s)
    row_w = jnp.zeros((n_rows,), jnp.float32).at[dest].set(w_s)
    blk_e = jnp.minimum(jnp.searchsorted(pend, jnp.arange(n_blk) * MOE_BLOCK_ROWS, side='right'), MOE_EXPERTS - 1)
    xg = xt[row_tok].reshape(n_blk, MOE_BLOCK_ROWS, d)

    def expert_block(args):
        xb, e = args
        return (jax.nn.silu(xb @ w_gate[e]) * (xb @ w_up[e])) @ w_down[e]

    y = lax.map(expert_block, (xg, blk_e)).reshape(n_rows, d)
    y = y * row_w[:, None].astype(y.dtype)
    out = jnp.zeros((n_tok, d), y.dtype).at[row_tok].add(y)
    return out.reshape(bsz, s, d)


def setup_inputs(seed: int = 0) -> dict:
    key = jax.random.key(seed)
    ks = jax.random.split(key, 32)
    f32 = jnp.float32

    def nrm(k, shape, scale):
        return jax.random.normal(k, shape, f32) * scale

    def gain(k, shape):
        return 1.0 + 0.05 * jax.random.normal(k, shape, f32)

    D, L, E, F = D_MODEL, DEPTH, MOE_EXPERTS, MOE_FF
    return {
        'x': nrm(ks[0], (BATCH, SEQ, D), 1.0),
        'mem': nrm(ks[1], (BATCH, MEM_LEN, D), 1.0),
        'positions': jnp.tile(jnp.arange(SEQ, dtype=jnp.int32)[None, :], (BATCH, 1)),
        'norm_mix_g': gain(ks[2], (L, D)),
        'w_in': nrm(ks[3], (L, D, D_IN), D ** -0.5),
        'mlstm_gate_b': jnp.concatenate([nrm(ks[4], (L, MLSTM_HEADS), 0.1),
                                         3.0 + nrm(ks[5], (L, MLSTM_HEADS), 0.5)], axis=-1),
        'mlstm_conv_w': nrm(ks[6], (L, MLSTM_CONV, 2 * MLSTM_QK_W), MLSTM_CONV ** -0.5),
        'mlstm_conv_b': nrm(ks[7], (L, 2 * MLSTM_QK_W), 0.02),
        'mlstm_out_g': gain(ks[8], (L, MLSTM_V_W)),
        'moba_q_g': gain(ks[9], (L, MOBA_HEAD_DIM)),
        'moba_k_g': gain(ks[10], (L, MOBA_HEAD_DIM)),
        'w_out': nrm(ks[11], (L, D_MIX, D), D_MIX ** -0.5),
        'norm_cross_g': gain(ks[12], (L, D)),
        'norm_mem_g': gain(ks[13], (L, D)),
        'xa_wq': nrm(ks[14], (L, D, D), D ** -0.5),
        'xa_wkv': nrm(ks[15], (L, D, 2 * D), D ** -0.5),
        'xa_q_g': gain(ks[16], (L, XA_HEAD_DIM)),
        'xa_k_g': gain(ks[17], (L, XA_HEAD_DIM)),
        'xa_wo': nrm(ks[18], (L, D, D), D ** -0.5),
        'norm_ffn_g': gain(ks[19], (L, D)),
        'router_group_w': nrm(ks[20], (L, D, MOE_GROUPS), D ** -0.5),
        'router_group_b': nrm(ks[21], (L, MOE_GROUPS), 0.01),
        'router_expert_w': nrm(ks[22], (L, D, E), D ** -0.5),
        'router_expert_b': nrm(ks[23], (L, E), 0.01),
        'exp_w_gate': nrm(ks[24], (L, E, D, F), D ** -0.5),
        'exp_w_up': nrm(ks[25], (L, E, D, F), D ** -0.5),
        'exp_w_down': nrm(ks[26], (L, E, F, D), F ** -0.5),
    }


def reference(x, mem, positions, norm_mix_g, w_in, mlstm_gate_b, mlstm_conv_w, mlstm_conv_b,
              mlstm_out_g, moba_q_g, moba_k_g, w_out, norm_cross_g, norm_mem_g, xa_wq, xa_wkv,
              xa_q_g, xa_k_g, xa_wo, norm_ffn_g, router_group_w, router_group_b, router_expert_w,
              router_expert_b, exp_w_gate, exp_w_up, exp_w_down):
    h = x
    for l in range(DEPTH):
        h = h + hybrid_mixer(rms_norm(h, norm_mix_g[l]), positions, w_in[l], mlstm_gate_b[l],
                             mlstm_conv_w[l], mlstm_conv_b[l], mlstm_out_g[l], moba_q_g[l],
                             moba_k_g[l], w_out[l])
        h = h + cross_attention(rms_norm(h, norm_cross_g[l]), rms_norm(mem, norm_mem_g[l]),
                                xa_wq[l], xa_wkv[l], xa_q_g[l], xa_k_g[l], xa_wo[l])
        h = h + hier_moe(rms_norm(h, norm_ffn_g[l]), router_group_w[l], router_group_b[l],
                         router_expert_w[l], router_expert_b[l], exp_w_gate[l], exp_w_up[l],
                         exp_w_down[l])
    return h
```

```python
import functools
import math

import jax
import jax.numpy as jnp
from jax import lax
from jax.experimental import pallas as pl
from jax.experimental.pallas import tpu as pltpu

F32 = jnp.float32
BF16 = jnp.bfloat16
I32 = jnp.int32

EPS = 1e-6
LANES = 128
SUBLANES = 8
VMEM_LIMIT = 56 * 1024 * 1024

MLSTM_HEADS = 4
MLSTM_QK = 128
MLSTM_V = 256
MLSTM_CHUNK = 128
MLSTM_CONV = 4
MOBA_HEADS = 8
MOBA_HD = 128
MOBA_BLOCK = 256
MOBA_TOPK = 3
ROPE_DIM = 32
ROPE_THETA = 500000.0
XA_HEADS = 4
MOE_GROUPS = 8
MOE_PER_GROUP = 8
MOE_EXPERTS = MOE_GROUPS * MOE_PER_GROUP
MOE_ROWS = 128

NT_DIMS = (((1,), (1,)), ((), ()))


def _params(*sem):
    return pltpu.CompilerParams(dimension_semantics=sem, vmem_limit_bytes=VMEM_LIMIT)


def _rms(x, g):
    return x * lax.rsqrt(jnp.mean(x * x, axis=-1, keepdims=True) + EPS) * g


def _dot(a, b):
    return jnp.dot(a, b, preferred_element_type=F32)


def _dot_nt(a, b, precision=None):
    return lax.dot_general(a, b, NT_DIMS, precision=precision, preferred_element_type=F32)


def _proj_in_kernel(x_ref, g_ref, w_ref, o_ref, xn_ref):
    @pl.when(pl.program_id(1) == 0)
    def _():
        xn_ref[...] = _rms(x_ref[...], g_ref[...]).astype(BF16)

    o_ref[...] = _dot(xn_ref[...], w_ref[...])


def _proj_in(x, g, w, tm, tn):
    s, d = x.shape
    n = w.shape[1]
    return pl.pallas_call(
        _proj_in_kernel,
        grid=(s // tm, n // tn),
        in_specs=[pl.BlockSpec((tm, d), lambda i, j: (i, 0)),
                  pl.BlockSpec((1, d), lambda i, j: (0, 0)),
                  pl.BlockSpec((d, tn), lambda i, j: (0, j))],
        out_specs=pl.BlockSpec((tm, tn), lambda i, j: (i, j)),
        out_shape=jax.ShapeDtypeStruct((s, n), F32),
        scratch_shapes=[pltpu.VMEM((tm, d), BF16)],
        compiler_params=_params("parallel", "arbitrary"),
        name="proj_in",
    )(x, g, w)


def _split3(x):
    hi = x.astype(BF16)
    r = x - hi.astype(F32)
    mid = r.astype(BF16)
    lo = (r - mid.astype(F32)).astype(BF16)
    return hi, mid, lo


def _log_sigmoid(x):
    return jnp.minimum(x, 0.0) - jnp.log(1.0 + jnp.exp(-jnp.abs(x)))


def _mlstm_kernel(q_ref, k_ref, v_ref, og_ref, gt_ref, gb_ref, cw_ref, cb_ref, outg_ref, hm_ref,
                  qext, kext, c_scr, n_scr, m_scr):
    L = MLSTM_CHUNK
    pad = SUBLANES
    qkw = MLSTM_HEADS * MLSTM_QK

    @pl.when(pl.program_id(0) == 0)
    def _():
        qext[0:pad, :] = jnp.zeros((pad, qkw), F32)
        kext[0:pad, :] = jnp.zeros((pad, qkw), F32)
        c_scr[...] = jnp.zeros_like(c_scr)
        n_scr[...] = jnp.zeros_like(n_scr)
        m_scr[...] = jnp.zeros_like(m_scr)

    qext[pad:pad + L, :] = q_ref[...]
    kext[pad:pad + L, :] = k_ref[...]

    def conv_silu(ext, lo):
        acc = jnp.broadcast_to(cb_ref[:, lo:lo + qkw], (L, qkw))
        for j in range(MLSTM_CONV):
            shift = MLSTM_CONV - 1 - j
            acc = acc + cw_ref[j:j + 1, lo:lo + qkw] * ext[pad - shift:pad - shift + L, :]
        return acc * jax.nn.sigmoid(acc)

    qc = conv_silu(qext, 0)
    kc = conv_silu(kext, qkw)
    qext[0:pad, :] = qext[L:L + pad, :]
    kext[0:pad, :] = kext[L:L + pad, :]

    gts = gt_ref[...] + gb_ref[...]
    gtr = gts.T
    row = lax.broadcasted_iota(I32, (L, L), 0)
    col = lax.broadcasted_iota(I32, (L, L), 1)
    causal = col <= row
    tri_lo = jnp.where(causal, 1.0, 0.0).astype(BF16)
    tri_up = jnp.where(row <= col, 1.0, 0.0).astype(BF16)
    bc_all = sum(_dot(tri_lo, part) for part in _split3(_log_sigmoid(gts)))
    br_all = sum(_dot(part, tri_up) for part in _split3(_log_sigmoid(gtr)))

    for h in range(MLSTM_HEADS):
        fi = MLSTM_HEADS + h
        m_prev = m_scr[h, 0:1, 0:1]
        b_c = bc_all[:, fi:fi + 1]
        b_r = br_all[fi:fi + 1, :]
        i_c = gts[:, h:h + 1]
        i_r = gtr[h:h + 1, :]
        g = b_c[L - 1:L, :]
        d = jnp.where(causal, (b_c - b_r) + i_r, -jnp.inf)
        inter = b_c + m_prev
        m_t = jnp.maximum(inter, jnp.max(d, axis=-1, keepdims=True))
        w_inter = jnp.exp(inter - m_t)
        qh = qc[:, h * MLSTM_QK:(h + 1) * MLSTM_QK] * (MLSTM_QK ** -0.5)
        kh = kc[:, h * MLSTM_QK:(h + 1) * MLSTM_QK]
        vb = v_ref[:, h * MLSTM_V:(h + 1) * MLSTM_V].astype(BF16)
        qb = qh.astype(BF16)
        p = jnp.exp(d - m_t) * _dot_nt(qb, kh.astype(BF16))
        c_prev = c_scr[h]
        n_prev = n_scr[h, 0:1, :]
        num = w_inter * _dot(qb, c_prev.astype(BF16)) + _dot(p.astype(BF16), vb)
        den = (w_inter * jnp.sum(qh * n_prev, axis=-1, keepdims=True)
               + jnp.sum(p, axis=-1, keepdims=True))
        hh = num * (1.0 / jnp.maximum(jnp.abs(den), jnp.exp(-m_t)))

        a = (g - b_c) + i_c
        m_new = jnp.maximum(g + m_prev, jnp.max(a, axis=0, keepdims=True))
        kw = kh * jnp.exp(a - m_new)
        decay = jnp.exp(g + m_prev - m_new)
        c_scr[h] = decay * c_prev + _dot(kw.T.astype(BF16), vb)
        n_scr[h, 0:1, :] = decay * n_prev + jnp.sum(kw, axis=0, keepdims=True)
        m_scr[h] = jnp.broadcast_to(m_new, (SUBLANES, LANES))

        hn = hh * lax.rsqrt(jnp.mean(hh * hh, axis=-1, keepdims=True) + EPS)
        vs = slice(h * MLSTM_V, (h + 1) * MLSTM_V)
        hn = hn * outg_ref[:, vs] * jax.nn.sigmoid(og_ref[:, vs])
        hm_ref[:, vs] = hn.astype(hm_ref.dtype)


def _mlstm(proj, gate_b, conv_w, conv_b, out_g):
    s = proj.shape[0]
    L = MLSTM_CHUNK
    qkw = MLSTM_HEADS * MLSTM_QK
    vw = MLSTM_HEADS * MLSTM_V
    gate_blk = proj.shape[1] // LANES - 1
    return pl.pallas_call(
        _mlstm_kernel,
        grid=(s // L,),
        in_specs=[pl.BlockSpec((L, qkw), lambda c: (c, 0)),
                  pl.BlockSpec((L, qkw), lambda c: (c, 1)),
                  pl.BlockSpec((L, vw), lambda c: (c, 1)),
                  pl.BlockSpec((L, vw), lambda c: (c, 2)),
                  pl.BlockSpec((L, LANES), lambda c: (c, gate_blk)),
                  pl.BlockSpec((1, LANES), lambda c: (0, 0)),
                  pl.BlockSpec((MLSTM_CONV, 2 * qkw), lambda c: (0, 0)),
                  pl.BlockSpec((1, 2 * qkw), lambda c: (0, 0)),
                  pl.BlockSpec((1, vw), lambda c: (0, 0))],
        out_specs=pl.BlockSpec((L, vw), lambda c: (c, 0)),
        out_shape=jax.ShapeDtypeStruct((s, vw), BF16),
        scratch_shapes=[pltpu.VMEM((L + SUBLANES, qkw), F32),
                        pltpu.VMEM((L + SUBLANES, qkw), F32),
                        pltpu.VMEM((MLSTM_HEADS, MLSTM_QK, MLSTM_V), F32),
                        pltpu.VMEM((MLSTM_HEADS, SUBLANES, LANES), F32),
                        pltpu.VMEM((MLSTM_HEADS, SUBLANES, LANES), F32)],
        compiler_params=_params("arbitrary"),
        name="mlstm",
    )(proj, proj, proj, proj, proj, gate_b, conv_w, conv_b, out_g)


def _moba_prep_kernel(q_ref, k_ref, v_ref, pos_ref, qg_ref, kg_ref, qn_ref, kn_ref, vb_ref, km_ref):
    rows = q_ref.shape[0]
    half = ROPE_DIM // 2
    lane = lax.broadcasted_iota(I32, (1, MOBA_HD), 1)
    inv_freq = jnp.exp((lane & (half - 1)).astype(F32) * (-(2.0 / ROPE_DIM) * math.log(ROPE_THETA)))
    ang = pos_ref[...].astype(F32) * inv_freq
    cos = jnp.where(lane < ROPE_DIM, jnp.cos(ang), 1.0)
    sin = jnp.sin(ang)
    sin = jnp.where(lane < half, -sin, jnp.where(lane < ROPE_DIM, sin, 0.0))

    def rope(x):
        partner = jnp.where(lane < half, pltpu.roll(x, MOBA_HD - half, 1), pltpu.roll(x, half, 1))
        return x * cos + partner * sin

    for h in range(MOBA_HEADS):
        hs = slice(h * MOBA_HD, (h + 1) * MOBA_HD)
        qn_ref[:, hs] = rope(_rms(q_ref[:, hs], qg_ref[...]))
        kn = rope(_rms(k_ref[:, hs], kg_ref[...]))
        kn_ref[:, hs] = kn.astype(BF16)
        km_ref[0, :, hs] = jnp.sum(kn, axis=0, keepdims=True) * (1.0 / rows)
    vb_ref[...] = v_ref[...].astype(BF16)


def _moba_prep(proj, pos, q_g, k_g):
    s = proj.shape[0]
    w = MOBA_HEADS * MOBA_HD
    bs = MOBA_BLOCK
    nb = s // bs
    first = (proj.shape[1] - LANES) // w - 3
    return pl.pallas_call(
        _moba_prep_kernel,
        grid=(nb,),
        in_specs=[pl.BlockSpec((bs, w), lambda i: (i, first)),
                  pl.BlockSpec((bs, w), lambda i: (i, first + 1)),
                  pl.BlockSpec((bs, w), lambda i: (i, first + 2)),
                  pl.BlockSpec((bs, 1), lambda i: (i, 0)),
                  pl.BlockSpec((1, MOBA_HD), lambda i: (0, 0)),
                  pl.BlockSpec((1, MOBA_HD), lambda i: (0, 0))],
        out_specs=[pl.BlockSpec((bs, w), lambda i: (i, 0)),
                   pl.BlockSpec((bs, w), lambda i: (i, 0)),
                   pl.BlockSpec((bs, w), lambda i: (i, 0)),
                   pl.BlockSpec((1, 1, w), lambda i: (i, 0, 0))],
        out_shape=[jax.ShapeDtypeStruct((s, w), F32),
                   jax.ShapeDtypeStruct((s, w), BF16),
                   jax.ShapeDtypeStruct((s, w), BF16),
                   jax.ShapeDtypeStruct((nb, 1, w), F32)],
        compiler_params=_params("parallel"),
        name="moba_prep",
    )(proj, proj, proj, pos, q_g, k_g)


def _moba_attn_kernel(q_ref, k_ref, v_ref, km_ref, o_ref):
    i = pl.program_id(1)
    bs = MOBA_BLOCK
    scale = MOBA_HD ** -0.5
    q = q_ref[...]
    km = km_ref[:, 0, :]
    nb = km.shape[0]
    gate = _dot_nt(q, km, precision=lax.Precision.HIGHEST)
    blk = lax.broadcasted_iota(I32, (bs, nb), 1)
    valid = blk < i
    gm = jnp.where(valid, gate, -jnp.inf)
    cnt = jnp.zeros((bs, nb), I32)
    for jj in range(nb):
        gj = gm[:, jj:jj + 1]
        beats = (gj > gm) | ((gj == gm) & (jj < blk))
        cnt = cnt + jnp.where(beats, 1, 0)
    sel = jnp.where(valid & (cnt < MOBA_TOPK), 1.0, 0.0)

    qb = q.astype(BF16)

    def kv_block(j):
        rows = pl.ds(pl.multiple_of(j * bs, bs), bs)
        return k_ref[rows, :], v_ref[rows, :]

    k0, v0 = kv_block(i)
    s0 = _dot_nt(qb, k0) * scale
    r = lax.broadcasted_iota(I32, (bs, bs), 0)
    c = lax.broadcasted_iota(I32, (bs, bs), 1)
    s0 = jnp.where(c <= r, s0, -jnp.inf)
    m0 = jnp.max(s0, axis=-1, keepdims=True)
    p0 = jnp.exp(s0 - m0)
    l0 = jnp.sum(p0, axis=-1, keepdims=True)
    acc0 = _dot(p0.astype(BF16), v0)

    def body(j, carry):
        m, l, acc = carry
        kj, vj = kv_block(j)
        on = jnp.max(jnp.where(blk == j, sel, 0.0), axis=-1, keepdims=True) > 0.5
        sj = jnp.where(on, _dot_nt(qb, kj) * scale, -jnp.inf)
        m_new = jnp.maximum(m, jnp.max(sj, axis=-1, keepdims=True))
        alpha = jnp.exp(m - m_new)
        p = jnp.exp(sj - m_new)
        l = alpha * l + jnp.sum(p, axis=-1, keepdims=True)
        acc = alpha * acc + _dot(p.astype(BF16), vj)
        return m_new, l, acc

    _, l, acc = lax.fori_loop(0, i, body, (m0, l0, acc0))
    o_ref[...] = (acc * (1.0 / l)).astype(o_ref.dtype)


def _moba_attn(qn, kn, vb, kmean):
    s, w = qn.shape
    bs = MOBA_BLOCK
    nb = s // bs
    return pl.pallas_call(
        _moba_attn_kernel,
        grid=(MOBA_HEADS, nb),
        in_specs=[pl.BlockSpec((bs, MOBA_HD), lambda h, i: (i, h)),
                  pl.BlockSpec((s, MOBA_HD), lambda h, i: (0, h)),
                  pl.BlockSpec((s, MOBA_HD), lambda h, i: (0, h)),
                  pl.BlockSpec((nb, 1, MOBA_HD), lambda h, i: (0, 0, h))],
        out_specs=pl.BlockSpec((bs, MOBA_HD), lambda h, i: (i, h)),
        out_shape=jax.ShapeDtypeStruct((s, w), BF16),
        compiler_params=_params("parallel", "parallel"),
        name="moba_attn",
    )(qn, kn, vb, kmean)


def _mix_out_kernel(hm_ref, ha_ref, w_ref, x_ref, o_ref):
    half = hm_ref.shape[1]
    o_ref[...] = (x_ref[...] + _dot(hm_ref[...], w_ref[0:half, :])
                  + _dot(ha_ref[...], w_ref[half:2 * half, :]))


def _mix_out(hm, ha, w, x, tm):
    s, d = x.shape
    half = hm.shape[1]
    return pl.pallas_call(
        _mix_out_kernel,
        grid=(s // tm,),
        in_specs=[pl.BlockSpec((tm, half), lambda i: (i, 0)),
                  pl.BlockSpec((tm, half), lambda i: (i, 0)),
                  pl.BlockSpec((2 * half, d), lambda i: (0, 0)),
                  pl.BlockSpec((tm, d), lambda i: (i, 0))],
        out_specs=pl.BlockSpec((tm, d), lambda i: (i, 0)),
        out_shape=jax.ShapeDtypeStruct((s, d), F32),
        compiler_params=_params("parallel"),
        name="mix_out",
    )(hm, ha, w, x)


def _mem_kv_kernel(mem_ref, g_ref, w_ref, kg_ref, o_ref, mn_ref):
    j = pl.program_id(0)

    @pl.when(j == 0)
    def _():
        mn_ref[...] = _rms(mem_ref[...], g_ref[...]).astype(BF16)

    y = _dot(mn_ref[...], w_ref[...])
    o_ref[...] = jnp.where(j < XA_HEADS, _rms(y, kg_ref[...]), y).astype(o_ref.dtype)


def _mem_kv(mem, g, wkv, k_g):
    m, d = mem.shape
    hd = d // XA_HEADS
    return pl.pallas_call(
        _mem_kv_kernel,
        grid=(2 * XA_HEADS,),
        in_specs=[pl.BlockSpec((m, d), lambda j: (0, 0)),
                  pl.BlockSpec((1, d), lambda j: (0, 0)),
                  pl.BlockSpec((d, hd), lambda j: (0, j)),
                  pl.BlockSpec((1, hd), lambda j: (0, 0))],
        out_specs=pl.BlockSpec((m, hd), lambda j: (0, j)),
        out_shape=jax.ShapeDtypeStruct((m, 2 * d), BF16),
        scratch_shapes=[pltpu.VMEM((m, d), BF16)],
        compiler_params=_params("arbitrary"),
        name="mem_kv",
    )(mem, g, wkv, k_g)


def _xattn_kernel(h_ref, g_ref, wq_ref, k_ref, v_ref, qg_ref, wo_ref, o_ref, hn_ref):
    j = pl.program_id(1)

    @pl.when(j == 0)
    def _():
        h = h_ref[...]
        hn_ref[...] = _rms(h, g_ref[...]).astype(BF16)
        o_ref[...] = h

    hd = wq_ref.shape[1]
    q = _rms(_dot(hn_ref[...], wq_ref[...]), qg_ref[...]).astype(BF16)
    sc = _dot_nt(q, k_ref[...]) * (hd ** -0.5)
    p = jnp.exp(sc - jnp.max(sc, axis=-1, keepdims=True))
    p = p * (1.0 / jnp.sum(p, axis=-1, keepdims=True))
    o = _dot(p.astype(BF16), v_ref[...]).astype(BF16)
    o_ref[...] += _dot(o, wo_ref[...])


def _xattn(h, g, wq, kv, q_g, wo, tm):
    s, d = h.shape
    m = kv.shape[0]
    hd = d // XA_HEADS
    return pl.pallas_call(
        _xattn_kernel,
        grid=(s // tm, XA_HEADS),
        in_specs=[pl.BlockSpec((tm, d), lambda i, j: (i, 0)),
                  pl.BlockSpec((1, d), lambda i, j: (0, 0)),
                  pl.BlockSpec((d, hd), lambda i, j: (0, j)),
                  pl.BlockSpec((m, hd), lambda i, j: (0, j)),
                  pl.BlockSpec((m, hd), lambda i, j: (0, XA_HEADS + j)),
                  pl.BlockSpec((1, hd), lambda i, j: (0, 0)),
                  pl.BlockSpec((hd, d), lambda i, j: (j, 0))],
        out_specs=pl.BlockSpec((tm, d), lambda i, j: (i, 0)),
        out_shape=jax.ShapeDtypeStruct((s, d), F32),
        scratch_shapes=[pltpu.VMEM((tm, d), BF16)],
        compiler_params=_params("parallel", "arbitrary"),
        name="xattn",
    )(h, g, wq, kv, kv, q_g, wo)


ROUTE_EID = 0
ROUTE_RANK = 2
ROUTE_WT = 4


def _router_kernel(h_ref, g_ref, w_ref, b_ref, out_ref, cnt_ref, carry):
    tm = h_ref.shape[0]

    @pl.when(pl.program_id(0) == 0)
    def _():
        carry[...] = jnp.zeros_like(carry)

    hn = _rms(h_ref[...], g_ref[...])
    logits = jnp.dot(hn, w_ref[...], precision=lax.Precision.HIGHEST,
                     preferred_element_type=F32) + b_ref[...]
    lane = lax.broadcasted_iota(I32, (tm, LANES), 1)

    def first_lane(mask):
        return jnp.min(jnp.where(mask, lane.astype(F32), float(LANES)), axis=-1,
                       keepdims=True).astype(I32)

    is_g = lane < MOE_GROUPS
    gmax = jnp.max(jnp.where(is_g, logits, -jnp.inf), axis=-1, keepdims=True)
    gsum = jnp.sum(jnp.where(is_g, jnp.exp(logits - gmax), 0.0), axis=-1, keepdims=True)
    gsel = first_lane(is_g & (logits == gmax))
    pg = 1.0 / gsum

    grp_lo = MOE_GROUPS + MOE_PER_GROUP * gsel
    in_grp = (lane >= grp_lo) & (lane < grp_lo + MOE_PER_GROUP)
    emax = jnp.max(jnp.where(in_grp, logits, -jnp.inf), axis=-1, keepdims=True)
    eexp = jnp.where(in_grp, jnp.exp(logits - emax), 0.0)
    eprob = eexp / jnp.sum(eexp, axis=-1, keepdims=True)
    p1 = jnp.max(jnp.where(in_grp, eprob, -1.0), axis=-1, keepdims=True)
    l1 = first_lane(in_grp & (eprob == p1))
    rest = in_grp & (lane != l1)
    p2 = jnp.max(jnp.where(rest, eprob, -1.0), axis=-1, keepdims=True)
    l2 = first_lane(rest & (eprob == p2))
    psum = p1 + p2
    w1 = pg * p1 / psum
    w2 = pg * p2 / psum

    oh1 = lane == l1
    oh2 = lane == l2
    onehot = jnp.where(oh1 | oh2, 1.0, 0.0)
    r = lax.broadcasted_iota(I32, (tm, tm), 0)
    c = lax.broadcasted_iota(I32, (tm, tm), 1)
    strict = jnp.where(c < r, 1.0, 0.0).astype(BF16)
    before = _dot(strict, onehot.astype(BF16)) + carry[0:1, :]
    rank1 = jnp.sum(jnp.where(oh1, before, 0.0), axis=-1, keepdims=True)
    rank2 = jnp.sum(jnp.where(oh2, before, 0.0), axis=-1, keepdims=True)
    total = carry[0:1, :] + jnp.sum(onehot, axis=0, keepdims=True)
    carry[...] = jnp.broadcast_to(total, carry.shape)
    cnt_ref[...] = jnp.broadcast_to(total, cnt_ref.shape)

    out = jnp.where(lane == ROUTE_EID, (l1 - MOE_GROUPS).astype(F32), 0.0)
    out = jnp.where(lane == ROUTE_EID + 1, (l2 - MOE_GROUPS).astype(F32), out)
    out = jnp.where(lane == ROUTE_RANK, rank1, out)
    out = jnp.where(lane == ROUTE_RANK + 1, rank2, out)
    out = jnp.where(lane == ROUTE_WT, w1, out)
    out = jnp.where(lane == ROUTE_WT + 1, w2, out)
    out_ref[...] = out


def _router(h, g, w, b, tm):
    s, d = h.shape
    return pl.pallas_call(
        _router_kernel,
        grid=(s // tm,),
        in_specs=[pl.BlockSpec((tm, d), lambda i: (i, 0)),
                  pl.BlockSpec((1, d), lambda i: (0, 0)),
                  pl.BlockSpec((d, LANES), lambda i: (0, 0)),
                  pl.BlockSpec((1, LANES), lambda i: (0, 0))],
        out_specs=[pl.BlockSpec((tm, LANES), lambda i: (i, 0)),
                   pl.BlockSpec((SUBLANES, LANES), lambda i: (0, 0))],
        out_shape=[jax.ShapeDtypeStruct((s, LANES), F32),
                   jax.ShapeDtypeStruct((SUBLANES, LANES), F32)],
        scratch_shapes=[pltpu.VMEM((SUBLANES, LANES), F32)],
        compiler_params=_params("arbitrary"),
        name="router",
    )(h, g, w, b)


def _dispatch_kernel(dest_ref, h_ref, g_ref, xg_in_ref, xg_ref, buf, sem):
    del xg_in_ref
    tm = h_ref.shape[0]
    base = pl.program_id(0) * (2 * tm)
    buf[...] = _rms(h_ref[...], g_ref[...])

    def row_copy(t, d):
        return pltpu.make_async_copy(buf.at[pl.ds(t, 1), :], xg_ref.at[pl.ds(d, 1), :], sem)

    def issue(t, carry):
        row_copy(t, dest_ref[base + 2 * t]).start()
        row_copy(t, dest_ref[base + 2 * t + 1]).start()
        return carry

    def drain(t, carry):
        row_copy(0, 0).wait()
        row_copy(0, 0).wait()
        return carry

    lax.fori_loop(0, tm, issue, 0)
    lax.fori_loop(0, tm, drain, 0)


def _dispatch(dest, h, g, xg_init, tm):
    s, d = h.shape
    grid_spec = pltpu.PrefetchScalarGridSpec(
        num_scalar_prefetch=1,
        grid=(s // tm,),
        in_specs=[pl.BlockSpec((tm, d), lambda i, dest: (i, 0)),
                  pl.BlockSpec((1, d), lambda i, dest: (0, 0)),
                  pl.BlockSpec(memory_space=pl.ANY)],
        out_specs=pl.BlockSpec(memory_space=pl.ANY),
        scratch_shapes=[pltpu.VMEM((tm, d), F32), pltpu.SemaphoreType.DMA(())],
    )
    return pl.pallas_call(
        _dispatch_kernel,
        grid_spec=grid_spec,
        out_shape=jax.ShapeDtypeStruct(xg_init.shape, F32),
        input_output_aliases={3: 0},
        compiler_params=_params("arbitrary"),
        name="moe_dispatch",
    )(dest, h, g, xg_init)


def _moe_ffn_kernel(be_ref, nu_ref, x_ref, wg_ref, wu_ref, wd_ref, y_ref, wgb, wub, wdb):
    b = pl.program_id(0)
    changed = (b == 0) | (be_ref[b] != be_ref[jnp.maximum(b - 1, 0)])

    @pl.when(changed)
    def _():
        wgb[...] = wg_ref[...].astype(BF16)
        wub[...] = wu_ref[...].astype(BF16)
        wdb[...] = wd_ref[...].astype(BF16)

    @pl.when(b < nu_ref[0])
    def _():
        xb = x_ref[...].astype(BF16)
        gate = _dot(xb, wgb[...])
        up = _dot(xb, wub[...])
        act = (gate * jax.nn.sigmoid(gate) * up).astype(BF16)
        y_ref[...] = _dot(act, wdb[...])

    @pl.when(b >= nu_ref[0])
    def _():
        y_ref[...] = jnp.zeros_like(y_ref)


def _moe_ffn(blk_e, n_used, xg, w_gate, w_up, w_down):
    n_rows, d = xg.shape
    ff = w_gate.shape[2]
    n_blk = n_rows // MOE_ROWS
    grid_spec = pltpu.PrefetchScalarGridSpec(
        num_scalar_prefetch=2,
        grid=(n_blk,),
        in_specs=[pl.BlockSpec((MOE_ROWS, d), lambda b, be, nu: (b, 0)),
                  pl.BlockSpec((None, d, ff), lambda b, be, nu: (be[b], 0, 0)),
                  pl.BlockSpec((None, d, ff), lambda b, be, nu: (be[b], 0, 0)),
                  pl.BlockSpec((None, ff, d), lambda b, be, nu: (be[b], 0, 0))],
        out_specs=pl.BlockSpec((MOE_ROWS, d), lambda b, be, nu: (b, 0)),
        scratch_shapes=[pltpu.VMEM((d, ff), BF16), pltpu.VMEM((d, ff), BF16),
                        pltpu.VMEM((ff, d), BF16)],
    )
    return pl.pallas_call(
        _moe_ffn_kernel,
        grid_spec=grid_spec,
        out_shape=jax.ShapeDtypeStruct((n_rows, d), F32),
        compiler_params=_params("arbitrary"),
        name="moe_ffn",
    )(blk_e, n_used, xg, w_gate, w_up, w_down)


def _combine_kernel(dest_ref, h_ref, r_ref, y_ref, o_ref, buf, sem):
    tm = h_ref.shape[0]
    base = pl.program_id(0) * (2 * tm)

    def row_copy(k, t, d):
        return pltpu.make_async_copy(y_ref.at[pl.ds(d, 1), :], buf.at[k, pl.ds(t, 1), :], sem)

    def issue(t, carry):
        row_copy(0, t, dest_ref[base + 2 * t]).start()
        row_copy(1, t, dest_ref[base + 2 * t + 1]).start()
        return carry

    def drain(t, carry):
        row_copy(0, 0, 0).wait()
        row_copy(1, 0, 0).wait()
        return carry

    lax.fori_loop(0, tm, issue, 0)
    lax.fori_loop(0, tm, drain, 0)
    r = r_ref[...]
    o_ref[...] = (h_ref[...] + r[:, ROUTE_WT:ROUTE_WT + 1] * buf[0]
                  + r[:, ROUTE_WT + 1:ROUTE_WT + 2] * buf[1])


def _combine(dest, h, routed, y, tm):
    s, d = h.shape
    grid_spec = pltpu.PrefetchScalarGridSpec(
        num_scalar_prefetch=1,
        grid=(s // tm,),
        in_specs=[pl.BlockSpec((tm, d), lambda i, dest: (i, 0)),
                  pl.BlockSpec((tm, LANES), lambda i, dest: (i, 0)),
                  pl.BlockSpec(memory_space=pl.ANY)],
        out_specs=pl.BlockSpec((tm, d), lambda i, dest: (i, 0)),
        scratch_shapes=[pltpu.VMEM((2, tm, d), F32), pltpu.SemaphoreType.DMA(())],
    )
    return pl.pallas_call(
        _combine_kernel,
        grid_spec=grid_spec,
        out_shape=jax.ShapeDtypeStruct((s, d), F32),
        compiler_params=_params("arbitrary"),
        name="moe_combine",
    )(dest, h, routed, y)


def _layer(h, mem, pos, norm_mix_g, w_in, gate_b, conv_w, conv_b, out_g, moba_q_g, moba_k_g, w_out,
           norm_cross_g, norm_mem_g, xa_wq, xa_wkv, xa_q_g, xa_k_g, xa_wo, norm_ffn_g,
           router_group_w, router_group_b, router_expert_w, router_expert_b,
           exp_w_gate, exp_w_up, exp_w_down):
    s, d = h.shape
    row = lambda v: v.reshape(1, -1)

    n_gate = 2 * MLSTM_HEADS
    gate_lo = 2 * MLSTM_HEADS * MLSTM_QK + 2 * MLSTM_HEADS * MLSTM_V
    w_cat = jnp.concatenate(
        [w_in[:, :gate_lo], w_in[:, gate_lo + n_gate:], w_in[:, gate_lo:gate_lo + n_gate],
         jnp.zeros((d, LANES - n_gate), w_in.dtype)], axis=1).astype(BF16)
    proj = _proj_in(h, row(norm_mix_g), w_cat, tm=512, tn=w_cat.shape[1] // 7)

    gate_b_row = jnp.pad(gate_b, (0, LANES - n_gate)).reshape(1, LANES)
    hm = _mlstm(proj, gate_b_row, conv_w, row(conv_b), row(out_g))
    qn, kn, vb, kmean = _moba_prep(proj, pos, row(moba_q_g), row(moba_k_g))
    ha = _moba_attn(qn, kn, vb, kmean)
    h = _mix_out(hm, ha, w_out.astype(BF16), h, tm=512)

    kv = _mem_kv(mem, row(norm_mem_g), xa_wkv.astype(BF16), row(xa_k_g))
    h = _xattn(h, row(norm_cross_g), xa_wq.astype(BF16), kv, row(xa_q_g), xa_wo.astype(BF16), tm=512)

    w_route = jnp.concatenate(
        [router_group_w, router_expert_w,
         jnp.zeros((d, LANES - MOE_GROUPS - MOE_EXPERTS), router_group_w.dtype)], axis=1)
    b_route = jnp.pad(jnp.concatenate([router_group_b, router_expert_b]),
                      (0, LANES - MOE_GROUPS - MOE_EXPERTS)).reshape(1, LANES)
    routed, counts = _router(h, row(norm_ffn_g), w_route, b_route, tm=512)

    eid = routed[:, ROUTE_EID:ROUTE_EID + 2].astype(I32)
    rank = routed[:, ROUTE_RANK:ROUTE_RANK + 2].astype(I32)
    cnt = counts[0, MOE_GROUPS:MOE_GROUPS + MOE_EXPERTS].astype(I32)
    padded = (cnt + MOE_ROWS - 1) // MOE_ROWS * MOE_ROWS
    pend = jnp.cumsum(padded)
    dest = ((pend - padded)[eid] + rank).reshape(-1)
    n_blk = -(-2 * s // MOE_ROWS) + MOE_EXPERTS
    blk_start = jnp.arange(n_blk, dtype=I32) * MOE_ROWS
    blk_e = jnp.minimum(jnp.sum((pend[None, :] <= blk_start[:, None]).astype(I32), axis=1),
                        MOE_EXPERTS - 1)
    n_used = (pend[-1:] // MOE_ROWS).astype(I32)

    xg = _dispatch(dest, h, row(norm_ffn_g), jnp.zeros((n_blk * MOE_ROWS, d), F32), tm=256)
    y = _moe_ffn(blk_e, n_used, xg, exp_w_gate, exp_w_up, exp_w_down)
    return _combine(dest, h, routed, y, tm=256)


def kernel(x, mem, positions, norm_mix_g, w_in, mlstm_gate_b, mlstm_conv_w, mlstm_conv_b, mlstm_out_g, moba_q_g, moba_k_g, w_out, norm_cross_g, norm_mem_g, xa_wq, xa_wkv, xa_q_g, xa_k_g, xa_wo, norm_ffn_g, router_group_w, router_group_b, router_expert_w, router_expert_b, exp_w_gate, exp_w_up, exp_w_down):
    bsz, s, _ = x.shape
    assert bsz == 1, "single-sequence prefill only"
    per_layer = (norm_mix_g, w_in, mlstm_gate_b, mlstm_conv_w, mlstm_conv_b, mlstm_out_g, moba_q_g,
                 moba_k_g, w_out, norm_cross_g, norm_mem_g, xa_wq, xa_wkv, xa_q_g, xa_k_g, xa_wo,
                 norm_ffn_g, router_group_w, router_group_b, router_expert_w, router_expert_b,
                 exp_w_gate, exp_w_up, exp_w_down)
    h = x[0]
    pos = positions.reshape(s, 1)
    for l in range(norm_mix_g.shape[0]):
        h = _layer(h, mem[0], pos, *(p[l] for p in per_layer))
    return h[None]
```

```python
import functools
import math

import jax
import jax.numpy as jnp
from jax import lax
from jax.experimental import pallas as pl
from jax.experimental.pallas import tpu as pltpu

F32 = jnp.float32
BF16 = jnp.bfloat16
I32 = jnp.int32

EPS = 1e-6
LANES = 128
SUBLANES = 8
VMEM_LIMIT = 56 * 1024 * 1024

MLSTM_HEADS = 4
MLSTM_QK = 128
MLSTM_V = 256
MLSTM_CHUNK = 128
MLSTM_CONV = 4
MOBA_HEADS = 8
MOBA_HD = 128
MOBA_BLOCK = 256
MOBA_TOPK = 3
ROPE_DIM = 32
ROPE_THETA = 500000.0
XA_HEADS = 4
MOE_GROUPS = 8
MOE_PER_GROUP = 8
MOE_EXPERTS = MOE_GROUPS * MOE_PER_GROUP
MOE_ROWS = 128

NT_DIMS = (((1,), (1,)), ((), ()))


def _params(*sem):
    return pltpu.CompilerParams(dimension_semantics=sem, vmem_limit_bytes=VMEM_LIMIT)


def _rms(x, g):
    return x * lax.rsqrt(jnp.mean(x * x, axis=-1, keepdims=True) + EPS) * g


def _dot(a, b):
    return jnp.dot(a, b, preferred_element_type=F32)


def _dot_nt(a, b, precision=None):
    return lax.dot_general(a, b, NT_DIMS, precision=precision, preferred_element_type=F32)


def _proj_in_kernel(x_ref, g_ref, wa_ref, wb_ref, wg_ref, o_ref, xn_ref, *, na, nb):
    j = pl.program_id(1)

    @pl.when(j == 0)
    def _():
        xn_ref[...] = _rms(x_ref[...], g_ref[...]).astype(BF16)

    @pl.when(j < na)
    def _():
        o_ref[...] = _dot(xn_ref[...], wa_ref[...])

    @pl.when((j >= na) & (j < na + nb))
    def _():
        o_ref[...] = _dot(xn_ref[...], wb_ref[...])

    @pl.when(j == na + nb)
    def _():
        o_ref[...] = _dot(xn_ref[...], wg_ref[...])


def _proj_in(x, g, w_a, w_b, w_g, tm):
    s, d = x.shape
    tn = w_g.shape[1]
    na, nb = w_a.shape[1] // tn, w_b.shape[1] // tn
    return pl.pallas_call(
        functools.partial(_proj_in_kernel, na=na, nb=nb),
        grid=(s // tm, na + nb + 1),
        in_specs=[pl.BlockSpec((tm, d), lambda i, j: (i, 0)),
                  pl.BlockSpec((1, d), lambda i, j: (0, 0)),
                  pl.BlockSpec((d, tn), lambda i, j: (0, jnp.minimum(j, na - 1))),
                  pl.BlockSpec((d, tn), lambda i, j: (0, jnp.clip(j - na, 0, nb - 1))),
                  pl.BlockSpec((d, tn), lambda i, j: (0, 0))],
        out_specs=pl.BlockSpec((tm, tn), lambda i, j: (i, j)),
        out_shape=jax.ShapeDtypeStruct((s, (na + nb + 1) * tn), F32),
        scratch_shapes=[pltpu.VMEM((tm, d), BF16)],
        compiler_params=_params("parallel", "arbitrary"),
        name="proj_in",
    )(x, g, w_a, w_b, w_g)


def _split3(x):
    hi = x.astype(BF16)
    r = x - hi.astype(F32)
    mid = r.astype(BF16)
    lo = (r - mid.astype(F32)).astype(BF16)
    return hi, mid, lo


def _log_sigmoid(x):
    return jnp.minimum(x, 0.0) - jnp.log(1.0 + jnp.exp(-jnp.abs(x)))


def _mlstm_kernel(q_ref, k_ref, v_ref, og_ref, gt_ref, gb_ref, cw_ref, cb_ref, outg_ref, hm_ref,
                  qext, kext, c_scr, n_scr, m_scr):
    L = MLSTM_CHUNK
    pad = SUBLANES
    qkw = MLSTM_HEADS * MLSTM_QK

    @pl.when(pl.program_id(0) == 0)
    def _():
        qext[0:pad, :] = jnp.zeros((pad, qkw), F32)
        kext[0:pad, :] = jnp.zeros((pad, qkw), F32)
        c_scr[...] = jnp.zeros_like(c_scr)
        n_scr[...] = jnp.zeros_like(n_scr)
        m_scr[...] = jnp.zeros_like(m_scr)

    qext[pad:pad + L, :] = q_ref[...]
    kext[pad:pad + L, :] = k_ref[...]

    def conv_silu(ext, lo):
        acc = jnp.broadcast_to(cb_ref[:, lo:lo + qkw], (L, qkw))
        for j in range(MLSTM_CONV):
            shift = MLSTM_CONV - 1 - j
            acc = acc + cw_ref[j:j + 1, lo:lo + qkw] * ext[pad - shift:pad - shift + L, :]
        return acc * jax.nn.sigmoid(acc)

    qc = conv_silu(qext, 0)
    kc = conv_silu(kext, qkw)
    qext[0:pad, :] = qext[L:L + pad, :]
    kext[0:pad, :] = kext[L:L + pad, :]

    gts = gt_ref[...] + gb_ref[...]
    gtr = gts.T
    row = lax.broadcasted_iota(I32, (L, L), 0)
    col = lax.broadcasted_iota(I32, (L, L), 1)
    causal = col <= row
    tri_lo = jnp.where(causal, 1.0, 0.0).astype(BF16)
    tri_up = jnp.where(row <= col, 1.0, 0.0).astype(BF16)
    bc_all = sum(_dot(tri_lo, part) for part in _split3(_log_sigmoid(gts)))
    br_all = sum(_dot(part, tri_up) for part in _split3(_log_sigmoid(gtr)))

    for h in range(MLSTM_HEADS):
        fi = MLSTM_HEADS + h
        m_prev = m_scr[h, 0:1, 0:1]
        b_c = bc_all[:, fi:fi + 1]
        b_r = br_all[fi:fi + 1, :]
        i_c = gts[:, h:h + 1]
        i_r = gtr[h:h + 1, :]
        g = b_c[L - 1:L, :]
        d = jnp.where(causal, (b_c - b_r) + i_r, -jnp.inf)
        inter = b_c + m_prev
        m_t = jnp.maximum(inter, jnp.max(d, axis=-1, keepdims=True))
        w_inter = jnp.exp(inter - m_t)
        qh = qc[:, h * MLSTM_QK:(h + 1) * MLSTM_QK] * (MLSTM_QK ** -0.5)
        kh = kc[:, h * MLSTM_QK:(h + 1) * MLSTM_QK]
        vb = v_ref[:, h * MLSTM_V:(h + 1) * MLSTM_V].astype(BF16)
        qb = qh.astype(BF16)
        p = jnp.exp(d - m_t) * _dot_nt(qb, kh.astype(BF16))
        c_prev = c_scr[h]
        n_prev = n_scr[h, 0:1, :]
        num = w_inter * _dot(qb, c_prev.astype(BF16)) + _dot(p.astype(BF16), vb)
        den = (w_inter * jnp.sum(qh * n_prev, axis=-1, keepdims=True)
               + jnp.sum(p, axis=-1, keepdims=True))
        hh = num * (1.0 / jnp.maximum(jnp.abs(den), jnp.exp(-m_t)))

        a = (g - b_c) + i_c
        m_new = jnp.maximum(g + m_prev, jnp.max(a, axis=0, keepdims=True))
        kw = kh * jnp.exp(a - m_new)
        decay = jnp.exp(g + m_prev - m_new)
        c_scr[h] = decay * c_prev + _dot(kw.T.astype(BF16), vb)
        n_scr[h, 0:1, :] = decay * n_prev + jnp.sum(kw, axis=0, keepdims=True)
        m_scr[h] = jnp.broadcast_to(m_new, (SUBLANES, LANES))

        hn = hh * lax.rsqrt(jnp.mean(hh * hh, axis=-1, keepdims=True) + EPS)
        vs = slice(h * MLSTM_V, (h + 1) * MLSTM_V)
        hn = hn * outg_ref[:, vs] * jax.nn.sigmoid(og_ref[:, vs])
        hm_ref[:, vs] = hn.astype(hm_ref.dtype)


def _mlstm(proj, gate_col, gate_b, conv_w, conv_b, out_g):
    s = proj.shape[0]
    L = MLSTM_CHUNK
    qkw = MLSTM_HEADS * MLSTM_QK
    vw = MLSTM_HEADS * MLSTM_V
    gate_blk = gate_col // LANES
    return pl.pallas_call(
        _mlstm_kernel,
        grid=(s // L,),
        in_specs=[pl.BlockSpec((L, qkw), lambda c: (c, 0)),
                  pl.BlockSpec((L, qkw), lambda c: (c, 1)),
                  pl.BlockSpec((L, vw), lambda c: (c, 1)),
                  pl.BlockSpec((L, vw), lambda c: (c, 2)),
                  pl.BlockSpec((L, LANES), lambda c: (c, gate_blk)),
                  pl.BlockSpec((1, LANES), lambda c: (0, 0)),
                  pl.BlockSpec((MLSTM_CONV, 2 * qkw), lambda c: (0, 0)),
                  pl.BlockSpec((1, 2 * qkw), lambda c: (0, 0)),
                  pl.BlockSpec((1, vw), lambda c: (0, 0))],
        out_specs=pl.BlockSpec((L, vw), lambda c: (c, 0)),
        out_shape=jax.ShapeDtypeStruct((s, vw), BF16),
        scratch_shapes=[pltpu.VMEM((L + SUBLANES, qkw), F32),
                        pltpu.VMEM((L + SUBLANES, qkw), F32),
                        pltpu.VMEM((MLSTM_HEADS, MLSTM_QK, MLSTM_V), F32),
                        pltpu.VMEM((MLSTM_HEADS, SUBLANES, LANES), F32),
                        pltpu.VMEM((MLSTM_HEADS, SUBLANES, LANES), F32)],
        compiler_params=_params("arbitrary"),
        name="mlstm",
    )(proj, proj, proj, proj, proj, gate_b, conv_w, conv_b, out_g)


def _moba_prep_kernel(q_ref, k_ref, v_ref, pos_ref, qg_ref, kg_ref, qt_ref, kn_ref, vt_ref, km_ref):
    rows = q_ref.shape[0]
    half = ROPE_DIM // 2
    lane = lax.broadcasted_iota(I32, (1, MOBA_HD), 1)
    inv_freq = jnp.exp((lane & (half - 1)).astype(F32) * (-(2.0 / ROPE_DIM) * math.log(ROPE_THETA)))
    ang = pos_ref[...].astype(F32) * inv_freq
    cos = jnp.where(lane < ROPE_DIM, jnp.cos(ang), 1.0)
    sin = jnp.sin(ang)
    sin = jnp.where(lane < half, -sin, jnp.where(lane < ROPE_DIM, sin, 0.0))

    def rope(x):
        partner = jnp.where(lane < half, pltpu.roll(x, MOBA_HD - half, 1), pltpu.roll(x, half, 1))
        return x * cos + partner * sin

    for h in range(MOBA_HEADS):
        hs = slice(h * MOBA_HD, (h + 1) * MOBA_HD)
        qt_ref[hs, :] = rope(_rms(q_ref[:, hs], qg_ref[...])).T
        kn = rope(_rms(k_ref[:, hs], kg_ref[...]))
        kn_ref[:, hs] = kn.astype(BF16)
        km_ref[0, :, hs] = jnp.sum(kn, axis=0, keepdims=True) * (1.0 / rows)
        vt_ref[0, hs, :] = v_ref[:, hs].T.astype(BF16)


def _moba_prep(proj, q_col, pos, q_g, k_g):
    s = proj.shape[0]
    w = MOBA_HEADS * MOBA_HD
    bs = MOBA_BLOCK
    nb = s // bs
    first = q_col // w
    return pl.pallas_call(
        _moba_prep_kernel,
        grid=(nb,),
        in_specs=[pl.BlockSpec((bs, w), lambda i: (i, first)),
                  pl.BlockSpec((bs, w), lambda i: (i, first + 1)),
                  pl.BlockSpec((bs, w), lambda i: (i, first + 2)),
                  pl.BlockSpec((bs, 1), lambda i: (i, 0)),
                  pl.BlockSpec((1, MOBA_HD), lambda i: (0, 0)),
                  pl.BlockSpec((1, MOBA_HD), lambda i: (0, 0))],
        out_specs=[pl.BlockSpec((w, bs), lambda i: (0, i)),
                   pl.BlockSpec((bs, w), lambda i: (i, 0)),
                   pl.BlockSpec((1, w, bs), lambda i: (i, 0, 0)),
                   pl.BlockSpec((1, 1, w), lambda i: (i, 0, 0))],
        out_shape=[jax.ShapeDtypeStruct((w, s), F32),
                   jax.ShapeDtypeStruct((s, w), BF16),
                   jax.ShapeDtypeStruct((nb, w, bs), BF16),
                   jax.ShapeDtypeStruct((nb, 1, w), F32)],
        compiler_params=_params("parallel"),
        name="moba_prep",
    )(proj, proj, proj, pos, q_g, k_g)


def _moba_attn_kernel(qt_ref, k_ref, vt_ref, km_ref, o_ref, bias_scr, qb_scr, m_scr, l_scr, *tiles):
    i = pl.program_id(1)
    bs = MOBA_BLOCK
    hd = MOBA_HD
    nh = len(tiles) // 3
    acc_scr = tiles[:nh]
    s_scr = tiles[nh:2 * nh]
    p_scr = tiles[2 * nh:]
    nb = km_ref.shape[0]
    c2 = (hd ** -0.5) * math.log2(math.e)
    blk = lax.broadcasted_iota(I32, (nb, bs), 0)
    blk_f = blk.astype(F32)
    valid = blk < i
    key = lax.broadcasted_iota(I32, (bs, bs), 0)
    qry = lax.broadcasted_iota(I32, (bs, bs), 1)
    own = pl.ds(pl.multiple_of(i * bs, bs), bs)

    for h in range(nh):
        hs = slice(h * hd, (h + 1) * hd)
        qt = qt_ref[hs, :]
        gate = jnp.dot(km_ref[:, 0, hs], qt, precision=lax.Precision.HIGHEST,
                       preferred_element_type=F32)
        cand = valid
        for _ in range(MOBA_TOPK):
            gmax = jnp.max(jnp.where(cand, gate, -jnp.inf), axis=0, keepdims=True)
            hit = cand & (gate == gmax)
            first = jnp.min(jnp.where(hit, blk_f, float(nb)), axis=0, keepdims=True)
            cand = cand & jnp.logical_not(hit & (blk_f == first))
        bias_scr[h] = jnp.where(valid & jnp.logical_not(cand), 0.0, -jnp.inf)

        qb = (qt * c2).astype(BF16)
        qb_scr[h] = qb
        s0 = jnp.where(key <= qry, _dot(k_ref[own, hs], qb), -jnp.inf)
        m0 = jnp.max(s0, axis=0, keepdims=True)
        p0 = jnp.exp2(s0 - m0)
        m_scr[h] = m0
        l_scr[h] = jnp.sum(p0, axis=0, keepdims=True)
        acc_scr[h][...] = _dot(vt_ref[i, hs, :], p0.astype(BF16))

    ck = 32

    def body(j, carry):
        rows = pl.ds(pl.multiple_of(j * bs, bs), bs)
        for h in range(nh):
            s_scr[h][...] = _dot(k_ref[rows, h * hd:(h + 1) * hd], qb_scr[h])
        alphas = []
        for h in range(nh):
            bias = bias_scr[h, pl.ds(j, 1), :]
            cmax = s_scr[h][0:ck, :]
            for c in range(1, bs // ck):
                cmax = jnp.maximum(cmax, s_scr[h][c * ck:(c + 1) * ck, :])
            m = m_scr[h]
            m_new = jnp.maximum(m, jnp.max(cmax, axis=0, keepdims=True) + bias)
            alpha = jnp.exp2(m - m_new)
            shift = m_new - bias
            psum = jnp.zeros((ck, bs), F32)
            for c in range(bs // ck):
                p = jnp.exp2(s_scr[h][c * ck:(c + 1) * ck, :] - shift)
                psum = psum + p
                p_scr[h][c * ck:(c + 1) * ck, :] = p.astype(BF16)
            m_scr[h] = m_new
            l_scr[h] = alpha * l_scr[h] + jnp.sum(psum, axis=0, keepdims=True)
            alphas.append(alpha)
        for h in range(nh):
            acc_scr[h][...] = (alphas[h] * acc_scr[h][...]
                               + _dot(vt_ref[j, h * hd:(h + 1) * hd, :], p_scr[h][...]))
        return carry

    lax.fori_loop(0, i, body, 0)
    for h in range(nh):
        out_t = acc_scr[h][...] * (1.0 / l_scr[h])
        o_ref[:, h * hd:(h + 1) * hd] = out_t.T.astype(o_ref.dtype)


MOBA_HEADS_PER_STEP = 4


def _moba_attn(qt, kn, vt, kmean):
    w, s = qt.shape
    bs = MOBA_BLOCK
    nb = s // bs
    nh = MOBA_HEADS_PER_STEP
    gw = nh * MOBA_HD
    return pl.pallas_call(
        _moba_attn_kernel,
        grid=(MOBA_HEADS // nh, nb),
        in_specs=[pl.BlockSpec((gw, bs), lambda g, i: (g, i)),
                  pl.BlockSpec((s, gw), lambda g, i: (0, g)),
                  pl.BlockSpec((nb, gw, bs), lambda g, i: (0, g, 0)),
                  pl.BlockSpec((nb, 1, gw), lambda g, i: (0, 0, g))],
        out_specs=pl.BlockSpec((bs, gw), lambda g, i: (i, g)),
        out_shape=jax.ShapeDtypeStruct((s, w), BF16),
        scratch_shapes=[pltpu.VMEM((nh, nb, bs), F32),
                        pltpu.VMEM((nh, MOBA_HD, bs), BF16),
                        pltpu.VMEM((nh, 1, bs), F32),
                        pltpu.VMEM((nh, 1, bs), F32)]
                       + [pltpu.VMEM((MOBA_HD, bs), F32) for _ in range(nh)]
                       + [pltpu.VMEM((bs, bs), F32) for _ in range(nh)]
                       + [pltpu.VMEM((bs, bs), BF16) for _ in range(nh)],
        compiler_params=_params("parallel", "arbitrary"),
        name="moba_attn",
    )(qt, kn, vt, kmean)


def _mix_out_kernel(hm_ref, ha_ref, w_ref, x_ref, o_ref):
    half = hm_ref.shape[1]
    o_ref[...] = (x_ref[...] + _dot(hm_ref[...], w_ref[0:half, :])
                  + _dot(ha_ref[...], w_ref[half:2 * half, :]))


def _mix_out(hm, ha, w, x, tm):
    s, d = x.shape
    half = hm.shape[1]
    return pl.pallas_call(
        _mix_out_kernel,
        grid=(s // tm,),
        in_specs=[pl.BlockSpec((tm, half), lambda i: (i, 0)),
                  pl.BlockSpec((tm, half), lambda i: (i, 0)),
                  pl.BlockSpec((2 * half, d), lambda i: (0, 0)),
                  pl.BlockSpec((tm, d), lambda i: (i, 0))],
        out_specs=pl.BlockSpec((tm, d), lambda i: (i, 0)),
        out_shape=jax.ShapeDtypeStruct((s, d), F32),
        compiler_params=_params("parallel"),
        name="mix_out",
    )(hm, ha, w, x)


def _mem_kv_kernel(mem_ref, g_ref, w_ref, kg_ref, o_ref, mn_ref):
    j = pl.program_id(0)

    @pl.when(j == 0)
    def _():
        mn_ref[...] = _rms(mem_ref[...], g_ref[...]).astype(BF16)

    y = _dot(mn_ref[...], w_ref[...])
    o_ref[...] = jnp.where(j < XA_HEADS, _rms(y, kg_ref[...]), y).astype(o_ref.dtype)


def _mem_kv(mem, g, wkv, k_g):
    m, d = mem.shape
    hd = d // XA_HEADS
    return pl.pallas_call(
        _mem_kv_kernel,
        grid=(2 * XA_HEADS,),
        in_specs=[pl.BlockSpec((m, d), lambda j: (0, 0)),
                  pl.BlockSpec((1, d), lambda j: (0, 0)),
                  pl.BlockSpec((d, hd), lambda j: (0, j)),
                  pl.BlockSpec((1, hd), lambda j: (0, 0))],
        out_specs=pl.BlockSpec((m, hd), lambda j: (0, j)),
        out_shape=jax.ShapeDtypeStruct((m, 2 * d), BF16),
        scratch_shapes=[pltpu.VMEM((m, d), BF16)],
        compiler_params=_params("arbitrary"),
        name="mem_kv",
    )(mem, g, wkv, k_g)


def _xattn_kernel(h_ref, g_ref, wq_ref, k_ref, v_ref, qg_ref, wo_ref, o_ref, hn_ref):
    j = pl.program_id(1)

    @pl.when(j == 0)
    def _():
        h = h_ref[...]
        hn_ref[...] = _rms(h, g_ref[...]).astype(BF16)
        o_ref[...] = h

    hd = wq_ref.shape[1]
    q = _rms(_dot(hn_ref[...], wq_ref[...]), qg_ref[...]).astype(BF16)
    sc = _dot_nt(q, k_ref[...]) * (hd ** -0.5)
    p = jnp.exp(sc - jnp.max(sc, axis=-1, keepdims=True))
    p = p * (1.0 / jnp.sum(p, axis=-1, keepdims=True))
    o = _dot(p.astype(BF16), v_ref[...]).astype(BF16)
    o_ref[...] += _dot(o, wo_ref[...])


def _xattn(h, g, wq, kv, q_g, wo, tm):
    s, d = h.shape
    m = kv.shape[0]
    hd = d // XA_HEADS
    return pl.pallas_call(
        _xattn_kernel,
        grid=(s // tm, XA_HEADS),
        in_specs=[pl.BlockSpec((tm, d), lambda i, j: (i, 0)),
                  pl.BlockSpec((1, d), lambda i, j: (0, 0)),
                  pl.BlockSpec((d, hd), lambda i, j: (0, j)),
                  pl.BlockSpec((m, hd), lambda i, j: (0, j)),
                  pl.BlockSpec((m, hd), lambda i, j: (0, XA_HEADS + j)),
                  pl.BlockSpec((1, hd), lambda i, j: (0, 0)),
                  pl.BlockSpec((hd, d), lambda i, j: (j, 0))],
        out_specs=pl.BlockSpec((tm, d), lambda i, j: (i, 0)),
        out_shape=jax.ShapeDtypeStruct((s, d), F32),
        scratch_shapes=[pltpu.VMEM((tm, d), BF16)],
        compiler_params=_params("parallel", "arbitrary"),
        name="xattn",
    )(h, g, wq, kv, kv, q_g, wo)


ROUTE_EID = 0
ROUTE_RANK = 2
ROUTE_WT = 4


def _router_kernel(h_ref, g_ref, w_ref, b_ref, out_ref, cnt_ref, carry):
    tm = h_ref.shape[0]

    @pl.when(pl.program_id(0) == 0)
    def _():
        carry[...] = jnp.zeros_like(carry)

    hn = _rms(h_ref[...], g_ref[...])
    logits = jnp.dot(hn, w_ref[...], precision=lax.Precision.HIGHEST,
                     preferred_element_type=F32) + b_ref[...]
    lane = lax.broadcasted_iota(I32, (tm, LANES), 1)

    def first_lane(mask):
        return jnp.min(jnp.where(mask, lane.astype(F32), float(LANES)), axis=-1,
                       keepdims=True).astype(I32)

    is_g = lane < MOE_GROUPS
    gmax = jnp.max(jnp.where(is_g, logits, -jnp.inf), axis=-1, keepdims=True)
    gsum = jnp.sum(jnp.where(is_g, jnp.exp(logits - gmax), 0.0), axis=-1, keepdims=True)
    gsel = first_lane(is_g & (logits == gmax))
    pg = 1.0 / gsum

    grp_lo = MOE_GROUPS + MOE_PER_GROUP * gsel
    in_grp = (lane >= grp_lo) & (lane < grp_lo + MOE_PER_GROUP)
    emax = jnp.max(jnp.where(in_grp, logits, -jnp.inf), axis=-1, keepdims=True)
    eexp = jnp.where(in_grp, jnp.exp(logits - emax), 0.0)
    eprob = eexp / jnp.sum(eexp, axis=-1, keepdims=True)
    p1 = jnp.max(jnp.where(in_grp, eprob, -1.0), axis=-1, keepdims=True)
    l1 = first_lane(in_grp & (eprob == p1))
    rest = in_grp & (lane != l1)
    p2 = jnp.max(jnp.where(rest, eprob, -1.0), axis=-1, keepdims=True)
    l2 = first_lane(rest & (eprob == p2))
    psum = p1 + p2
    w1 = pg * p1 / psum
    w2 = pg * p2 / psum

    oh1 = lane == l1
    oh2 = lane == l2
    onehot = jnp.where(oh1 | oh2, 1.0, 0.0)
    r = lax.broadcasted_iota(I32, (tm, tm), 0)
    c = lax.broadcasted_iota(I32, (tm, tm), 1)
    strict = jnp.where(c < r, 1.0, 0.0).astype(BF16)
    before = _dot(strict, onehot.astype(BF16)) + carry[0:1, :]
    rank1 = jnp.sum(jnp.where(oh1, before, 0.0), axis=-1, keepdims=True)
    rank2 = jnp.sum(jnp.where(oh2, before, 0.0), axis=-1, keepdims=True)
    total = carry[0:1, :] + jnp.sum(onehot, axis=0, keepdims=True)
    carry[...] = jnp.broadcast_to(total, carry.shape)
    cnt_ref[...] = jnp.broadcast_to(total, cnt_ref.shape)

    out = jnp.where(lane == ROUTE_EID, (l1 - MOE_GROUPS).astype(F32), 0.0)
    out = jnp.where(lane == ROUTE_EID + 1, (l2 - MOE_GROUPS).astype(F32), out)
    out = jnp.where(lane == ROUTE_RANK, rank1, out)
    out = jnp.where(lane == ROUTE_RANK + 1, rank2, out)
    out = jnp.where(lane == ROUTE_WT, w1, out)
    out = jnp.where(lane == ROUTE_WT + 1, w2, out)
    out_ref[...] = out


def _router(h, g, w, b, tm):
    s, d = h.shape
    return pl.pallas_call(
        _router_kernel,
        grid=(s // tm,),
        in_specs=[pl.BlockSpec((tm, d), lambda i: (i, 0)),
                  pl.BlockSpec((1, d), lambda i: (0, 0)),
                  pl.BlockSpec((d, LANES), lambda i: (0, 0)),
                  pl.BlockSpec((1, LANES), lambda i: (0, 0))],
        out_specs=[pl.BlockSpec((tm, LANES), lambda i: (i, 0)),
                   pl.BlockSpec((SUBLANES, LANES), lambda i: (0, 0))],
        out_shape=[jax.ShapeDtypeStruct((s, LANES), F32),
                   jax.ShapeDtypeStruct((SUBLANES, LANES), F32)],
        scratch_shapes=[pltpu.VMEM((SUBLANES, LANES), F32)],
        compiler_params=_params("arbitrary"),
        name="router",
    )(h, g, w, b)


def _dispatch_kernel(d0_ref, d1_ref, h_ref, g_ref, xg_in_ref, xg_ref, buf, sem):
    del xg_in_ref
    tm = h_ref.shape[0]
    base = pl.program_id(0) * tm
    buf[...] = _rms(h_ref[...], g_ref[...])

    def row_copy(t, d):
        return pltpu.make_async_copy(buf.at[pl.ds(t, 1), :], xg_ref.at[pl.ds(d, 1), :], sem)

    def issue(t, carry):
        row_copy(t, d0_ref[base + t]).start()
        row_copy(t, d1_ref[base + t]).start()
        return carry

    def drain(t, carry):
        row_copy(0, 0).wait()
        row_copy(0, 0).wait()
        return carry

    lax.fori_loop(0, tm, issue, 0)
    lax.fori_loop(0, tm, drain, 0)


def _dispatch(dest, h, g, xg_init, tm):
    s, d = h.shape
    grid_spec = pltpu.PrefetchScalarGridSpec(
        num_scalar_prefetch=2,
        grid=(s // tm,),
        in_specs=[pl.BlockSpec((tm, d), lambda i, d0, d1: (i, 0)),
                  pl.BlockSpec((1, d), lambda i, d0, d1: (0, 0)),
                  pl.BlockSpec(memory_space=pl.ANY)],
        out_specs=pl.BlockSpec(memory_space=pl.ANY),
        scratch_shapes=[pltpu.VMEM((tm, d), F32), pltpu.SemaphoreType.DMA(())],
    )
    return pl.pallas_call(
        _dispatch_kernel,
        grid_spec=grid_spec,
        out_shape=jax.ShapeDtypeStruct(xg_init.shape, F32),
        input_output_aliases={4: 0},
        compiler_params=_params("arbitrary"),
        name="moe_dispatch",
    )(dest[0], dest[1], h, g, xg_init)


def _moe_ffn_kernel(be_ref, nu_ref, nx_ref, x_ref, wg_hbm, wu_hbm, wd_hbm, y_ref,
                    wgf, wuf, wdf, wgb, wub, wdb, slot_ref, sem):
    b = pl.program_id(0)
    e = be_ref[b]
    used = b < nu_ref[0]

    def weight_copies(expert, slot):
        return (pltpu.make_async_copy(wg_hbm.at[expert], wgf.at[slot], sem.at[slot, 0]),
                pltpu.make_async_copy(wu_hbm.at[expert], wuf.at[slot], sem.at[slot, 1]),
                pltpu.make_async_copy(wd_hbm.at[expert], wdf.at[slot], sem.at[slot, 2]))

    @pl.when(b == 0)
    def _():
        slot_ref[0] = 0
        for c in weight_copies(e, 0):
            c.start()

    @pl.when(used & ((b == 0) | (e != be_ref[jnp.maximum(b - 1, 0)])))
    def _():
        slot = slot_ref[0]
        for c in weight_copies(e, slot):
            c.wait()
        nxt = nx_ref[e]

        @pl.when(nxt >= 0)
        def _():
            for c in weight_copies(nxt, 1 - slot):
                c.start()

        wgb[...] = wgf[slot].astype(BF16)
        wub[...] = wuf[slot].astype(BF16)
        wdb[...] = wdf[slot].astype(BF16)
        slot_ref[0] = 1 - slot

    @pl.when(used)
    def _():
        xb = x_ref[...].astype(BF16)
        gate = _dot(xb, wgb[...])
        up = _dot(xb, wub[...])
        act = (gate * jax.nn.sigmoid(gate) * up).astype(BF16)
        y_ref[...] = _dot(act, wdb[...])

    @pl.when(jnp.logical_not(used))
    def _():
        y_ref[...] = jnp.zeros_like(y_ref)


def _moe_ffn(blk_e, n_used, next_e, xg, w_gate, w_up, w_down):
    n_rows, d = xg.shape
    ff = w_gate.shape[2]
    n_blk = n_rows // MOE_ROWS
    grid_spec = pltpu.PrefetchScalarGridSpec(
        num_scalar_prefetch=3,
        grid=(n_blk,),
        in_specs=[pl.BlockSpec((MOE_ROWS, d), lambda b, be, nu, nx: (b, 0)),
                  pl.BlockSpec(memory_space=pl.ANY),
                  pl.BlockSpec(memory_space=pl.ANY),
                  pl.BlockSpec(memory_space=pl.ANY)],
        out_specs=pl.BlockSpec((MOE_ROWS, d), lambda b, be, nu, nx: (b, 0)),
        scratch_shapes=[pltpu.VMEM((2, d, ff), F32), pltpu.VMEM((2, d, ff), F32),
                        pltpu.VMEM((2, ff, d), F32),
                        pltpu.VMEM((d, ff), BF16), pltpu.VMEM((d, ff), BF16),
                        pltpu.VMEM((ff, d), BF16),
                        pltpu.SMEM((1,), I32), pltpu.SemaphoreType.DMA((2, 3))],
    )
    return pl.pallas_call(
        _moe_ffn_kernel,
        grid_spec=grid_spec,
        out_shape=jax.ShapeDtypeStruct((n_rows, d), F32),
        compiler_params=_params("arbitrary"),
        name="moe_ffn",
    )(blk_e, n_used, next_e, xg, w_gate, w_up, w_down)


def _combine_kernel(d0_ref, d1_ref, h_ref, r_ref, y_ref, o_ref, buf, sem):
    tm = h_ref.shape[0]
    base = pl.program_id(0) * tm

    def row_copy(k, t, d):
        return pltpu.make_async_copy(y_ref.at[pl.ds(d, 1), :], buf.at[k, pl.ds(t, 1), :], sem)

    def issue(t, carry):
        row_copy(0, t, d0_ref[base + t]).start()
        row_copy(1, t, d1_ref[base + t]).start()
        return carry

    def drain(t, carry):
        row_copy(0, 0, 0).wait()
        row_copy(1, 0, 0).wait()
        return carry

    lax.fori_loop(0, tm, issue, 0)
    lax.fori_loop(0, tm, drain, 0)
    r = r_ref[...]
    o_ref[...] = (h_ref[...] + r[:, ROUTE_WT:ROUTE_WT + 1] * buf[0]
                  + r[:, ROUTE_WT + 1:ROUTE_WT + 2] * buf[1])


def _combine(dest, h, routed, y, tm):
    s, d = h.shape
    grid_spec = pltpu.PrefetchScalarGridSpec(
        num_scalar_prefetch=2,
        grid=(s // tm,),
        in_specs=[pl.BlockSpec((tm, d), lambda i, d0, d1: (i, 0)),
                  pl.BlockSpec((tm, LANES), lambda i, d0, d1: (i, 0)),
                  pl.BlockSpec(memory_space=pl.ANY)],
        out_specs=pl.BlockSpec((tm, d), lambda i, d0, d1: (i, 0)),
        scratch_shapes=[pltpu.VMEM((2, tm, d), F32), pltpu.SemaphoreType.DMA(())],
    )
    return pl.pallas_call(
        _combine_kernel,
        grid_spec=grid_spec,
        out_shape=jax.ShapeDtypeStruct((s, d), F32),
        compiler_params=_params("arbitrary"),
        name="moe_combine",
    )(dest[0], dest[1], h, routed, y)


def _layer(h, mem, pos, norm_mix_g, w_in, gate_b, conv_w, conv_b, out_g, moba_q_g, moba_k_g, w_out,
           norm_cross_g, norm_mem_g, xa_wq, xa_wkv, xa_q_g, xa_k_g, xa_wo, norm_ffn_g,
           router_group_w, router_group_b, router_expert_w, router_expert_b,
           exp_w_gate, exp_w_up, exp_w_down):
    s, d = h.shape
    row = lambda v: v.reshape(1, -1)

    n_gate = 2 * MLSTM_HEADS
    gate_lo = 2 * MLSTM_HEADS * MLSTM_QK + 2 * MLSTM_HEADS * MLSTM_V
    proj_tn = 512
    w_a = w_in[:, :gate_lo].astype(BF16)
    w_b = w_in[:, gate_lo + n_gate:].astype(BF16)
    w_g = jnp.pad(w_in[:, gate_lo:gate_lo + n_gate], ((0, 0), (0, proj_tn - n_gate))).astype(BF16)
    proj = _proj_in(h, row(norm_mix_g), w_a, w_b, w_g, tm=512)

    gate_b_row = jnp.pad(gate_b, (0, LANES - n_gate)).reshape(1, LANES)
    hm = _mlstm(proj, w_a.shape[1] + w_b.shape[1], gate_b_row, conv_w, row(conv_b), row(out_g))
    qn, kn, vb, kmean = _moba_prep(proj, gate_lo, pos, row(moba_q_g), row(moba_k_g))
    ha = _moba_attn(qn, kn, vb, kmean)
    h = _mix_out(hm, ha, w_out.astype(BF16), h, tm=512)

    kv = _mem_kv(mem, row(norm_mem_g), xa_wkv.astype(BF16), row(xa_k_g))
    h = _xattn(h, row(norm_cross_g), xa_wq.astype(BF16), kv, row(xa_q_g), xa_wo.astype(BF16), tm=512)

    w_route = jnp.concatenate(
        [router_group_w, router_expert_w,
         jnp.zeros((d, LANES - MOE_GROUPS - MOE_EXPERTS), router_group_w.dtype)], axis=1)
    b_route = jnp.pad(jnp.concatenate([router_group_b, router_expert_b]),
                      (0, LANES - MOE_GROUPS - MOE_EXPERTS)).reshape(1, LANES)
    routed, counts = _router(h, row(norm_ffn_g), w_route, b_route, tm=512)

    cnt = counts[0, MOE_GROUPS:MOE_GROUPS + MOE_EXPERTS].astype(I32)
    padded = (cnt + MOE_ROWS - 1) // MOE_ROWS * MOE_ROWS
    pend = jnp.cumsum(padded)
    seg_start = (pend - padded).astype(F32)[None, :]
    expert_ids = jnp.arange(MOE_EXPERTS, dtype=F32)[None, :]

    def dest_of(k):
        mine = routed[:, ROUTE_EID + k:ROUTE_EID + k + 1] == expert_ids
        return (jnp.sum(jnp.where(mine, seg_start, 0.0), axis=1) + routed[:, ROUTE_RANK + k]).astype(I32)

    dest = (dest_of(0), dest_of(1))
    n_blk = -(-2 * s // MOE_ROWS) + MOE_EXPERTS
    blk_start = jnp.arange(n_blk, dtype=I32) * MOE_ROWS
    blk_e = jnp.minimum(jnp.sum((pend[None, :] <= blk_start[:, None]).astype(I32), axis=1),
                        MOE_EXPERTS - 1)
    n_used = (pend[-1:] // MOE_ROWS).astype(I32)
    ids = jnp.arange(MOE_EXPERTS, dtype=I32)
    later_used = (ids[None, :] > ids[:, None]) & (cnt[None, :] > 0)
    next_e = jnp.min(jnp.where(later_used, ids[None, :], MOE_EXPERTS), axis=1)
    next_e = jnp.where(next_e < MOE_EXPERTS, next_e, -1).astype(I32)

    xg = _dispatch(dest, h, row(norm_ffn_g), jnp.zeros((n_blk * MOE_ROWS, d), F32), tm=256)
    y = _moe_ffn(blk_e, n_used, next_e, xg, exp_w_gate, exp_w_up, exp_w_down)
    return _combine(dest, h, routed, y, tm=256)


def kernel(x, mem, positions, norm_mix_g, w_in, mlstm_gate_b, mlstm_conv_w, mlstm_conv_b, mlstm_out_g, moba_q_g, moba_k_g, w_out, norm_cross_g, norm_mem_g, xa_wq, xa_wkv, xa_q_g, xa_k_g, xa_wo, norm_ffn_g, router_group_w, router_group_b, router_expert_w, router_expert_b, exp_w_gate, exp_w_up, exp_w_down):
    bsz, s, _ = x.shape
    assert bsz == 1, "single-sequence prefill only"
    per_layer = (norm_mix_g, w_in, mlstm_gate_b, mlstm_conv_w, mlstm_conv_b, mlstm_out_g, moba_q_g,
                 moba_k_g, w_out, norm_cross_g, norm_mem_g, xa_wq, xa_wkv, xa_q_g, xa_k_g, xa_wo,
                 norm_ffn_g, router_group_w, router_group_b, router_expert_w, router_expert_b,
                 exp_w_gate, exp_w_up, exp_w_down)
    h = x[0]
    pos = positions.reshape(s, 1)
    for l in range(norm_mix_g.shape[0]):
        h = _layer(h, mem[0], pos, *(p[l] for p in per_layer))
    return h[None]
```

```python
import functools
import math

import jax
import jax.numpy as jnp
from jax import lax
from jax.experimental import pallas as pl
from jax.experimental.pallas import tpu as pltpu

F32 = jnp.float32
BF16 = jnp.bfloat16
I32 = jnp.int32

EPS = 1e-6
LANES = 128
SUBLANES = 8
VMEM_LIMIT = 56 * 1024 * 1024

MLSTM_HEADS = 4
MLSTM_QK = 128
MLSTM_V = 256
MLSTM_CHUNK = 128
MLSTM_CONV = 4
MOBA_HEADS = 8
MOBA_HD = 128
MOBA_BLOCK = 256
MOBA_TOPK = 3
ROPE_DIM = 32
ROPE_THETA = 500000.0
XA_HEADS = 4
MOE_GROUPS = 8
MOE_PER_GROUP = 8
MOE_EXPERTS = MOE_GROUPS * MOE_PER_GROUP
MOE_ROWS = 128

NT_DIMS = (((1,), (1,)), ((), ()))


def _params(*sem):
    return pltpu.CompilerParams(dimension_semantics=sem, vmem_limit_bytes=VMEM_LIMIT)


def _rms(x, g):
    return x * lax.rsqrt(jnp.mean(x * x, axis=-1, keepdims=True) + EPS) * g


def _dot(a, b):
    return jnp.dot(a, b, preferred_element_type=F32)


def _dot_nt(a, b, precision=None):
    return lax.dot_general(a, b, NT_DIMS, precision=precision, preferred_element_type=F32)


def _proj_in_kernel(x_ref, g_ref, wa_ref, wb_ref, wg_ref, o_ref, xn_ref, *, na, nb):
    j = pl.program_id(1)

    @pl.when(j == 0)
    def _():
        xn_ref[...] = _rms(x_ref[...], g_ref[...]).astype(BF16)

    @pl.when(j < na)
    def _():
        o_ref[...] = _dot(xn_ref[...], wa_ref[...])

    @pl.when((j >= na) & (j < na + nb))
    def _():
        o_ref[...] = _dot(xn_ref[...], wb_ref[...])

    @pl.when(j == na + nb)
    def _():
        o_ref[...] = _dot(xn_ref[...], wg_ref[...])


def _proj_in(x, g, w_a, w_b, w_g, tm):
    s, d = x.shape
    tn = w_g.shape[1]
    na, nb = w_a.shape[1] // tn, w_b.shape[1] // tn
    return pl.pallas_call(
        functools.partial(_proj_in_kernel, na=na, nb=nb),
        grid=(s // tm, na + nb + 1),
        in_specs=[pl.BlockSpec((tm, d), lambda i, j: (i, 0)),
                  pl.BlockSpec((1, d), lambda i, j: (0, 0)),
                  pl.BlockSpec((d, tn), lambda i, j: (0, jnp.minimum(j, na - 1))),
                  pl.BlockSpec((d, tn), lambda i, j: (0, jnp.clip(j - na, 0, nb - 1))),
                  pl.BlockSpec((d, tn), lambda i, j: (0, 0))],
        out_specs=pl.BlockSpec((tm, tn), lambda i, j: (i, j)),
        out_shape=jax.ShapeDtypeStruct((s, (na + nb + 1) * tn), F32),
        scratch_shapes=[pltpu.VMEM((tm, d), BF16)],
        compiler_params=_params("parallel", "arbitrary"),
        name="proj_in",
    )(x, g, w_a, w_b, w_g)


def _split3(x):
    hi = x.astype(BF16)
    r = x - hi.astype(F32)
    mid = r.astype(BF16)
    lo = (r - mid.astype(F32)).astype(BF16)
    return hi, mid, lo


def _log_sigmoid(x):
    return jnp.minimum(x, 0.0) - jnp.log(1.0 + jnp.exp(-jnp.abs(x)))


def _mlstm_kernel(q_ref, k_ref, v_ref, og_ref, gt_ref, gb_ref, cw_ref, cb_ref, outg_ref, hm_ref,
                  qext, kext, c_scr, n_scr, m_scr):
    L = MLSTM_CHUNK
    pad = SUBLANES
    qkw = MLSTM_HEADS * MLSTM_QK

    @pl.when(pl.program_id(0) == 0)
    def _():
        qext[0:pad, :] = jnp.zeros((pad, qkw), F32)
        kext[0:pad, :] = jnp.zeros((pad, qkw), F32)
        c_scr[...] = jnp.zeros_like(c_scr)
        n_scr[...] = jnp.zeros_like(n_scr)
        m_scr[...] = jnp.zeros_like(m_scr)

    qext[pad:pad + L, :] = q_ref[...]
    kext[pad:pad + L, :] = k_ref[...]

    def conv_silu(ext, lo):
        acc = jnp.broadcast_to(cb_ref[:, lo:lo + qkw], (L, qkw))
        for j in range(MLSTM_CONV):
            shift = MLSTM_CONV - 1 - j
            acc = acc + cw_ref[j:j + 1, lo:lo + qkw] * ext[pad - shift:pad - shift + L, :]
        return acc * jax.nn.sigmoid(acc)

    qc = conv_silu(qext, 0)
    kc = conv_silu(kext, qkw)
    qext[0:pad, :] = qext[L:L + pad, :]
    kext[0:pad, :] = kext[L:L + pad, :]

    gts = gt_ref[...] + gb_ref[...]
    gtr = gts.T
    row = lax.broadcasted_iota(I32, (L, L), 0)
    col = lax.broadcasted_iota(I32, (L, L), 1)
    causal = col <= row
    tri_lo = jnp.where(causal, 1.0, 0.0).astype(BF16)
    tri_up = jnp.where(row <= col, 1.0, 0.0).astype(BF16)
    bc_all = sum(_dot(tri_lo, part) for part in _split3(_log_sigmoid(gts)))
    br_all = sum(_dot(part, tri_up) for part in _split3(_log_sigmoid(gtr)))

    for h in range(MLSTM_HEADS):
        fi = MLSTM_HEADS + h
        m_prev = m_scr[h, 0:1, 0:1]
        b_c = bc_all[:, fi:fi + 1]
        b_r = br_all[fi:fi + 1, :]
        i_c = gts[:, h:h + 1]
        i_r = gtr[h:h + 1, :]
        g = b_c[L - 1:L, :]
        d = jnp.where(causal, (b_c - b_r) + i_r, -jnp.inf)
        inter = b_c + m_prev
        m_t = jnp.maximum(inter, jnp.max(d, axis=-1, keepdims=True))
        w_inter = jnp.exp(inter - m_t)
        qh = qc[:, h * MLSTM_QK:(h + 1) * MLSTM_QK] * (MLSTM_QK ** -0.5)
        kh = kc[:, h * MLSTM_QK:(h + 1) * MLSTM_QK]
        vb = v_ref[:, h * MLSTM_V:(h + 1) * MLSTM_V].astype(BF16)
        qb = qh.astype(BF16)
        p = jnp.exp(d - m_t) * _dot_nt(qb, kh.astype(BF16))
        c_prev = c_scr[h]
        n_prev = n_scr[h, 0:1, :]
        num = w_inter * _dot(qb, c_prev.astype(BF16)) + _dot(p.astype(BF16), vb)
        den = (w_inter * jnp.sum(qh * n_prev, axis=-1, keepdims=True)
               + jnp.sum(p, axis=-1, keepdims=True))
        hh = num * (1.0 / jnp.maximum(jnp.abs(den), jnp.exp(-m_t)))

        a = (g - b_c) + i_c
        m_new = jnp.maximum(g + m_prev, jnp.max(a, axis=0, keepdims=True))
        kw = kh * jnp.exp(a - m_new)
        decay = jnp.exp(g + m_prev - m_new)
        c_scr[h] = decay * c_prev + _dot(kw.T.astype(BF16), vb)
        n_scr[h, 0:1, :] = decay * n_prev + jnp.sum(kw, axis=0, keepdims=True)
        m_scr[h] = jnp.broadcast_to(m_new, (SUBLANES, LANES))

        hn = hh * lax.rsqrt(jnp.mean(hh * hh, axis=-1, keepdims=True) + EPS)
        vs = slice(h * MLSTM_V, (h + 1) * MLSTM_V)
        hn = hn * outg_ref[:, vs] * jax.nn.sigmoid(og_ref[:, vs])
        hm_ref[:, vs] = hn.astype(hm_ref.dtype)


def _mlstm(proj, gate_col, gate_b, conv_w, conv_b, out_g):
    s = proj.shape[0]
    L = MLSTM_CHUNK
    qkw = MLSTM_HEADS * MLSTM_QK
    vw = MLSTM_HEADS * MLSTM_V
    gate_blk = gate_col // LANES
    return pl.pallas_call(
        _mlstm_kernel,
        grid=(s // L,),
        in_specs=[pl.BlockSpec((L, qkw), lambda c: (c, 0)),
                  pl.BlockSpec((L, qkw), lambda c: (c, 1)),
                  pl.BlockSpec((L, vw), lambda c: (c, 1)),
                  pl.BlockSpec((L, vw), lambda c: (c, 2)),
                  pl.BlockSpec((L, LANES), lambda c: (c, gate_blk)),
                  pl.BlockSpec((1, LANES), lambda c: (0, 0)),
                  pl.BlockSpec((MLSTM_CONV, 2 * qkw), lambda c: (0, 0)),
                  pl.BlockSpec((1, 2 * qkw), lambda c: (0, 0)),
                  pl.BlockSpec((1, vw), lambda c: (0, 0))],
        out_specs=pl.BlockSpec((L, vw), lambda c: (c, 0)),
        out_shape=jax.ShapeDtypeStruct((s, vw), BF16),
        scratch_shapes=[pltpu.VMEM((L + SUBLANES, qkw), F32),
                        pltpu.VMEM((L + SUBLANES, qkw), F32),
                        pltpu.VMEM((MLSTM_HEADS, MLSTM_QK, MLSTM_V), F32),
                        pltpu.VMEM((MLSTM_HEADS, SUBLANES, LANES), F32),
                        pltpu.VMEM((MLSTM_HEADS, SUBLANES, LANES), F32)],
        compiler_params=_params("arbitrary"),
        name="mlstm",
    )(proj, proj, proj, proj, proj, gate_b, conv_w, conv_b, out_g)


def _moba_prep_kernel(q_ref, k_ref, v_ref, pos_ref, qg_ref, kg_ref, qt_ref, kn_ref, vt_ref, km_ref):
    rows = q_ref.shape[0]
    half = ROPE_DIM // 2
    lane = lax.broadcasted_iota(I32, (1, MOBA_HD), 1)
    inv_freq = jnp.exp((lane & (half - 1)).astype(F32) * (-(2.0 / ROPE_DIM) * math.log(ROPE_THETA)))
    ang = pos_ref[...].astype(F32) * inv_freq
    cos = jnp.where(lane < ROPE_DIM, jnp.cos(ang), 1.0)
    sin = jnp.sin(ang)
    sin = jnp.where(lane < half, -sin, jnp.where(lane < ROPE_DIM, sin, 0.0))

    def rope(x):
        partner = jnp.where(lane < half, pltpu.roll(x, MOBA_HD - half, 1), pltpu.roll(x, half, 1))
        return x * cos + partner * sin

    for h in range(MOBA_HEADS):
        hs = slice(h * MOBA_HD, (h + 1) * MOBA_HD)
        qt_ref[hs, :] = rope(_rms(q_ref[:, hs], qg_ref[...])).T
        kn = rope(_rms(k_ref[:, hs], kg_ref[...]))
        kn_ref[:, hs] = kn.astype(BF16)
        km_ref[0, :, hs] = jnp.sum(kn, axis=0, keepdims=True) * (1.0 / rows)
        vt_ref[0, hs, :] = v_ref[:, hs].T.astype(BF16)


def _moba_prep(proj, q_col, pos, q_g, k_g):
    s = proj.shape[0]
    w = MOBA_HEADS * MOBA_HD
    bs = MOBA_BLOCK
    nb = s // bs
    first = q_col // w
    return pl.pallas_call(
        _moba_prep_kernel,
        grid=(nb,),
        in_specs=[pl.BlockSpec((bs, w), lambda i: (i, first)),
                  pl.BlockSpec((bs, w), lambda i: (i, first + 1)),
                  pl.BlockSpec((bs, w), lambda i: (i, first + 2)),
                  pl.BlockSpec((bs, 1), lambda i: (i, 0)),
                  pl.BlockSpec((1, MOBA_HD), lambda i: (0, 0)),
                  pl.BlockSpec((1, MOBA_HD), lambda i: (0, 0))],
        out_specs=[pl.BlockSpec((w, bs), lambda i: (0, i)),
                   pl.BlockSpec((bs, w), lambda i: (i, 0)),
                   pl.BlockSpec((1, w, bs), lambda i: (i, 0, 0)),
                   pl.BlockSpec((1, 1, w), lambda i: (i, 0, 0))],
        out_shape=[jax.ShapeDtypeStruct((w, s), F32),
                   jax.ShapeDtypeStruct((s, w), BF16),
                   jax.ShapeDtypeStruct((nb, w, bs), BF16),
                   jax.ShapeDtypeStruct((nb, 1, w), F32)],
        compiler_params=_params("parallel"),
        name="moba_prep",
    )(proj, proj, proj, pos, q_g, k_g)


def _moba_attn_kernel(qt_ref, k_ref, vt_ref, km_ref, o_ref, bias_scr, qb_scr, m_scr, l_scr, *tiles):
    i = pl.program_id(1)
    bs = MOBA_BLOCK
    hd = MOBA_HD
    nh = len(tiles) // 3
    acc_scr = tiles[:nh]
    s_scr = tiles[nh:2 * nh]
    p_scr = tiles[2 * nh:]
    nb = km_ref.shape[0]
    c2 = (hd ** -0.5) * math.log2(math.e)
    blk = lax.broadcasted_iota(I32, (nb, bs), 0)
    blk_f = blk.astype(F32)
    valid = blk < i
    key = lax.broadcasted_iota(I32, (bs, bs), 0)
    qry = lax.broadcasted_iota(I32, (bs, bs), 1)
    own = pl.ds(pl.multiple_of(i * bs, bs), bs)

    for h in range(nh):
        hs = slice(h * hd, (h + 1) * hd)
        qt = qt_ref[hs, :]
        gate = jnp.dot(km_ref[:, 0, hs], qt, precision=lax.Precision.HIGHEST,
                       preferred_element_type=F32)
        cand = valid
        for _ in range(MOBA_TOPK):
            gmax = jnp.max(jnp.where(cand, gate, -jnp.inf), axis=0, keepdims=True)
            hit = cand & (gate == gmax)
            first = jnp.min(jnp.where(hit, blk_f, float(nb)), axis=0, keepdims=True)
            cand = cand & jnp.logical_not(hit & (blk_f == first))
        bias_scr[h] = jnp.where(valid & jnp.logical_not(cand), 0.0, -jnp.inf)

        qb = (qt * c2).astype(BF16)
        qb_scr[h] = qb
        s0 = jnp.where(key <= qry, _dot(k_ref[own, hs], qb), -jnp.inf)
        m0 = jnp.max(s0, axis=0, keepdims=True)
        p0 = jnp.exp2(s0 - m0)
        m_scr[h] = m0
        l_scr[h] = jnp.sum(p0, axis=0, keepdims=True)
        acc_scr[h][...] = _dot(vt_ref[i, hs, :], p0.astype(BF16))

    ck = 32

    def body(j, carry):
        rows = pl.ds(pl.multiple_of(j * bs, bs), bs)
        for h in range(nh):
            s_scr[h][...] = _dot(k_ref[rows, h * hd:(h + 1) * hd], qb_scr[h])
        alphas = []
        for h in range(nh):
            bias = bias_scr[h, pl.ds(j, 1), :]
            cmax = s_scr[h][0:ck, :]
            for c in range(1, bs // ck):
                cmax = jnp.maximum(cmax, s_scr[h][c * ck:(c + 1) * ck, :])
            m = m_scr[h]
            m_new = jnp.maximum(m, jnp.max(cmax, axis=0, keepdims=True) + bias)
            alpha = jnp.exp2(m - m_new)
            shift = m_new - bias
            psum = jnp.zeros((ck, bs), F32)
            for c in range(bs // ck):
                p = jnp.exp2(s_scr[h][c * ck:(c + 1) * ck, :] - shift)
                psum = psum + p
                p_scr[h][c * ck:(c + 1) * ck, :] = p.astype(BF16)
            m_scr[h] = m_new
            l_scr[h] = alpha * l_scr[h] + jnp.sum(psum, axis=0, keepdims=True)
            alphas.append(alpha)
        for h in range(nh):
            acc_scr[h][...] = (alphas[h] * acc_scr[h][...]
                               + _dot(vt_ref[j, h * hd:(h + 1) * hd, :], p_scr[h][...]))
        return carry

    lax.fori_loop(0, i, body, 0)
    for h in range(nh):
        out_t = acc_scr[h][...] * (1.0 / l_scr[h])
        o_ref[:, h * hd:(h + 1) * hd] = out_t.T.astype(o_ref.dtype)


MOBA_HEADS_PER_STEP = 8


def _moba_attn(qt, kn, vt, kmean):
    w, s = qt.shape
    bs = MOBA_BLOCK
    nb = s // bs
    nh = MOBA_HEADS_PER_STEP
    gw = nh * MOBA_HD
    resident = pl.Buffered(1)
    return pl.pallas_call(
        _moba_attn_kernel,
        grid=(MOBA_HEADS // nh, nb),
        in_specs=[pl.BlockSpec((gw, bs), lambda g, i: (g, i)),
                  pl.BlockSpec((s, gw), lambda g, i: (0, g), pipeline_mode=resident),
                  pl.BlockSpec((nb, gw, bs), lambda g, i: (0, g, 0), pipeline_mode=resident),
                  pl.BlockSpec((nb, 1, gw), lambda g, i: (0, 0, g))],
        out_specs=pl.BlockSpec((bs, gw), lambda g, i: (i, g)),
        out_shape=jax.ShapeDtypeStruct((s, w), BF16),
        scratch_shapes=[pltpu.VMEM((nh, nb, bs), F32),
                        pltpu.VMEM((nh, MOBA_HD, bs), BF16),
                        pltpu.VMEM((nh, 1, bs), F32),
                        pltpu.VMEM((nh, 1, bs), F32)]
                       + [pltpu.VMEM((MOBA_HD, bs), F32) for _ in range(nh)]
                       + [pltpu.VMEM((bs, bs), F32) for _ in range(nh)]
                       + [pltpu.VMEM((bs, bs), BF16) for _ in range(nh)],
        compiler_params=_params("parallel", "arbitrary"),
        name="moba_attn",
    )(qt, kn, vt, kmean)


def _mix_out_kernel(hm_ref, ha_ref, w_ref, x_ref, o_ref):
    half = hm_ref.shape[1]
    o_ref[...] = (x_ref[...] + _dot(hm_ref[...], w_ref[0:half, :])
                  + _dot(ha_ref[...], w_ref[half:2 * half, :]))


def _mix_out(hm, ha, w, x, tm):
    s, d = x.shape
    half = hm.shape[1]
    return pl.pallas_call(
        _mix_out_kernel,
        grid=(s // tm,),
        in_specs=[pl.BlockSpec((tm, half), lambda i: (i, 0)),
                  pl.BlockSpec((tm, half), lambda i: (i, 0)),
                  pl.BlockSpec((2 * half, d), lambda i: (0, 0)),
                  pl.BlockSpec((tm, d), lambda i: (i, 0))],
        out_specs=pl.BlockSpec((tm, d), lambda i: (i, 0)),
        out_shape=jax.ShapeDtypeStruct((s, d), F32),
        compiler_params=_params("parallel"),
        name="mix_out",
    )(hm, ha, w, x)


def _mem_kv_kernel(mem_ref, g_ref, w_ref, kg_ref, o_ref, mn_ref):
    j = pl.program_id(0)

    @pl.when(j == 0)
    def _():
        mn_ref[...] = _rms(mem_ref[...], g_ref[...]).astype(BF16)

    y = _dot(mn_ref[...], w_ref[...])
    o_ref[...] = jnp.where(j < XA_HEADS, _rms(y, kg_ref[...]), y).astype(o_ref.dtype)


def _mem_kv(mem, g, wkv, k_g):
    m, d = mem.shape
    hd = d // XA_HEADS
    return pl.pallas_call(
        _mem_kv_kernel,
        grid=(2 * XA_HEADS,),
        in_specs=[pl.BlockSpec((m, d), lambda j: (0, 0)),
                  pl.BlockSpec((1, d), lambda j: (0, 0)),
                  pl.BlockSpec((d, hd), lambda j: (0, j)),
                  pl.BlockSpec((1, hd), lambda j: (0, 0))],
        out_specs=pl.BlockSpec((m, hd), lambda j: (0, j)),
        out_shape=jax.ShapeDtypeStruct((m, 2 * d), BF16),
        scratch_shapes=[pltpu.VMEM((m, d), BF16)],
        compiler_params=_params("arbitrary"),
        name="mem_kv",
    )(mem, g, wkv, k_g)


def _xattn_kernel(h_ref, g_ref, wq_ref, k_ref, v_ref, qg_ref, wo_ref, o_ref, hn_ref):
    j = pl.program_id(1)

    @pl.when(j == 0)
    def _():
        h = h_ref[...]
        hn_ref[...] = _rms(h, g_ref[...]).astype(BF16)
        o_ref[...] = h

    hd = wq_ref.shape[1]
    q = _rms(_dot(hn_ref[...], wq_ref[...]), qg_ref[...]).astype(BF16)
    sc = _dot_nt(q, k_ref[...]) * (hd ** -0.5)
    p = jnp.exp(sc - jnp.max(sc, axis=-1, keepdims=True))
    p = p * (1.0 / jnp.sum(p, axis=-1, keepdims=True))
    o = _dot(p.astype(BF16), v_ref[...]).astype(BF16)
    o_ref[...] += _dot(o, wo_ref[...])


def _xattn(h, g, wq, kv, q_g, wo, tm):
    s, d = h.shape
    m = kv.shape[0]
    hd = d // XA_HEADS
    return pl.pallas_call(
        _xattn_kernel,
        grid=(s // tm, XA_HEADS),
        in_specs=[pl.BlockSpec((tm, d), lambda i, j: (i, 0)),
                  pl.BlockSpec((1, d), lambda i, j: (0, 0)),
                  pl.BlockSpec((d, hd), lambda i, j: (0, j)),
                  pl.BlockSpec((m, hd), lambda i, j: (0, j)),
                  pl.BlockSpec((m, hd), lambda i, j: (0, XA_HEADS + j)),
                  pl.BlockSpec((1, hd), lambda i, j: (0, 0)),
                  pl.BlockSpec((hd, d), lambda i, j: (j, 0))],
        out_specs=pl.BlockSpec((tm, d), lambda i, j: (i, 0)),
        out_shape=jax.ShapeDtypeStruct((s, d), F32),
        scratch_shapes=[pltpu.VMEM((tm, d), BF16)],
        compiler_params=_params("parallel", "arbitrary"),
        name="xattn",
    )(h, g, wq, kv, kv, q_g, wo)


ROUTE_EID = 0
ROUTE_RANK = 2
ROUTE_WT = 4


def _router_kernel(h_ref, g_ref, w_ref, b_ref, out_ref, cnt_ref, carry):
    tm = h_ref.shape[0]

    @pl.when(pl.program_id(0) == 0)
    def _():
        carry[...] = jnp.zeros_like(carry)

    hn = _rms(h_ref[...], g_ref[...])
    hi = hn.astype(BF16)
    lo = (hn - hi.astype(F32)).astype(BF16)
    logits = (_dot(hi, w_ref[0]) + _dot(hi, w_ref[1]) + _dot(lo, w_ref[0])
              + b_ref[...])
    lane = lax.broadcasted_iota(I32, (tm, LANES), 1)

    def first_lane(mask):
        return jnp.min(jnp.where(mask, lane.astype(F32), float(LANES)), axis=-1,
                       keepdims=True).astype(I32)

    is_g = lane < MOE_GROUPS
    gmax = jnp.max(jnp.where(is_g, logits, -jnp.inf), axis=-1, keepdims=True)
    gsum = jnp.sum(jnp.where(is_g, jnp.exp(logits - gmax), 0.0), axis=-1, keepdims=True)
    gsel = first_lane(is_g & (logits == gmax))
    pg = 1.0 / gsum

    grp_lo = MOE_GROUPS + MOE_PER_GROUP * gsel
    in_grp = (lane >= grp_lo) & (lane < grp_lo + MOE_PER_GROUP)
    emax = jnp.max(jnp.where(in_grp, logits, -jnp.inf), axis=-1, keepdims=True)
    eexp = jnp.where(in_grp, jnp.exp(logits - emax), 0.0)
    eprob = eexp / jnp.sum(eexp, axis=-1, keepdims=True)
    p1 = jnp.max(jnp.where(in_grp, eprob, -1.0), axis=-1, keepdims=True)
    l1 = first_lane(in_grp & (eprob == p1))
    rest = in_grp & (lane != l1)
    p2 = jnp.max(jnp.where(rest, eprob, -1.0), axis=-1, keepdims=True)
    l2 = first_lane(rest & (eprob == p2))
    psum = p1 + p2
    w1 = pg * p1 / psum
    w2 = pg * p2 / psum

    oh1 = lane == l1
    oh2 = lane == l2
    onehot = jnp.where(oh1 | oh2, 1.0, 0.0)
    r = lax.broadcasted_iota(I32, (tm, tm), 0)
    c = lax.broadcasted_iota(I32, (tm, tm), 1)
    strict = jnp.where(c < r, 1.0, 0.0).astype(BF16)
    before = _dot(strict, onehot.astype(BF16)) + carry[0:1, :]
    rank1 = jnp.sum(jnp.where(oh1, before, 0.0), axis=-1, keepdims=True)
    rank2 = jnp.sum(jnp.where(oh2, before, 0.0), axis=-1, keepdims=True)
    total = carry[0:1, :] + jnp.sum(onehot, axis=0, keepdims=True)
    carry[...] = jnp.broadcast_to(total, carry.shape)
    cnt_ref[...] = jnp.broadcast_to(total, cnt_ref.shape)

    out = jnp.where(lane == ROUTE_EID, (l1 - MOE_GROUPS).astype(F32), 0.0)
    out = jnp.where(lane == ROUTE_EID + 1, (l2 - MOE_GROUPS).astype(F32), out)
    out = jnp.where(lane == ROUTE_RANK, rank1, out)
    out = jnp.where(lane == ROUTE_RANK + 1, rank2, out)
    out = jnp.where(lane == ROUTE_WT, w1, out)
    out = jnp.where(lane == ROUTE_WT + 1, w2, out)
    out_ref[...] = out


def _router(h, g, w, b, tm):
    s, d = h.shape
    return pl.pallas_call(
        _router_kernel,
        grid=(s // tm,),
        in_specs=[pl.BlockSpec((tm, d), lambda i: (i, 0)),
                  pl.BlockSpec((1, d), lambda i: (0, 0)),
                  pl.BlockSpec((2, d, LANES), lambda i: (0, 0, 0)),
                  pl.BlockSpec((1, LANES), lambda i: (0, 0))],
        out_specs=[pl.BlockSpec((tm, LANES), lambda i: (i, 0)),
                   pl.BlockSpec((SUBLANES, LANES), lambda i: (0, 0))],
        out_shape=[jax.ShapeDtypeStruct((s, LANES), F32),
                   jax.ShapeDtypeStruct((SUBLANES, LANES), F32)],
        scratch_shapes=[pltpu.VMEM((SUBLANES, LANES), F32)],
        compiler_params=_params("arbitrary"),
        name="router",
    )(h, g, w, b)


def _dispatch_kernel(d0_ref, d1_ref, zb_ref, h_ref, g_ref, xg_ref, buf, zero, sem, zsem):
    tm = h_ref.shape[0]
    base = pl.program_id(0) * tm

    @pl.when(pl.program_id(0) == 0)
    def _():
        zero[...] = jnp.zeros_like(zero)
        n_blk = xg_ref.shape[0] // MOE_ROWS

        def zero_copy(b):
            rows = pl.ds(pl.multiple_of(b * MOE_ROWS, MOE_ROWS), MOE_ROWS)
            return pltpu.make_async_copy(zero, xg_ref.at[rows, :], zsem)

        def start(b, carry):
            @pl.when(zb_ref[b] != 0)
            def _():
                zero_copy(b).start()
            return carry

        def finish(b, carry):
            @pl.when(zb_ref[b] != 0)
            def _():
                zero_copy(0).wait()
            return carry

        lax.fori_loop(0, n_blk, start, 0)
        lax.fori_loop(0, n_blk, finish, 0)

    buf[...] = _rms(h_ref[...], g_ref[...])

    def row_copy(t, d):
        return pltpu.make_async_copy(buf.at[pl.ds(t, 1), :], xg_ref.at[pl.ds(d, 1), :], sem)

    def issue(t, carry):
        row_copy(t, d0_ref[base + t]).start()
        row_copy(t, d1_ref[base + t]).start()
        return carry

    def drain(t, carry):
        row_copy(0, 0).wait()
        row_copy(0, 0).wait()
        return carry

    lax.fori_loop(0, tm, issue, 0)
    lax.fori_loop(0, tm, drain, 0)


def _dispatch(dest, zero_blk, h, g, tm):
    s, d = h.shape
    n_rows = zero_blk.shape[0] * MOE_ROWS
    grid_spec = pltpu.PrefetchScalarGridSpec(
        num_scalar_prefetch=3,
        grid=(s // tm,),
        in_specs=[pl.BlockSpec((tm, d), lambda i, d0, d1, zb: (i, 0)),
                  pl.BlockSpec((1, d), lambda i, d0, d1, zb: (0, 0))],
        out_specs=pl.BlockSpec(memory_space=pl.ANY),
        scratch_shapes=[pltpu.VMEM((tm, d), F32), pltpu.VMEM((MOE_ROWS, d), F32),
                        pltpu.SemaphoreType.DMA(()), pltpu.SemaphoreType.DMA(())],
    )
    return pl.pallas_call(
        _dispatch_kernel,
        grid_spec=grid_spec,
        out_shape=jax.ShapeDtypeStruct((n_rows, d), F32),
        compiler_params=_params("arbitrary"),
        name="moe_dispatch",
    )(dest[0], dest[1], zero_blk, h, g)


def _moe_ffn_kernel(be_ref, nu_ref, nx_ref, x_ref, wg_hbm, wu_hbm, wd_hbm, y_ref,
                    wgf, wuf, wdf, wgb, wub, wdb, slot_ref, sem):
    b = pl.program_id(0)
    e = be_ref[b]
    used = b < nu_ref[0]

    def weight_copies(expert, slot):
        return (pltpu.make_async_copy(wg_hbm.at[expert], wgf.at[slot], sem.at[slot, 0]),
                pltpu.make_async_copy(wu_hbm.at[expert], wuf.at[slot], sem.at[slot, 1]),
                pltpu.make_async_copy(wd_hbm.at[expert], wdf.at[slot], sem.at[slot, 2]))

    @pl.when(b == 0)
    def _():
        slot_ref[0] = 0
        for c in weight_copies(e, 0):
            c.start()

    @pl.when(used & ((b == 0) | (e != be_ref[jnp.maximum(b - 1, 0)])))
    def _():
        slot = slot_ref[0]
        for c in weight_copies(e, slot):
            c.wait()
        nxt = nx_ref[e]

        @pl.when(nxt >= 0)
        def _():
            for c in weight_copies(nxt, 1 - slot):
                c.start()

        wgb[...] = wgf[slot].astype(BF16)
        wub[...] = wuf[slot].astype(BF16)
        wdb[...] = wdf[slot].astype(BF16)
        slot_ref[0] = 1 - slot

    @pl.when(used)
    def _():
        xb = x_ref[...].astype(BF16)
        gate = _dot(xb, wgb[...])
        up = _dot(xb, wub[...])
        act = (gate * jax.nn.sigmoid(gate) * up).astype(BF16)
        y_ref[...] = _dot(act, wdb[...])

    @pl.when(jnp.logical_not(used))
    def _():
        y_ref[...] = jnp.zeros_like(y_ref)


def _moe_ffn(blk_e, n_used, next_e, xg, w_gate, w_up, w_down):
    n_rows, d = xg.shape
    ff = w_gate.shape[2]
    n_blk = n_rows // MOE_ROWS
    def row_blk(b, be, nu, nx):
        return jnp.minimum(b, nu[0] - 1), 0

    grid_spec = pltpu.PrefetchScalarGridSpec(
        num_scalar_prefetch=3,
        grid=(n_blk,),
        in_specs=[pl.BlockSpec((MOE_ROWS, d), row_blk),
                  pl.BlockSpec(memory_space=pl.ANY),
                  pl.BlockSpec(memory_space=pl.ANY),
                  pl.BlockSpec(memory_space=pl.ANY)],
        out_specs=pl.BlockSpec((MOE_ROWS, d), lambda b, be, nu, nx: (b, 0)),
        scratch_shapes=[pltpu.VMEM((2, d, ff), F32), pltpu.VMEM((2, d, ff), F32),
                        pltpu.VMEM((2, ff, d), F32),
                        pltpu.VMEM((d, ff), BF16), pltpu.VMEM((d, ff), BF16),
                        pltpu.VMEM((ff, d), BF16),
                        pltpu.SMEM((1,), I32), pltpu.SemaphoreType.DMA((2, 3))],
    )
    return pl.pallas_call(
        _moe_ffn_kernel,
        grid_spec=grid_spec,
        out_shape=jax.ShapeDtypeStruct((n_rows, d), F32),
        compiler_params=_params("arbitrary"),
        name="moe_ffn",
    )(blk_e, n_used, next_e, xg, w_gate, w_up, w_down)


def _combine_kernel(d0_ref, d1_ref, h_ref, r_ref, y_ref, o_ref, buf, sem):
    tm = h_ref.shape[0]
    base = pl.program_id(0) * tm

    def row_copy(k, t, d):
        return pltpu.make_async_copy(y_ref.at[pl.ds(d, 1), :], buf.at[k, pl.ds(t, 1), :], sem)

    def issue(t, carry):
        row_copy(0, t, d0_ref[base + t]).start()
        row_copy(1, t, d1_ref[base + t]).start()
        return carry

    def drain(t, carry):
        row_copy(0, 0, 0).wait()
        row_copy(1, 0, 0).wait()
        return carry

    lax.fori_loop(0, tm, issue, 0)
    lax.fori_loop(0, tm, drain, 0)
    r = r_ref[...]
    o_ref[...] = (h_ref[...] + r[:, ROUTE_WT:ROUTE_WT + 1] * buf[0]
                  + r[:, ROUTE_WT + 1:ROUTE_WT + 2] * buf[1])


def _combine(dest, h, routed, y, tm):
    s, d = h.shape
    grid_spec = pltpu.PrefetchScalarGridSpec(
        num_scalar_prefetch=2,
        grid=(s // tm,),
        in_specs=[pl.BlockSpec((tm, d), lambda i, d0, d1: (i, 0)),
                  pl.BlockSpec((tm, LANES), lambda i, d0, d1: (i, 0)),
                  pl.BlockSpec(memory_space=pl.ANY)],
        out_specs=pl.BlockSpec((tm, d), lambda i, d0, d1: (i, 0)),
        scratch_shapes=[pltpu.VMEM((2, tm, d), F32), pltpu.SemaphoreType.DMA(())],
    )
    return pl.pallas_call(
        _combine_kernel,
        grid_spec=grid_spec,
        out_shape=jax.ShapeDtypeStruct((s, d), F32),
        compiler_params=_params("arbitrary"),
        name="moe_combine",
    )(dest[0], dest[1], h, routed, y)


def _layer(h, mem, pos, norm_mix_g, w_in, gate_b, conv_w, conv_b, out_g, moba_q_g, moba_k_g, w_out,
           norm_cross_g, norm_mem_g, xa_wq, xa_wkv, xa_q_g, xa_k_g, xa_wo, norm_ffn_g,
           router_group_w, router_group_b, router_expert_w, router_expert_b,
           exp_w_gate, exp_w_up, exp_w_down):
    s, d = h.shape
    row = lambda v: v.reshape(1, -1)

    n_gate = 2 * MLSTM_HEADS
    gate_lo = 2 * MLSTM_HEADS * MLSTM_QK + 2 * MLSTM_HEADS * MLSTM_V
    proj_tn = 512
    w_a = w_in[:, :gate_lo].astype(BF16)
    w_b = w_in[:, gate_lo + n_gate:].astype(BF16)
    w_g = jnp.pad(w_in[:, gate_lo:gate_lo + n_gate], ((0, 0), (0, proj_tn - n_gate))).astype(BF16)
    proj = _proj_in(h, row(norm_mix_g), w_a, w_b, w_g, tm=min(s, 1024))

    gate_b_row = jnp.pad(gate_b, (0, LANES - n_gate)).reshape(1, LANES)
    hm = _mlstm(proj, w_a.shape[1] + w_b.shape[1], gate_b_row, conv_w, row(conv_b), row(out_g))
    qn, kn, vb, kmean = _moba_prep(proj, gate_lo, pos, row(moba_q_g), row(moba_k_g))
    ha = _moba_attn(qn, kn, vb, kmean)
    h = _mix_out(hm, ha, w_out.astype(BF16), h, tm=512)

    kv = _mem_kv(mem, row(norm_mem_g), xa_wkv.astype(BF16), row(xa_k_g))
    h = _xattn(h, row(norm_cross_g), xa_wq.astype(BF16), kv, row(xa_q_g), xa_wo.astype(BF16),
               tm=min(s, 1024))

    w_route = jnp.concatenate(
        [router_group_w, router_expert_w,
         jnp.zeros((d, LANES - MOE_GROUPS - MOE_EXPERTS), router_group_w.dtype)], axis=1)
    w_route_hi = w_route.astype(BF16)
    w_route = jnp.stack([w_route_hi, (w_route - w_route_hi.astype(F32)).astype(BF16)])
    b_route = jnp.pad(jnp.concatenate([router_group_b, router_expert_b]),
                      (0, LANES - MOE_GROUPS - MOE_EXPERTS)).reshape(1, LANES)
    routed, counts = _router(h, row(norm_ffn_g), w_route, b_route, tm=512)

    cnt = counts[0, MOE_GROUPS:MOE_GROUPS + MOE_EXPERTS].astype(I32)
    padded = (cnt + MOE_ROWS - 1) // MOE_ROWS * MOE_ROWS
    pend = jnp.cumsum(padded)
    seg_start = (pend - padded).astype(F32)[None, :]
    expert_ids = jnp.arange(MOE_EXPERTS, dtype=F32)[None, :]

    def dest_of(k):
        mine = routed[:, ROUTE_EID + k:ROUTE_EID + k + 1] == expert_ids
        return (jnp.sum(jnp.where(mine, seg_start, 0.0), axis=1) + routed[:, ROUTE_RANK + k]).astype(I32)

    dest = (dest_of(0), dest_of(1))
    n_blk = -(-2 * s // MOE_ROWS) + MOE_EXPERTS
    blk_start = jnp.arange(n_blk, dtype=I32) * MOE_ROWS
    blk_e = jnp.minimum(jnp.sum((pend[None, :] <= blk_start[:, None]).astype(I32), axis=1),
                        MOE_EXPERTS - 1)
    n_used = (pend[-1:] // MOE_ROWS).astype(I32)
    ids = jnp.arange(MOE_EXPERTS, dtype=I32)
    later_used = (ids[None, :] > ids[:, None]) & (cnt[None, :] > 0)
    next_e = jnp.min(jnp.where(later_used, ids[None, :], MOE_EXPERTS), axis=1)
    next_e = jnp.where(next_e < MOE_EXPERTS, next_e, -1).astype(I32)

    is_blk_e = blk_e[:, None] == ids[None, :]
    seg_end = jnp.sum(jnp.where(is_blk_e, (pend - padded + cnt)[None, :], 0), axis=1)
    zero_blk = (blk_start + MOE_ROWS > seg_end).astype(I32)
    xg = _dispatch(dest, zero_blk, h, row(norm_ffn_g), tm=256)
    y = _moe_ffn(blk_e, n_used, next_e, xg, exp_w_gate, exp_w_up, exp_w_down)
    return _combine(dest, h, routed, y, tm=256)


def kernel(x, mem, positions, norm_mix_g, w_in, mlstm_gate_b, mlstm_conv_w, mlstm_conv_b, mlstm_out_g, moba_q_g, moba_k_g, w_out, norm_cross_g, norm_mem_g, xa_wq, xa_wkv, xa_q_g, xa_k_g, xa_wo, norm_ffn_g, router_group_w, router_group_b, router_expert_w, router_expert_b, exp_w_gate, exp_w_up, exp_w_down):
    bsz, s, _ = x.shape
    assert bsz == 1, "single-sequence prefill only"
    per_layer = (norm_mix_g, w_in, mlstm_gate_b, mlstm_conv_w, mlstm_conv_b, mlstm_out_g, moba_q_g,
                 moba_k_g, w_out, norm_cross_g, norm_mem_g, xa_wq, xa_wkv, xa_q_g, xa_k_g, xa_wo,
                 norm_ffn_g, router_group_w, router_group_b, router_expert_w, router_expert_b,
                 exp_w_gate, exp_w_up, exp_w_down)
    h = x[0]
    pos = positions.reshape(s, 1)
    for l in range(norm_mix_g.shape[0]):
        h = _layer(h, mem[0], pos, *(p[l] for p in per_layer))
    return h[None]
```

```python
import functools
import math

import jax
import jax.numpy as jnp
from jax import lax
from jax.experimental import pallas as pl
from jax.experimental.pallas import tpu as pltpu

F32 = jnp.float32
BF16 = jnp.bfloat16
I32 = jnp.int32

EPS = 1e-6
LANES = 128
SUBLANES = 8
VMEM_LIMIT = 56 * 1024 * 1024

MLSTM_HEADS = 4
MLSTM_QK = 128
MLSTM_V = 256
MLSTM_CHUNK = 128
MLSTM_CONV = 4
MOBA_HEADS = 8
MOBA_HD = 128
MOBA_BLOCK = 256
MOBA_TOPK = 3
ROPE_DIM = 32
ROPE_THETA = 500000.0
XA_HEADS = 4
MOE_GROUPS = 8
MOE_PER_GROUP = 8
MOE_EXPERTS = MOE_GROUPS * MOE_PER_GROUP
MOE_ROWS = 128
ROW_DMA_UNROLL = 8

NT_DIMS = (((1,), (1,)), ((), ()))


def _params(*sem):
    return pltpu.CompilerParams(dimension_semantics=sem, vmem_limit_bytes=VMEM_LIMIT)


def _rms(x, g):
    return x * lax.rsqrt(jnp.mean(x * x, axis=-1, keepdims=True) + EPS) * g


def _dot(a, b):
    return jnp.dot(a, b, preferred_element_type=F32)


def _dot_nt(a, b, precision=None):
    return lax.dot_general(a, b, NT_DIMS, precision=precision, preferred_element_type=F32)


def _proj_in_kernel(x_ref, g_ref, wa_ref, wb_ref, wg_ref, o_ref, xn_ref, *, na, nb):
    j = pl.program_id(1)

    @pl.when(j == 0)
    def _():
        xn_ref[...] = _rms(x_ref[...], g_ref[...]).astype(BF16)

    @pl.when(j < na)
    def _():
        o_ref[...] = _dot(xn_ref[...], wa_ref[...])

    @pl.when((j >= na) & (j < na + nb))
    def _():
        o_ref[...] = _dot(xn_ref[...], wb_ref[...])

    @pl.when(j == na + nb)
    def _():
        o_ref[...] = _dot(xn_ref[...], wg_ref[...])


def _proj_in(x, g, w_a, w_b, w_g, tm):
    s, d = x.shape
    tn = w_g.shape[1]
    na, nb = w_a.shape[1] // tn, w_b.shape[1] // tn
    return pl.pallas_call(
        functools.partial(_proj_in_kernel, na=na, nb=nb),
        grid=(s // tm, na + nb + 1),
        in_specs=[pl.BlockSpec((tm, d), lambda i, j: (i, 0)),
                  pl.BlockSpec((1, d), lambda i, j: (0, 0)),
                  pl.BlockSpec((d, tn), lambda i, j: (0, jnp.minimum(j, na - 1))),
                  pl.BlockSpec((d, tn), lambda i, j: (0, jnp.clip(j - na, 0, nb - 1))),
                  pl.BlockSpec((d, tn), lambda i, j: (0, 0))],
        out_specs=pl.BlockSpec((tm, tn), lambda i, j: (i, j)),
        out_shape=jax.ShapeDtypeStruct((s, (na + nb + 1) * tn), F32),
        scratch_shapes=[pltpu.VMEM((tm, d), BF16)],
        compiler_params=_params("parallel", "arbitrary"),
        name="proj_in",
    )(x, g, w_a, w_b, w_g)


def _split3(x):
    hi = x.astype(BF16)
    r = x - hi.astype(F32)
    mid = r.astype(BF16)
    lo = (r - mid.astype(F32)).astype(BF16)
    return hi, mid, lo


def _log_sigmoid(x):
    return jnp.minimum(x, 0.0) - jnp.log(1.0 + jnp.exp(-jnp.abs(x)))


def _mlstm_kernel(q_ref, k_ref, v_ref, og_ref, gt_ref, gb_ref, cw_ref, cb_ref, outg_ref, hm_ref,
                  qext, kext, c_scr, n_scr, m_scr):
    L = MLSTM_CHUNK
    pad = SUBLANES
    qkw = MLSTM_HEADS * MLSTM_QK

    @pl.when(pl.program_id(0) == 0)
    def _():
        qext[0:pad, :] = jnp.zeros((pad, qkw), F32)
        kext[0:pad, :] = jnp.zeros((pad, qkw), F32)
        c_scr[...] = jnp.zeros_like(c_scr)
        n_scr[...] = jnp.zeros_like(n_scr)
        m_scr[...] = jnp.zeros_like(m_scr)

    qext[pad:pad + L, :] = q_ref[...]
    kext[pad:pad + L, :] = k_ref[...]

    def conv_silu(ext, lo):
        acc = jnp.broadcast_to(cb_ref[:, lo:lo + qkw], (L, qkw))
        for j in range(MLSTM_CONV):
            shift = MLSTM_CONV - 1 - j
            acc = acc + cw_ref[j:j + 1, lo:lo + qkw] * ext[pad - shift:pad - shift + L, :]
        return acc * jax.nn.sigmoid(acc)

    qc = conv_silu(qext, 0)
    kc = conv_silu(kext, qkw)
    qext[0:pad, :] = qext[L:L + pad, :]
    kext[0:pad, :] = kext[L:L + pad, :]

    gts = gt_ref[...] + gb_ref[...]
    gtr = gts.T
    row = lax.broadcasted_iota(I32, (L, L), 0)
    col = lax.broadcasted_iota(I32, (L, L), 1)
    causal = col <= row
    tri_lo = jnp.where(causal, 1.0, 0.0).astype(BF16)
    tri_up = jnp.where(row <= col, 1.0, 0.0).astype(BF16)
    bc_all = sum(_dot(tri_lo, part) for part in _split3(_log_sigmoid(gts)))
    br_all = sum(_dot(part, tri_up) for part in _split3(_log_sigmoid(gtr)))

    for h in range(MLSTM_HEADS):
        fi = MLSTM_HEADS + h
        m_prev = m_scr[h, 0:1, 0:1]
        b_c = bc_all[:, fi:fi + 1]
        b_r = br_all[fi:fi + 1, :]
        i_c = gts[:, h:h + 1]
        i_r = gtr[h:h + 1, :]
        g = b_c[L - 1:L, :]
        d = jnp.where(causal, (b_c - b_r) + i_r, -jnp.inf)
        inter = b_c + m_prev
        m_t = jnp.maximum(inter, jnp.max(d, axis=-1, keepdims=True))
        w_inter = jnp.exp(inter - m_t)
        qh = qc[:, h * MLSTM_QK:(h + 1) * MLSTM_QK] * (MLSTM_QK ** -0.5)
        kh = kc[:, h * MLSTM_QK:(h + 1) * MLSTM_QK]
        vb = v_ref[:, h * MLSTM_V:(h + 1) * MLSTM_V].astype(BF16)
        qb = qh.astype(BF16)
        p = jnp.exp(d - m_t) * _dot_nt(qb, kh.astype(BF16))
        c_prev = c_scr[h]
        n_prev = n_scr[h, 0:1, :]
        num = w_inter * _dot(qb, c_prev.astype(BF16)) + _dot(p.astype(BF16), vb)
        den = (w_inter * jnp.sum(qh * n_prev, axis=-1, keepdims=True)
               + jnp.sum(p, axis=-1, keepdims=True))
        hh = num * (1.0 / jnp.maximum(jnp.abs(den), jnp.exp(-m_t)))

        a = (g - b_c) + i_c
        m_new = jnp.maximum(g + m_prev, jnp.max(a, axis=0, keepdims=True))
        kw = kh * jnp.exp(a - m_new)
        decay = jnp.exp(g + m_prev - m_new)
        c_scr[h] = decay * c_prev + _dot(kw.T.astype(BF16), vb)
        n_scr[h, 0:1, :] = decay * n_prev + jnp.sum(kw, axis=0, keepdims=True)
        m_scr[h] = jnp.broadcast_to(m_new, (SUBLANES, LANES))

        hn = hh * lax.rsqrt(jnp.mean(hh * hh, axis=-1, keepdims=True) + EPS)
        vs = slice(h * MLSTM_V, (h + 1) * MLSTM_V)
        hn = hn * outg_ref[:, vs] * jax.nn.sigmoid(og_ref[:, vs])
        hm_ref[:, vs] = hn.astype(hm_ref.dtype)


def _mlstm(proj, gate_col, gate_b, conv_w, conv_b, out_g):
    s = proj.shape[0]
    L = MLSTM_CHUNK
    qkw = MLSTM_HEADS * MLSTM_QK
    vw = MLSTM_HEADS * MLSTM_V
    gate_blk = gate_col // LANES
    return pl.pallas_call(
        _mlstm_kernel,
        grid=(s // L,),
        in_specs=[pl.BlockSpec((L, qkw), lambda c: (c, 0)),
                  pl.BlockSpec((L, qkw), lambda c: (c, 1)),
                  pl.BlockSpec((L, vw), lambda c: (c, 1)),
                  pl.BlockSpec((L, vw), lambda c: (c, 2)),
                  pl.BlockSpec((L, LANES), lambda c: (c, gate_blk)),
                  pl.BlockSpec((1, LANES), lambda c: (0, 0)),
                  pl.BlockSpec((MLSTM_CONV, 2 * qkw), lambda c: (0, 0)),
                  pl.BlockSpec((1, 2 * qkw), lambda c: (0, 0)),
                  pl.BlockSpec((1, vw), lambda c: (0, 0))],
        out_specs=pl.BlockSpec((L, vw), lambda c: (c, 0)),
        out_shape=jax.ShapeDtypeStruct((s, vw), BF16),
        scratch_shapes=[pltpu.VMEM((L + SUBLANES, qkw), F32),
                        pltpu.VMEM((L + SUBLANES, qkw), F32),
                        pltpu.VMEM((MLSTM_HEADS, MLSTM_QK, MLSTM_V), F32),
                        pltpu.VMEM((MLSTM_HEADS, SUBLANES, LANES), F32),
                        pltpu.VMEM((MLSTM_HEADS, SUBLANES, LANES), F32)],
        compiler_params=_params("arbitrary"),
        name="mlstm",
    )(proj, proj, proj, proj, proj, gate_b, conv_w, conv_b, out_g)


def _moba_prep_kernel(q_ref, k_ref, v_ref, pos_ref, qg_ref, kg_ref, qt_ref, kn_ref, vt_ref, km_ref):
    rows = q_ref.shape[0]
    half = ROPE_DIM // 2
    lane = lax.broadcasted_iota(I32, (1, MOBA_HD), 1)
    inv_freq = jnp.exp((lane & (half - 1)).astype(F32) * (-(2.0 / ROPE_DIM) * math.log(ROPE_THETA)))
    ang = pos_ref[...].astype(F32) * inv_freq
    cos = jnp.where(lane < ROPE_DIM, jnp.cos(ang), 1.0)
    sin = jnp.sin(ang)
    sin = jnp.where(lane < half, -sin, jnp.where(lane < ROPE_DIM, sin, 0.0))

    def rope(x):
        partner = jnp.where(lane < half, pltpu.roll(x, MOBA_HD - half, 1), pltpu.roll(x, half, 1))
        return x * cos + partner * sin

    for h in range(MOBA_HEADS):
        hs = slice(h * MOBA_HD, (h + 1) * MOBA_HD)
        qt_ref[hs, :] = rope(_rms(q_ref[:, hs], qg_ref[...])).T
        kn = rope(_rms(k_ref[:, hs], kg_ref[...]))
        kn_ref[:, hs] = kn.astype(BF16)
        km_ref[0, :, hs] = jnp.sum(kn, axis=0, keepdims=True) * (1.0 / rows)
        vt_ref[0, hs, :] = v_ref[:, hs].T.astype(BF16)


def _moba_prep(proj, q_col, pos, q_g, k_g):
    s = proj.shape[0]
    w = MOBA_HEADS * MOBA_HD
    bs = MOBA_BLOCK
    nb = s // bs
    first = q_col // w
    return pl.pallas_call(
        _moba_prep_kernel,
        grid=(nb,),
        in_specs=[pl.BlockSpec((bs, w), lambda i: (i, first)),
                  pl.BlockSpec((bs, w), lambda i: (i, first + 1)),
                  pl.BlockSpec((bs, w), lambda i: (i, first + 2)),
                  pl.BlockSpec((bs, 1), lambda i: (i, 0)),
                  pl.BlockSpec((1, MOBA_HD), lambda i: (0, 0)),
                  pl.BlockSpec((1, MOBA_HD), lambda i: (0, 0))],
        out_specs=[pl.BlockSpec((w, bs), lambda i: (0, i)),
                   pl.BlockSpec((bs, w), lambda i: (i, 0)),
                   pl.BlockSpec((1, w, bs), lambda i: (i, 0, 0)),
                   pl.BlockSpec((1, 1, w), lambda i: (i, 0, 0))],
        out_shape=[jax.ShapeDtypeStruct((w, s), F32),
                   jax.ShapeDtypeStruct((s, w), BF16),
                   jax.ShapeDtypeStruct((nb, w, bs), BF16),
                   jax.ShapeDtypeStruct((nb, 1, w), F32)],
        compiler_params=_params("parallel"),
        name="moba_prep",
    )(proj, proj, proj, pos, q_g, k_g)


def _moba_attn_kernel(qt_ref, k_ref, vt_ref, km_ref, o_ref, bias_scr, qb_scr, m_scr, l_scr, *tiles):
    i = pl.program_id(1)
    bs = MOBA_BLOCK
    hd = MOBA_HD
    nh = len(tiles) // 3
    acc_scr = tiles[:nh]
    s_scr = tiles[nh:2 * nh]
    p_scr = tiles[2 * nh:]
    nb = km_ref.shape[0]
    c2 = (hd ** -0.5) * math.log2(math.e)
    blk = lax.broadcasted_iota(I32, (nb, bs), 0)
    blk_f = blk.astype(F32)
    valid = blk < i
    key = lax.broadcasted_iota(I32, (bs, bs), 0)
    qry = lax.broadcasted_iota(I32, (bs, bs), 1)
    own = pl.ds(pl.multiple_of(i * bs, bs), bs)

    for h in range(nh):
        hs = slice(h * hd, (h + 1) * hd)
        qt = qt_ref[hs, :]
        gate = jnp.dot(km_ref[:, 0, hs], qt, precision=lax.Precision.HIGHEST,
                       preferred_element_type=F32)
        cand = valid
        for _ in range(MOBA_TOPK):
            gmax = jnp.max(jnp.where(cand, gate, -jnp.inf), axis=0, keepdims=True)
            hit = cand & (gate == gmax)
            first = jnp.min(jnp.where(hit, blk_f, float(nb)), axis=0, keepdims=True)
            cand = cand & jnp.logical_not(hit & (blk_f == first))
        bias_scr[h] = jnp.where(valid & jnp.logical_not(cand), 0.0, -jnp.inf)

        qb = (qt * c2).astype(BF16)
        qb_scr[h] = qb
        s0 = jnp.where(key <= qry, _dot(k_ref[own, hs], qb), -jnp.inf)
        m0 = jnp.max(s0, axis=0, keepdims=True)
        p0 = jnp.exp2(s0 - m0)
        m_scr[h] = m0
        l_scr[h] = jnp.sum(p0, axis=0, keepdims=True)
        acc_scr[h][...] = _dot(vt_ref[i, hs, :], p0.astype(BF16))

    ck = 32

    def body(j, carry):
        rows = pl.ds(pl.multiple_of(j * bs, bs), bs)
        for h in range(nh):
            s_scr[h][...] = _dot(k_ref[rows, h * hd:(h + 1) * hd], qb_scr[h])
        alphas = []
        for h in range(nh):
            bias = bias_scr[h, pl.ds(j, 1), :]
            cmax = s_scr[h][0:ck, :]
            for c in range(1, bs // ck):
                cmax = jnp.maximum(cmax, s_scr[h][c * ck:(c + 1) * ck, :])
            m = m_scr[h]
            m_new = jnp.maximum(m, jnp.max(cmax, axis=0, keepdims=True) + bias)
            alpha = jnp.exp2(m - m_new)
            shift = m_new - bias
            psum = jnp.zeros((ck, bs), F32)
            for c in range(bs // ck):
                p = jnp.exp2(s_scr[h][c * ck:(c + 1) * ck, :] - shift)
                psum = psum + p
                p_scr[h][c * ck:(c + 1) * ck, :] = p.astype(BF16)
            m_scr[h] = m_new
            l_scr[h] = alpha * l_scr[h] + jnp.sum(psum, axis=0, keepdims=True)
            alphas.append(alpha)
        for h in range(nh):
            acc_scr[h][...] = (alphas[h] * acc_scr[h][...]
                               + _dot(vt_ref[j, h * hd:(h + 1) * hd, :], p_scr[h][...]))
        return carry

    lax.fori_loop(0, i, body, 0)
    for h in range(nh):
        out_t = acc_scr[h][...] * (1.0 / l_scr[h])
        o_ref[:, h * hd:(h + 1) * hd] = out_t.T.astype(o_ref.dtype)


MOBA_HEADS_PER_STEP = 8


def _moba_attn(qt, kn, vt, kmean):
    w, s = qt.shape
    bs = MOBA_BLOCK
    nb = s // bs
    nh = MOBA_HEADS_PER_STEP
    gw = nh * MOBA_HD
    resident = pl.Buffered(1)
    return pl.pallas_call(
        _moba_attn_kernel,
        grid=(MOBA_HEADS // nh, nb),
        in_specs=[pl.BlockSpec((gw, bs), lambda g, i: (g, i)),
                  pl.BlockSpec((s, gw), lambda g, i: (0, g), pipeline_mode=resident),
                  pl.BlockSpec((nb, gw, bs), lambda g, i: (0, g, 0), pipeline_mode=resident),
                  pl.BlockSpec((nb, 1, gw), lambda g, i: (0, 0, g))],
        out_specs=pl.BlockSpec((bs, gw), lambda g, i: (i, g)),
        out_shape=jax.ShapeDtypeStruct((s, w), BF16),
        scratch_shapes=[pltpu.VMEM((nh, nb, bs), F32),
                        pltpu.VMEM((nh, MOBA_HD, bs), BF16),
                        pltpu.VMEM((nh, 1, bs), F32),
                        pltpu.VMEM((nh, 1, bs), F32)]
                       + [pltpu.VMEM((MOBA_HD, bs), F32) for _ in range(nh)]
                       + [pltpu.VMEM((bs, bs), F32) for _ in range(nh)]
                       + [pltpu.VMEM((bs, bs), BF16) for _ in range(nh)],
        compiler_params=_params("parallel", "arbitrary"),
        name="moba_attn",
    )(qt, kn, vt, kmean)


def _mix_out_kernel(hm_ref, ha_ref, w_ref, x_ref, o_ref, wb_ref):
    half = hm_ref.shape[1]

    @pl.when(pl.program_id(0) == 0)
    def _():
        wb_ref[...] = w_ref[...].astype(BF16)

    o_ref[...] = (x_ref[...] + _dot(hm_ref[...], wb_ref[0:half, :])
                  + _dot(ha_ref[...], wb_ref[half:2 * half, :]))


def _mix_out(hm, ha, w, x, tm):
    s, d = x.shape
    half = hm.shape[1]
    return pl.pallas_call(
        _mix_out_kernel,
        grid=(s // tm,),
        in_specs=[pl.BlockSpec((tm, half), lambda i: (i, 0)),
                  pl.BlockSpec((tm, half), lambda i: (i, 0)),
                  pl.BlockSpec((2 * half, d), lambda i: (0, 0), pipeline_mode=pl.Buffered(1)),
                  pl.BlockSpec((tm, d), lambda i: (i, 0))],
        out_specs=pl.BlockSpec((tm, d), lambda i: (i, 0)),
        out_shape=jax.ShapeDtypeStruct((s, d), F32),
        scratch_shapes=[pltpu.VMEM((2 * half, d), BF16)],
        compiler_params=_params("arbitrary"),
        name="mix_out",
    )(hm, ha, w, x)


def _mem_kv_kernel(mem_ref, g_ref, w_ref, kg_ref, o_ref, mn_ref):
    j = pl.program_id(0)

    @pl.when(j == 0)
    def _():
        mn_ref[...] = _rms(mem_ref[...], g_ref[...]).astype(BF16)

    y = _dot(mn_ref[...], w_ref[...].astype(BF16))
    o_ref[...] = jnp.where(j < XA_HEADS, _rms(y, kg_ref[...]), y).astype(o_ref.dtype)


def _mem_kv(mem, g, wkv, k_g):
    m, d = mem.shape
    hd = d // XA_HEADS
    return pl.pallas_call(
        _mem_kv_kernel,
        grid=(2 * XA_HEADS,),
        in_specs=[pl.BlockSpec((m, d), lambda j: (0, 0)),
                  pl.BlockSpec((1, d), lambda j: (0, 0)),
                  pl.BlockSpec((d, hd), lambda j: (0, j)),
                  pl.BlockSpec((1, hd), lambda j: (0, 0))],
        out_specs=pl.BlockSpec((m, hd), lambda j: (0, j)),
        out_shape=jax.ShapeDtypeStruct((m, 2 * d), BF16),
        scratch_shapes=[pltpu.VMEM((m, d), BF16)],
        compiler_params=_params("arbitrary"),
        name="mem_kv",
    )(mem, g, wkv, k_g)


def _xattn_kernel(h_ref, g_ref, wq_ref, k_ref, v_ref, qg_ref, wo_ref, o_ref, hn_ref, wqb_ref, wob_ref):
    i = pl.program_id(0)
    j = pl.program_id(1)

    @pl.when(i == 0)
    def _():
        wqb_ref[j] = wq_ref[...].astype(BF16)
        wob_ref[j] = wo_ref[...].astype(BF16)

    @pl.when(j == 0)
    def _():
        h = h_ref[...]
        hn_ref[...] = _rms(h, g_ref[...]).astype(BF16)
        o_ref[...] = h

    hd = wq_ref.shape[1]
    q = _rms(_dot(hn_ref[...], wqb_ref[j]), qg_ref[...]).astype(BF16)
    sc = _dot_nt(q, k_ref[...]) * (hd ** -0.5)
    p = jnp.exp(sc - jnp.max(sc, axis=-1, keepdims=True))
    p = p * (1.0 / jnp.sum(p, axis=-1, keepdims=True))
    o = _dot(p.astype(BF16), v_ref[...]).astype(BF16)
    o_ref[...] += _dot(o, wob_ref[j])


def _xattn(h, g, wq, kv, q_g, wo, tm):
    s, d = h.shape
    m = kv.shape[0]
    hd = d // XA_HEADS

    def head_once(i, j):
        return jnp.where(i == 0, j, XA_HEADS - 1)

    return pl.pallas_call(
        _xattn_kernel,
        grid=(s // tm, XA_HEADS),
        in_specs=[pl.BlockSpec((tm, d), lambda i, j: (i, 0)),
                  pl.BlockSpec((1, d), lambda i, j: (0, 0)),
                  pl.BlockSpec((d, hd), lambda i, j: (0, head_once(i, j))),
                  pl.BlockSpec((m, hd), lambda i, j: (0, j)),
                  pl.BlockSpec((m, hd), lambda i, j: (0, XA_HEADS + j)),
                  pl.BlockSpec((1, hd), lambda i, j: (0, 0)),
                  pl.BlockSpec((hd, d), lambda i, j: (head_once(i, j), 0))],
        out_specs=pl.BlockSpec((tm, d), lambda i, j: (i, 0)),
        out_shape=jax.ShapeDtypeStruct((s, d), F32),
        scratch_shapes=[pltpu.VMEM((tm, d), BF16),
                        pltpu.VMEM((XA_HEADS, d, hd), BF16),
                        pltpu.VMEM((XA_HEADS, hd, d), BF16)],
        compiler_params=_params("arbitrary", "arbitrary"),
        name="xattn",
    )(h, g, wq, kv, kv, q_g, wo)


ROUTE_EID = 0
ROUTE_RANK = 2
ROUTE_WT = 4


def _router_kernel(h_ref, g_ref, w_ref, b_ref, out_ref, cnt_ref, carry):
    tm = h_ref.shape[0]

    @pl.when(pl.program_id(0) == 0)
    def _():
        carry[...] = jnp.zeros_like(carry)

    hn = _rms(h_ref[...], g_ref[...])
    hi = hn.astype(BF16)
    lo = (hn - hi.astype(F32)).astype(BF16)
    logits = (_dot(hi, w_ref[0]) + _dot(hi, w_ref[1]) + _dot(lo, w_ref[0])
              + b_ref[...])
    lane = lax.broadcasted_iota(I32, (tm, LANES), 1)

    def first_lane(mask):
        return jnp.min(jnp.where(mask, lane.astype(F32), float(LANES)), axis=-1,
                       keepdims=True).astype(I32)

    is_g = lane < MOE_GROUPS
    gmax = jnp.max(jnp.where(is_g, logits, -jnp.inf), axis=-1, keepdims=True)
    gsum = jnp.sum(jnp.where(is_g, jnp.exp(logits - gmax), 0.0), axis=-1, keepdims=True)
    gsel = first_lane(is_g & (logits == gmax))
    pg = 1.0 / gsum

    grp_lo = MOE_GROUPS + MOE_PER_GROUP * gsel
    in_grp = (lane >= grp_lo) & (lane < grp_lo + MOE_PER_GROUP)
    emax = jnp.max(jnp.where(in_grp, logits, -jnp.inf), axis=-1, keepdims=True)
    eexp = jnp.where(in_grp, jnp.exp(logits - emax), 0.0)
    eprob = eexp / jnp.sum(eexp, axis=-1, keepdims=True)
    p1 = jnp.max(jnp.where(in_grp, eprob, -1.0), axis=-1, keepdims=True)
    l1 = first_lane(in_grp & (eprob == p1))
    rest = in_grp & (lane != l1)
    p2 = jnp.max(jnp.where(rest, eprob, -1.0), axis=-1, keepdims=True)
    l2 = first_lane(rest & (eprob == p2))
    psum = p1 + p2
    w1 = pg * p1 / psum
    w2 = pg * p2 / psum

    oh1 = lane == l1
    oh2 = lane == l2
    onehot = jnp.where(oh1 | oh2, 1.0, 0.0)
    r = lax.broadcasted_iota(I32, (tm, tm), 0)
    c = lax.broadcasted_iota(I32, (tm, tm), 1)
    strict = jnp.where(c < r, 1.0, 0.0).astype(BF16)
    before = _dot(strict, onehot.astype(BF16)) + carry[0:1, :]
    rank1 = jnp.sum(jnp.where(oh1, before, 0.0), axis=-1, keepdims=True)
    rank2 = jnp.sum(jnp.where(oh2, before, 0.0), axis=-1, keepdims=True)
    total = carry[0:1, :] + jnp.sum(onehot, axis=0, keepdims=True)
    carry[...] = jnp.broadcast_to(total, carry.shape)
    cnt_ref[...] = jnp.broadcast_to(total, cnt_ref.shape)

    out = jnp.where(lane == ROUTE_EID, (l1 - MOE_GROUPS).astype(F32), 0.0)
    out = jnp.where(lane == ROUTE_EID + 1, (l2 - MOE_GROUPS).astype(F32), out)
    out = jnp.where(lane == ROUTE_RANK, rank1, out)
    out = jnp.where(lane == ROUTE_RANK + 1, rank2, out)
    out = jnp.where(lane == ROUTE_WT, w1, out)
    out = jnp.where(lane == ROUTE_WT + 1, w2, out)
    out_ref[...] = out


def _router(h, g, w, b, tm):
    s, d = h.shape
    return pl.pallas_call(
        _router_kernel,
        grid=(s // tm,),
        in_specs=[pl.BlockSpec((tm, d), lambda i: (i, 0)),
                  pl.BlockSpec((1, d), lambda i: (0, 0)),
                  pl.BlockSpec((2, d, LANES), lambda i: (0, 0, 0)),
                  pl.BlockSpec((1, LANES), lambda i: (0, 0))],
        out_specs=[pl.BlockSpec((tm, LANES), lambda i: (i, 0)),
                   pl.BlockSpec((SUBLANES, LANES), lambda i: (0, 0))],
        out_shape=[jax.ShapeDtypeStruct((s, LANES), F32),
                   jax.ShapeDtypeStruct((SUBLANES, LANES), F32)],
        scratch_shapes=[pltpu.VMEM((SUBLANES, LANES), F32)],
        compiler_params=_params("arbitrary"),
        name="router",
    )(h, g, w, b)


def _dispatch_kernel(d0_ref, d1_ref, zb_ref, h_ref, g_ref, xg_ref, buf, zero, sem, zsem):
    tm = h_ref.shape[0]
    base = pl.program_id(0) * tm

    @pl.when(pl.program_id(0) == 0)
    def _():
        zero[...] = jnp.zeros_like(zero)
        n_blk = xg_ref.shape[0] // MOE_ROWS

        def zero_copy(b):
            rows = pl.ds(pl.multiple_of(b * MOE_ROWS, MOE_ROWS), MOE_ROWS)
            return pltpu.make_async_copy(zero, xg_ref.at[rows, :], zsem)

        def start(b, carry):
            @pl.when(zb_ref[b] != 0)
            def _():
                zero_copy(b).start()
            return carry

        def finish(b, carry):
            @pl.when(zb_ref[b] != 0)
            def _():
                zero_copy(0).wait()
            return carry

        lax.fori_loop(0, n_blk, start, 0)
        lax.fori_loop(0, n_blk, finish, 0)

    buf[...] = _rms(h_ref[...], g_ref[...])

    def row_copy(t, d):
        return pltpu.make_async_copy(buf.at[pl.ds(t, 1), :], xg_ref.at[pl.ds(d, 1), :], sem)

    def issue(t, carry):
        row_copy(t, d0_ref[base + t]).start()
        row_copy(t, d1_ref[base + t]).start()
        return carry

    lax.fori_loop(0, tm, issue, 0, unroll=ROW_DMA_UNROLL)
    whole = pltpu.make_async_copy(buf, xg_ref.at[pl.ds(0, tm), :], sem)
    whole.wait()
    whole.wait()


def _dispatch(dest, zero_blk, h, g, tm):
    s, d = h.shape
    n_rows = zero_blk.shape[0] * MOE_ROWS
    grid_spec = pltpu.PrefetchScalarGridSpec(
        num_scalar_prefetch=3,
        grid=(s // tm,),
        in_specs=[pl.BlockSpec((tm, d), lambda i, d0, d1, zb: (i, 0)),
                  pl.BlockSpec((1, d), lambda i, d0, d1, zb: (0, 0))],
        out_specs=pl.BlockSpec(memory_space=pl.ANY),
        scratch_shapes=[pltpu.VMEM((tm, d), F32), pltpu.VMEM((MOE_ROWS, d), F32),
                        pltpu.SemaphoreType.DMA(()), pltpu.SemaphoreType.DMA(())],
    )
    return pl.pallas_call(
        _dispatch_kernel,
        grid_spec=grid_spec,
        out_shape=jax.ShapeDtypeStruct((n_rows, d), F32),
        compiler_params=_params("arbitrary"),
        name="moe_dispatch",
    )(dest[0], dest[1], zero_blk, h, g)


def _moe_ffn_kernel(be_ref, nu_ref, nx_ref, x_ref, wg_hbm, wu_hbm, wd_hbm, y_ref,
                    wgf, wuf, wdf, wgb, wub, wdb, slot_ref, sem):
    b = pl.program_id(0)
    e = be_ref[b]
    used = b < nu_ref[0]

    def weight_copies(expert, slot):
        return (pltpu.make_async_copy(wg_hbm.at[expert], wgf.at[slot], sem.at[slot, 0]),
                pltpu.make_async_copy(wu_hbm.at[expert], wuf.at[slot], sem.at[slot, 1]),
                pltpu.make_async_copy(wd_hbm.at[expert], wdf.at[slot], sem.at[slot, 2]))

    @pl.when(b == 0)
    def _():
        slot_ref[0] = 0
        for c in weight_copies(e, 0):
            c.start()

    @pl.when(used & ((b == 0) | (e != be_ref[jnp.maximum(b - 1, 0)])))
    def _():
        slot = slot_ref[0]
        for c in weight_copies(e, slot):
            c.wait()
        nxt = nx_ref[e]

        @pl.when(nxt >= 0)
        def _():
            for c in weight_copies(nxt, 1 - slot):
                c.start()

        wgb[...] = wgf[slot].astype(BF16)
        wub[...] = wuf[slot].astype(BF16)
        wdb[...] = wdf[slot].astype(BF16)
        slot_ref[0] = 1 - slot

    @pl.when(used)
    def _():
        xb = x_ref[...].astype(BF16)
        gate = _dot(xb, wgb[...])
        up = _dot(xb, wub[...])
        act = (gate * jax.nn.sigmoid(gate) * up).astype(BF16)
        y_ref[...] = _dot(act, wdb[...])

    @pl.when(jnp.logical_not(used))
    def _():
        y_ref[...] = jnp.zeros_like(y_ref)


def _moe_ffn(blk_e, n_used, next_e, xg, w_gate, w_up, w_down):
    n_rows, d = xg.shape
    ff = w_gate.shape[2]
    n_blk = n_rows // MOE_ROWS
    def row_blk(b, be, nu, nx):
        return jnp.minimum(b, nu[0] - 1), 0

    grid_spec = pltpu.PrefetchScalarGridSpec(
        num_scalar_prefetch=3,
        grid=(n_blk,),
        in_specs=[pl.BlockSpec((MOE_ROWS, d), row_blk),
                  pl.BlockSpec(memory_space=pl.ANY),
                  pl.BlockSpec(memory_space=pl.ANY),
                  pl.BlockSpec(memory_space=pl.ANY)],
        out_specs=pl.BlockSpec((MOE_ROWS, d), lambda b, be, nu, nx: (b, 0)),
        scratch_shapes=[pltpu.VMEM((2, d, ff), F32), pltpu.VMEM((2, d, ff), F32),
                        pltpu.VMEM((2, ff, d), F32),
                        pltpu.VMEM((d, ff), BF16), pltpu.VMEM((d, ff), BF16),
                        pltpu.VMEM((ff, d), BF16),
                        pltpu.SMEM((1,), I32), pltpu.SemaphoreType.DMA((2, 3))],
    )
    return pl.pallas_call(
        _moe_ffn_kernel,
        grid_spec=grid_spec,
        out_shape=jax.ShapeDtypeStruct((n_rows, d), F32),
        compiler_params=_params("arbitrary"),
        name="moe_ffn",
    )(blk_e, n_used, next_e, xg, w_gate, w_up, w_down)


def _combine_kernel(d0_ref, d1_ref, h_ref, r_ref, y_ref, o_ref, buf, sem):
    tm = h_ref.shape[0]
    base = pl.program_id(0) * tm

    def row_copy(k, t, d):
        return pltpu.make_async_copy(y_ref.at[pl.ds(d, 1), :], buf.at[k, pl.ds(t, 1), :], sem)

    def issue(t, carry):
        row_copy(0, t, d0_ref[base + t]).start()
        row_copy(1, t, d1_ref[base + t]).start()
        return carry

    lax.fori_loop(0, tm, issue, 0, unroll=ROW_DMA_UNROLL)
    for k in range(2):
        pltpu.make_async_copy(y_ref.at[pl.ds(0, tm), :], buf.at[k], sem).wait()
    r = r_ref[...]
    o_ref[...] = (h_ref[...] + r[:, ROUTE_WT:ROUTE_WT + 1] * buf[0]
                  + r[:, ROUTE_WT + 1:ROUTE_WT + 2] * buf[1])


def _combine(dest, h, routed, y, tm):
    s, d = h.shape
    grid_spec = pltpu.PrefetchScalarGridSpec(
        num_scalar_prefetch=2,
        grid=(s // tm,),
        in_specs=[pl.BlockSpec((tm, d), lambda i, d0, d1: (i, 0)),
                  pl.BlockSpec((tm, LANES), lambda i, d0, d1: (i, 0)),
                  pl.BlockSpec(memory_space=pl.ANY)],
        out_specs=pl.BlockSpec((tm, d), lambda i, d0, d1: (i, 0)),
        scratch_shapes=[pltpu.VMEM((2, tm, d), F32), pltpu.SemaphoreType.DMA(())],
    )
    return pl.pallas_call(
        _combine_kernel,
        grid_spec=grid_spec,
        out_shape=jax.ShapeDtypeStruct((s, d), F32),
        compiler_params=_params("arbitrary"),
        name="moe_combine",
    )(dest[0], dest[1], h, routed, y)


def _layer(h, mem, pos, norm_mix_g, w_in, gate_b, conv_w, conv_b, out_g, moba_q_g, moba_k_g, w_out,
           norm_cross_g, norm_mem_g, xa_wq, xa_wkv, xa_q_g, xa_k_g, xa_wo, norm_ffn_g,
           router_group_w, router_group_b, router_expert_w, router_expert_b,
           exp_w_gate, exp_w_up, exp_w_down):
    s, d = h.shape
    row = lambda v: v.reshape(1, -1)

    n_gate = 2 * MLSTM_HEADS
    gate_lo = 2 * MLSTM_HEADS * MLSTM_QK + 2 * MLSTM_HEADS * MLSTM_V
    proj_tn = 512
    w_a = w_in[:, :gate_lo].astype(BF16)
    w_b = w_in[:, gate_lo + n_gate:].astype(BF16)
    w_g = jnp.pad(w_in[:, gate_lo:gate_lo + n_gate], ((0, 0), (0, proj_tn - n_gate))).astype(BF16)
    proj = _proj_in(h, row(norm_mix_g), w_a, w_b, w_g, tm=min(s, 1024))

    gate_b_row = jnp.pad(gate_b, (0, LANES - n_gate)).reshape(1, LANES)
    hm = _mlstm(proj, w_a.shape[1] + w_b.shape[1], gate_b_row, conv_w, row(conv_b), row(out_g))
    qn, kn, vb, kmean = _moba_prep(proj, gate_lo, pos, row(moba_q_g), row(moba_k_g))
    ha = _moba_attn(qn, kn, vb, kmean)
    h = _mix_out(hm, ha, w_out, h, tm=512)

    kv = _mem_kv(mem, row(norm_mem_g), xa_wkv, row(xa_k_g))
    h = _xattn(h, row(norm_cross_g), xa_wq, kv, row(xa_q_g), xa_wo, tm=512)

    w_route = jnp.concatenate(
        [router_group_w, router_expert_w,
         jnp.zeros((d, LANES - MOE_GROUPS - MOE_EXPERTS), router_group_w.dtype)], axis=1)
    w_route_hi = w_route.astype(BF16)
    w_route = jnp.stack([w_route_hi, (w_route - w_route_hi.astype(F32)).astype(BF16)])
    b_route = jnp.pad(jnp.concatenate([router_group_b, router_expert_b]),
                      (0, LANES - MOE_GROUPS - MOE_EXPERTS)).reshape(1, LANES)
    routed, counts = _router(h, row(norm_ffn_g), w_route, b_route, tm=512)

    cnt = counts[0, MOE_GROUPS:MOE_GROUPS + MOE_EXPERTS].astype(I32)
    padded = (cnt + MOE_ROWS - 1) // MOE_ROWS * MOE_ROWS
    pend = jnp.cumsum(padded)
    seg_start = (pend - padded).astype(F32)[None, :]
    expert_ids = jnp.arange(MOE_EXPERTS, dtype=F32)[None, :]

    def dest_of(k):
        mine = routed[:, ROUTE_EID + k:ROUTE_EID + k + 1] == expert_ids
        return (jnp.sum(jnp.where(mine, seg_start, 0.0), axis=1) + routed[:, ROUTE_RANK + k]).astype(I32)

    dest = (dest_of(0), dest_of(1))
    n_blk = -(-2 * s // MOE_ROWS) + MOE_EXPERTS
    blk_start = jnp.arange(n_blk, dtype=I32) * MOE_ROWS
    blk_e = jnp.minimum(jnp.sum((pend[None, :] <= blk_start[:, None]).astype(I32), axis=1),
                        MOE_EXPERTS - 1)
    n_used = (pend[-1:] // MOE_ROWS).astype(I32)
    ids = jnp.arange(MOE_EXPERTS, dtype=I32)
    later_used = (ids[None, :] > ids[:, None]) & (cnt[None, :] > 0)
    next_e = jnp.min(jnp.where(later_used, ids[None, :], MOE_EXPERTS), axis=1)
    next_e = jnp.where(next_e < MOE_EXPERTS, next_e, -1).astype(I32)

    is_blk_e = blk_e[:, None] == ids[None, :]
    seg_end = jnp.sum(jnp.where(is_blk_e, (pend - padded + cnt)[None, :], 0), axis=1)
    zero_blk = (blk_start + MOE_ROWS > seg_end).astype(I32)
    xg = _dispatch(dest, zero_blk, h, row(norm_ffn_g), tm=256)
    y = _moe_ffn(blk_e, n_used, next_e, xg, exp_w_gate, exp_w_up, exp_w_down)
    return _combine(dest, h, routed, y, tm=256)


def kernel(x, mem, positions, norm_mix_g, w_in, mlstm_gate_b, mlstm_conv_w, mlstm_conv_b, mlstm_out_g, moba_q_g, moba_k_g, w_out, norm_cross_g, norm_mem_g, xa_wq, xa_wkv, xa_q_g, xa_k_g, xa_wo, norm_ffn_g, router_group_w, router_group_b, router_expert_w, router_expert_b, exp_w_gate, exp_w_up, exp_w_down):
    bsz, s, _ = x.shape
    assert bsz == 1, "single-sequence prefill only"
    per_layer = (norm_mix_g, w_in, mlstm_gate_b, mlstm_conv_w, mlstm_conv_b, mlstm_out_g, moba_q_g,
                 moba_k_g, w_out, norm_cross_g, norm_mem_g, xa_wq, xa_wkv, xa_q_g, xa_k_g, xa_wo,
                 norm_ffn_g, router_group_w, router_group_b, router_expert_w, router_expert_b,
                 exp_w_gate, exp_w_up, exp_w_down)
    h = x[0]
    pos = positions.reshape(s, 1)
    for l in range(norm_mix_g.shape[0]):
        h = _layer(h, mem[0], pos, *(p[l] for p in per_layer))
    return h[None]
```

```python
import functools
import math

import jax
import jax.numpy as jnp
from jax import lax
from jax.experimental import pallas as pl
from jax.experimental.pallas import tpu as pltpu

F32 = jnp.float32
BF16 = jnp.bfloat16
I32 = jnp.int32

EPS = 1e-6
LANES = 128
SUBLANES = 8
VMEM_LIMIT = 56 * 1024 * 1024

MLSTM_HEADS = 4
MLSTM_QK = 128
MLSTM_V = 256
MLSTM_CHUNK = 128
MLSTM_CONV = 4
MOBA_HEADS = 8
MOBA_HD = 128
MOBA_BLOCK = 256
MOBA_TOPK = 3
ROPE_DIM = 32
ROPE_THETA = 500000.0
XA_HEADS = 4
MOE_GROUPS = 8
MOE_PER_GROUP = 8
MOE_EXPERTS = MOE_GROUPS * MOE_PER_GROUP
MOE_ROWS = 128
ROW_DMA_UNROLL = 8

NT_DIMS = (((1,), (1,)), ((), ()))


def _params(*sem):
    return pltpu.CompilerParams(dimension_semantics=sem, vmem_limit_bytes=VMEM_LIMIT)


def _rms(x, g):
    return x * lax.rsqrt(jnp.mean(x * x, axis=-1, keepdims=True) + EPS) * g


def _dot(a, b):
    return jnp.dot(a, b, preferred_element_type=F32)


def _dot_nt(a, b, precision=None):
    return lax.dot_general(a, b, NT_DIMS, precision=precision, preferred_element_type=F32)


def _proj_in_kernel(x_ref, g_ref, wa_ref, wb_ref, wg_ref, o_ref, xn_ref, *, na, nb):
    j = pl.program_id(1)

    @pl.when(j == 0)
    def _():
        xn_ref[...] = _rms(x_ref[...], g_ref[...]).astype(BF16)

    @pl.when(j < na)
    def _():
        o_ref[...] = _dot(xn_ref[...], wa_ref[...])

    @pl.when((j >= na) & (j < na + nb))
    def _():
        o_ref[...] = _dot(xn_ref[...], wb_ref[...])

    @pl.when(j == na + nb)
    def _():
        o_ref[...] = _dot(xn_ref[...], wg_ref[...])


def _proj_in(x, g, w_a, w_b, w_g, tm):
    s, d = x.shape
    tn = w_g.shape[1]
    na, nb = w_a.shape[1] // tn, w_b.shape[1] // tn
    return pl.pallas_call(
        functools.partial(_proj_in_kernel, na=na, nb=nb),
        grid=(s // tm, na + nb + 1),
        in_specs=[pl.BlockSpec((tm, d), lambda i, j: (i, 0)),
                  pl.BlockSpec((1, d), lambda i, j: (0, 0)),
                  pl.BlockSpec((d, tn), lambda i, j: (0, jnp.minimum(j, na - 1))),
                  pl.BlockSpec((d, tn), lambda i, j: (0, jnp.clip(j - na, 0, nb - 1))),
                  pl.BlockSpec((d, tn), lambda i, j: (0, 0))],
        out_specs=pl.BlockSpec((tm, tn), lambda i, j: (i, j)),
        out_shape=jax.ShapeDtypeStruct((s, (na + nb + 1) * tn), F32),
        scratch_shapes=[pltpu.VMEM((tm, d), BF16)],
        compiler_params=_params("parallel", "arbitrary"),
        name="proj_in",
    )(x, g, w_a, w_b, w_g)


def _split3(x):
    hi = x.astype(BF16)
    r = x - hi.astype(F32)
    mid = r.astype(BF16)
    lo = (r - mid.astype(F32)).astype(BF16)
    return hi, mid, lo


def _log_sigmoid(x):
    return jnp.minimum(x, 0.0) - jnp.log(1.0 + jnp.exp(-jnp.abs(x)))


def _mlstm_kernel(q_ref, k_ref, v_ref, og_ref, gt_ref, gb_ref, cw_ref, cb_ref, outg_ref, hm_ref,
                  qext, kext, c_scr, n_scr, m_scr):
    L = MLSTM_CHUNK
    pad = SUBLANES
    qkw = MLSTM_HEADS * MLSTM_QK

    @pl.when(pl.program_id(0) == 0)
    def _():
        qext[0:pad, :] = jnp.zeros((pad, qkw), F32)
        kext[0:pad, :] = jnp.zeros((pad, qkw), F32)
        c_scr[...] = jnp.zeros_like(c_scr)
        n_scr[...] = jnp.zeros_like(n_scr)
        m_scr[...] = jnp.zeros_like(m_scr)

    qext[pad:pad + L, :] = q_ref[...]
    kext[pad:pad + L, :] = k_ref[...]

    def conv_silu(ext, lo):
        acc = jnp.broadcast_to(cb_ref[:, lo:lo + qkw], (L, qkw))
        for j in range(MLSTM_CONV):
            shift = MLSTM_CONV - 1 - j
            acc = acc + cw_ref[j:j + 1, lo:lo + qkw] * ext[pad - shift:pad - shift + L, :]
        return acc * jax.nn.sigmoid(acc)

    qc = conv_silu(qext, 0)
    kc = conv_silu(kext, qkw)
    qext[0:pad, :] = qext[L:L + pad, :]
    kext[0:pad, :] = kext[L:L + pad, :]

    gts = gt_ref[...] + gb_ref[...]
    gtr = gts.T
    row = lax.broadcasted_iota(I32, (L, L), 0)
    col = lax.broadcasted_iota(I32, (L, L), 1)
    causal = col <= row
    tri_lo = jnp.where(causal, 1.0, 0.0).astype(BF16)
    tri_up = jnp.where(row <= col, 1.0, 0.0).astype(BF16)
    bc_all = sum(_dot(tri_lo, part) for part in _split3(_log_sigmoid(gts)))
    br_all = sum(_dot(part, tri_up) for part in _split3(_log_sigmoid(gtr)))

    for h in range(MLSTM_HEADS):
        fi = MLSTM_HEADS + h
        m_prev = m_scr[h, 0:1, 0:1]
        b_c = bc_all[:, fi:fi + 1]
        b_r = br_all[fi:fi + 1, :]
        i_c = gts[:, h:h + 1]
        i_r = gtr[h:h + 1, :]
        g = b_c[L - 1:L, :]
        d = jnp.where(causal, (b_c - b_r) + i_r, -jnp.inf)
        inter = b_c + m_prev
        m_t = jnp.maximum(inter, jnp.max(d, axis=-1, keepdims=True))
        w_inter = jnp.exp(inter - m_t)
        qh = qc[:, h * MLSTM_QK:(h + 1) * MLSTM_QK] * (MLSTM_QK ** -0.5)
        kh = kc[:, h * MLSTM_QK:(h + 1) * MLSTM_QK]
        vb = v_ref[:, h * MLSTM_V:(h + 1) * MLSTM_V].astype(BF16)
        qb = qh.astype(BF16)
        p = jnp.exp(d - m_t) * _dot_nt(qb, kh.astype(BF16))
        c_prev = c_scr[h]
        n_prev = n_scr[h, 0:1, :]
        num = w_inter * _dot(qb, c_prev.astype(BF16)) + _dot(p.astype(BF16), vb)
        den = (w_inter * jnp.sum(qh * n_prev, axis=-1, keepdims=True)
               + jnp.sum(p, axis=-1, keepdims=True))
        hh = num * (1.0 / jnp.maximum(jnp.abs(den), jnp.exp(-m_t)))

        a = (g - b_c) + i_c
        m_new = jnp.maximum(g + m_prev, jnp.max(a, axis=0, keepdims=True))
        kw = kh * jnp.exp(a - m_new)
        decay = jnp.exp(g + m_prev - m_new)
        c_scr[h] = decay * c_prev + _dot(kw.T.astype(BF16), vb)
        n_scr[h, 0:1, :] = decay * n_prev + jnp.sum(kw, axis=0, keepdims=True)
        m_scr[h] = jnp.broadcast_to(m_new, (SUBLANES, LANES))

        hn = hh * lax.rsqrt(jnp.mean(hh * hh, axis=-1, keepdims=True) + EPS)
        vs = slice(h * MLSTM_V, (h + 1) * MLSTM_V)
        hn = hn * outg_ref[:, vs] * jax.nn.sigmoid(og_ref[:, vs])
        hm_ref[:, vs] = hn.astype(hm_ref.dtype)


def _mlstm(proj, gate_col, gate_b, conv_w, conv_b, out_g):
    s = proj.shape[0]
    L = MLSTM_CHUNK
    qkw = MLSTM_HEADS * MLSTM_QK
    vw = MLSTM_HEADS * MLSTM_V
    gate_blk = gate_col // LANES
    return pl.pallas_call(
        _mlstm_kernel,
        grid=(s // L,),
        in_specs=[pl.BlockSpec((L, qkw), lambda c: (c, 0)),
                  pl.BlockSpec((L, qkw), lambda c: (c, 1)),
                  pl.BlockSpec((L, vw), lambda c: (c, 1)),
                  pl.BlockSpec((L, vw), lambda c: (c, 2)),
                  pl.BlockSpec((L, LANES), lambda c: (c, gate_blk)),
                  pl.BlockSpec((1, LANES), lambda c: (0, 0)),
                  pl.BlockSpec((MLSTM_CONV, 2 * qkw), lambda c: (0, 0)),
                  pl.BlockSpec((1, 2 * qkw), lambda c: (0, 0)),
                  pl.BlockSpec((1, vw), lambda c: (0, 0))],
        out_specs=pl.BlockSpec((L, vw), lambda c: (c, 0)),
        out_shape=jax.ShapeDtypeStruct((s, vw), BF16),
        scratch_shapes=[pltpu.VMEM((L + SUBLANES, qkw), F32),
                        pltpu.VMEM((L + SUBLANES, qkw), F32),
                        pltpu.VMEM((MLSTM_HEADS, MLSTM_QK, MLSTM_V), F32),
                        pltpu.VMEM((MLSTM_HEADS, SUBLANES, LANES), F32),
                        pltpu.VMEM((MLSTM_HEADS, SUBLANES, LANES), F32)],
        compiler_params=_params("arbitrary"),
        name="mlstm",
    )(proj, proj, proj, proj, proj, gate_b, conv_w, conv_b, out_g)


def _moba_prep_kernel(q_ref, k_ref, v_ref, pos_ref, qg_ref, kg_ref, qt_ref, kn_ref, vt_ref, km_ref):
    rows = q_ref.shape[0]
    half = ROPE_DIM // 2
    lane = lax.broadcasted_iota(I32, (1, MOBA_HD), 1)
    inv_freq = jnp.exp((lane & (half - 1)).astype(F32) * (-(2.0 / ROPE_DIM) * math.log(ROPE_THETA)))
    ang = pos_ref[...].astype(F32) * inv_freq
    cos = jnp.where(lane < ROPE_DIM, jnp.cos(ang), 1.0)
    sin = jnp.sin(ang)
    sin = jnp.where(lane < half, -sin, jnp.where(lane < ROPE_DIM, sin, 0.0))

    def rope(x):
        partner = jnp.where(lane < half, pltpu.roll(x, MOBA_HD - half, 1), pltpu.roll(x, half, 1))
        return x * cos + partner * sin

    for h in range(MOBA_HEADS):
        hs = slice(h * MOBA_HD, (h + 1) * MOBA_HD)
        qt_ref[hs, :] = rope(_rms(q_ref[:, hs], qg_ref[...])).T
        kn = rope(_rms(k_ref[:, hs], kg_ref[...]))
        kn_ref[:, hs] = kn.astype(BF16)
        km_ref[0, :, hs] = jnp.sum(kn, axis=0, keepdims=True) * (1.0 / rows)
        vt_ref[0, hs, :] = v_ref[:, hs].T.astype(BF16)


def _moba_prep(proj, q_col, pos, q_g, k_g):
    s = proj.shape[0]
    w = MOBA_HEADS * MOBA_HD
    bs = MOBA_BLOCK
    nb = s // bs
    first = q_col // w
    return pl.pallas_call(
        _moba_prep_kernel,
        grid=(nb,),
        in_specs=[pl.BlockSpec((bs, w), lambda i: (i, first)),
                  pl.BlockSpec((bs, w), lambda i: (i, first + 1)),
                  pl.BlockSpec((bs, w), lambda i: (i, first + 2)),
                  pl.BlockSpec((bs, 1), lambda i: (i, 0)),
                  pl.BlockSpec((1, MOBA_HD), lambda i: (0, 0)),
                  pl.BlockSpec((1, MOBA_HD), lambda i: (0, 0))],
        out_specs=[pl.BlockSpec((w, bs), lambda i: (0, i)),
                   pl.BlockSpec((bs, w), lambda i: (i, 0)),
                   pl.BlockSpec((1, w, bs), lambda i: (i, 0, 0)),
                   pl.BlockSpec((1, 1, w), lambda i: (i, 0, 0))],
        out_shape=[jax.ShapeDtypeStruct((w, s), F32),
                   jax.ShapeDtypeStruct((s, w), BF16),
                   jax.ShapeDtypeStruct((nb, w, bs), BF16),
                   jax.ShapeDtypeStruct((nb, 1, w), F32)],
        compiler_params=_params("parallel"),
        name="moba_prep",
    )(proj, proj, proj, pos, q_g, k_g)


def _moba_attn_kernel(qt_ref, k_ref, vt_ref, km_ref, o_ref, bias_scr, qb_scr, m_scr, l_scr, *tiles):
    i = pl.program_id(1)
    bs = MOBA_BLOCK
    hd = MOBA_HD
    nh = len(tiles) // 3
    acc_scr = tiles[:nh]
    s_scr = tiles[nh:2 * nh]
    p_scr = tiles[2 * nh:]
    nb = km_ref.shape[0]
    c2 = (hd ** -0.5) * math.log2(math.e)
    blk = lax.broadcasted_iota(I32, (nb, bs), 0)
    blk_f = blk.astype(F32)
    valid = blk < i
    key = lax.broadcasted_iota(I32, (bs, bs), 0)
    qry = lax.broadcasted_iota(I32, (bs, bs), 1)
    own = pl.ds(pl.multiple_of(i * bs, bs), bs)

    for h in range(nh):
        hs = slice(h * hd, (h + 1) * hd)
        qt = qt_ref[hs, :]
        gate = jnp.dot(km_ref[:, 0, hs], qt, precision=lax.Precision.HIGHEST,
                       preferred_element_type=F32)
        cand = valid
        for _ in range(MOBA_TOPK):
            gmax = jnp.max(jnp.where(cand, gate, -jnp.inf), axis=0, keepdims=True)
            hit = cand & (gate == gmax)
            first = jnp.min(jnp.where(hit, blk_f, float(nb)), axis=0, keepdims=True)
            cand = cand & jnp.logical_not(hit & (blk_f == first))
        bias_scr[h] = jnp.where(valid & jnp.logical_not(cand), 0.0, -jnp.inf)

        qb = (qt * c2).astype(BF16)
        qb_scr[h] = qb
        s0 = jnp.where(key <= qry, _dot(k_ref[own, hs], qb), -jnp.inf)
        m0 = jnp.max(s0, axis=0, keepdims=True)
        p0 = jnp.exp2(s0 - m0)
        m_scr[h] = m0
        l_scr[h] = jnp.sum(p0, axis=0, keepdims=True)
        acc_scr[h][...] = _dot(vt_ref[i, hs, :], p0.astype(BF16))

    ck = 32

    def body(j, carry):
        rows = pl.ds(pl.multiple_of(j * bs, bs), bs)
        for h in range(nh):
            s_scr[h][...] = _dot(k_ref[rows, h * hd:(h + 1) * hd], qb_scr[h])
        alphas = []
        for h in range(nh):
            bias = bias_scr[h, pl.ds(j, 1), :]
            cmax = s_scr[h][0:ck, :]
            for c in range(1, bs // ck):
                cmax = jnp.maximum(cmax, s_scr[h][c * ck:(c + 1) * ck, :])
            m = m_scr[h]
            m_new = jnp.maximum(m, jnp.max(cmax, axis=0, keepdims=True) + bias)
            alpha = jnp.exp2(m - m_new)
            shift = m_new - bias
            psum = jnp.zeros((ck, bs), F32)
            for c in range(bs // ck):
                p = jnp.exp2(s_scr[h][c * ck:(c + 1) * ck, :] - shift)
                psum = psum + p
                p_scr[h][c * ck:(c + 1) * ck, :] = p.astype(BF16)
            m_scr[h] = m_new
            l_scr[h] = alpha * l_scr[h] + jnp.sum(psum, axis=0, keepdims=True)
            alphas.append(alpha)
        for h in range(nh):
            acc_scr[h][...] = (alphas[h] * acc_scr[h][...]
                               + _dot(vt_ref[j, h * hd:(h + 1) * hd, :], p_scr[h][...]))
        return carry

    lax.fori_loop(0, i, body, 0)
    for h in range(nh):
        out_t = acc_scr[h][...] * (1.0 / l_scr[h])
        o_ref[:, h * hd:(h + 1) * hd] = out_t.T.astype(o_ref.dtype)


MOBA_HEADS_PER_STEP = 8


def _moba_attn(qt, kn, vt, kmean):
    w, s = qt.shape
    bs = MOBA_BLOCK
    nb = s // bs
    nh = MOBA_HEADS_PER_STEP
    gw = nh * MOBA_HD
    resident = pl.Buffered(1)
    return pl.pallas_call(
        _moba_attn_kernel,
        grid=(MOBA_HEADS // nh, nb),
        in_specs=[pl.BlockSpec((gw, bs), lambda g, i: (g, i)),
                  pl.BlockSpec((s, gw), lambda g, i: (0, g), pipeline_mode=resident),
                  pl.BlockSpec((nb, gw, bs), lambda g, i: (0, g, 0), pipeline_mode=resident),
                  pl.BlockSpec((nb, 1, gw), lambda g, i: (0, 0, g))],
        out_specs=pl.BlockSpec((bs, gw), lambda g, i: (i, g)),
        out_shape=jax.ShapeDtypeStruct((s, w), BF16),
        scratch_shapes=[pltpu.VMEM((nh, nb, bs), F32),
                        pltpu.VMEM((nh, MOBA_HD, bs), BF16),
                        pltpu.VMEM((nh, 1, bs), F32),
                        pltpu.VMEM((nh, 1, bs), F32)]
                       + [pltpu.VMEM((MOBA_HD, bs), F32) for _ in range(nh)]
                       + [pltpu.VMEM((bs, bs), F32) for _ in range(nh)]
                       + [pltpu.VMEM((bs, bs), BF16) for _ in range(nh)],
        compiler_params=_params("parallel", "arbitrary"),
        name="moba_attn",
    )(qt, kn, vt, kmean)


def _mix_out_kernel(hm_ref, ha_ref, w_ref, x_ref, o_ref, wb_ref):
    half = hm_ref.shape[1]

    @pl.when(pl.program_id(0) == 0)
    def _():
        wb_ref[...] = w_ref[...].astype(BF16)

    o_ref[...] = (x_ref[...] + _dot(hm_ref[...], wb_ref[0:half, :])
                  + _dot(ha_ref[...], wb_ref[half:2 * half, :]))


def _mix_out(hm, ha, w, x, tm):
    s, d = x.shape
    half = hm.shape[1]
    return pl.pallas_call(
        _mix_out_kernel,
        grid=(s // tm,),
        in_specs=[pl.BlockSpec((tm, half), lambda i: (i, 0)),
                  pl.BlockSpec((tm, half), lambda i: (i, 0)),
                  pl.BlockSpec((2 * half, d), lambda i: (0, 0), pipeline_mode=pl.Buffered(1)),
                  pl.BlockSpec((tm, d), lambda i: (i, 0))],
        out_specs=pl.BlockSpec((tm, d), lambda i: (i, 0)),
        out_shape=jax.ShapeDtypeStruct((s, d), F32),
        scratch_shapes=[pltpu.VMEM((2 * half, d), BF16)],
        compiler_params=_params("arbitrary"),
        name="mix_out",
    )(hm, ha, w, x)


def _mem_kv_kernel(mem_ref, g_ref, w_ref, kg_ref, o_ref, mn_ref):
    j = pl.program_id(0)

    @pl.when(j == 0)
    def _():
        mn_ref[...] = _rms(mem_ref[...], g_ref[...]).astype(BF16)

    y = _dot(mn_ref[...], w_ref[...].astype(BF16))
    o_ref[...] = jnp.where(j < XA_HEADS, _rms(y, kg_ref[...]), y).astype(o_ref.dtype)


def _mem_kv(mem, g, wkv, k_g):
    m, d = mem.shape
    hd = d // XA_HEADS
    return pl.pallas_call(
        _mem_kv_kernel,
        grid=(2 * XA_HEADS,),
        in_specs=[pl.BlockSpec((m, d), lambda j: (0, 0)),
                  pl.BlockSpec((1, d), lambda j: (0, 0)),
                  pl.BlockSpec((d, hd), lambda j: (0, j)),
                  pl.BlockSpec((1, hd), lambda j: (0, 0))],
        out_specs=pl.BlockSpec((m, hd), lambda j: (0, j)),
        out_shape=jax.ShapeDtypeStruct((m, 2 * d), BF16),
        scratch_shapes=[pltpu.VMEM((m, d), BF16)],
        compiler_params=_params("arbitrary"),
        name="mem_kv",
    )(mem, g, wkv, k_g)


def _xattn_kernel(h_ref, g_ref, wq_hbm, kv_ref, qg_ref, wo_hbm, o_ref, wqb, wob, stage, o_all, sem):
    d = h_ref.shape[1]
    hd = d // XA_HEADS
    n_slab = 2 * XA_HEADS

    @pl.when(pl.program_id(0) == 0)
    def _():
        def slab_copy(k):
            src = wq_hbm if k < XA_HEADS else wo_hbm
            cols = pl.ds((k % XA_HEADS) * hd, hd)
            return pltpu.make_async_copy(src.at[:, cols], stage.at[k % 2], sem.at[k % 2])

        slab_copy(0).start()
        for k in range(n_slab):
            if k + 1 < n_slab:
                slab_copy(k + 1).start()
            slab_copy(k).wait()
            dst = wqb if k < XA_HEADS else wob
            dst[:, (k % XA_HEADS) * hd:(k % XA_HEADS + 1) * hd] = stage[k % 2].astype(BF16)

    h = h_ref[...]
    q_all = _dot(_rms(h, g_ref[...]).astype(BF16), wqb[...])
    for j in range(XA_HEADS):
        hs = slice(j * hd, (j + 1) * hd)
        q = _rms(q_all[:, hs], qg_ref[...]).astype(BF16)
        sc = _dot_nt(q, kv_ref[:, hs]) * (hd ** -0.5)
        p = jnp.exp(sc - jnp.max(sc, axis=-1, keepdims=True))
        p = p * (1.0 / jnp.sum(p, axis=-1, keepdims=True))
        o_all[:, hs] = _dot(p.astype(BF16), kv_ref[:, d + j * hd:d + (j + 1) * hd]).astype(BF16)
    o_ref[...] = h + _dot(o_all[...], wob[...])


def _xattn(h, g, wq, kv, q_g, wo, tm):
    s, d = h.shape
    m = kv.shape[0]
    hd = d // XA_HEADS
    return pl.pallas_call(
        _xattn_kernel,
        grid=(s // tm,),
        in_specs=[pl.BlockSpec((tm, d), lambda i: (i, 0)),
                  pl.BlockSpec((1, d), lambda i: (0, 0)),
                  pl.BlockSpec(memory_space=pl.ANY),
                  pl.BlockSpec((m, 2 * d), lambda i: (0, 0)),
                  pl.BlockSpec((1, hd), lambda i: (0, 0)),
                  pl.BlockSpec(memory_space=pl.ANY)],
        out_specs=pl.BlockSpec((tm, d), lambda i: (i, 0)),
        out_shape=jax.ShapeDtypeStruct((s, d), F32),
        scratch_shapes=[pltpu.VMEM((d, d), BF16), pltpu.VMEM((d, d), BF16),
                        pltpu.VMEM((2, d, hd), F32), pltpu.VMEM((tm, d), BF16),
                        pltpu.SemaphoreType.DMA((2,))],
        compiler_params=_params("arbitrary"),
        name="xattn",
    )(h, g, wq, kv, q_g, wo)


ROUTE_EID = 0
ROUTE_RANK = 2
ROUTE_WT = 4


def _router_kernel(h_ref, g_ref, w_ref, b_ref, out_ref, cnt_ref, carry):
    tm = h_ref.shape[0]

    @pl.when(pl.program_id(0) == 0)
    def _():
        carry[...] = jnp.zeros_like(carry)

    hn = _rms(h_ref[...], g_ref[...])
    hi = hn.astype(BF16)
    lo = (hn - hi.astype(F32)).astype(BF16)
    logits = (_dot(hi, w_ref[0]) + _dot(hi, w_ref[1]) + _dot(lo, w_ref[0])
              + b_ref[...])
    lane = lax.broadcasted_iota(I32, (tm, LANES), 1)

    def first_lane(mask):
        return jnp.min(jnp.where(mask, lane.astype(F32), float(LANES)), axis=-1,
                       keepdims=True).astype(I32)

    is_g = lane < MOE_GROUPS
    gmax = jnp.max(jnp.where(is_g, logits, -jnp.inf), axis=-1, keepdims=True)
    gsum = jnp.sum(jnp.where(is_g, jnp.exp(logits - gmax), 0.0), axis=-1, keepdims=True)
    gsel = first_lane(is_g & (logits == gmax))
    pg = 1.0 / gsum

    grp_lo = MOE_GROUPS + MOE_PER_GROUP * gsel
    in_grp = (lane >= grp_lo) & (lane < grp_lo + MOE_PER_GROUP)
    emax = jnp.max(jnp.where(in_grp, logits, -jnp.inf), axis=-1, keepdims=True)
    eexp = jnp.where(in_grp, jnp.exp(logits - emax), 0.0)
    eprob = eexp / jnp.sum(eexp, axis=-1, keepdims=True)
    p1 = jnp.max(jnp.where(in_grp, eprob, -1.0), axis=-1, keepdims=True)
    l1 = first_lane(in_grp & (eprob == p1))
    rest = in_grp & (lane != l1)
    p2 = jnp.max(jnp.where(rest, eprob, -1.0), axis=-1, keepdims=True)
    l2 = first_lane(rest & (eprob == p2))
    psum = p1 + p2
    w1 = pg * p1 / psum
    w2 = pg * p2 / psum

    oh1 = lane == l1
    oh2 = lane == l2
    onehot = jnp.where(oh1 | oh2, 1.0, 0.0)
    r = lax.broadcasted_iota(I32, (tm, tm), 0)
    c = lax.broadcasted_iota(I32, (tm, tm), 1)
    strict = jnp.where(c < r, 1.0, 0.0).astype(BF16)
    before = _dot(strict, onehot.astype(BF16)) + carry[0:1, :]
    rank1 = jnp.sum(jnp.where(oh1, before, 0.0), axis=-1, keepdims=True)
    rank2 = jnp.sum(jnp.where(oh2, before, 0.0), axis=-1, keepdims=True)
    total = carry[0:1, :] + jnp.sum(onehot, axis=0, keepdims=True)
    carry[...] = jnp.broadcast_to(total, carry.shape)
    cnt_ref[...] = jnp.broadcast_to(total, cnt_ref.shape)

    out = jnp.where(lane == ROUTE_EID, (l1 - MOE_GROUPS).astype(F32), 0.0)
    out = jnp.where(lane == ROUTE_EID + 1, (l2 - MOE_GROUPS).astype(F32), out)
    out = jnp.where(lane == ROUTE_RANK, rank1, out)
    out = jnp.where(lane == ROUTE_RANK + 1, rank2, out)
    out = jnp.where(lane == ROUTE_WT, w1, out)
    out = jnp.where(lane == ROUTE_WT + 1, w2, out)
    out_ref[...] = out


def _router(h, g, w, b, tm):
    s, d = h.shape
    return pl.pallas_call(
        _router_kernel,
        grid=(s // tm,),
        in_specs=[pl.BlockSpec((tm, d), lambda i: (i, 0)),
                  pl.BlockSpec((1, d), lambda i: (0, 0)),
                  pl.BlockSpec((2, d, LANES), lambda i: (0, 0, 0)),
                  pl.BlockSpec((1, LANES), lambda i: (0, 0))],
        out_specs=[pl.BlockSpec((tm, LANES), lambda i: (i, 0)),
                   pl.BlockSpec((SUBLANES, LANES), lambda i: (0, 0))],
        out_shape=[jax.ShapeDtypeStruct((s, LANES), F32),
                   jax.ShapeDtypeStruct((SUBLANES, LANES), F32)],
        scratch_shapes=[pltpu.VMEM((SUBLANES, LANES), F32)],
        compiler_params=_params("arbitrary"),
        name="router",
    )(h, g, w, b)


def _moe_ffn_kernel(be_ref, nu_ref, nx_ref, src_ref, h_hbm, g_ref, wg_hbm, wu_hbm, wd_hbm, y_ref,
                    xbuf, wgf, wuf, wdf, wgb, wub, wdb, slot_ref, sem, xsem):
    b = pl.program_id(0)
    e = be_ref[b]
    used = b < nu_ref[0]

    def gather_rows(blk):
        slot = blk % 2
        base = blk * MOE_ROWS

        def issue(r, carry):
            pltpu.make_async_copy(h_hbm.at[pl.ds(src_ref[base + r], 1), :],
                                  xbuf.at[slot, pl.ds(r, 1), :], xsem.at[slot]).start()
            return carry

        lax.fori_loop(0, MOE_ROWS, issue, 0, unroll=ROW_DMA_UNROLL)

    @pl.when(b == 0)
    def _():
        gather_rows(b)

    @pl.when(b + 1 < nu_ref[0])
    def _():
        gather_rows(b + 1)

    def weight_copies(expert, slot):
        return (pltpu.make_async_copy(wg_hbm.at[expert], wgf.at[slot], sem.at[slot, 0]),
                pltpu.make_async_copy(wu_hbm.at[expert], wuf.at[slot], sem.at[slot, 1]),
                pltpu.make_async_copy(wd_hbm.at[expert], wdf.at[slot], sem.at[slot, 2]))

    @pl.when(b == 0)
    def _():
        slot_ref[0] = 0
        for c in weight_copies(e, 0):
            c.start()

    @pl.when(used & ((b == 0) | (e != be_ref[jnp.maximum(b - 1, 0)])))
    def _():
        slot = slot_ref[0]
        for c in weight_copies(e, slot):
            c.wait()
        nxt = nx_ref[e]

        @pl.when(nxt >= 0)
        def _():
            for c in weight_copies(nxt, 1 - slot):
                c.start()

        wgb[...] = wgf[slot].astype(BF16)
        wub[...] = wuf[slot].astype(BF16)
        wdb[...] = wdf[slot].astype(BF16)
        slot_ref[0] = 1 - slot

    @pl.when(used)
    def _():
        slot = b % 2
        pltpu.make_async_copy(h_hbm.at[pl.ds(0, MOE_ROWS), :], xbuf.at[slot], xsem.at[slot]).wait()
        xb = _rms(xbuf[slot], g_ref[...]).astype(BF16)
        gate = _dot(xb, wgb[...])
        up = _dot(xb, wub[...])
        act = (gate * jax.nn.sigmoid(gate) * up).astype(BF16)
        y_ref[...] = _dot(act, wdb[...])

    @pl.when(jnp.logical_not(used))
    def _():
        y_ref[...] = jnp.zeros_like(y_ref)


def _row_source_kernel(d0_ref, d1_ref, src_ref):
    def clear(r, carry):
        src_ref[r] = 0
        return carry

    def put(t, carry):
        src_ref[d0_ref[t]] = t
        src_ref[d1_ref[t]] = t
        return carry

    lax.fori_loop(0, src_ref.shape[0], clear, 0, unroll=ROW_DMA_UNROLL)
    lax.fori_loop(0, d0_ref.shape[0], put, 0, unroll=ROW_DMA_UNROLL)


def _row_source(dest, n_rows):
    smem = pl.BlockSpec(memory_space=pltpu.SMEM)
    return pl.pallas_call(
        _row_source_kernel,
        in_specs=[smem, smem],
        out_specs=smem,
        out_shape=jax.ShapeDtypeStruct((n_rows,), I32),
        name="moe_row_source",
    )(dest[0], dest[1])


def _moe_ffn(blk_e, n_used, next_e, row_src, h, g, w_gate, w_up, w_down):
    s, d = h.shape
    n_rows = row_src.shape[0]
    ff = w_gate.shape[2]
    n_blk = n_rows // MOE_ROWS
    grid_spec = pltpu.PrefetchScalarGridSpec(
        num_scalar_prefetch=4,
        grid=(n_blk,),
        in_specs=[pl.BlockSpec(memory_space=pl.ANY),
                  pl.BlockSpec((1, d), lambda b, be, nu, nx, src: (0, 0)),
                  pl.BlockSpec(memory_space=pl.ANY),
                  pl.BlockSpec(memory_space=pl.ANY),
                  pl.BlockSpec(memory_space=pl.ANY)],
        out_specs=pl.BlockSpec((MOE_ROWS, d), lambda b, be, nu, nx, src: (b, 0)),
        scratch_shapes=[pltpu.VMEM((2, MOE_ROWS, d), F32),
                        pltpu.VMEM((2, d, ff), F32), pltpu.VMEM((2, d, ff), F32),
                        pltpu.VMEM((2, ff, d), F32),
                        pltpu.VMEM((d, ff), BF16), pltpu.VMEM((d, ff), BF16),
                        pltpu.VMEM((ff, d), BF16),
                        pltpu.SMEM((1,), I32), pltpu.SemaphoreType.DMA((2, 3)),
                        pltpu.SemaphoreType.DMA((2,))],
    )
    return pl.pallas_call(
        _moe_ffn_kernel,
        grid_spec=grid_spec,
        out_shape=jax.ShapeDtypeStruct((n_rows, d), F32),
        compiler_params=_params("arbitrary"),
        name="moe_ffn",
    )(blk_e, n_used, next_e, row_src, h, g, w_gate, w_up, w_down)


def _combine_kernel(d0_ref, d1_ref, h_ref, r_ref, y_ref, o_ref, buf, sem):
    tm = h_ref.shape[0]
    base = pl.program_id(0) * tm

    def row_copy(k, t, d):
        return pltpu.make_async_copy(y_ref.at[pl.ds(d, 1), :], buf.at[k, pl.ds(t, 1), :], sem)

    def issue(t, carry):
        row_copy(0, t, d0_ref[base + t]).start()
        row_copy(1, t, d1_ref[base + t]).start()
        return carry

    lax.fori_loop(0, tm, issue, 0, unroll=ROW_DMA_UNROLL)
    for k in range(2):
        pltpu.make_async_copy(y_ref.at[pl.ds(0, tm), :], buf.at[k], sem).wait()
    r = r_ref[...]
    o_ref[...] = (h_ref[...] + r[:, ROUTE_WT:ROUTE_WT + 1] * buf[0]
                  + r[:, ROUTE_WT + 1:ROUTE_WT + 2] * buf[1])


def _combine(dest, h, routed, y, tm):
    s, d = h.shape
    grid_spec = pltpu.PrefetchScalarGridSpec(
        num_scalar_prefetch=2,
        grid=(s // tm,),
        in_specs=[pl.BlockSpec((tm, d), lambda i, d0, d1: (i, 0)),
                  pl.BlockSpec((tm, LANES), lambda i, d0, d1: (i, 0)),
                  pl.BlockSpec(memory_space=pl.ANY)],
        out_specs=pl.BlockSpec((tm, d), lambda i, d0, d1: (i, 0)),
        scratch_shapes=[pltpu.VMEM((2, tm, d), F32), pltpu.SemaphoreType.DMA(())],
    )
    return pl.pallas_call(
        _combine_kernel,
        grid_spec=grid_spec,
        out_shape=jax.ShapeDtypeStruct((s, d), F32),
        compiler_params=_params("arbitrary"),
        name="moe_combine",
    )(dest[0], dest[1], h, routed, y)


def _layer(h, mem, pos, norm_mix_g, w_in, gate_b, conv_w, conv_b, out_g, moba_q_g, moba_k_g, w_out,
           norm_cross_g, norm_mem_g, xa_wq, xa_wkv, xa_q_g, xa_k_g, xa_wo, norm_ffn_g,
           router_group_w, router_group_b, router_expert_w, router_expert_b,
           exp_w_gate, exp_w_up, exp_w_down):
    s, d = h.shape
    row = lambda v: v.reshape(1, -1)

    n_gate = 2 * MLSTM_HEADS
    gate_lo = 2 * MLSTM_HEADS * MLSTM_QK + 2 * MLSTM_HEADS * MLSTM_V
    proj_tn = 512
    w_a = w_in[:, :gate_lo].astype(BF16)
    w_b = w_in[:, gate_lo + n_gate:].astype(BF16)
    w_g = jnp.pad(w_in[:, gate_lo:gate_lo + n_gate], ((0, 0), (0, proj_tn - n_gate))).astype(BF16)
    proj = _proj_in(h, row(norm_mix_g), w_a, w_b, w_g, tm=min(s, 1024))

    gate_b_row = jnp.pad(gate_b, (0, LANES - n_gate)).reshape(1, LANES)
    hm = _mlstm(proj, w_a.shape[1] + w_b.shape[1], gate_b_row, conv_w, row(conv_b), row(out_g))
    qn, kn, vb, kmean = _moba_prep(proj, gate_lo, pos, row(moba_q_g), row(moba_k_g))
    ha = _moba_attn(qn, kn, vb, kmean)
    h = _mix_out(hm, ha, w_out, h, tm=512)

    kv = _mem_kv(mem, row(norm_mem_g), xa_wkv, row(xa_k_g))
    h = _xattn(h, row(norm_cross_g), xa_wq, kv, row(xa_q_g), xa_wo, tm=512)

    w_route = jnp.concatenate(
        [router_group_w, router_expert_w,
         jnp.zeros((d, LANES - MOE_GROUPS - MOE_EXPERTS), router_group_w.dtype)], axis=1)
    w_route_hi = w_route.astype(BF16)
    w_route = jnp.stack([w_route_hi, (w_route - w_route_hi.astype(F32)).astype(BF16)])
    b_route = jnp.pad(jnp.concatenate([router_group_b, router_expert_b]),
                      (0, LANES - MOE_GROUPS - MOE_EXPERTS)).reshape(1, LANES)
    routed, counts = _router(h, row(norm_ffn_g), w_route, b_route, tm=512)

    cnt = counts[0, MOE_GROUPS:MOE_GROUPS + MOE_EXPERTS].astype(I32)
    padded = (cnt + MOE_ROWS - 1) // MOE_ROWS * MOE_ROWS
    pend = jnp.cumsum(padded)
    seg_start = (pend - padded).astype(F32)[None, :]
    expert_ids = jnp.arange(MOE_EXPERTS, dtype=F32)[None, :]

    def dest_of(k):
        mine = routed[:, ROUTE_EID + k:ROUTE_EID + k + 1] == expert_ids
        return (jnp.sum(jnp.where(mine, seg_start, 0.0), axis=1) + routed[:, ROUTE_RANK + k]).astype(I32)

    dest = (dest_of(0), dest_of(1))
    n_blk = -(-2 * s // MOE_ROWS) + MOE_EXPERTS
    blk_start = jnp.arange(n_blk, dtype=I32) * MOE_ROWS
    blk_e = jnp.minimum(jnp.sum((pend[None, :] <= blk_start[:, None]).astype(I32), axis=1),
                        MOE_EXPERTS - 1)
    n_used = (pend[-1:] // MOE_ROWS).astype(I32)
    ids = jnp.arange(MOE_EXPERTS, dtype=I32)
    later_used = (ids[None, :] > ids[:, None]) & (cnt[None, :] > 0)
    next_e = jnp.min(jnp.where(later_used, ids[None, :], MOE_EXPERTS), axis=1)
    next_e = jnp.where(next_e < MOE_EXPERTS, next_e, -1).astype(I32)

    row_src = _row_source(dest, n_blk * MOE_ROWS)
    y = _moe_ffn(blk_e, n_used, next_e, row_src, h, row(norm_ffn_g), exp_w_gate, exp_w_up, exp_w_down)
    return _combine(dest, h, routed, y, tm=256)


def kernel(x, mem, positions, norm_mix_g, w_in, mlstm_gate_b, mlstm_conv_w, mlstm_conv_b, mlstm_out_g, moba_q_g, moba_k_g, w_out, norm_cross_g, norm_mem_g, xa_wq, xa_wkv, xa_q_g, xa_k_g, xa_wo, norm_ffn_g, router_group_w, router_group_b, router_expert_w, router_expert_b, exp_w_gate, exp_w_up, exp_w_down):
    bsz, s, _ = x.shape
    assert bsz == 1, "single-sequence prefill only"
    per_layer = (norm_mix_g, w_in, mlstm_gate_b, mlstm_conv_w, mlstm_conv_b, mlstm_out_g, moba_q_g,
                 moba_k_g, w_out, norm_cross_g, norm_mem_g, xa_wq, xa_wkv, xa_q_g, xa_k_g, xa_wo,
                 norm_ffn_g, router_group_w, router_group_b, router_expert_w, router_expert_b,
                 exp_w_gate, exp_w_up, exp_w_down)
    h = x[0]
    pos = positions.reshape(s, 1)
    for l in range(norm_mix_g.shape[0]):
        h = _layer(h, mem[0], pos, *(p[l] for p in per_layer))
    return h[None]
```

```python
import functools
import math

import jax
import jax.numpy as jnp
from jax import lax
from jax.experimental import pallas as pl
from jax.experimental.pallas import tpu as pltpu

F32 = jnp.float32
BF16 = jnp.bfloat16
I32 = jnp.int32

EPS = 1e-6
LANES = 128
SUBLANES = 8
VMEM_LIMIT = 56 * 1024 * 1024

MLSTM_HEADS = 4
MLSTM_QK = 128
MLSTM_V = 256
MLSTM_CHUNK = 128
MLSTM_CONV = 4
MOBA_HEADS = 8
MOBA_HD = 128
MOBA_BLOCK = 256
MOBA_TOPK = 3
ROPE_DIM = 32
ROPE_THETA = 500000.0
XA_HEADS = 4
MOE_GROUPS = 8
MOE_PER_GROUP = 8
MOE_EXPERTS = MOE_GROUPS * MOE_PER_GROUP
MOE_ROWS = 128
ROW_DMA_UNROLL = 8

NT_DIMS = (((1,), (1,)), ((), ()))


def _params(*sem):
    return pltpu.CompilerParams(dimension_semantics=sem, vmem_limit_bytes=VMEM_LIMIT)


def _rms(x, g):
    return x * lax.rsqrt(jnp.mean(x * x, axis=-1, keepdims=True) + EPS) * g


def _dot(a, b):
    return jnp.dot(a, b, preferred_element_type=F32)


def _dot_nt(a, b, precision=None):
    return lax.dot_general(a, b, NT_DIMS, precision=precision, preferred_element_type=F32)


def _proj_in_kernel(x_ref, g_ref, wa_ref, wb_ref, wg_ref, o_ref, xn_ref, *, na, nb):
    j = pl.program_id(1)

    @pl.when(j == 0)
    def _():
        xn_ref[...] = _rms(x_ref[...], g_ref[...]).astype(BF16)

    @pl.when(j < na)
    def _():
        o_ref[...] = _dot(xn_ref[...], wa_ref[...].astype(BF16))

    @pl.when((j >= na) & (j < na + nb))
    def _():
        o_ref[...] = _dot(xn_ref[...], wb_ref[...].astype(BF16))

    @pl.when(j == na + nb)
    def _():
        o_ref[...] = _dot(xn_ref[...], wg_ref[...])


def _proj_in(x, g, w_a, a_cols, w_b, w_g, tm):
    s, d = x.shape
    tn = w_g.shape[1]
    na, nb = a_cols // tn, w_b.shape[1] // tn
    return pl.pallas_call(
        functools.partial(_proj_in_kernel, na=na, nb=nb),
        grid=(s // tm, na + nb + 1),
        in_specs=[pl.BlockSpec((tm, d), lambda i, j: (i, 0)),
                  pl.BlockSpec((1, d), lambda i, j: (0, 0)),
                  pl.BlockSpec((d, tn), lambda i, j: (0, jnp.minimum(j, na - 1))),
                  pl.BlockSpec((d, tn), lambda i, j: (0, jnp.clip(j - na, 0, nb - 1))),
                  pl.BlockSpec((d, tn), lambda i, j: (0, 0))],
        out_specs=pl.BlockSpec((tm, tn), lambda i, j: (i, j)),
        out_shape=jax.ShapeDtypeStruct((s, (na + nb + 1) * tn), F32),
        scratch_shapes=[pltpu.VMEM((tm, d), BF16)],
        compiler_params=_params("parallel", "arbitrary"),
        name="proj_in",
    )(x, g, w_a, w_b, w_g)


def _split3(x):
    hi = x.astype(BF16)
    r = x - hi.astype(F32)
    mid = r.astype(BF16)
    lo = (r - mid.astype(F32)).astype(BF16)
    return hi, mid, lo


def _log_sigmoid(x):
    return jnp.minimum(x, 0.0) - jnp.log(1.0 + jnp.exp(-jnp.abs(x)))


def _mlstm_kernel(q_ref, k_ref, v_ref, og_ref, gt_ref, gb_ref, cw_ref, cb_ref, outg_ref, hm_ref,
                  qext, kext, c_scr, n_scr, m_scr):
    L = MLSTM_CHUNK
    pad = SUBLANES
    qkw = MLSTM_HEADS * MLSTM_QK

    @pl.when(pl.program_id(0) == 0)
    def _():
        qext[0:pad, :] = jnp.zeros((pad, qkw), F32)
        kext[0:pad, :] = jnp.zeros((pad, qkw), F32)
        c_scr[...] = jnp.zeros_like(c_scr)
        n_scr[...] = jnp.zeros_like(n_scr)
        m_scr[...] = jnp.zeros_like(m_scr)

    qext[pad:pad + L, :] = q_ref[...]
    kext[pad:pad + L, :] = k_ref[...]

    def conv_silu(ext, lo):
        acc = jnp.broadcast_to(cb_ref[:, lo:lo + qkw], (L, qkw))
        for j in range(MLSTM_CONV):
            shift = MLSTM_CONV - 1 - j
            acc = acc + cw_ref[j:j + 1, lo:lo + qkw] * ext[pad - shift:pad - shift + L, :]
        return acc * jax.nn.sigmoid(acc)

    qc = conv_silu(qext, 0)
    kc = conv_silu(kext, qkw)
    qext[0:pad, :] = qext[L:L + pad, :]
    kext[0:pad, :] = kext[L:L + pad, :]

    gts = gt_ref[...] + gb_ref[...]
    gtr = gts.T
    row = lax.broadcasted_iota(I32, (L, L), 0)
    col = lax.broadcasted_iota(I32, (L, L), 1)
    causal = col <= row
    tri_lo = jnp.where(causal, 1.0, 0.0).astype(BF16)
    tri_up = jnp.where(row <= col, 1.0, 0.0).astype(BF16)
    bc_all = sum(_dot(tri_lo, part) for part in _split3(_log_sigmoid(gts)))
    br_all = sum(_dot(part, tri_up) for part in _split3(_log_sigmoid(gtr)))

    for h in range(MLSTM_HEADS):
        fi = MLSTM_HEADS + h
        m_prev = m_scr[h, 0:1, 0:1]
        b_c = bc_all[:, fi:fi + 1]
        b_r = br_all[fi:fi + 1, :]
        i_c = gts[:, h:h + 1]
        i_r = gtr[h:h + 1, :]
        g = b_c[L - 1:L, :]
        d = jnp.where(causal, (b_c - b_r) + i_r, -jnp.inf)
        inter = b_c + m_prev
        m_t = jnp.maximum(inter, jnp.max(d, axis=-1, keepdims=True))
        w_inter = jnp.exp(inter - m_t)
        qh = qc[:, h * MLSTM_QK:(h + 1) * MLSTM_QK] * (MLSTM_QK ** -0.5)
        kh = kc[:, h * MLSTM_QK:(h + 1) * MLSTM_QK]
        vb = v_ref[:, h * MLSTM_V:(h + 1) * MLSTM_V].astype(BF16)
        qb = qh.astype(BF16)
        p = jnp.exp(d - m_t) * _dot_nt(qb, kh.astype(BF16))
        c_prev = c_scr[h]
        n_prev = n_scr[h, 0:1, :]
        num = w_inter * _dot(qb, c_prev.astype(BF16)) + _dot(p.astype(BF16), vb)
        den = (w_inter * jnp.sum(qh * n_prev, axis=-1, keepdims=True)
               + jnp.sum(p, axis=-1, keepdims=True))
        hh = num * (1.0 / jnp.maximum(jnp.abs(den), jnp.exp(-m_t)))

        a = (g - b_c) + i_c
        m_new = jnp.maximum(g + m_prev, jnp.max(a, axis=0, keepdims=True))
        kw = kh * jnp.exp(a - m_new)
        decay = jnp.exp(g + m_prev - m_new)
        c_scr[h] = decay * c_prev + _dot(kw.T.astype(BF16), vb)
        n_scr[h, 0:1, :] = decay * n_prev + jnp.sum(kw, axis=0, keepdims=True)
        m_scr[h] = jnp.broadcast_to(m_new, (SUBLANES, LANES))

        hn = hh * lax.rsqrt(jnp.mean(hh * hh, axis=-1, keepdims=True) + EPS)
        vs = slice(h * MLSTM_V, (h + 1) * MLSTM_V)
        hn = hn * outg_ref[:, vs] * jax.nn.sigmoid(og_ref[:, vs])
        hm_ref[:, vs] = hn.astype(hm_ref.dtype)


def _mlstm(proj, gate_col, gate_b, conv_w, conv_b, out_g):
    s = proj.shape[0]
    L = MLSTM_CHUNK
    qkw = MLSTM_HEADS * MLSTM_QK
    vw = MLSTM_HEADS * MLSTM_V
    gate_blk = gate_col // LANES
    return pl.pallas_call(
        _mlstm_kernel,
        grid=(s // L,),
        in_specs=[pl.BlockSpec((L, qkw), lambda c: (c, 0)),
                  pl.BlockSpec((L, qkw), lambda c: (c, 1)),
                  pl.BlockSpec((L, vw), lambda c: (c, 1)),
                  pl.BlockSpec((L, vw), lambda c: (c, 2)),
                  pl.BlockSpec((L, LANES), lambda c: (c, gate_blk)),
                  pl.BlockSpec((1, LANES), lambda c: (0, 0)),
                  pl.BlockSpec((MLSTM_CONV, 2 * qkw), lambda c: (0, 0)),
                  pl.BlockSpec((1, 2 * qkw), lambda c: (0, 0)),
                  pl.BlockSpec((1, vw), lambda c: (0, 0))],
        out_specs=pl.BlockSpec((L, vw), lambda c: (c, 0)),
        out_shape=jax.ShapeDtypeStruct((s, vw), BF16),
        scratch_shapes=[pltpu.VMEM((L + SUBLANES, qkw), F32),
                        pltpu.VMEM((L + SUBLANES, qkw), F32),
                        pltpu.VMEM((MLSTM_HEADS, MLSTM_QK, MLSTM_V), F32),
                        pltpu.VMEM((MLSTM_HEADS, SUBLANES, LANES), F32),
                        pltpu.VMEM((MLSTM_HEADS, SUBLANES, LANES), F32)],
        compiler_params=_params("arbitrary"),
        name="mlstm",
    )(proj, proj, proj, proj, proj, gate_b, conv_w, conv_b, out_g)


def _moba_prep_kernel(q_ref, k_ref, v_ref, pos_ref, qg_ref, kg_ref, qt_ref, kn_ref, vt_ref, km_ref):
    rows = q_ref.shape[0]
    half = ROPE_DIM // 2
    lane = lax.broadcasted_iota(I32, (1, MOBA_HD), 1)
    inv_freq = jnp.exp((lane & (half - 1)).astype(F32) * (-(2.0 / ROPE_DIM) * math.log(ROPE_THETA)))
    ang = pos_ref[...].astype(F32) * inv_freq
    cos = jnp.where(lane < ROPE_DIM, jnp.cos(ang), 1.0)
    sin = jnp.sin(ang)
    sin = jnp.where(lane < half, -sin, jnp.where(lane < ROPE_DIM, sin, 0.0))

    def rope(x):
        partner = jnp.where(lane < half, pltpu.roll(x, MOBA_HD - half, 1), pltpu.roll(x, half, 1))
        return x * cos + partner * sin

    for h in range(MOBA_HEADS):
        hs = slice(h * MOBA_HD, (h + 1) * MOBA_HD)
        qt_ref[hs, :] = rope(_rms(q_ref[:, hs], qg_ref[...])).T
        kn = rope(_rms(k_ref[:, hs], kg_ref[...]))
        kn_ref[:, hs] = kn.astype(BF16)
        km_ref[0, :, hs] = jnp.sum(kn, axis=0, keepdims=True) * (1.0 / rows)
        vt_ref[0, hs, :] = v_ref[:, hs].T.astype(BF16)


def _moba_prep(proj, q_col, pos, q_g, k_g):
    s = proj.shape[0]
    w = MOBA_HEADS * MOBA_HD
    bs = MOBA_BLOCK
    nb = s // bs
    first = q_col // w
    return pl.pallas_call(
        _moba_prep_kernel,
        grid=(nb,),
        in_specs=[pl.BlockSpec((bs, w), lambda i: (i, first)),
                  pl.BlockSpec((bs, w), lambda i: (i, first + 1)),
                  pl.BlockSpec((bs, w), lambda i: (i, first + 2)),
                  pl.BlockSpec((bs, 1), lambda i: (i, 0)),
                  pl.BlockSpec((1, MOBA_HD), lambda i: (0, 0)),
                  pl.BlockSpec((1, MOBA_HD), lambda i: (0, 0))],
        out_specs=[pl.BlockSpec((w, bs), lambda i: (0, i)),
                   pl.BlockSpec((bs, w), lambda i: (i, 0)),
                   pl.BlockSpec((1, w, bs), lambda i: (i, 0, 0)),
                   pl.BlockSpec((1, 1, w), lambda i: (i, 0, 0))],
        out_shape=[jax.ShapeDtypeStruct((w, s), F32),
                   jax.ShapeDtypeStruct((s, w), BF16),
                   jax.ShapeDtypeStruct((nb, w, bs), BF16),
                   jax.ShapeDtypeStruct((nb, 1, w), F32)],
        compiler_params=_params("parallel"),
        name="moba_prep",
    )(proj, proj, proj, pos, q_g, k_g)


def _moba_attn_kernel(qt_ref, k_ref, vt_ref, km_ref, o_ref, bias_scr, qb_scr, m_scr, l_scr, *tiles):
    i = pl.program_id(1)
    bs = MOBA_BLOCK
    hd = MOBA_HD
    nh = len(tiles) // 3
    acc_scr = tiles[:nh]
    s_scr = tiles[nh:2 * nh]
    p_scr = tiles[2 * nh:]
    nb = km_ref.shape[0]
    c2 = (hd ** -0.5) * math.log2(math.e)
    blk = lax.broadcasted_iota(I32, (nb, bs), 0)
    blk_f = blk.astype(F32)
    valid = blk < i
    key = lax.broadcasted_iota(I32, (bs, bs), 0)
    qry = lax.broadcasted_iota(I32, (bs, bs), 1)
    own = pl.ds(pl.multiple_of(i * bs, bs), bs)

    for h in range(nh):
        hs = slice(h * hd, (h + 1) * hd)
        qt = qt_ref[hs, :]
        gate = jnp.dot(km_ref[:, 0, hs], qt, precision=lax.Precision.HIGHEST,
                       preferred_element_type=F32)
        cand = valid
        for _ in range(MOBA_TOPK):
            gmax = jnp.max(jnp.where(cand, gate, -jnp.inf), axis=0, keepdims=True)
            hit = cand & (gate == gmax)
            first = jnp.min(jnp.where(hit, blk_f, float(nb)), axis=0, keepdims=True)
            cand = cand & jnp.logical_not(hit & (blk_f == first))
        bias_scr[h] = jnp.where(valid & jnp.logical_not(cand), 0.0, -jnp.inf)

        qb = (qt * c2).astype(BF16)
        qb_scr[h] = qb
        s0 = jnp.where(key <= qry, _dot(k_ref[own, hs], qb), -jnp.inf)
        m0 = jnp.max(s0, axis=0, keepdims=True)
        p0 = jnp.exp2(s0 - m0)
        m_scr[h] = m0
        l_scr[h] = jnp.sum(p0, axis=0, keepdims=True)
        acc_scr[h][...] = _dot(vt_ref[i, hs, :], p0.astype(BF16))

    ck = 32

    def body(j, carry):
        rows = pl.ds(pl.multiple_of(j * bs, bs), bs)
        for h in range(nh):
            s_scr[h][...] = _dot(k_ref[rows, h * hd:(h + 1) * hd], qb_scr[h])
        alphas = []
        for h in range(nh):
            bias = bias_scr[h, pl.ds(j, 1), :]
            cmax = s_scr[h][0:ck, :]
            for c in range(1, bs // ck):
                cmax = jnp.maximum(cmax, s_scr[h][c * ck:(c + 1) * ck, :])
            m = m_scr[h]
            m_new = jnp.maximum(m, jnp.max(cmax, axis=0, keepdims=True) + bias)
            alpha = jnp.exp2(m - m_new)
            shift = m_new - bias
            psum = jnp.zeros((ck, bs), F32)
            for c in range(bs // ck):
                p = jnp.exp2(s_scr[h][c * ck:(c + 1) * ck, :] - shift)
                psum = psum + p
                p_scr[h][c * ck:(c + 1) * ck, :] = p.astype(BF16)
            m_scr[h] = m_new
            l_scr[h] = alpha * l_scr[h] + jnp.sum(psum, axis=0, keepdims=True)
            alphas.append(alpha)
        for h in range(nh):
            acc_scr[h][...] = (alphas[h] * acc_scr[h][...]
                               + _dot(vt_ref[j, h * hd:(h + 1) * hd, :], p_scr[h][...]))
        return carry

    lax.fori_loop(0, i, body, 0)
    for h in range(nh):
        out_t = acc_scr[h][...] * (1.0 / l_scr[h])
        o_ref[:, h * hd:(h + 1) * hd] = out_t.T.astype(o_ref.dtype)


MOBA_HEADS_PER_STEP = 8


def _moba_attn(qt, kn, vt, kmean):
    w, s = qt.shape
    bs = MOBA_BLOCK
    nb = s // bs
    nh = MOBA_HEADS_PER_STEP
    gw = nh * MOBA_HD
    resident = pl.Buffered(1)
    return pl.pallas_call(
        _moba_attn_kernel,
        grid=(MOBA_HEADS // nh, nb),
        in_specs=[pl.BlockSpec((gw, bs), lambda g, i: (g, i)),
                  pl.BlockSpec((s, gw), lambda g, i: (0, g), pipeline_mode=resident),
                  pl.BlockSpec((nb, gw, bs), lambda g, i: (0, g, 0), pipeline_mode=resident),
                  pl.BlockSpec((nb, 1, gw), lambda g, i: (0, 0, g))],
        out_specs=pl.BlockSpec((bs, gw), lambda g, i: (i, g)),
        out_shape=jax.ShapeDtypeStruct((s, w), BF16),
        scratch_shapes=[pltpu.VMEM((nh, nb, bs), F32),
                        pltpu.VMEM((nh, MOBA_HD, bs), BF16),
                        pltpu.VMEM((nh, 1, bs), F32),
                        pltpu.VMEM((nh, 1, bs), F32)]
                       + [pltpu.VMEM((MOBA_HD, bs), F32) for _ in range(nh)]
                       + [pltpu.VMEM((bs, bs), F32) for _ in range(nh)]
                       + [pltpu.VMEM((bs, bs), BF16) for _ in range(nh)],
        compiler_params=_params("parallel", "arbitrary"),
        name="moba_attn",
    )(qt, kn, vt, kmean)


def _mix_out_kernel(hm_ref, ha_ref, w_ref, x_ref, o_ref, wb_ref):
    half = hm_ref.shape[1]

    @pl.when(pl.program_id(0) == 0)
    def _():
        wb_ref[...] = w_ref[...].astype(BF16)

    o_ref[...] = (x_ref[...] + _dot(hm_ref[...], wb_ref[0:half, :])
                  + _dot(ha_ref[...], wb_ref[half:2 * half, :]))


def _mix_out(hm, ha, w, x, tm):
    s, d = x.shape
    half = hm.shape[1]
    return pl.pallas_call(
        _mix_out_kernel,
        grid=(s // tm,),
        in_specs=[pl.BlockSpec((tm, half), lambda i: (i, 0)),
                  pl.BlockSpec((tm, half), lambda i: (i, 0)),
                  pl.BlockSpec((2 * half, d), lambda i: (0, 0), pipeline_mode=pl.Buffered(1)),
                  pl.BlockSpec((tm, d), lambda i: (i, 0))],
        out_specs=pl.BlockSpec((tm, d), lambda i: (i, 0)),
        out_shape=jax.ShapeDtypeStruct((s, d), F32),
        scratch_shapes=[pltpu.VMEM((2 * half, d), BF16)],
        compiler_params=_params("arbitrary"),
        name="mix_out",
    )(hm, ha, w, x)


def _mem_kv_kernel(mem_ref, g_ref, w_ref, kg_ref, o_ref, mn_ref):
    j = pl.program_id(0)

    @pl.when(j == 0)
    def _():
        mn_ref[...] = _rms(mem_ref[...], g_ref[...]).astype(BF16)

    y = _dot(mn_ref[...], w_ref[...].astype(BF16))
    o_ref[...] = jnp.where(j < XA_HEADS, _rms(y, kg_ref[...]), y).astype(o_ref.dtype)


def _mem_kv(mem, g, wkv, k_g):
    m, d = mem.shape
    hd = d // XA_HEADS
    return pl.pallas_call(
        _mem_kv_kernel,
        grid=(2 * XA_HEADS,),
        in_specs=[pl.BlockSpec((m, d), lambda j: (0, 0)),
                  pl.BlockSpec((1, d), lambda j: (0, 0)),
                  pl.BlockSpec((d, hd), lambda j: (0, j)),
                  pl.BlockSpec((1, hd), lambda j: (0, 0))],
        out_specs=pl.BlockSpec((m, hd), lambda j: (0, j)),
        out_shape=jax.ShapeDtypeStruct((m, 2 * d), BF16),
        scratch_shapes=[pltpu.VMEM((m, d), BF16)],
        compiler_params=_params("arbitrary"),
        name="mem_kv",
    )(mem, g, wkv, k_g)


def _xattn_kernel(h_ref, g_ref, wq_hbm, kv_ref, qg_ref, wo_hbm, o_ref, wqb, wob, stage, o_all, sem):
    d = h_ref.shape[1]
    hd = d // XA_HEADS
    n_slab = 2 * XA_HEADS

    @pl.when(pl.program_id(0) == 0)
    def _():
        def slab_copy(k):
            src = wq_hbm if k < XA_HEADS else wo_hbm
            cols = pl.ds((k % XA_HEADS) * hd, hd)
            return pltpu.make_async_copy(src.at[:, cols], stage.at[k % 2], sem.at[k % 2])

        slab_copy(0).start()
        for k in range(n_slab):
            if k + 1 < n_slab:
                slab_copy(k + 1).start()
            slab_copy(k).wait()
            dst = wqb if k < XA_HEADS else wob
            dst[:, (k % XA_HEADS) * hd:(k % XA_HEADS + 1) * hd] = stage[k % 2].astype(BF16)

    h = h_ref[...]
    q_all = _dot(_rms(h, g_ref[...]).astype(BF16), wqb[...])
    for j in range(XA_HEADS):
        hs = slice(j * hd, (j + 1) * hd)
        q = _rms(q_all[:, hs], qg_ref[...]).astype(BF16)
        sc = _dot_nt(q, kv_ref[:, hs]) * (hd ** -0.5)
        p = jnp.exp(sc - jnp.max(sc, axis=-1, keepdims=True))
        p = p * (1.0 / jnp.sum(p, axis=-1, keepdims=True))
        o_all[:, hs] = _dot(p.astype(BF16), kv_ref[:, d + j * hd:d + (j + 1) * hd]).astype(BF16)
    o_ref[...] = h + _dot(o_all[...], wob[...])


def _xattn(h, g, wq, kv, q_g, wo, tm):
    s, d = h.shape
    m = kv.shape[0]
    hd = d // XA_HEADS
    return pl.pallas_call(
        _xattn_kernel,
        grid=(s // tm,),
        in_specs=[pl.BlockSpec((tm, d), lambda i: (i, 0)),
                  pl.BlockSpec((1, d), lambda i: (0, 0)),
                  pl.BlockSpec(memory_space=pl.ANY),
                  pl.BlockSpec((m, 2 * d), lambda i: (0, 0)),
                  pl.BlockSpec((1, hd), lambda i: (0, 0)),
                  pl.BlockSpec(memory_space=pl.ANY)],
        out_specs=pl.BlockSpec((tm, d), lambda i: (i, 0)),
        out_shape=jax.ShapeDtypeStruct((s, d), F32),
        scratch_shapes=[pltpu.VMEM((d, d), BF16), pltpu.VMEM((d, d), BF16),
                        pltpu.VMEM((2, d, hd), F32), pltpu.VMEM((tm, d), BF16),
                        pltpu.SemaphoreType.DMA((2,))],
        compiler_params=_params("arbitrary"),
        name="xattn",
    )(h, g, wq, kv, q_g, wo)


ROUTE_EID = 0
ROUTE_RANK = 2
ROUTE_WT = 4


def _router_kernel(h_ref, g_ref, w_ref, b_ref, out_ref, cnt_ref, carry):
    tm = h_ref.shape[0]

    @pl.when(pl.program_id(0) == 0)
    def _():
        carry[...] = jnp.zeros_like(carry)

    hn = _rms(h_ref[...], g_ref[...])
    hi = hn.astype(BF16)
    lo = (hn - hi.astype(F32)).astype(BF16)
    logits = (_dot(hi, w_ref[0]) + _dot(hi, w_ref[1]) + _dot(lo, w_ref[0])
              + b_ref[...])
    lane = lax.broadcasted_iota(I32, (tm, LANES), 1)

    def first_lane(mask):
        return jnp.min(jnp.where(mask, lane.astype(F32), float(LANES)), axis=-1,
                       keepdims=True).astype(I32)

    is_g = lane < MOE_GROUPS
    gmax = jnp.max(jnp.where(is_g, logits, -jnp.inf), axis=-1, keepdims=True)
    gsum = jnp.sum(jnp.where(is_g, jnp.exp(logits - gmax), 0.0), axis=-1, keepdims=True)
    gsel = first_lane(is_g & (logits == gmax))
    pg = 1.0 / gsum

    grp_lo = MOE_GROUPS + MOE_PER_GROUP * gsel
    in_grp = (lane >= grp_lo) & (lane < grp_lo + MOE_PER_GROUP)
    emax = jnp.max(jnp.where(in_grp, logits, -jnp.inf), axis=-1, keepdims=True)
    eexp = jnp.where(in_grp, jnp.exp(logits - emax), 0.0)
    eprob = eexp / jnp.sum(eexp, axis=-1, keepdims=True)
    p1 = jnp.max(jnp.where(in_grp, eprob, -1.0), axis=-1, keepdims=True)
    l1 = first_lane(in_grp & (eprob == p1))
    rest = in_grp & (lane != l1)
    p2 = jnp.max(jnp.where(rest, eprob, -1.0), axis=-1, keepdims=True)
    l2 = first_lane(rest & (eprob == p2))
    psum = p1 + p2
    w1 = pg * p1 / psum
    w2 = pg * p2 / psum

    oh1 = lane == l1
    oh2 = lane == l2
    onehot = jnp.where(oh1 | oh2, 1.0, 0.0)
    r = lax.broadcasted_iota(I32, (tm, tm), 0)
    c = lax.broadcasted_iota(I32, (tm, tm), 1)
    strict = jnp.where(c < r, 1.0, 0.0).astype(BF16)
    before = _dot(strict, onehot.astype(BF16)) + carry[0:1, :]
    rank1 = jnp.sum(jnp.where(oh1, before, 0.0), axis=-1, keepdims=True)
    rank2 = jnp.sum(jnp.where(oh2, before, 0.0), axis=-1, keepdims=True)
    total = carry[0:1, :] + jnp.sum(onehot, axis=0, keepdims=True)
    carry[...] = jnp.broadcast_to(total, carry.shape)
    cnt_ref[...] = jnp.broadcast_to(total, cnt_ref.shape)

    out = jnp.where(lane == ROUTE_EID, (l1 - MOE_GROUPS).astype(F32), 0.0)
    out = jnp.where(lane == ROUTE_EID + 1, (l2 - MOE_GROUPS).astype(F32), out)
    out = jnp.where(lane == ROUTE_RANK, rank1, out)
    out = jnp.where(lane == ROUTE_RANK + 1, rank2, out)
    out = jnp.where(lane == ROUTE_WT, w1, out)
    out = jnp.where(lane == ROUTE_WT + 1, w2, out)
    out_ref[...] = out


def _router(h, g, w, b, tm):
    s, d = h.shape
    return pl.pallas_call(
        _router_kernel,
        grid=(s // tm,),
        in_specs=[pl.BlockSpec((tm, d), lambda i: (i, 0)),
                  pl.BlockSpec((1, d), lambda i: (0, 0)),
                  pl.BlockSpec((2, d, LANES), lambda i: (0, 0, 0)),
                  pl.BlockSpec((1, LANES), lambda i: (0, 0))],
        out_specs=[pl.BlockSpec((tm, LANES), lambda i: (i, 0)),
                   pl.BlockSpec((SUBLANES, LANES), lambda i: (0, 0))],
        out_shape=[jax.ShapeDtypeStruct((s, LANES), F32),
                   jax.ShapeDtypeStruct((SUBLANES, LANES), F32)],
        scratch_shapes=[pltpu.VMEM((SUBLANES, LANES), F32)],
        compiler_params=_params("arbitrary"),
        name="router",
    )(h, g, w, b)


def _moe_ffn_kernel(be_ref, nu_ref, nx_ref, src_ref, h_hbm, g_ref, wg_hbm, wu_hbm, wd_hbm, y_ref,
                    xbuf, wgf, wuf, wdf, wgb, wub, wdb, slot_ref, sem, xsem):
    b = pl.program_id(0)
    e = be_ref[b]
    used = b < nu_ref[0]

    def gather_rows(blk):
        slot = blk % 2
        base = blk * MOE_ROWS

        def issue(r, carry):
            pltpu.make_async_copy(h_hbm.at[pl.ds(src_ref[base + r], 1), :],
                                  xbuf.at[slot, pl.ds(r, 1), :], xsem.at[slot]).start()
            return carry

        lax.fori_loop(0, MOE_ROWS, issue, 0, unroll=ROW_DMA_UNROLL)

    @pl.when(b == 0)
    def _():
        gather_rows(b)

    @pl.when(b + 1 < nu_ref[0])
    def _():
        gather_rows(b + 1)

    def weight_streams(expert, slot):
        return ((wg_hbm.at[expert], wgf.at[slot], sem.at[slot, 0]),
                (wu_hbm.at[expert], wuf.at[slot], sem.at[slot, 1]),
                (wd_hbm.at[expert], wdf.at[slot], sem.at[slot, 2]))

    def start_weights(expert, slot):
        for src, dst, s in weight_streams(expert, slot):
            pltpu.async_copy(src, dst, s, priority=1)

    @pl.when(b == 0)
    def _():
        slot_ref[0] = 0
        start_weights(e, 0)

    @pl.when(used & ((b == 0) | (e != be_ref[jnp.maximum(b - 1, 0)])))
    def _():
        slot = slot_ref[0]
        for src, dst, s in weight_streams(e, slot):
            pltpu.make_async_copy(src, dst, s).wait()
        nxt = nx_ref[e]

        @pl.when(nxt >= 0)
        def _():
            start_weights(nxt, 1 - slot)

        wgb[...] = wgf[slot].astype(BF16)
        wub[...] = wuf[slot].astype(BF16)
        wdb[...] = wdf[slot].astype(BF16)
        slot_ref[0] = 1 - slot

    @pl.when(used)
    def _():
        slot = b % 2
        pltpu.make_async_copy(h_hbm.at[pl.ds(0, MOE_ROWS), :], xbuf.at[slot], xsem.at[slot]).wait()
        xb = _rms(xbuf[slot], g_ref[...]).astype(BF16)
        gate = _dot(xb, wgb[...])
        up = _dot(xb, wub[...])
        act = (gate * jax.nn.sigmoid(gate) * up).astype(BF16)
        y_ref[...] = _dot(act, wdb[...])

    @pl.when(jnp.logical_not(used))
    def _():
        y_ref[...] = jnp.zeros_like(y_ref)


def _row_source_kernel(d0_ref, d1_ref, src_ref):
    def clear(r, carry):
        src_ref[r] = 0
        return carry

    def put(t, carry):
        src_ref[d0_ref[t]] = t
        src_ref[d1_ref[t]] = t
        return carry

    lax.fori_loop(0, src_ref.shape[0], clear, 0, unroll=ROW_DMA_UNROLL)
    lax.fori_loop(0, d0_ref.shape[0], put, 0, unroll=ROW_DMA_UNROLL)


def _row_source(dest, n_rows):
    smem = pl.BlockSpec(memory_space=pltpu.SMEM)
    return pl.pallas_call(
        _row_source_kernel,
        in_specs=[smem, smem],
        out_specs=smem,
        out_shape=jax.ShapeDtypeStruct((n_rows,), I32),
        name="moe_row_source",
    )(dest[0], dest[1])


def _moe_ffn(blk_e, n_used, next_e, row_src, h, g, w_gate, w_up, w_down):
    s, d = h.shape
    n_rows = row_src.shape[0]
    ff = w_gate.shape[2]
    n_blk = n_rows // MOE_ROWS
    grid_spec = pltpu.PrefetchScalarGridSpec(
        num_scalar_prefetch=4,
        grid=(n_blk,),
        in_specs=[pl.BlockSpec(memory_space=pl.ANY),
                  pl.BlockSpec((1, d), lambda b, be, nu, nx, src: (0, 0)),
                  pl.BlockSpec(memory_space=pl.ANY),
                  pl.BlockSpec(memory_space=pl.ANY),
                  pl.BlockSpec(memory_space=pl.ANY)],
        out_specs=pl.BlockSpec((MOE_ROWS, d), lambda b, be, nu, nx, src: (b, 0)),
        scratch_shapes=[pltpu.VMEM((2, MOE_ROWS, d), F32),
                        pltpu.VMEM((2, d, ff), F32), pltpu.VMEM((2, d, ff), F32),
                        pltpu.VMEM((2, ff, d), F32),
                        pltpu.VMEM((d, ff), BF16), pltpu.VMEM((d, ff), BF16),
                        pltpu.VMEM((ff, d), BF16),
                        pltpu.SMEM((1,), I32), pltpu.SemaphoreType.DMA((2, 3)),
                        pltpu.SemaphoreType.DMA((2,))],
    )
    return pl.pallas_call(
        _moe_ffn_kernel,
        grid_spec=grid_spec,
        out_shape=jax.ShapeDtypeStruct((n_rows, d), F32),
        compiler_params=_params("arbitrary"),
        name="moe_ffn",
    )(blk_e, n_used, next_e, row_src, h, g, w_gate, w_up, w_down)


def _combine_kernel(d0_ref, d1_ref, h_ref, r_ref, y_ref, o_ref, buf, sem):
    tm = h_ref.shape[0]
    base = pl.program_id(0) * tm

    def row_copy(k, t, d):
        return pltpu.make_async_copy(y_ref.at[pl.ds(d, 1), :], buf.at[k, pl.ds(t, 1), :], sem)

    def issue(t, carry):
        row_copy(0, t, d0_ref[base + t]).start()
        row_copy(1, t, d1_ref[base + t]).start()
        return carry

    lax.fori_loop(0, tm, issue, 0, unroll=ROW_DMA_UNROLL)
    for k in range(2):
        pltpu.make_async_copy(y_ref.at[pl.ds(0, tm), :], buf.at[k], sem).wait()
    r = r_ref[...]
    o_ref[...] = (h_ref[...] + r[:, ROUTE_WT:ROUTE_WT + 1] * buf[0]
                  + r[:, ROUTE_WT + 1:ROUTE_WT + 2] * buf[1])


def _combine(dest, h, routed, y, tm):
    s, d = h.shape
    grid_spec = pltpu.PrefetchScalarGridSpec(
        num_scalar_prefetch=2,
        grid=(s // tm,),
        in_specs=[pl.BlockSpec((tm, d), lambda i, d0, d1: (i, 0)),
                  pl.BlockSpec((tm, LANES), lambda i, d0, d1: (i, 0)),
                  pl.BlockSpec(memory_space=pl.ANY)],
        out_specs=pl.BlockSpec((tm, d), lambda i, d0, d1: (i, 0)),
        scratch_shapes=[pltpu.VMEM((2, tm, d), F32), pltpu.SemaphoreType.DMA(())],
    )
    return pl.pallas_call(
        _combine_kernel,
        grid_spec=grid_spec,
        out_shape=jax.ShapeDtypeStruct((s, d), F32),
        compiler_params=_params("arbitrary"),
        name="moe_combine",
    )(dest[0], dest[1], h, routed, y)


def _layer(h, mem, pos, norm_mix_g, w_in, gate_b, conv_w, conv_b, out_g, moba_q_g, moba_k_g, w_out,
           norm_cross_g, norm_mem_g, xa_wq, xa_wkv, xa_q_g, xa_k_g, xa_wo, norm_ffn_g,
           router_group_w, router_group_b, router_expert_w, router_expert_b,
           exp_w_gate, exp_w_up, exp_w_down):
    s, d = h.shape
    row = lambda v: v.reshape(1, -1)

    n_gate = 2 * MLSTM_HEADS
    gate_lo = 2 * MLSTM_HEADS * MLSTM_QK + 2 * MLSTM_HEADS * MLSTM_V
    proj_tn = 512
    w_b = w_in[:, gate_lo + n_gate:]
    w_g = jnp.pad(w_in[:, gate_lo:gate_lo + n_gate], ((0, 0), (0, proj_tn - n_gate))).astype(BF16)
    proj = _proj_in(h, row(norm_mix_g), w_in, gate_lo, w_b, w_g, tm=min(s, 1024))

    gate_b_row = jnp.pad(gate_b, (0, LANES - n_gate)).reshape(1, LANES)
    hm = _mlstm(proj, gate_lo + w_b.shape[1], gate_b_row, conv_w, row(conv_b), row(out_g))
    qn, kn, vb, kmean = _moba_prep(proj, gate_lo, pos, row(moba_q_g), row(moba_k_g))
    ha = _moba_attn(qn, kn, vb, kmean)
    h = _mix_out(hm, ha, w_out, h, tm=512)

    kv = _mem_kv(mem, row(norm_mem_g), xa_wkv, row(xa_k_g))
    h = _xattn(h, row(norm_cross_g), xa_wq, kv, row(xa_q_g), xa_wo, tm=512)

    w_route = jnp.concatenate(
        [router_group_w, router_expert_w,
         jnp.zeros((d, LANES - MOE_GROUPS - MOE_EXPERTS), router_group_w.dtype)], axis=1)
    w_route_hi = w_route.astype(BF16)
    w_route = jnp.stack([w_route_hi, (w_route - w_route_hi.astype(F32)).astype(BF16)])
    b_route = jnp.pad(jnp.concatenate([router_group_b, router_expert_b]),
                      (0, LANES - MOE_GROUPS - MOE_EXPERTS)).reshape(1, LANES)
    routed, counts = _router(h, row(norm_ffn_g), w_route, b_route, tm=512)

    cnt = counts[0, MOE_GROUPS:MOE_GROUPS + MOE_EXPERTS].astype(I32)
    padded = (cnt + MOE_ROWS - 1) // MOE_ROWS * MOE_ROWS
    pend = jnp.cumsum(padded)
    seg_start = (pend - padded).astype(F32)[None, :]
    expert_ids = jnp.arange(MOE_EXPERTS, dtype=F32)[None, :]

    def dest_of(k):
        mine = routed[:, ROUTE_EID + k:ROUTE_EID + k + 1] == expert_ids
        return (jnp.sum(jnp.where(mine, seg_start, 0.0), axis=1) + routed[:, ROUTE_RANK + k]).astype(I32)

    dest = (dest_of(0), dest_of(1))
    n_blk = -(-2 * s // MOE_ROWS) + MOE_EXPERTS
    blk_start = jnp.arange(n_blk, dtype=I32) * MOE_ROWS
    blk_e = jnp.minimum(jnp.sum((pend[None, :] <= blk_start[:, None]).astype(I32), axis=1),
                        MOE_EXPERTS - 1)
    n_used = (pend[-1:] // MOE_ROWS).astype(I32)
    ids = jnp.arange(MOE_EXPERTS, dtype=I32)
    later_used = (ids[None, :] > ids[:, None]) & (cnt[None, :] > 0)
    next_e = jnp.min(jnp.where(later_used, ids[None, :], MOE_EXPERTS), axis=1)
    next_e = jnp.where(next_e < MOE_EXPERTS, next_e, -1).astype(I32)

    row_src = _row_source(dest, n_blk * MOE_ROWS)
    y = _moe_ffn(blk_e, n_used, next_e, row_src, h, row(norm_ffn_g), exp_w_gate, exp_w_up, exp_w_down)
    return _combine(dest, h, routed, y, tm=256)


def kernel(x, mem, positions, norm_mix_g, w_in, mlstm_gate_b, mlstm_conv_w, mlstm_conv_b, mlstm_out_g, moba_q_g, moba_k_g, w_out, norm_cross_g, norm_mem_g, xa_wq, xa_wkv, xa_q_g, xa_k_g, xa_wo, norm_ffn_g, router_group_w, router_group_b, router_expert_w, router_expert_b, exp_w_gate, exp_w_up, exp_w_down):
    bsz, s, _ = x.shape
    assert bsz == 1, "single-sequence prefill only"
    per_layer = (norm_mix_g, w_in, mlstm_gate_b, mlstm_conv_w, mlstm_conv_b, mlstm_out_g, moba_q_g,
                 moba_k_g, w_out, norm_cross_g, norm_mem_g, xa_wq, xa_wkv, xa_q_g, xa_k_g, xa_wo,
                 norm_ffn_g, router_group_w, router_group_b, router_expert_w, router_expert_b,
                 exp_w_gate, exp_w_up, exp_w_down)
    h = x[0]
    pos = positions.reshape(s, 1)
    for l in range(norm_mix_g.shape[0]):
        h = _layer(h, mem[0], pos, *(p[l] for p in per_layer))
    return h[None]
```

```python
import functools
import math

import jax
import jax.numpy as jnp
from jax import lax
from jax.experimental import pallas as pl
from jax.experimental.pallas import tpu as pltpu

F32 = jnp.float32
BF16 = jnp.bfloat16
I32 = jnp.int32

EPS = 1e-6
LANES = 128
SUBLANES = 8
VMEM_LIMIT = 56 * 1024 * 1024

MLSTM_HEADS = 4
MLSTM_QK = 128
MLSTM_V = 256
MLSTM_CHUNK = 128
MLSTM_CONV = 4
MOBA_HEADS = 8
MOBA_HD = 128
MOBA_BLOCK = 256
MOBA_TOPK = 3
MOBA_VT_ROWS = MOBA_HD + 16
ROPE_DIM = 32
ROPE_THETA = 500000.0
XA_HEADS = 4
MOE_GROUPS = 8
MOE_PER_GROUP = 8
MOE_EXPERTS = MOE_GROUPS * MOE_PER_GROUP
MOE_ROWS = 128
ROW_DMA_UNROLL = 8

NT_DIMS = (((1,), (1,)), ((), ()))


def _params(*sem):
    return pltpu.CompilerParams(dimension_semantics=sem, vmem_limit_bytes=VMEM_LIMIT)


def _rms(x, g):
    return x * lax.rsqrt(jnp.mean(x * x, axis=-1, keepdims=True) + EPS) * g


def _dot(a, b):
    return jnp.dot(a, b, preferred_element_type=F32)


def _dot_nt(a, b, precision=None):
    return lax.dot_general(a, b, NT_DIMS, precision=precision, preferred_element_type=F32)


def _proj_in_kernel(x_ref, g_ref, wa_ref, wb_ref, wg_ref, o_ref, xn_ref, *, na, nb):
    j = pl.program_id(1)

    @pl.when(j == 0)
    def _():
        xn_ref[...] = _rms(x_ref[...], g_ref[...]).astype(BF16)

    @pl.when(j < na)
    def _():
        o_ref[...] = _dot(xn_ref[...], wa_ref[...])

    @pl.when((j >= na) & (j < na + nb))
    def _():
        o_ref[...] = _dot(xn_ref[...], wb_ref[...])

    @pl.when(j == na + nb)
    def _():
        o_ref[...] = _dot(xn_ref[...], wg_ref[...])


def _proj_in(x, g, w_a, w_b, w_g, tm):
    s, d = x.shape
    tn = w_g.shape[1]
    na, nb = w_a.shape[1] // tn, w_b.shape[1] // tn
    return pl.pallas_call(
        functools.partial(_proj_in_kernel, na=na, nb=nb),
        grid=(s // tm, na + nb + 1),
        in_specs=[pl.BlockSpec((tm, d), lambda i, j: (i, 0)),
                  pl.BlockSpec((1, d), lambda i, j: (0, 0)),
                  pl.BlockSpec((d, tn), lambda i, j: (0, jnp.minimum(j, na - 1))),
                  pl.BlockSpec((d, tn), lambda i, j: (0, jnp.clip(j - na, 0, nb - 1))),
                  pl.BlockSpec((d, tn), lambda i, j: (0, 0))],
        out_specs=pl.BlockSpec((tm, tn), lambda i, j: (i, j)),
        out_shape=jax.ShapeDtypeStruct((s, (na + nb + 1) * tn), F32),
        scratch_shapes=[pltpu.VMEM((tm, d), BF16)],
        compiler_params=_params("parallel", "arbitrary"),
        name="proj_in",
    )(x, g, w_a, w_b, w_g)


def _split3(x):
    hi = x.astype(BF16)
    r = x - hi.astype(F32)
    mid = r.astype(BF16)
    lo = (r - mid.astype(F32)).astype(BF16)
    return hi, mid, lo


def _log_sigmoid(x):
    return jnp.minimum(x, 0.0) - jnp.log(1.0 + jnp.exp(-jnp.abs(x)))


def _mlstm_kernel(q_ref, k_ref, v_ref, og_ref, gt_ref, gb_ref, cw_ref, cb_ref, outg_ref, hm_ref,
                  qext, kext, c_scr, n_scr, m_scr):
    L = MLSTM_CHUNK
    pad = SUBLANES
    qkw = MLSTM_HEADS * MLSTM_QK

    @pl.when(pl.program_id(0) == 0)
    def _():
        qext[0:pad, :] = jnp.zeros((pad, qkw), F32)
        kext[0:pad, :] = jnp.zeros((pad, qkw), F32)
        c_scr[...] = jnp.zeros_like(c_scr)
        n_scr[...] = jnp.zeros_like(n_scr)
        m_scr[...] = jnp.zeros_like(m_scr)

    qext[pad:pad + L, :] = q_ref[...]
    kext[pad:pad + L, :] = k_ref[...]

    def conv_silu(ext, lo):
        acc = jnp.broadcast_to(cb_ref[:, lo:lo + qkw], (L, qkw))
        for j in range(MLSTM_CONV):
            shift = MLSTM_CONV - 1 - j
            acc = acc + cw_ref[j:j + 1, lo:lo + qkw] * ext[pad - shift:pad - shift + L, :]
        return acc * jax.nn.sigmoid(acc)

    qc = conv_silu(qext, 0)
    kc = conv_silu(kext, qkw)
    qext[0:pad, :] = qext[L:L + pad, :]
    kext[0:pad, :] = kext[L:L + pad, :]

    gts = gt_ref[...] + gb_ref[...]
    gtr = gts.T
    row = lax.broadcasted_iota(I32, (L, L), 0)
    col = lax.broadcasted_iota(I32, (L, L), 1)
    causal = col <= row
    tri_lo = jnp.where(causal, 1.0, 0.0).astype(BF16)
    tri_up = jnp.where(row <= col, 1.0, 0.0).astype(BF16)
    bc_all = sum(_dot(tri_lo, part) for part in _split3(_log_sigmoid(gts)))
    br_all = sum(_dot(part, tri_up) for part in _split3(_log_sigmoid(gtr)))

    for h in range(MLSTM_HEADS):
        fi = MLSTM_HEADS + h
        m_prev = m_scr[h, 0:1, 0:1]
        b_c = bc_all[:, fi:fi + 1]
        b_r = br_all[fi:fi + 1, :]
        i_c = gts[:, h:h + 1]
        i_r = gtr[h:h + 1, :]
        g = b_c[L - 1:L, :]
        d = jnp.where(causal, (b_c - b_r) + i_r, -jnp.inf)
        inter = b_c + m_prev
        m_t = jnp.maximum(inter, jnp.max(d, axis=-1, keepdims=True))
        w_inter = jnp.exp(inter - m_t)
        qh = qc[:, h * MLSTM_QK:(h + 1) * MLSTM_QK] * (MLSTM_QK ** -0.5)
        kh = kc[:, h * MLSTM_QK:(h + 1) * MLSTM_QK]
        vb = v_ref[:, h * MLSTM_V:(h + 1) * MLSTM_V].astype(BF16)
        qb = qh.astype(BF16)
        p = jnp.exp(d - m_t) * _dot_nt(qb, kh.astype(BF16))
        c_prev = c_scr[h]
        n_prev = n_scr[h, 0:1, :]
        num = w_inter * _dot(qb, c_prev.astype(BF16)) + _dot(p.astype(BF16), vb)
        den = (w_inter * jnp.sum(qh * n_prev, axis=-1, keepdims=True)
               + jnp.sum(p, axis=-1, keepdims=True))
        hh = num * (1.0 / jnp.maximum(jnp.abs(den), jnp.exp(-m_t)))

        a = (g - b_c) + i_c
        m_new = jnp.maximum(g + m_prev, jnp.max(a, axis=0, keepdims=True))
        kw = kh * jnp.exp(a - m_new)
        decay = jnp.exp(g + m_prev - m_new)
        c_scr[h] = decay * c_prev + _dot(kw.T.astype(BF16), vb)
        n_scr[h, 0:1, :] = decay * n_prev + jnp.sum(kw, axis=0, keepdims=True)
        m_scr[h] = jnp.broadcast_to(m_new, (SUBLANES, LANES))

        hn = hh * lax.rsqrt(jnp.mean(hh * hh, axis=-1, keepdims=True) + EPS)
        vs = slice(h * MLSTM_V, (h + 1) * MLSTM_V)
        hn = hn * outg_ref[:, vs] * jax.nn.sigmoid(og_ref[:, vs])
        hm_ref[:, vs] = hn.astype(hm_ref.dtype)


def _mlstm(proj, gate_col, gate_b, conv_w, conv_b, out_g):
    s = proj.shape[0]
    L = MLSTM_CHUNK
    qkw = MLSTM_HEADS * MLSTM_QK
    vw = MLSTM_HEADS * MLSTM_V
    gate_blk = gate_col // LANES
    return pl.pallas_call(
        _mlstm_kernel,
        grid=(s // L,),
        in_specs=[pl.BlockSpec((L, qkw), lambda c: (c, 0)),
                  pl.BlockSpec((L, qkw), lambda c: (c, 1)),
                  pl.BlockSpec((L, vw), lambda c: (c, 1)),
                  pl.BlockSpec((L, vw), lambda c: (c, 2)),
                  pl.BlockSpec((L, LANES), lambda c: (c, gate_blk)),
                  pl.BlockSpec((1, LANES), lambda c: (0, 0)),
                  pl.BlockSpec((MLSTM_CONV, 2 * qkw), lambda c: (0, 0)),
                  pl.BlockSpec((1, 2 * qkw), lambda c: (0, 0)),
                  pl.BlockSpec((1, vw), lambda c: (0, 0))],
        out_specs=pl.BlockSpec((L, vw), lambda c: (c, 0)),
        out_shape=jax.ShapeDtypeStruct((s, vw), BF16),
        scratch_shapes=[pltpu.VMEM((L + SUBLANES, qkw), F32),
                        pltpu.VMEM((L + SUBLANES, qkw), F32),
                        pltpu.VMEM((MLSTM_HEADS, MLSTM_QK, MLSTM_V), F32),
                        pltpu.VMEM((MLSTM_HEADS, SUBLANES, LANES), F32),
                        pltpu.VMEM((MLSTM_HEADS, SUBLANES, LANES), F32)],
        compiler_params=_params("arbitrary"),
        name="mlstm",
    )(proj, proj, proj, proj, proj, gate_b, conv_w, conv_b, out_g)


def _moba_prep_kernel(q_ref, k_ref, v_ref, pos_ref, qg_ref, kg_ref, qt_ref, kn_ref, vt_ref, km_ref):
    rows = q_ref.shape[0]
    half = ROPE_DIM // 2
    lane = lax.broadcasted_iota(I32, (1, MOBA_HD), 1)
    inv_freq = jnp.exp((lane & (half - 1)).astype(F32) * (-(2.0 / ROPE_DIM) * math.log(ROPE_THETA)))
    ang = pos_ref[...].astype(F32) * inv_freq
    cos = jnp.where(lane < ROPE_DIM, jnp.cos(ang), 1.0)
    sin = jnp.sin(ang)
    sin = jnp.where(lane < half, -sin, jnp.where(lane < ROPE_DIM, sin, 0.0))

    def rope(x):
        partner = jnp.where(lane < half, pltpu.roll(x, MOBA_HD - half, 1), pltpu.roll(x, half, 1))
        return x * cos + partner * sin

    for h in range(MOBA_HEADS):
        hs = slice(h * MOBA_HD, (h + 1) * MOBA_HD)
        qt_ref[hs, :] = rope(_rms(q_ref[:, hs], qg_ref[...])).T
        kn = rope(_rms(k_ref[:, hs], kg_ref[...]))
        kn_ref[:, hs] = kn.astype(BF16)
        km_ref[0, :, hs] = jnp.sum(kn, axis=0, keepdims=True) * (1.0 / rows)
        vlo = h * MOBA_VT_ROWS
        vt_ref[0, vlo:vlo + MOBA_HD, :] = v_ref[:, hs].T.astype(BF16)
        vt_ref[0, vlo + MOBA_HD:vlo + MOBA_VT_ROWS, :] = jnp.ones((MOBA_VT_ROWS - MOBA_HD, rows), BF16)


def _moba_prep(proj, q_col, pos, q_g, k_g):
    s = proj.shape[0]
    w = MOBA_HEADS * MOBA_HD
    bs = MOBA_BLOCK
    nb = s // bs
    first = q_col // w
    return pl.pallas_call(
        _moba_prep_kernel,
        grid=(nb,),
        in_specs=[pl.BlockSpec((bs, w), lambda i: (i, first)),
                  pl.BlockSpec((bs, w), lambda i: (i, first + 1)),
                  pl.BlockSpec((bs, w), lambda i: (i, first + 2)),
                  pl.BlockSpec((bs, 1), lambda i: (i, 0)),
                  pl.BlockSpec((1, MOBA_HD), lambda i: (0, 0)),
                  pl.BlockSpec((1, MOBA_HD), lambda i: (0, 0))],
        out_specs=[pl.BlockSpec((w, bs), lambda i: (0, i)),
                   pl.BlockSpec((bs, w), lambda i: (i, 0)),
                   pl.BlockSpec((1, MOBA_HEADS * MOBA_VT_ROWS, bs), lambda i: (i, 0, 0)),
                   pl.BlockSpec((1, 1, w), lambda i: (i, 0, 0))],
        out_shape=[jax.ShapeDtypeStruct((w, s), F32),
                   jax.ShapeDtypeStruct((s, w), BF16),
                   jax.ShapeDtypeStruct((nb, MOBA_HEADS * MOBA_VT_ROWS, bs), BF16),
                   jax.ShapeDtypeStruct((nb, 1, w), F32)],
        compiler_params=_params("parallel"),
        name="moba_prep",
    )(proj, proj, proj, pos, q_g, k_g)


def _moba_attn_kernel(qt_ref, k_ref, vt_ref, km_ref, o_ref, bias_scr, qb_scr, m_scr, *tiles):
    i = pl.program_id(1)
    bs = MOBA_BLOCK
    hd = MOBA_HD
    nh = len(tiles) // 3
    vr = MOBA_VT_ROWS
    acc_scr = tiles[:nh]
    s_scr = tiles[nh:2 * nh]
    p_scr = tiles[2 * nh:]
    nb = km_ref.shape[0]
    c2 = (hd ** -0.5) * math.log2(math.e)
    blk = lax.broadcasted_iota(I32, (nb, bs), 0)
    blk_f = blk.astype(F32)
    valid = blk < i
    key = lax.broadcasted_iota(I32, (bs, bs), 0)
    qry = lax.broadcasted_iota(I32, (bs, bs), 1)
    own = pl.ds(pl.multiple_of(i * bs, bs), bs)

    for h in range(nh):
        hs = slice(h * hd, (h + 1) * hd)
        qt = qt_ref[hs, :]
        gate = jnp.dot(km_ref[:, 0, hs], qt, precision=lax.Precision.HIGHEST,
                       preferred_element_type=F32)
        cand = valid
        for _ in range(MOBA_TOPK):
            gmax = jnp.max(jnp.where(cand, gate, -jnp.inf), axis=0, keepdims=True)
            hit = cand & (gate == gmax)
            first = jnp.min(jnp.where(hit, blk_f, float(nb)), axis=0, keepdims=True)
            cand = cand & jnp.logical_not(hit & (blk_f == first))
        bias_scr[h] = jnp.where(valid & jnp.logical_not(cand), 0.0, -jnp.inf)

        qb = (qt * c2).astype(BF16)
        qb_scr[h] = qb
        s0 = jnp.where(key <= qry, _dot(k_ref[own, hs], qb), -jnp.inf)
        m0 = jnp.max(s0, axis=0, keepdims=True)
        p0 = jnp.exp2(s0 - m0)
        m_scr[h] = m0
        acc_scr[h][...] = _dot(vt_ref[i, h * vr:(h + 1) * vr, :], p0.astype(BF16))

    ck = 64

    def body(j, carry):
        rows = pl.ds(pl.multiple_of(j * bs, bs), bs)
        for h in range(nh):
            s_scr[h][...] = _dot(k_ref[rows, h * hd:(h + 1) * hd], qb_scr[h])
        alphas = []
        for h in range(nh):
            bias = bias_scr[h, pl.ds(j, 1), :]
            cmax = s_scr[h][0:ck, :]
            for c in range(1, bs // ck):
                cmax = jnp.maximum(cmax, s_scr[h][c * ck:(c + 1) * ck, :])
            m = m_scr[h]
            m_new = jnp.maximum(m, jnp.max(cmax, axis=0, keepdims=True) + bias)
            alpha = jnp.exp2(m - m_new)
            shift = m_new - bias
            for c in range(bs // ck):
                p = jnp.exp2(s_scr[h][c * ck:(c + 1) * ck, :] - shift)
                p_scr[h][c * ck:(c + 1) * ck, :] = p.astype(BF16)
            m_scr[h] = m_new
            alphas.append(alpha)
        for h in range(nh):
            acc_scr[h][...] = (alphas[h] * acc_scr[h][...]
                               + _dot(vt_ref[j, h * vr:(h + 1) * vr, :], p_scr[h][...]))
        return carry

    lax.fori_loop(0, i, body, 0)
    for h in range(nh):
        acc = acc_scr[h][...]
        out_t = acc[0:hd] * (1.0 / acc[hd:hd + 1])
        o_ref[:, h * hd:(h + 1) * hd] = out_t.T.astype(o_ref.dtype)


MOBA_HEADS_PER_STEP = 8


def _moba_attn(qt, kn, vt, kmean):
    w, s = qt.shape
    bs = MOBA_BLOCK
    nb = s // bs
    nh = MOBA_HEADS_PER_STEP
    gw = nh * MOBA_HD
    resident = pl.Buffered(1)
    return pl.pallas_call(
        _moba_attn_kernel,
        grid=(MOBA_HEADS // nh, nb),
        in_specs=[pl.BlockSpec((gw, bs), lambda g, i: (g, i)),
                  pl.BlockSpec((s, gw), lambda g, i: (0, g), pipeline_mode=resident),
                  pl.BlockSpec((nb, nh * MOBA_VT_ROWS, bs), lambda g, i: (0, g, 0),
                               pipeline_mode=resident),
                  pl.BlockSpec((nb, 1, gw), lambda g, i: (0, 0, g))],
        out_specs=pl.BlockSpec((bs, gw), lambda g, i: (i, g)),
        out_shape=jax.ShapeDtypeStruct((s, w), BF16),
        scratch_shapes=[pltpu.VMEM((nh, nb, bs), F32),
                        pltpu.VMEM((nh, MOBA_HD, bs), BF16),
                        pltpu.VMEM((nh, 1, bs), F32)]
                       + [pltpu.VMEM((MOBA_VT_ROWS, bs), F32) for _ in range(nh)]
                       + [pltpu.VMEM((bs, bs), F32) for _ in range(nh)]
                       + [pltpu.VMEM((bs, bs), BF16) for _ in range(nh)],
        compiler_params=_params("parallel", "arbitrary"),
        name="moba_attn",
    )(qt, kn, vt, kmean)


def _mix_out_kernel(hm_ref, ha_ref, w_ref, x_ref, o_ref, wb_ref):
    half = hm_ref.shape[1]

    @pl.when(pl.program_id(0) == 0)
    def _():
        wb_ref[...] = w_ref[...].astype(BF16)

    o_ref[...] = (x_ref[...] + _dot(hm_ref[...], wb_ref[0:half, :])
                  + _dot(ha_ref[...], wb_ref[half:2 * half, :]))


def _mix_out(hm, ha, w, x, tm):
    s, d = x.shape
    half = hm.shape[1]
    return pl.pallas_call(
        _mix_out_kernel,
        grid=(s // tm,),
        in_specs=[pl.BlockSpec((tm, half), lambda i: (i, 0)),
                  pl.BlockSpec((tm, half), lambda i: (i, 0)),
                  pl.BlockSpec((2 * half, d), lambda i: (0, 0), pipeline_mode=pl.Buffered(1)),
                  pl.BlockSpec((tm, d), lambda i: (i, 0))],
        out_specs=pl.BlockSpec((tm, d), lambda i: (i, 0)),
        out_shape=jax.ShapeDtypeStruct((s, d), F32),
        scratch_shapes=[pltpu.VMEM((2 * half, d), BF16)],
        compiler_params=_params("arbitrary"),
        name="mix_out",
    )(hm, ha, w, x)


def _mem_kv_kernel(mem_ref, g_ref, w_ref, kg_ref, o_ref, mn_ref):
    j = pl.program_id(0)

    @pl.when(j == 0)
    def _():
        mn_ref[...] = _rms(mem_ref[...], g_ref[...]).astype(BF16)

    y = _dot(mn_ref[...], w_ref[...].astype(BF16))
    o_ref[...] = jnp.where(j < XA_HEADS, _rms(y, kg_ref[...]), y).astype(o_ref.dtype)


def _mem_kv(mem, g, wkv, k_g):
    m, d = mem.shape
    hd = d // XA_HEADS
    return pl.pallas_call(
        _mem_kv_kernel,
        grid=(2 * XA_HEADS,),
        in_specs=[pl.BlockSpec((m, d), lambda j: (0, 0)),
                  pl.BlockSpec((1, d), lambda j: (0, 0)),
                  pl.BlockSpec((d, hd), lambda j: (0, j)),
                  pl.BlockSpec((1, hd), lambda j: (0, 0))],
        out_specs=pl.BlockSpec((m, hd), lambda j: (0, j)),
        out_shape=jax.ShapeDtypeStruct((m, 2 * d), BF16),
        scratch_shapes=[pltpu.VMEM((m, d), BF16)],
        compiler_params=_params("arbitrary"),
        name="mem_kv",
    )(mem, g, wkv, k_g)


def _xattn_kernel(h_ref, g_ref, wq_hbm, kv_ref, qg_ref, wo_hbm, o_ref, wqb, wob, stage, o_all, sem):
    d = h_ref.shape[1]
    hd = d // XA_HEADS
    n_slab = 2 * XA_HEADS

    @pl.when(pl.program_id(0) == 0)
    def _():
        def slab_copy(k):
            src = wq_hbm if k < XA_HEADS else wo_hbm
            cols = pl.ds((k % XA_HEADS) * hd, hd)
            return pltpu.make_async_copy(src.at[:, cols], stage.at[k % 2], sem.at[k % 2])

        slab_copy(0).start()
        for k in range(n_slab):
            if k + 1 < n_slab:
                slab_copy(k + 1).start()
            slab_copy(k).wait()
            dst = wqb if k < XA_HEADS else wob
            dst[:, (k % XA_HEADS) * hd:(k % XA_HEADS + 1) * hd] = stage[k % 2].astype(BF16)

    h = h_ref[...]
    q_all = _dot(_rms(h, g_ref[...]).astype(BF16), wqb[...])
    for j in range(XA_HEADS):
        hs = slice(j * hd, (j + 1) * hd)
        q = _rms(q_all[:, hs], qg_ref[...]).astype(BF16)
        sc = _dot_nt(q, kv_ref[:, hs]) * (hd ** -0.5)
        p = jnp.exp(sc - jnp.max(sc, axis=-1, keepdims=True))
        p = p * (1.0 / jnp.sum(p, axis=-1, keepdims=True))
        o_all[:, hs] = _dot(p.astype(BF16), kv_ref[:, d + j * hd:d + (j + 1) * hd]).astype(BF16)
    o_ref[...] = h + _dot(o_all[...], wob[...])


def _xattn(h, g, wq, kv, q_g, wo, tm):
    s, d = h.shape
    m = kv.shape[0]
    hd = d // XA_HEADS
    return pl.pallas_call(
        _xattn_kernel,
        grid=(s // tm,),
        in_specs=[pl.BlockSpec((tm, d), lambda i: (i, 0)),
                  pl.BlockSpec((1, d), lambda i: (0, 0)),
                  pl.BlockSpec(memory_space=pl.ANY),
                  pl.BlockSpec((m, 2 * d), lambda i: (0, 0)),
                  pl.BlockSpec((1, hd), lambda i: (0, 0)),
                  pl.BlockSpec(memory_space=pl.ANY)],
        out_specs=pl.BlockSpec((tm, d), lambda i: (i, 0)),
        out_shape=jax.ShapeDtypeStruct((s, d), F32),
        scratch_shapes=[pltpu.VMEM((d, d), BF16), pltpu.VMEM((d, d), BF16),
                        pltpu.VMEM((2, d, hd), F32), pltpu.VMEM((tm, d), BF16),
                        pltpu.SemaphoreType.DMA((2,))],
        compiler_params=_params("arbitrary"),
        name="xattn",
    )(h, g, wq, kv, q_g, wo)


ROUTE_EID = 0
ROUTE_RANK = 2
ROUTE_WT = 4


def _router_kernel(h_ref, g_ref, w_ref, b_ref, out_ref, cnt_ref, carry):
    tm = h_ref.shape[0]

    @pl.when(pl.program_id(0) == 0)
    def _():
        carry[...] = jnp.zeros_like(carry)

    hn = _rms(h_ref[...], g_ref[...])
    hi = hn.astype(BF16)
    lo = (hn - hi.astype(F32)).astype(BF16)
    logits = (_dot(hi, w_ref[0]) + _dot(hi, w_ref[1]) + _dot(lo, w_ref[0])
              + b_ref[...])
    lane = lax.broadcasted_iota(I32, (tm, LANES), 1)

    def first_lane(mask):
        return jnp.min(jnp.where(mask, lane.astype(F32), float(LANES)), axis=-1,
                       keepdims=True).astype(I32)

    is_g = lane < MOE_GROUPS
    gmax = jnp.max(jnp.where(is_g, logits, -jnp.inf), axis=-1, keepdims=True)
    gsum = jnp.sum(jnp.where(is_g, jnp.exp(logits - gmax), 0.0), axis=-1, keepdims=True)
    gsel = first_lane(is_g & (logits == gmax))
    pg = 1.0 / gsum

    grp_lo = MOE_GROUPS + MOE_PER_GROUP * gsel
    in_grp = (lane >= grp_lo) & (lane < grp_lo + MOE_PER_GROUP)
    emax = jnp.max(jnp.where(in_grp, logits, -jnp.inf), axis=-1, keepdims=True)
    eexp = jnp.where(in_grp, jnp.exp(logits - emax), 0.0)
    eprob = eexp / jnp.sum(eexp, axis=-1, keepdims=True)
    p1 = jnp.max(jnp.where(in_grp, eprob, -1.0), axis=-1, keepdims=True)
    l1 = first_lane(in_grp & (eprob == p1))
    rest = in_grp & (lane != l1)
    p2 = jnp.max(jnp.where(rest, eprob, -1.0), axis=-1, keepdims=True)
    l2 = first_lane(rest & (eprob == p2))
    psum = p1 + p2
    w1 = pg * p1 / psum
    w2 = pg * p2 / psum

    oh1 = lane == l1
    oh2 = lane == l2
    onehot = jnp.where(oh1 | oh2, 1.0, 0.0)
    r = lax.broadcasted_iota(I32, (tm, tm), 0)
    c = lax.broadcasted_iota(I32, (tm, tm), 1)
    strict = jnp.where(c < r, 1.0, 0.0).astype(BF16)
    before = _dot(strict, onehot.astype(BF16)) + carry[0:1, :]
    rank1 = jnp.sum(jnp.where(oh1, before, 0.0), axis=-1, keepdims=True)
    rank2 = jnp.sum(jnp.where(oh2, before, 0.0), axis=-1, keepdims=True)
    total = carry[0:1, :] + jnp.sum(onehot, axis=0, keepdims=True)
    carry[...] = jnp.broadcast_to(total, carry.shape)
    cnt_ref[...] = jnp.broadcast_to(total, cnt_ref.shape)

    out = jnp.where(lane == ROUTE_EID, (l1 - MOE_GROUPS).astype(F32), 0.0)
    out = jnp.where(lane == ROUTE_EID + 1, (l2 - MOE_GROUPS).astype(F32), out)
    out = jnp.where(lane == ROUTE_RANK, rank1, out)
    out = jnp.where(lane == ROUTE_RANK + 1, rank2, out)
    out = jnp.where(lane == ROUTE_WT, w1, out)
    out = jnp.where(lane == ROUTE_WT + 1, w2, out)
    out_ref[...] = out


def _router(h, g, w, b, tm):
    s, d = h.shape
    return pl.pallas_call(
        _router_kernel,
        grid=(s // tm,),
        in_specs=[pl.BlockSpec((tm, d), lambda i: (i, 0)),
                  pl.BlockSpec((1, d), lambda i: (0, 0)),
                  pl.BlockSpec((2, d, LANES), lambda i: (0, 0, 0)),
                  pl.BlockSpec((1, LANES), lambda i: (0, 0))],
        out_specs=[pl.BlockSpec((tm, LANES), lambda i: (i, 0)),
                   pl.BlockSpec((SUBLANES, LANES), lambda i: (0, 0))],
        out_shape=[jax.ShapeDtypeStruct((s, LANES), F32),
                   jax.ShapeDtypeStruct((SUBLANES, LANES), F32)],
        scratch_shapes=[pltpu.VMEM((SUBLANES, LANES), F32)],
        compiler_params=_params("arbitrary"),
        name="router",
    )(h, g, w, b)


def _dispatch_kernel(d0_ref, d1_ref, zb_ref, h_ref, g_ref, xg_ref, buf, zero, sem, zsem):
    tm = h_ref.shape[0]
    base = pl.program_id(0) * tm

    @pl.when(pl.program_id(0) == 0)
    def _():
        zero[...] = jnp.zeros_like(zero)
        n_blk = xg_ref.shape[0] // MOE_ROWS

        def zero_copy(b):
            rows = pl.ds(pl.multiple_of(b * MOE_ROWS, MOE_ROWS), MOE_ROWS)
            return pltpu.make_async_copy(zero, xg_ref.at[rows, :], zsem)

        def start(b, carry):
            @pl.when(zb_ref[b] != 0)
            def _():
                zero_copy(b).start()
            return carry

        def finish(b, carry):
            @pl.when(zb_ref[b] != 0)
            def _():
                zero_copy(0).wait()
            return carry

        lax.fori_loop(0, n_blk, start, 0)
        lax.fori_loop(0, n_blk, finish, 0)

    buf[...] = _rms(h_ref[...], g_ref[...])

    def row_copy(t, d):
        return pltpu.make_async_copy(buf.at[pl.ds(t, 1), :], xg_ref.at[pl.ds(d, 1), :], sem)

    def issue(t, carry):
        row_copy(t, d0_ref[base + t]).start()
        row_copy(t, d1_ref[base + t]).start()
        return carry

    lax.fori_loop(0, tm, issue, 0, unroll=ROW_DMA_UNROLL)
    whole = pltpu.make_async_copy(buf, xg_ref.at[pl.ds(0, tm), :], sem)
    whole.wait()
    whole.wait()


def _dispatch(dest, zero_blk, h, g, tm):
    s, d = h.shape
    n_rows = zero_blk.shape[0] * MOE_ROWS
    grid_spec = pltpu.PrefetchScalarGridSpec(
        num_scalar_prefetch=3,
        grid=(s // tm,),
        in_specs=[pl.BlockSpec((tm, d), lambda i, d0, d1, zb: (i, 0)),
                  pl.BlockSpec((1, d), lambda i, d0, d1, zb: (0, 0))],
        out_specs=pl.BlockSpec(memory_space=pl.ANY),
        scratch_shapes=[pltpu.VMEM((tm, d), F32), pltpu.VMEM((MOE_ROWS, d), F32),
                        pltpu.SemaphoreType.DMA(()), pltpu.SemaphoreType.DMA(())],
    )
    return pl.pallas_call(
        _dispatch_kernel,
        grid_spec=grid_spec,
        out_shape=jax.ShapeDtypeStruct((n_rows, d), F32),
        compiler_params=_params("arbitrary"),
        name="moe_dispatch",
    )(dest[0], dest[1], zero_blk, h, g)


def _moe_ffn_kernel(be_ref, nu_ref, nx_ref, x_ref, wg_hbm, wu_hbm, wd_hbm, y_ref,
                    wgf, wuf, wdf, wgb, wub, wdb, slot_ref, sem):
    b = pl.program_id(0)
    e = be_ref[b]
    used = b < nu_ref[0]

    def weight_copies(expert, slot):
        return (pltpu.make_async_copy(wg_hbm.at[expert], wgf.at[slot], sem.at[slot, 0]),
                pltpu.make_async_copy(wu_hbm.at[expert], wuf.at[slot], sem.at[slot, 1]),
                pltpu.make_async_copy(wd_hbm.at[expert], wdf.at[slot], sem.at[slot, 2]))

    @pl.when(b == 0)
    def _():
        slot_ref[0] = 0
        for c in weight_copies(e, 0):
            c.start()

    @pl.when(used & ((b == 0) | (e != be_ref[jnp.maximum(b - 1, 0)])))
    def _():
        slot = slot_ref[0]
        for c in weight_copies(e, slot):
            c.wait()
        nxt = nx_ref[e]

        @pl.when(nxt >= 0)
        def _():
            for c in weight_copies(nxt, 1 - slot):
                c.start()

        wgb[...] = wgf[slot].astype(BF16)
        wub[...] = wuf[slot].astype(BF16)
        wdb[...] = wdf[slot].astype(BF16)
        slot_ref[0] = 1 - slot

    @pl.when(used)
    def _():
        xb = x_ref[...].astype(BF16)
        gate = _dot(xb, wgb[...])
        up = _dot(xb, wub[...])
        act = (gate * jax.nn.sigmoid(gate) * up).astype(BF16)
        y_ref[...] = _dot(act, wdb[...])

    @pl.when(jnp.logical_not(used))
    def _():
        y_ref[...] = jnp.zeros_like(y_ref)


def _moe_ffn(blk_e, n_used, next_e, xg, w_gate, w_up, w_down):
    n_rows, d = xg.shape
    ff = w_gate.shape[2]
    n_blk = n_rows // MOE_ROWS
    def row_blk(b, be, nu, nx):
        return jnp.minimum(b, nu[0] - 1), 0

    grid_spec = pltpu.PrefetchScalarGridSpec(
        num_scalar_prefetch=3,
        grid=(n_blk,),
        in_specs=[pl.BlockSpec((MOE_ROWS, d), row_blk),
                  pl.BlockSpec(memory_space=pl.ANY),
                  pl.BlockSpec(memory_space=pl.ANY),
                  pl.BlockSpec(memory_space=pl.ANY)],
        out_specs=pl.BlockSpec((MOE_ROWS, d), lambda b, be, nu, nx: (b, 0)),
        scratch_shapes=[pltpu.VMEM((2, d, ff), F32), pltpu.VMEM((2, d, ff), F32),
                        pltpu.VMEM((2, ff, d), F32),
                        pltpu.VMEM((d, ff), BF16), pltpu.VMEM((d, ff), BF16),
                        pltpu.VMEM((ff, d), BF16),
                        pltpu.SMEM((1,), I32), pltpu.SemaphoreType.DMA((2, 3))],
    )
    return pl.pallas_call(
        _moe_ffn_kernel,
        grid_spec=grid_spec,
        out_shape=jax.ShapeDtypeStruct((n_rows, d), F32),
        compiler_params=_params("arbitrary"),
        name="moe_ffn",
    )(blk_e, n_used, next_e, xg, w_gate, w_up, w_down)


def _combine_kernel(d0_ref, d1_ref, h_ref, r_ref, y_ref, o_ref, buf, sem):
    tm = h_ref.shape[0]
    base = pl.program_id(0) * tm

    def row_copy(k, t, d):
        return pltpu.make_async_copy(y_ref.at[pl.ds(d, 1), :], buf.at[k, pl.ds(t, 1), :], sem)

    def issue(t, carry):
        row_copy(0, t, d0_ref[base + t]).start()
        row_copy(1, t, d1_ref[base + t]).start()
        return carry

    lax.fori_loop(0, tm, issue, 0, unroll=ROW_DMA_UNROLL)
    for k in range(2):
        pltpu.make_async_copy(y_ref.at[pl.ds(0, tm), :], buf.at[k], sem).wait()
    r = r_ref[...]
    o_ref[...] = (h_ref[...] + r[:, ROUTE_WT:ROUTE_WT + 1] * buf[0]
                  + r[:, ROUTE_WT + 1:ROUTE_WT + 2] * buf[1])


def _combine(dest, h, routed, y, tm):
    s, d = h.shape
    grid_spec = pltpu.PrefetchScalarGridSpec(
        num_scalar_prefetch=2,
        grid=(s // tm,),
        in_specs=[pl.BlockSpec((tm, d), lambda i, d0, d1: (i, 0)),
                  pl.BlockSpec((tm, LANES), lambda i, d0, d1: (i, 0)),
                  pl.BlockSpec(memory_space=pl.ANY)],
        out_specs=pl.BlockSpec((tm, d), lambda i, d0, d1: (i, 0)),
        scratch_shapes=[pltpu.VMEM((2, tm, d), F32), pltpu.SemaphoreType.DMA(())],
    )
    return pl.pallas_call(
        _combine_kernel,
        grid_spec=grid_spec,
        out_shape=jax.ShapeDtypeStruct((s, d), F32),
        compiler_params=_params("arbitrary"),
        name="moe_combine",
    )(dest[0], dest[1], h, routed, y)


def _layer(h, mem, pos, norm_mix_g, w_in, gate_b, conv_w, conv_b, out_g, moba_q_g, moba_k_g, w_out,
           norm_cross_g, norm_mem_g, xa_wq, xa_wkv, xa_q_g, xa_k_g, xa_wo, norm_ffn_g,
           router_group_w, router_group_b, router_expert_w, router_expert_b,
           exp_w_gate, exp_w_up, exp_w_down):
    s, d = h.shape
    row = lambda v: v.reshape(1, -1)

    n_gate = 2 * MLSTM_HEADS
    gate_lo = 2 * MLSTM_HEADS * MLSTM_QK + 2 * MLSTM_HEADS * MLSTM_V
    proj_tn = 512
    w_a = w_in[:, :gate_lo].astype(BF16)
    w_b = w_in[:, gate_lo + n_gate:].astype(BF16)
    w_g = jnp.pad(w_in[:, gate_lo:gate_lo + n_gate], ((0, 0), (0, proj_tn - n_gate))).astype(BF16)
    proj = _proj_in(h, row(norm_mix_g), w_a, w_b, w_g, tm=min(s, 1024))

    gate_b_row = jnp.pad(gate_b, (0, LANES - n_gate)).reshape(1, LANES)
    hm = _mlstm(proj, w_a.shape[1] + w_b.shape[1], gate_b_row, conv_w, row(conv_b), row(out_g))
    qn, kn, vb, kmean = _moba_prep(proj, gate_lo, pos, row(moba_q_g), row(moba_k_g))
    ha = _moba_attn(qn, kn, vb, kmean)
    h = _mix_out(hm, ha, w_out, h, tm=512)

    kv = _mem_kv(mem, row(norm_mem_g), xa_wkv, row(xa_k_g))
    h = _xattn(h, row(norm_cross_g), xa_wq, kv, row(xa_q_g), xa_wo, tm=512)

    w_route = jnp.concatenate(
        [router_group_w, router_expert_w,
         jnp.zeros((d, LANES - MOE_GROUPS - MOE_EXPERTS), router_group_w.dtype)], axis=1)
    w_route_hi = w_route.astype(BF16)
    w_route = jnp.stack([w_route_hi, (w_route - w_route_hi.astype(F32)).astype(BF16)])
    b_route = jnp.pad(jnp.concatenate([router_group_b, router_expert_b]),
                      (0, LANES - MOE_GROUPS - MOE_EXPERTS)).reshape(1, LANES)
    routed, counts = _router(h, row(norm_ffn_g), w_route, b_route, tm=512)

    cnt = counts[0, MOE_GROUPS:MOE_GROUPS + MOE_EXPERTS].astype(I32)
    padded = (cnt + MOE_ROWS - 1) // MOE_ROWS * MOE_ROWS
    pend = jnp.cumsum(padded)
    seg_start = (pend - padded).astype(F32)[None, :]
    expert_ids = jnp.arange(MOE_EXPERTS, dtype=F32)[None, :]

    def dest_of(k):
        mine = routed[:, ROUTE_EID + k:ROUTE_EID + k + 1] == expert_ids
        return (jnp.sum(jnp.where(mine, seg_start, 0.0), axis=1) + routed[:, ROUTE_RANK + k]).astype(I32)

    dest = (dest_of(0), dest_of(1))
    n_blk = -(-2 * s // MOE_ROWS) + MOE_EXPERTS
    blk_start = jnp.arange(n_blk, dtype=I32) * MOE_ROWS
    blk_e = jnp.minimum(jnp.sum((pend[None, :] <= blk_start[:, None]).astype(I32), axis=1),
                        MOE_EXPERTS - 1)
    n_used = (pend[-1:] // MOE_ROWS).astype(I32)
    ids = jnp.arange(MOE_EXPERTS, dtype=I32)
    later_used = (ids[None, :] > ids[:, None]) & (cnt[None, :] > 0)
    next_e = jnp.min(jnp.where(later_used, ids[None, :], MOE_EXPERTS), axis=1)
    next_e = jnp.where(next_e < MOE_EXPERTS, next_e, -1).astype(I32)

    is_blk_e = blk_e[:, None] == ids[None, :]
    seg_end = jnp.sum(jnp.where(is_blk_e, (pend - padded + cnt)[None, :], 0), axis=1)
    zero_blk = (blk_start + MOE_ROWS > seg_end).astype(I32)
    xg = _dispatch(dest, zero_blk, h, row(norm_ffn_g), tm=256)
    y = _moe_ffn(blk_e, n_used, next_e, xg, exp_w_gate, exp_w_up, exp_w_down)
    return _combine(dest, h, routed, y, tm=256)


def kernel(x, mem, positions, norm_mix_g, w_in, mlstm_gate_b, mlstm_conv_w, mlstm_conv_b, mlstm_out_g, moba_q_g, moba_k_g, w_out, norm_cross_g, norm_mem_g, xa_wq, xa_wkv, xa_q_g, xa_k_g, xa_wo, norm_ffn_g, router_group_w, router_group_b, router_expert_w, router_expert_b, exp_w_gate, exp_w_up, exp_w_down):
    bsz, s, _ = x.shape
    assert bsz == 1, "single-sequence prefill only"
    per_layer = (norm_mix_g, w_in, mlstm_gate_b, mlstm_conv_w, mlstm_conv_b, mlstm_out_g, moba_q_g,
                 moba_k_g, w_out, norm_cross_g, norm_mem_g, xa_wq, xa_wkv, xa_q_g, xa_k_g, xa_wo,
                 norm_ffn_g, router_group_w, router_group_b, router_expert_w, router_expert_b,
                 exp_w_gate, exp_w_up, exp_w_down)
    h = x[0]
    pos = positions.reshape(s, 1)
    for l in range(norm_mix_g.shape[0]):
        h = _layer(h, mem[0], pos, *(p[l] for p in per_layer))
    return h[None]
```

```python
import functools
import math

import jax
import jax.numpy as jnp
from jax import lax
from jax.experimental import pallas as pl
from jax.experimental.pallas import tpu as pltpu

F32 = jnp.float32
BF16 = jnp.bfloat16
I32 = jnp.int32

EPS = 1e-6
LANES = 128
SUBLANES = 8
VMEM_LIMIT = 56 * 1024 * 1024

MLSTM_HEADS = 4
MLSTM_QK = 128
MLSTM_V = 256
MLSTM_CHUNK = 128
MLSTM_CONV = 4
MOBA_HEADS = 8
MOBA_HD = 128
MOBA_BLOCK = 256
MOBA_TOPK = 3
MOBA_VT_ROWS = MOBA_HD + 16
ROPE_DIM = 32
ROPE_THETA = 500000.0
XA_HEADS = 4
MOE_GROUPS = 8
MOE_PER_GROUP = 8
MOE_EXPERTS = MOE_GROUPS * MOE_PER_GROUP
MOE_ROWS = 128
ROW_DMA_UNROLL = 8

NT_DIMS = (((1,), (1,)), ((), ()))


def _params(*sem):
    return pltpu.CompilerParams(dimension_semantics=sem, vmem_limit_bytes=VMEM_LIMIT)


def _rms(x, g):
    return x * lax.rsqrt(jnp.mean(x * x, axis=-1, keepdims=True) + EPS) * g


def _dot(a, b):
    return jnp.dot(a, b, preferred_element_type=F32)


def _dot_nt(a, b, precision=None):
    return lax.dot_general(a, b, NT_DIMS, precision=precision, preferred_element_type=F32)


def _proj_in_kernel(x_ref, g_ref, wa_ref, wb_ref, wg_ref, o_ref, xn_ref, *, na, nb):
    j = pl.program_id(1)

    @pl.when(j == 0)
    def _():
        xn_ref[...] = _rms(x_ref[...], g_ref[...]).astype(BF16)

    @pl.when(j < na)
    def _():
        o_ref[...] = _dot(xn_ref[...], wa_ref[...])

    @pl.when((j >= na) & (j < na + nb))
    def _():
        o_ref[...] = _dot(xn_ref[...], wb_ref[...])

    @pl.when(j == na + nb)
    def _():
        o_ref[...] = _dot(xn_ref[...], wg_ref[...])


def _proj_in(x, g, w_a, w_b, w_g, tm):
    s, d = x.shape
    tn = w_g.shape[1]
    na, nb = w_a.shape[1] // tn, w_b.shape[1] // tn
    return pl.pallas_call(
        functools.partial(_proj_in_kernel, na=na, nb=nb),
        grid=(s // tm, na + nb + 1),
        in_specs=[pl.BlockSpec((tm, d), lambda i, j: (i, 0)),
                  pl.BlockSpec((1, d), lambda i, j: (0, 0)),
                  pl.BlockSpec((d, tn), lambda i, j: (0, jnp.minimum(j, na - 1))),
                  pl.BlockSpec((d, tn), lambda i, j: (0, jnp.clip(j - na, 0, nb - 1))),
                  pl.BlockSpec((d, tn), lambda i, j: (0, 0))],
        out_specs=pl.BlockSpec((tm, tn), lambda i, j: (i, j)),
        out_shape=jax.ShapeDtypeStruct((s, (na + nb + 1) * tn), F32),
        scratch_shapes=[pltpu.VMEM((tm, d), BF16)],
        compiler_params=_params("parallel", "arbitrary"),
        name="proj_in",
    )(x, g, w_a, w_b, w_g)


def _split3(x):
    hi = x.astype(BF16)
    r = x - hi.astype(F32)
    mid = r.astype(BF16)
    lo = (r - mid.astype(F32)).astype(BF16)
    return hi, mid, lo


def _log_sigmoid(x):
    return jnp.minimum(x, 0.0) - jnp.log(1.0 + jnp.exp(-jnp.abs(x)))


def _mlstm_kernel(q_ref, k_ref, v_ref, og_ref, gt_ref, gb_ref, cw_ref, cb_ref, outg_ref, hm_ref,
                  qext, kext, c_scr, n_scr, m_scr):
    L = MLSTM_CHUNK
    pad = SUBLANES
    qkw = MLSTM_HEADS * MLSTM_QK

    @pl.when(pl.program_id(0) == 0)
    def _():
        qext[0:pad, :] = jnp.zeros((pad, qkw), F32)
        kext[0:pad, :] = jnp.zeros((pad, qkw), F32)
        c_scr[...] = jnp.zeros_like(c_scr)
        n_scr[...] = jnp.zeros_like(n_scr)
        m_scr[...] = jnp.zeros_like(m_scr)

    qext[pad:pad + L, :] = q_ref[...]
    kext[pad:pad + L, :] = k_ref[...]

    def conv_silu(ext, lo):
        acc = jnp.broadcast_to(cb_ref[:, lo:lo + qkw], (L, qkw))
        for j in range(MLSTM_CONV):
            shift = MLSTM_CONV - 1 - j
            acc = acc + cw_ref[j:j + 1, lo:lo + qkw] * ext[pad - shift:pad - shift + L, :]
        return acc * jax.nn.sigmoid(acc)

    qc = conv_silu(qext, 0)
    kc = conv_silu(kext, qkw)
    qext[0:pad, :] = qext[L:L + pad, :]
    kext[0:pad, :] = kext[L:L + pad, :]

    gts = gt_ref[...] + gb_ref[...]
    gtr = gts.T
    row = lax.broadcasted_iota(I32, (L, L), 0)
    col = lax.broadcasted_iota(I32, (L, L), 1)
    causal = col <= row
    tri_lo = jnp.where(causal, 1.0, 0.0).astype(BF16)
    tri_up = jnp.where(row <= col, 1.0, 0.0).astype(BF16)
    bc_all = sum(_dot(tri_lo, part) for part in _split3(_log_sigmoid(gts)))
    br_all = sum(_dot(part, tri_up) for part in _split3(_log_sigmoid(gtr)))

    for h in range(MLSTM_HEADS):
        fi = MLSTM_HEADS + h
        m_prev = m_scr[h, 0:1, 0:1]
        b_c = bc_all[:, fi:fi + 1]
        b_r = br_all[fi:fi + 1, :]
        i_c = gts[:, h:h + 1]
        i_r = gtr[h:h + 1, :]
        g = b_c[L - 1:L, :]
        d = jnp.where(causal, (b_c - b_r) + i_r, -jnp.inf)
        inter = b_c + m_prev
        m_t = jnp.maximum(inter, jnp.max(d, axis=-1, keepdims=True))
        w_inter = jnp.exp(inter - m_t)
        qh = qc[:, h * MLSTM_QK:(h + 1) * MLSTM_QK] * (MLSTM_QK ** -0.5)
        kh = kc[:, h * MLSTM_QK:(h + 1) * MLSTM_QK]
        vb = v_ref[:, h * MLSTM_V:(h + 1) * MLSTM_V].astype(BF16)
        qb = qh.astype(BF16)
        p = jnp.exp(d - m_t) * _dot_nt(qb, kh.astype(BF16))
        c_prev = c_scr[h]
        n_prev = n_scr[h, 0:1, :]
        num = w_inter * _dot(qb, c_prev.astype(BF16)) + _dot(p.astype(BF16), vb)
        den = (w_inter * jnp.sum(qh * n_prev, axis=-1, keepdims=True)
               + jnp.sum(p, axis=-1, keepdims=True))
        hh = num * (1.0 / jnp.maximum(jnp.abs(den), jnp.exp(-m_t)))

        a = (g - b_c) + i_c
        m_new = jnp.maximum(g + m_prev, jnp.max(a, axis=0, keepdims=True))
        kw = kh * jnp.exp(a - m_new)
        decay = jnp.exp(g + m_prev - m_new)
        c_scr[h] = decay * c_prev + _dot(kw.T.astype(BF16), vb)
        n_scr[h, 0:1, :] = decay * n_prev + jnp.sum(kw, axis=0, keepdims=True)
        m_scr[h] = jnp.broadcast_to(m_new, (SUBLANES, LANES))

        hn = hh * lax.rsqrt(jnp.mean(hh * hh, axis=-1, keepdims=True) + EPS)
        vs = slice(h * MLSTM_V, (h + 1) * MLSTM_V)
        hn = hn * outg_ref[:, vs] * jax.nn.sigmoid(og_ref[:, vs])
        hm_ref[:, vs] = hn.astype(hm_ref.dtype)


def _mlstm(proj, gate_col, gate_b, conv_w, conv_b, out_g):
    s = proj.shape[0]
    L = MLSTM_CHUNK
    qkw = MLSTM_HEADS * MLSTM_QK
    vw = MLSTM_HEADS * MLSTM_V
    gate_blk = gate_col // LANES
    return pl.pallas_call(
        _mlstm_kernel,
        grid=(s // L,),
        in_specs=[pl.BlockSpec((L, qkw), lambda c: (c, 0)),
                  pl.BlockSpec((L, qkw), lambda c: (c, 1)),
                  pl.BlockSpec((L, vw), lambda c: (c, 1)),
                  pl.BlockSpec((L, vw), lambda c: (c, 2)),
                  pl.BlockSpec((L, LANES), lambda c: (c, gate_blk)),
                  pl.BlockSpec((1, LANES), lambda c: (0, 0)),
                  pl.BlockSpec((MLSTM_CONV, 2 * qkw), lambda c: (0, 0)),
                  pl.BlockSpec((1, 2 * qkw), lambda c: (0, 0)),
                  pl.BlockSpec((1, vw), lambda c: (0, 0))],
        out_specs=pl.BlockSpec((L, vw), lambda c: (c, 0)),
        out_shape=jax.ShapeDtypeStruct((s, vw), BF16),
        scratch_shapes=[pltpu.VMEM((L + SUBLANES, qkw), F32),
                        pltpu.VMEM((L + SUBLANES, qkw), F32),
                        pltpu.VMEM((MLSTM_HEADS, MLSTM_QK, MLSTM_V), F32),
                        pltpu.VMEM((MLSTM_HEADS, SUBLANES, LANES), F32),
                        pltpu.VMEM((MLSTM_HEADS, SUBLANES, LANES), F32)],
        compiler_params=_params("arbitrary"),
        name="mlstm",
    )(proj, proj, proj, proj, proj, gate_b, conv_w, conv_b, out_g)


def _moba_prep_kernel(q_ref, k_ref, v_ref, pos_ref, qg_ref, kg_ref, qt_ref, kn_ref, vt_ref, km_ref):
    rows = q_ref.shape[0]
    half = ROPE_DIM // 2
    lane = lax.broadcasted_iota(I32, (1, MOBA_HD), 1)
    inv_freq = jnp.exp((lane & (half - 1)).astype(F32) * (-(2.0 / ROPE_DIM) * math.log(ROPE_THETA)))
    ang = pos_ref[...].astype(F32) * inv_freq
    cos = jnp.where(lane < ROPE_DIM, jnp.cos(ang), 1.0)
    sin = jnp.sin(ang)
    sin = jnp.where(lane < half, -sin, jnp.where(lane < ROPE_DIM, sin, 0.0))

    def rope(x):
        partner = jnp.where(lane < half, pltpu.roll(x, MOBA_HD - half, 1), pltpu.roll(x, half, 1))
        return x * cos + partner * sin

    for h in range(MOBA_HEADS):
        hs = slice(h * MOBA_HD, (h + 1) * MOBA_HD)
        qt_ref[hs, :] = rope(_rms(q_ref[:, hs], qg_ref[...])).T
        kn = rope(_rms(k_ref[:, hs], kg_ref[...]))
        kn_ref[:, hs] = kn.astype(BF16)
        km_ref[0, :, hs] = jnp.sum(kn, axis=0, keepdims=True) * (1.0 / rows)
        vlo = h * MOBA_VT_ROWS
        vt_ref[0, vlo:vlo + MOBA_HD, :] = v_ref[:, hs].T.astype(BF16)
        vt_ref[0, vlo + MOBA_HD:vlo + MOBA_VT_ROWS, :] = jnp.ones((MOBA_VT_ROWS - MOBA_HD, rows), BF16)


def _moba_prep(proj, q_col, pos, q_g, k_g):
    s = proj.shape[0]
    w = MOBA_HEADS * MOBA_HD
    bs = MOBA_BLOCK
    nb = s // bs
    first = q_col // w
    return pl.pallas_call(
        _moba_prep_kernel,
        grid=(nb,),
        in_specs=[pl.BlockSpec((bs, w), lambda i: (i, first)),
                  pl.BlockSpec((bs, w), lambda i: (i, first + 1)),
                  pl.BlockSpec((bs, w), lambda i: (i, first + 2)),
                  pl.BlockSpec((bs, 1), lambda i: (i, 0)),
                  pl.BlockSpec((1, MOBA_HD), lambda i: (0, 0)),
                  pl.BlockSpec((1, MOBA_HD), lambda i: (0, 0))],
        out_specs=[pl.BlockSpec((w, bs), lambda i: (0, i)),
                   pl.BlockSpec((bs, w), lambda i: (i, 0)),
                   pl.BlockSpec((1, MOBA_HEADS * MOBA_VT_ROWS, bs), lambda i: (i, 0, 0)),
                   pl.BlockSpec((1, 1, w), lambda i: (i, 0, 0))],
        out_shape=[jax.ShapeDtypeStruct((w, s), F32),
                   jax.ShapeDtypeStruct((s, w), BF16),
                   jax.ShapeDtypeStruct((nb, MOBA_HEADS * MOBA_VT_ROWS, bs), BF16),
                   jax.ShapeDtypeStruct((nb, 1, w), F32)],
        compiler_params=_params("parallel"),
        name="moba_prep",
    )(proj, proj, proj, pos, q_g, k_g)


def _moba_attn_kernel(qt_ref, k_ref, vt_ref, km_ref, o_ref, bias_scr, qb_scr, m_scr, *tiles):
    i = pl.program_id(1)
    bs = MOBA_BLOCK
    hd = MOBA_HD
    nh = len(tiles) // 3
    vr = MOBA_VT_ROWS
    acc_scr = tiles[:nh]
    s_scr = tiles[nh:2 * nh]
    p_scr = tiles[2 * nh:]
    nb = km_ref.shape[0]
    c2 = (hd ** -0.5) * math.log2(math.e)
    blk = lax.broadcasted_iota(I32, (nb, bs), 0)
    blk_f = blk.astype(F32)
    valid = blk < i
    key = lax.broadcasted_iota(I32, (bs, bs), 0)
    qry = lax.broadcasted_iota(I32, (bs, bs), 1)
    own = pl.ds(pl.multiple_of(i * bs, bs), bs)

    for h in range(nh):
        hs = slice(h * hd, (h + 1) * hd)
        qt = qt_ref[hs, :]
        gate = jnp.dot(km_ref[:, 0, hs], qt, precision=lax.Precision.HIGHEST,
                       preferred_element_type=F32)
        cand = valid
        for _ in range(MOBA_TOPK):
            gmax = jnp.max(jnp.where(cand, gate, -jnp.inf), axis=0, keepdims=True)
            hit = cand & (gate == gmax)
            first = jnp.min(jnp.where(hit, blk_f, float(nb)), axis=0, keepdims=True)
            cand = cand & jnp.logical_not(hit & (blk_f == first))
        bias_scr[h] = jnp.where(valid & jnp.logical_not(cand), 0.0, -jnp.inf)

        qb = (qt * c2).astype(BF16)
        qb_scr[h] = qb
        s0 = jnp.where(key <= qry, _dot(k_ref[own, hs], qb), -jnp.inf)
        m0 = jnp.max(s0, axis=0, keepdims=True)
        p0 = jnp.exp2(s0 - m0)
        m_scr[h] = m0
        acc_scr[h][...] = _dot(vt_ref[i, h * vr:(h + 1) * vr, :], p0.astype(BF16))

    ck = 64

    def body(j, carry):
        rows = pl.ds(pl.multiple_of(j * bs, bs), bs)
        for h in range(nh):
            s_scr[h][...] = _dot(k_ref[rows, h * hd:(h + 1) * hd], qb_scr[h])
        alphas = []
        for h in range(nh):
            bias = bias_scr[h, pl.ds(j, 1), :]
            cmax = s_scr[h][0:ck, :]
            for c in range(1, bs // ck):
                cmax = jnp.maximum(cmax, s_scr[h][c * ck:(c + 1) * ck, :])
            m = m_scr[h]
            m_new = jnp.maximum(m, jnp.max(cmax, axis=0, keepdims=True) + bias)
            alpha = jnp.exp2(m - m_new)
            shift = m_new - bias
            for c in range(bs // ck):
                p = jnp.exp2(s_scr[h][c * ck:(c + 1) * ck, :] - shift)
                p_scr[h][c * ck:(c + 1) * ck, :] = p.astype(BF16)
            m_scr[h] = m_new
            alphas.append(alpha)
        for h in range(nh):
            acc_scr[h][...] = (alphas[h] * acc_scr[h][...]
                               + _dot(vt_ref[j, h * vr:(h + 1) * vr, :], p_scr[h][...]))
        return carry

    lax.fori_loop(0, i, body, 0)
    for h in range(nh):
        acc = acc_scr[h][...]
        out_t = acc[0:hd] * (1.0 / acc[hd:hd + 1])
        o_ref[:, h * hd:(h + 1) * hd] = out_t.T.astype(o_ref.dtype)


MOBA_HEADS_PER_STEP = 8


def _moba_attn(qt, kn, vt, kmean):
    w, s = qt.shape
    bs = MOBA_BLOCK
    nb = s // bs
    nh = MOBA_HEADS_PER_STEP
    gw = nh * MOBA_HD
    resident = pl.Buffered(1)
    return pl.pallas_call(
        _moba_attn_kernel,
        grid=(MOBA_HEADS // nh, nb),
        in_specs=[pl.BlockSpec((gw, bs), lambda g, i: (g, i)),
                  pl.BlockSpec((s, gw), lambda g, i: (0, g), pipeline_mode=resident),
                  pl.BlockSpec((nb, nh * MOBA_VT_ROWS, bs), lambda g, i: (0, g, 0),
                               pipeline_mode=resident),
                  pl.BlockSpec((nb, 1, gw), lambda g, i: (0, 0, g))],
        out_specs=pl.BlockSpec((bs, gw), lambda g, i: (i, g)),
        out_shape=jax.ShapeDtypeStruct((s, w), BF16),
        scratch_shapes=[pltpu.VMEM((nh, nb, bs), F32),
                        pltpu.VMEM((nh, MOBA_HD, bs), BF16),
                        pltpu.VMEM((nh, 1, bs), F32)]
                       + [pltpu.VMEM((MOBA_VT_ROWS, bs), F32) for _ in range(nh)]
                       + [pltpu.VMEM((bs, bs), F32) for _ in range(nh)]
                       + [pltpu.VMEM((bs, bs), BF16) for _ in range(nh)],
        compiler_params=_params("parallel", "arbitrary"),
        name="moba_attn",
    )(qt, kn, vt, kmean)


def _mix_out_kernel(hm_ref, ha_ref, w_ref, x_ref, o_ref, wb_ref):
    half = hm_ref.shape[1]

    @pl.when(pl.program_id(0) == 0)
    def _():
        wb_ref[...] = w_ref[...].astype(BF16)

    o_ref[...] = (x_ref[...] + _dot(hm_ref[...], wb_ref[0:half, :])
                  + _dot(ha_ref[...], wb_ref[half:2 * half, :]))


def _mix_out(hm, ha, w, x, tm):
    s, d = x.shape
    half = hm.shape[1]
    return pl.pallas_call(
        _mix_out_kernel,
        grid=(s // tm,),
        in_specs=[pl.BlockSpec((tm, half), lambda i: (i, 0)),
                  pl.BlockSpec((tm, half), lambda i: (i, 0)),
                  pl.BlockSpec((2 * half, d), lambda i: (0, 0), pipeline_mode=pl.Buffered(1)),
                  pl.BlockSpec((tm, d), lambda i: (i, 0))],
        out_specs=pl.BlockSpec((tm, d), lambda i: (i, 0)),
        out_shape=jax.ShapeDtypeStruct((s, d), F32),
        scratch_shapes=[pltpu.VMEM((2 * half, d), BF16)],
        compiler_params=_params("arbitrary"),
        name="mix_out",
    )(hm, ha, w, x)


def _mem_kv_kernel(mem_ref, g_ref, w_ref, kg_ref, o_ref, mn_ref):
    j = pl.program_id(0)

    @pl.when(j == 0)
    def _():
        mn_ref[...] = _rms(mem_ref[...], g_ref[...]).astype(BF16)

    y = _dot(mn_ref[...], w_ref[...].astype(BF16))
    o_ref[...] = jnp.where(j < XA_HEADS, _rms(y, kg_ref[...]), y).astype(o_ref.dtype)


def _mem_kv(mem, g, wkv, k_g):
    m, d = mem.shape
    hd = d // XA_HEADS
    return pl.pallas_call(
        _mem_kv_kernel,
        grid=(2 * XA_HEADS,),
        in_specs=[pl.BlockSpec((m, d), lambda j: (0, 0)),
                  pl.BlockSpec((1, d), lambda j: (0, 0)),
                  pl.BlockSpec((d, hd), lambda j: (0, j)),
                  pl.BlockSpec((1, hd), lambda j: (0, 0))],
        out_specs=pl.BlockSpec((m, hd), lambda j: (0, j)),
        out_shape=jax.ShapeDtypeStruct((m, 2 * d), BF16),
        scratch_shapes=[pltpu.VMEM((m, d), BF16)],
        compiler_params=_params("arbitrary"),
        name="mem_kv",
    )(mem, g, wkv, k_g)


def _xattn_kernel(h_ref, g_ref, wq_hbm, kv_ref, qg_ref, wo_hbm, o_ref, wqb, wob, stage, o_all, sem):
    d = h_ref.shape[1]
    hd = d // XA_HEADS
    n_slab = 2 * XA_HEADS

    @pl.when(pl.program_id(0) == 0)
    def _():
        def slab_copy(k):
            src = wq_hbm if k < XA_HEADS else wo_hbm
            cols = pl.ds((k % XA_HEADS) * hd, hd)
            return pltpu.make_async_copy(src.at[:, cols], stage.at[k % 2], sem.at[k % 2])

        slab_copy(0).start()
        for k in range(n_slab):
            if k + 1 < n_slab:
                slab_copy(k + 1).start()
            slab_copy(k).wait()
            dst = wqb if k < XA_HEADS else wob
            dst[:, (k % XA_HEADS) * hd:(k % XA_HEADS + 1) * hd] = stage[k % 2].astype(BF16)

    h = h_ref[...]
    q_all = _dot(_rms(h, g_ref[...]).astype(BF16), wqb[...])
    for j in range(XA_HEADS):
        hs = slice(j * hd, (j + 1) * hd)
        q = _rms(q_all[:, hs], qg_ref[...]).astype(BF16)
        sc = _dot_nt(q, kv_ref[:, hs]) * (hd ** -0.5)
        p = jnp.exp(sc - jnp.max(sc, axis=-1, keepdims=True))
        p = p * (1.0 / jnp.sum(p, axis=-1, keepdims=True))
        o_all[:, hs] = _dot(p.astype(BF16), kv_ref[:, d + j * hd:d + (j + 1) * hd]).astype(BF16)
    o_ref[...] = h + _dot(o_all[...], wob[...])


def _xattn(h, g, wq, kv, q_g, wo, tm):
    s, d = h.shape
    m = kv.shape[0]
    hd = d // XA_HEADS
    return pl.pallas_call(
        _xattn_kernel,
        grid=(s // tm,),
        in_specs=[pl.BlockSpec((tm, d), lambda i: (i, 0)),
                  pl.BlockSpec((1, d), lambda i: (0, 0)),
                  pl.BlockSpec(memory_space=pl.ANY),
                  pl.BlockSpec((m, 2 * d), lambda i: (0, 0)),
                  pl.BlockSpec((1, hd), lambda i: (0, 0)),
                  pl.BlockSpec(memory_space=pl.ANY)],
        out_specs=pl.BlockSpec((tm, d), lambda i: (i, 0)),
        out_shape=jax.ShapeDtypeStruct((s, d), F32),
        scratch_shapes=[pltpu.VMEM((d, d), BF16), pltpu.VMEM((d, d), BF16),
                        pltpu.VMEM((2, d, hd), F32), pltpu.VMEM((tm, d), BF16),
                        pltpu.SemaphoreType.DMA((2,))],
        compiler_params=_params("arbitrary"),
        name="xattn",
    )(h, g, wq, kv, q_g, wo)


ROUTE_EID = 0
ROUTE_RANK = 2
ROUTE_WT = 4


def _router_kernel(h_ref, g_ref, w_ref, b_ref, out_ref, cnt_ref, carry):
    tm = h_ref.shape[0]

    @pl.when(pl.program_id(0) == 0)
    def _():
        carry[...] = jnp.zeros_like(carry)

    hn = _rms(h_ref[...], g_ref[...])
    hi = hn.astype(BF16)
    lo = (hn - hi.astype(F32)).astype(BF16)
    logits = (_dot(hi, w_ref[0]) + _dot(hi, w_ref[1]) + _dot(lo, w_ref[0])
              + b_ref[...])
    lane = lax.broadcasted_iota(I32, (tm, LANES), 1)

    def first_lane(mask):
        return jnp.min(jnp.where(mask, lane.astype(F32), float(LANES)), axis=-1,
                       keepdims=True).astype(I32)

    is_g = lane < MOE_GROUPS
    gmax = jnp.max(jnp.where(is_g, logits, -jnp.inf), axis=-1, keepdims=True)
    gsum = jnp.sum(jnp.where(is_g, jnp.exp(logits - gmax), 0.0), axis=-1, keepdims=True)
    gsel = first_lane(is_g & (logits == gmax))
    pg = 1.0 / gsum

    grp_lo = MOE_GROUPS + MOE_PER_GROUP * gsel
    in_grp = (lane >= grp_lo) & (lane < grp_lo + MOE_PER_GROUP)
    emax = jnp.max(jnp.where(in_grp, logits, -jnp.inf), axis=-1, keepdims=True)
    eexp = jnp.where(in_grp, jnp.exp(logits - emax), 0.0)
    eprob = eexp / jnp.sum(eexp, axis=-1, keepdims=True)
    p1 = jnp.max(jnp.where(in_grp, eprob, -1.0), axis=-1, keepdims=True)
    l1 = first_lane(in_grp & (eprob == p1))
    rest = in_grp & (lane != l1)
    p2 = jnp.max(jnp.where(rest, eprob, -1.0), axis=-1, keepdims=True)
    l2 = first_lane(rest & (eprob == p2))
    psum = p1 + p2
    w1 = pg * p1 / psum
    w2 = pg * p2 / psum

    oh1 = lane == l1
    oh2 = lane == l2
    onehot = jnp.where(oh1 | oh2, 1.0, 0.0)
    r = lax.broadcasted_iota(I32, (tm, tm), 0)
    c = lax.broadcasted_iota(I32, (tm, tm), 1)
    strict = jnp.where(c < r, 1.0, 0.0).astype(BF16)
    before = _dot(strict, onehot.astype(BF16)) + carry[0:1, :]
    rank1 = jnp.sum(jnp.where(oh1, before, 0.0), axis=-1, keepdims=True)
    rank2 = jnp.sum(jnp.where(oh2, before, 0.0), axis=-1, keepdims=True)
    total = carry[0:1, :] + jnp.sum(onehot, axis=0, keepdims=True)
    carry[...] = jnp.broadcast_to(total, carry.shape)
    cnt_ref[...] = jnp.broadcast_to(total, cnt_ref.shape)

    out = jnp.where(lane == ROUTE_EID, (l1 - MOE_GROUPS).astype(F32), 0.0)
    out = jnp.where(lane == ROUTE_EID + 1, (l2 - MOE_GROUPS).astype(F32), out)
    out = jnp.where(lane == ROUTE_RANK, rank1, out)
    out = jnp.where(lane == ROUTE_RANK + 1, rank2, out)
    out = jnp.where(lane == ROUTE_WT, w1, out)
    out = jnp.where(lane == ROUTE_WT + 1, w2, out)
    out_ref[...] = out


def _router(h, g, w, b, tm):
    s, d = h.shape
    return pl.pallas_call(
        _router_kernel,
        grid=(s // tm,),
        in_specs=[pl.BlockSpec((tm, d), lambda i: (i, 0)),
                  pl.BlockSpec((1, d), lambda i: (0, 0)),
                  pl.BlockSpec((2, d, LANES), lambda i: (0, 0, 0)),
                  pl.BlockSpec((1, LANES), lambda i: (0, 0))],
        out_specs=[pl.BlockSpec((tm, LANES), lambda i: (i, 0)),
                   pl.BlockSpec((SUBLANES, LANES), lambda i: (0, 0))],
        out_shape=[jax.ShapeDtypeStruct((s, LANES), F32),
                   jax.ShapeDtypeStruct((SUBLANES, LANES), F32)],
        scratch_shapes=[pltpu.VMEM((SUBLANES, LANES), F32)],
        compiler_params=_params("arbitrary"),
        name="router",
    )(h, g, w, b)


def _row_tiles(x):
    return x.astype(BF16).reshape(x.shape[0], x.shape[1] // LANES, LANES)


def _from_row_tiles(x3):
    return x3.reshape(x3.shape[0], x3.shape[1] * LANES)


def _dispatch_kernel(d0_ref, d1_ref, zb_ref, h_ref, g_ref, xg_ref, buf, zero, sem, zsem):
    tm = h_ref.shape[0]
    base = pl.program_id(0) * tm

    @pl.when(pl.program_id(0) == 0)
    def _():
        zero[...] = jnp.zeros_like(zero)
        n_blk = xg_ref.shape[0] // MOE_ROWS

        def zero_copy(b):
            rows = pl.ds(pl.multiple_of(b * MOE_ROWS, MOE_ROWS), MOE_ROWS)
            return pltpu.make_async_copy(zero, xg_ref.at[rows], zsem)

        def start(b, carry):
            @pl.when(zb_ref[b] != 0)
            def _():
                zero_copy(b).start()
            return carry

        def finish(b, carry):
            @pl.when(zb_ref[b] != 0)
            def _():
                zero_copy(0).wait()
            return carry

        lax.fori_loop(0, n_blk, start, 0)
        lax.fori_loop(0, n_blk, finish, 0)

    buf[...] = _row_tiles(_rms(h_ref[...], g_ref[...]))

    def row_copy(t, d):
        return pltpu.make_async_copy(buf.at[pl.ds(t, 1)], xg_ref.at[pl.ds(d, 1)], sem)

    def issue(t, carry):
        row_copy(t, d0_ref[base + t]).start()
        row_copy(t, d1_ref[base + t]).start()
        return carry

    lax.fori_loop(0, tm, issue, 0, unroll=ROW_DMA_UNROLL)
    whole = pltpu.make_async_copy(buf, xg_ref.at[pl.ds(0, tm)], sem)
    whole.wait()
    whole.wait()


def _dispatch(dest, zero_blk, h, g, tm):
    s, d = h.shape
    n_rows = zero_blk.shape[0] * MOE_ROWS
    grid_spec = pltpu.PrefetchScalarGridSpec(
        num_scalar_prefetch=3,
        grid=(s // tm,),
        in_specs=[pl.BlockSpec((tm, d), lambda i, d0, d1, zb: (i, 0)),
                  pl.BlockSpec((1, d), lambda i, d0, d1, zb: (0, 0))],
        out_specs=pl.BlockSpec(memory_space=pl.ANY),
        scratch_shapes=[pltpu.VMEM((tm, d // LANES, LANES), BF16),
                        pltpu.VMEM((MOE_ROWS, d // LANES, LANES), BF16),
                        pltpu.SemaphoreType.DMA(()), pltpu.SemaphoreType.DMA(())],
    )
    return pl.pallas_call(
        _dispatch_kernel,
        grid_spec=grid_spec,
        out_shape=jax.ShapeDtypeStruct((n_rows, d // LANES, LANES), BF16),
        compiler_params=_params("arbitrary"),
        name="moe_dispatch",
    )(dest[0], dest[1], zero_blk, h, g)


def _moe_ffn_kernel(be_ref, nu_ref, nx_ref, x_ref, wg_hbm, wu_hbm, wd_hbm, y_ref,
                    wgf, wuf, wdf, wgb, wub, wdb, slot_ref, sem):
    b = pl.program_id(0)
    e = be_ref[b]
    used = b < nu_ref[0]

    def weight_copies(expert, slot):
        return (pltpu.make_async_copy(wg_hbm.at[expert], wgf.at[slot], sem.at[slot, 0]),
                pltpu.make_async_copy(wu_hbm.at[expert], wuf.at[slot], sem.at[slot, 1]),
                pltpu.make_async_copy(wd_hbm.at[expert], wdf.at[slot], sem.at[slot, 2]))

    @pl.when(b == 0)
    def _():
        slot_ref[0] = 0
        for c in weight_copies(e, 0):
            c.start()

    @pl.when(used & ((b == 0) | (e != be_ref[jnp.maximum(b - 1, 0)])))
    def _():
        slot = slot_ref[0]
        for c in weight_copies(e, slot):
            c.wait()
        nxt = nx_ref[e]

        @pl.when(nxt >= 0)
        def _():
            for c in weight_copies(nxt, 1 - slot):
                c.start()

        wgb[...] = wgf[slot].astype(BF16)
        wub[...] = wuf[slot].astype(BF16)
        wdb[...] = wdf[slot].astype(BF16)
        slot_ref[0] = 1 - slot

    @pl.when(used)
    def _():
        xb = _from_row_tiles(x_ref[...])
        gate = _dot(xb, wgb[...])
        up = _dot(xb, wub[...])
        act = (gate * jax.nn.sigmoid(gate) * up).astype(BF16)
        y_ref[...] = _row_tiles(_dot(act, wdb[...]))

    @pl.when(jnp.logical_not(used))
    def _():
        y_ref[...] = jnp.zeros_like(y_ref)


def _moe_ffn(blk_e, n_used, next_e, xg, w_gate, w_up, w_down):
    n_rows, nt, _ = xg.shape
    d = nt * LANES
    ff = w_gate.shape[2]
    n_blk = n_rows // MOE_ROWS
    def row_blk(b, be, nu, nx):
        return jnp.minimum(b, nu[0] - 1), 0, 0

    grid_spec = pltpu.PrefetchScalarGridSpec(
        num_scalar_prefetch=3,
        grid=(n_blk,),
        in_specs=[pl.BlockSpec((MOE_ROWS, nt, LANES), row_blk),
                  pl.BlockSpec(memory_space=pl.ANY),
                  pl.BlockSpec(memory_space=pl.ANY),
                  pl.BlockSpec(memory_space=pl.ANY)],
        out_specs=pl.BlockSpec((MOE_ROWS, nt, LANES), lambda b, be, nu, nx: (b, 0, 0)),
        scratch_shapes=[pltpu.VMEM((2, d, ff), F32), pltpu.VMEM((2, d, ff), F32),
                        pltpu.VMEM((2, ff, d), F32),
                        pltpu.VMEM((d, ff), BF16), pltpu.VMEM((d, ff), BF16),
                        pltpu.VMEM((ff, d), BF16),
                        pltpu.SMEM((1,), I32), pltpu.SemaphoreType.DMA((2, 3))],
    )
    return pl.pallas_call(
        _moe_ffn_kernel,
        grid_spec=grid_spec,
        out_shape=jax.ShapeDtypeStruct((n_rows, nt, LANES), BF16),
        compiler_params=_params("arbitrary"),
        name="moe_ffn",
    )(blk_e, n_used, next_e, xg, w_gate, w_up, w_down)


def _combine_kernel(d0_ref, d1_ref, h_ref, r_ref, y_ref, o_ref, buf, sem):
    tm = h_ref.shape[0]
    base = pl.program_id(0) * tm

    def row_copy(k, t, d):
        return pltpu.make_async_copy(y_ref.at[pl.ds(d, 1)], buf.at[k, pl.ds(t, 1)], sem)

    def issue(t, carry):
        row_copy(0, t, d0_ref[base + t]).start()
        row_copy(1, t, d1_ref[base + t]).start()
        return carry

    lax.fori_loop(0, tm, issue, 0, unroll=ROW_DMA_UNROLL)
    for k in range(2):
        pltpu.make_async_copy(y_ref.at[pl.ds(0, tm)], buf.at[k], sem).wait()
    r = r_ref[...]
    o_ref[...] = (h_ref[...] + r[:, ROUTE_WT:ROUTE_WT + 1] * _from_row_tiles(buf[0]).astype(F32)
                  + r[:, ROUTE_WT + 1:ROUTE_WT + 2] * _from_row_tiles(buf[1]).astype(F32))


def _combine(dest, h, routed, y, tm):
    s, d = h.shape
    grid_spec = pltpu.PrefetchScalarGridSpec(
        num_scalar_prefetch=2,
        grid=(s // tm,),
        in_specs=[pl.BlockSpec((tm, d), lambda i, d0, d1: (i, 0)),
                  pl.BlockSpec((tm, LANES), lambda i, d0, d1: (i, 0)),
                  pl.BlockSpec(memory_space=pl.ANY)],
        out_specs=pl.BlockSpec((tm, d), lambda i, d0, d1: (i, 0)),
        scratch_shapes=[pltpu.VMEM((2, tm, d // LANES, LANES), BF16), pltpu.SemaphoreType.DMA(())],
    )
    return pl.pallas_call(
        _combine_kernel,
        grid_spec=grid_spec,
        out_shape=jax.ShapeDtypeStruct((s, d), F32),
        compiler_params=_params("arbitrary"),
        name="moe_combine",
    )(dest[0], dest[1], h, routed, y)


def _layer(h, mem, pos, norm_mix_g, w_in, gate_b, conv_w, conv_b, out_g, moba_q_g, moba_k_g, w_out,
           norm_cross_g, norm_mem_g, xa_wq, xa_wkv, xa_q_g, xa_k_g, xa_wo, norm_ffn_g,
           router_group_w, router_group_b, router_expert_w, router_expert_b,
           exp_w_gate, exp_w_up, exp_w_down):
    s, d = h.shape
    row = lambda v: v.reshape(1, -1)

    n_gate = 2 * MLSTM_HEADS
    gate_lo = 2 * MLSTM_HEADS * MLSTM_QK + 2 * MLSTM_HEADS * MLSTM_V
    proj_tn = 512
    w_a = w_in[:, :gate_lo].astype(BF16)
    w_b = w_in[:, gate_lo + n_gate:].astype(BF16)
    w_g = jnp.pad(w_in[:, gate_lo:gate_lo + n_gate], ((0, 0), (0, proj_tn - n_gate))).astype(BF16)
    proj = _proj_in(h, row(norm_mix_g), w_a, w_b, w_g, tm=min(s, 1024))

    gate_b_row = jnp.pad(gate_b, (0, LANES - n_gate)).reshape(1, LANES)
    hm = _mlstm(proj, w_a.shape[1] + w_b.shape[1], gate_b_row, conv_w, row(conv_b), row(out_g))
    qn, kn, vb, kmean = _moba_prep(proj, gate_lo, pos, row(moba_q_g), row(moba_k_g))
    ha = _moba_attn(qn, kn, vb, kmean)
    h = _mix_out(hm, ha, w_out, h, tm=512)

    kv = _mem_kv(mem, row(norm_mem_g), xa_wkv, row(xa_k_g))
    h = _xattn(h, row(norm_cross_g), xa_wq, kv, row(xa_q_g), xa_wo, tm=512)

    w_route = jnp.concatenate(
        [router_group_w, router_expert_w,
         jnp.zeros((d, LANES - MOE_GROUPS - MOE_EXPERTS), router_group_w.dtype)], axis=1)
    w_route_hi = w_route.astype(BF16)
    w_route = jnp.stack([w_route_hi, (w_route - w_route_hi.astype(F32)).astype(BF16)])
    b_route = jnp.pad(jnp.concatenate([router_group_b, router_expert_b]),
                      (0, LANES - MOE_GROUPS - MOE_EXPERTS)).reshape(1, LANES)
    routed, counts = _router(h, row(norm_ffn_g), w_route, b_route, tm=512)

    cnt = counts[0, MOE_GROUPS:MOE_GROUPS + MOE_EXPERTS].astype(I32)
    padded = (cnt + MOE_ROWS - 1) // MOE_ROWS * MOE_ROWS
    pend = jnp.cumsum(padded)
    seg_start = (pend - padded).astype(F32)[None, :]
    expert_ids = jnp.arange(MOE_EXPERTS, dtype=F32)[None, :]

    def dest_of(k):
        mine = routed[:, ROUTE_EID + k:ROUTE_EID + k + 1] == expert_ids
        return (jnp.sum(jnp.where(mine, seg_start, 0.0), axis=1) + routed[:, ROUTE_RANK + k]).astype(I32)

    dest = (dest_of(0), dest_of(1))
    n_blk = -(-2 * s // MOE_ROWS) + MOE_EXPERTS
    blk_start = jnp.arange(n_blk, dtype=I32) * MOE_ROWS
    blk_e = jnp.minimum(jnp.sum((pend[None, :] <= blk_start[:, None]).astype(I32), axis=1),
                        MOE_EXPERTS - 1)
    n_used = (pend[-1:] // MOE_ROWS).astype(I32)
    ids = jnp.arange(MOE_EXPERTS, dtype=I32)
    later_used = (ids[None, :] > ids[:, None]) & (cnt[None, :] > 0)
    next_e = jnp.min(jnp.where(later_used, ids[None, :], MOE_EXPERTS), axis=1)
    next_e = jnp.where(next_e < MOE_EXPERTS, next_e, -1).astype(I32)

    is_blk_e = blk_e[:, None] == ids[None, :]
    seg_end = jnp.sum(jnp.where(is_blk_e, (pend - padded + cnt)[None, :], 0), axis=1)
    zero_blk = (blk_start + MOE_ROWS > seg_end).astype(I32)
    xg = _dispatch(dest, zero_blk, h, row(norm_ffn_g), tm=256)
    y = _moe_ffn(blk_e, n_used, next_e, xg, exp_w_gate, exp_w_up, exp_w_down)
    return _combine(dest, h, routed, y, tm=256)


def kernel(x, mem, positions, norm_mix_g, w_in, mlstm_gate_b, mlstm_conv_w, mlstm_conv_b, mlstm_out_g, moba_q_g, moba_k_g, w_out, norm_cross_g, norm_mem_g, xa_wq, xa_wkv, xa_q_g, xa_k_g, xa_wo, norm_ffn_g, router_group_w, router_group_b, router_expert_w, router_expert_b, exp_w_gate, exp_w_up, exp_w_down):
    bsz, s, _ = x.shape
    assert bsz == 1, "single-sequence prefill only"
    per_layer = (norm_mix_g, w_in, mlstm_gate_b, mlstm_conv_w, mlstm_conv_b, mlstm_out_g, moba_q_g,
                 moba_k_g, w_out, norm_cross_g, norm_mem_g, xa_wq, xa_wkv, xa_q_g, xa_k_g, xa_wo,
                 norm_ffn_g, router_group_w, router_group_b, router_expert_w, router_expert_b,
                 exp_w_gate, exp_w_up, exp_w_down)
    h = x[0]
    pos = positions.reshape(s, 1)
    for l in range(norm_mix_g.shape[0]):
        h = _layer(h, mem[0], pos, *(p[l] for p in per_layer))
    return h[None]
```

```python
import functools
import math

import jax
import jax.numpy as jnp
from jax import lax
from jax.experimental import pallas as pl
from jax.experimental.pallas import tpu as pltpu

F32 = jnp.float32
BF16 = jnp.bfloat16
I32 = jnp.int32

EPS = 1e-6
LANES = 128
SUBLANES = 8
VMEM_LIMIT = 56 * 1024 * 1024

MLSTM_HEADS = 4
MLSTM_QK = 128
MLSTM_V = 256
MLSTM_CHUNK = 128
MLSTM_CONV = 4
MLSTM_CHUNKS_PER_STEP = 1
MOBA_HEADS = 8
MOBA_HD = 128
MOBA_BLOCK = 256
MOBA_TOPK = 3
MOBA_VT_ROWS = MOBA_HD + 16
ROPE_DIM = 32
ROPE_THETA = 500000.0
XA_HEADS = 4
MOE_GROUPS = 8
MOE_PER_GROUP = 8
MOE_EXPERTS = MOE_GROUPS * MOE_PER_GROUP
MOE_ROWS = 128
ROW_DMA_UNROLL = 8

NT_DIMS = (((1,), (1,)), ((), ()))


def _params(*sem):
    return pltpu.CompilerParams(dimension_semantics=sem, vmem_limit_bytes=VMEM_LIMIT)


def _rms(x, g):
    return x * lax.rsqrt(jnp.mean(x * x, axis=-1, keepdims=True) + EPS) * g


def _dot(a, b):
    return jnp.dot(a, b, preferred_element_type=F32)


def _dot_nt(a, b, precision=None):
    return lax.dot_general(a, b, NT_DIMS, precision=precision, preferred_element_type=F32)


def _proj_in_kernel(x_ref, g_ref, wa_ref, wb_ref, wg_ref, o_ref, xn_ref, *, na, nb):
    j = pl.program_id(1)

    @pl.when(j == 0)
    def _():
        xn_ref[...] = _rms(x_ref[...], g_ref[...]).astype(BF16)

    @pl.when(j < na)
    def _():
        o_ref[...] = _dot(xn_ref[...], wa_ref[...])

    @pl.when((j >= na) & (j < na + nb))
    def _():
        o_ref[...] = _dot(xn_ref[...], wb_ref[...])

    @pl.when(j == na + nb)
    def _():
        o_ref[...] = _dot(xn_ref[...], wg_ref[...])


def _proj_in(x, g, w_a, w_b, w_g, tm):
    s, d = x.shape
    tn = w_g.shape[1]
    na, nb = w_a.shape[1] // tn, w_b.shape[1] // tn
    return pl.pallas_call(
        functools.partial(_proj_in_kernel, na=na, nb=nb),
        grid=(s // tm, na + nb + 1),
        in_specs=[pl.BlockSpec((tm, d), lambda i, j: (i, 0)),
                  pl.BlockSpec((1, d), lambda i, j: (0, 0)),
                  pl.BlockSpec((d, tn), lambda i, j: (0, jnp.minimum(j, na - 1))),
                  pl.BlockSpec((d, tn), lambda i, j: (0, jnp.clip(j - na, 0, nb - 1))),
                  pl.BlockSpec((d, tn), lambda i, j: (0, 0))],
        out_specs=pl.BlockSpec((tm, tn), lambda i, j: (i, j)),
        out_shape=jax.ShapeDtypeStruct((s, (na + nb + 1) * tn), F32),
        scratch_shapes=[pltpu.VMEM((tm, d), BF16)],
        compiler_params=_params("parallel", "arbitrary"),
        name="proj_in",
    )(x, g, w_a, w_b, w_g)


def _split3(x):
    hi = x.astype(BF16)
    r = x - hi.astype(F32)
    mid = r.astype(BF16)
    lo = (r - mid.astype(F32)).astype(BF16)
    return hi, mid, lo


def _log_sigmoid(x):
    return jnp.minimum(x, 0.0) - jnp.log(1.0 + jnp.exp(-jnp.abs(x)))


def _mlstm_kernel(q_ref, k_ref, v_ref, og_ref, gt_ref, gb_ref, cw_ref, cb_ref, outg_ref, hm_ref,
                  qext, kext, c_scr, n_scr, m_scr):
    L = MLSTM_CHUNK
    pad = SUBLANES
    qkw = MLSTM_HEADS * MLSTM_QK
    rows_blk = q_ref.shape[0]

    @pl.when(pl.program_id(0) == 0)
    def _():
        qext[0:pad, :] = jnp.zeros((pad, qkw), F32)
        kext[0:pad, :] = jnp.zeros((pad, qkw), F32)
        c_scr[...] = jnp.zeros_like(c_scr)
        n_scr[...] = jnp.zeros_like(n_scr)
        m_scr[...] = jnp.zeros_like(m_scr)

    qext[pad:pad + rows_blk, :] = q_ref[...]
    kext[pad:pad + rows_blk, :] = k_ref[...]

    def conv_silu(ext, lo):
        acc = jnp.broadcast_to(cb_ref[:, lo:lo + qkw], (rows_blk, qkw))
        for j in range(MLSTM_CONV):
            shift = MLSTM_CONV - 1 - j
            acc = acc + cw_ref[j:j + 1, lo:lo + qkw] * ext[pad - shift:pad - shift + rows_blk, :]
        return acc * jax.nn.sigmoid(acc)

    qc_all = conv_silu(qext, 0)
    kc_all = conv_silu(kext, qkw)
    qext[0:pad, :] = qext[rows_blk:rows_blk + pad, :]
    kext[0:pad, :] = kext[rows_blk:rows_blk + pad, :]

    row = lax.broadcasted_iota(I32, (L, L), 0)
    col = lax.broadcasted_iota(I32, (L, L), 1)
    causal = col <= row
    tri_lo = jnp.where(causal, 1.0, 0.0).astype(BF16)
    tri_up = jnp.where(row <= col, 1.0, 0.0).astype(BF16)
    for ck in range(rows_blk // L):
        rs = slice(ck * L, (ck + 1) * L)
        _mlstm_chunk(rs, qc_all[rs], kc_all[rs], v_ref, og_ref, gt_ref, gb_ref, outg_ref, hm_ref,
                     c_scr, n_scr, m_scr, causal, tri_lo, tri_up)


def _mlstm_chunk(rs, qc, kc, v_ref, og_ref, gt_ref, gb_ref, outg_ref, hm_ref, c_scr, n_scr, m_scr,
                 causal, tri_lo, tri_up):
    L = MLSTM_CHUNK
    gts = gt_ref[rs, :] + gb_ref[...]
    gtr = gts.T
    bc_all = sum(_dot(tri_lo, part) for part in _split3(_log_sigmoid(gts)))
    br_all = sum(_dot(part, tri_up) for part in _split3(_log_sigmoid(gtr)))

    for h in range(MLSTM_HEADS):
        fi = MLSTM_HEADS + h
        m_prev = m_scr[h, 0:1, 0:1]
        b_c = bc_all[:, fi:fi + 1]
        b_r = br_all[fi:fi + 1, :]
        i_c = gts[:, h:h + 1]
        i_r = gtr[h:h + 1, :]
        g = b_c[L - 1:L, :]
        d = jnp.where(causal, (b_c - b_r) + i_r, -jnp.inf)
        inter = b_c + m_prev
        m_t = jnp.maximum(inter, jnp.max(d, axis=-1, keepdims=True))
        w_inter = jnp.exp(inter - m_t)
        qh = qc[:, h * MLSTM_QK:(h + 1) * MLSTM_QK] * (MLSTM_QK ** -0.5)
        kh = kc[:, h * MLSTM_QK:(h + 1) * MLSTM_QK]
        vb = v_ref[rs, h * MLSTM_V:(h + 1) * MLSTM_V].astype(BF16)
        qb = qh.astype(BF16)
        p = jnp.exp(d - m_t) * _dot_nt(qb, kh.astype(BF16))
        c_prev = c_scr[h]
        n_prev = n_scr[h, 0:1, :]
        num = w_inter * _dot(qb, c_prev.astype(BF16)) + _dot(p.astype(BF16), vb)
        den = (w_inter * jnp.sum(qh * n_prev, axis=-1, keepdims=True)
               + jnp.sum(p, axis=-1, keepdims=True))
        hh = num * (1.0 / jnp.maximum(jnp.abs(den), jnp.exp(-m_t)))

        a = (g - b_c) + i_c
        m_new = jnp.maximum(g + m_prev, jnp.max(a, axis=0, keepdims=True))
        kw = kh * jnp.exp(a - m_new)
        decay = jnp.exp(g + m_prev - m_new)
        c_scr[h] = decay * c_prev + _dot(kw.T.astype(BF16), vb)
        n_scr[h, 0:1, :] = decay * n_prev + jnp.sum(kw, axis=0, keepdims=True)
        m_scr[h] = jnp.broadcast_to(m_new, (SUBLANES, LANES))

        hn = hh * lax.rsqrt(jnp.mean(hh * hh, axis=-1, keepdims=True) + EPS)
        vs = slice(h * MLSTM_V, (h + 1) * MLSTM_V)
        hn = hn * outg_ref[:, vs] * jax.nn.sigmoid(og_ref[rs, vs])
        hm_ref[rs, vs] = hn.astype(hm_ref.dtype)


def _mlstm(proj, gate_col, gate_b, conv_w, conv_b, out_g):
    s = proj.shape[0]
    L = MLSTM_CHUNKS_PER_STEP * MLSTM_CHUNK
    qkw = MLSTM_HEADS * MLSTM_QK
    vw = MLSTM_HEADS * MLSTM_V
    gate_blk = gate_col // LANES
    return pl.pallas_call(
        _mlstm_kernel,
        grid=(s // L,),
        in_specs=[pl.BlockSpec((L, qkw), lambda c: (c, 0)),
                  pl.BlockSpec((L, qkw), lambda c: (c, 1)),
                  pl.BlockSpec((L, vw), lambda c: (c, 1)),
                  pl.BlockSpec((L, vw), lambda c: (c, 2)),
                  pl.BlockSpec((L, LANES), lambda c: (c, gate_blk)),
                  pl.BlockSpec((1, LANES), lambda c: (0, 0)),
                  pl.BlockSpec((MLSTM_CONV, 2 * qkw), lambda c: (0, 0)),
                  pl.BlockSpec((1, 2 * qkw), lambda c: (0, 0)),
                  pl.BlockSpec((1, vw), lambda c: (0, 0))],
        out_specs=pl.BlockSpec((L, vw), lambda c: (c, 0)),
        out_shape=jax.ShapeDtypeStruct((s, vw), BF16),
        scratch_shapes=[pltpu.VMEM((L + SUBLANES, qkw), F32),
                        pltpu.VMEM((L + SUBLANES, qkw), F32),
                        pltpu.VMEM((MLSTM_HEADS, MLSTM_QK, MLSTM_V), F32),
                        pltpu.VMEM((MLSTM_HEADS, SUBLANES, LANES), F32),
                        pltpu.VMEM((MLSTM_HEADS, SUBLANES, LANES), F32)],
        compiler_params=_params("arbitrary"),
        name="mlstm",
    )(proj, proj, proj, proj, proj, gate_b, conv_w, conv_b, out_g)


def _moba_prep_kernel(q_ref, k_ref, v_ref, pos_ref, qg_ref, kg_ref, qt_ref, kn_ref, vt_ref, km_ref):
    rows = q_ref.shape[0]
    half = ROPE_DIM // 2
    lane = lax.broadcasted_iota(I32, (1, MOBA_HD), 1)
    inv_freq = jnp.exp((lane & (half - 1)).astype(F32) * (-(2.0 / ROPE_DIM) * math.log(ROPE_THETA)))
    ang = pos_ref[...].astype(F32) * inv_freq
    cos = jnp.where(lane < ROPE_DIM, jnp.cos(ang), 1.0)
    sin = jnp.sin(ang)
    sin = jnp.where(lane < half, -sin, jnp.where(lane < ROPE_DIM, sin, 0.0))

    def rope(x):
        partner = jnp.where(lane < half, pltpu.roll(x, MOBA_HD - half, 1), pltpu.roll(x, half, 1))
        return x * cos + partner * sin

    for h in range(MOBA_HEADS):
        hs = slice(h * MOBA_HD, (h + 1) * MOBA_HD)
        qt_ref[hs, :] = rope(_rms(q_ref[:, hs], qg_ref[...])).T
        kn = rope(_rms(k_ref[:, hs], kg_ref[...]))
        kn_ref[:, hs] = kn.astype(BF16)
        km_ref[0, :, hs] = jnp.sum(kn, axis=0, keepdims=True) * (1.0 / rows)
        vlo = h * MOBA_VT_ROWS
        vt_ref[0, vlo:vlo + MOBA_HD, :] = v_ref[:, hs].T.astype(BF16)
        vt_ref[0, vlo + MOBA_HD:vlo + MOBA_VT_ROWS, :] = jnp.ones((MOBA_VT_ROWS - MOBA_HD, rows), BF16)


def _moba_prep(proj, q_col, pos, q_g, k_g):
    s = proj.shape[0]
    w = MOBA_HEADS * MOBA_HD
    bs = MOBA_BLOCK
    nb = s // bs
    first = q_col // w
    return pl.pallas_call(
        _moba_prep_kernel,
        grid=(nb,),
        in_specs=[pl.BlockSpec((bs, w), lambda i: (i, first)),
                  pl.BlockSpec((bs, w), lambda i: (i, first + 1)),
                  pl.BlockSpec((bs, w), lambda i: (i, first + 2)),
                  pl.BlockSpec((bs, 1), lambda i: (i, 0)),
                  pl.BlockSpec((1, MOBA_HD), lambda i: (0, 0)),
                  pl.BlockSpec((1, MOBA_HD), lambda i: (0, 0))],
        out_specs=[pl.BlockSpec((w, bs), lambda i: (0, i)),
                   pl.BlockSpec((bs, w), lambda i: (i, 0)),
                   pl.BlockSpec((1, MOBA_HEADS * MOBA_VT_ROWS, bs), lambda i: (i, 0, 0)),
                   pl.BlockSpec((1, 1, w), lambda i: (i, 0, 0))],
        out_shape=[jax.ShapeDtypeStruct((w, s), F32),
                   jax.ShapeDtypeStruct((s, w), BF16),
                   jax.ShapeDtypeStruct((nb, MOBA_HEADS * MOBA_VT_ROWS, bs), BF16),
                   jax.ShapeDtypeStruct((nb, 1, w), F32)],
        compiler_params=_params("parallel"),
        name="moba_prep",
    )(proj, proj, proj, pos, q_g, k_g)


def _moba_attn_kernel(qt_ref, k_ref, vt_ref, km_ref, o_ref, bias_scr, qb_scr, m_scr, *tiles):
    i = pl.program_id(1)
    bs = MOBA_BLOCK
    hd = MOBA_HD
    nh = len(tiles) // 3
    vr = MOBA_VT_ROWS
    acc_scr = tiles[:nh]
    s_scr = tiles[nh:2 * nh]
    p_scr = tiles[2 * nh:]
    nb = km_ref.shape[0]
    c2 = (hd ** -0.5) * math.log2(math.e)
    blk = lax.broadcasted_iota(I32, (nb, bs), 0)
    blk_f = blk.astype(F32)
    valid = blk < i
    key = lax.broadcasted_iota(I32, (bs, bs), 0)
    qry = lax.broadcasted_iota(I32, (bs, bs), 1)
    own = pl.ds(pl.multiple_of(i * bs, bs), bs)

    for h in range(nh):
        hs = slice(h * hd, (h + 1) * hd)
        qt = qt_ref[hs, :]
        gate = jnp.dot(km_ref[:, 0, hs], qt, precision=lax.Precision.HIGHEST,
                       preferred_element_type=F32)
        cand = valid
        for _ in range(MOBA_TOPK):
            gmax = jnp.max(jnp.where(cand, gate, -jnp.inf), axis=0, keepdims=True)
            hit = cand & (gate == gmax)
            first = jnp.min(jnp.where(hit, blk_f, float(nb)), axis=0, keepdims=True)
            cand = cand & jnp.logical_not(hit & (blk_f == first))
        bias_scr[h] = jnp.where(valid & jnp.logical_not(cand), 0.0, -jnp.inf)

        qb = (qt * c2).astype(BF16)
        qb_scr[h] = qb
        s0 = jnp.where(key <= qry, _dot(k_ref[own, hs], qb), -jnp.inf)
        m0 = jnp.max(s0, axis=0, keepdims=True)
        p0 = jnp.exp2(s0 - m0)
        m_scr[h] = m0
        acc_scr[h][...] = _dot(vt_ref[i, h * vr:(h + 1) * vr, :], p0.astype(BF16))

    ck = 64

    def body(j, carry):
        rows = pl.ds(pl.multiple_of(j * bs, bs), bs)
        for h in range(nh):
            s_scr[h][...] = _dot(k_ref[rows, h * hd:(h + 1) * hd], qb_scr[h])
        alphas = []
        for h in range(nh):
            bias = bias_scr[h, pl.ds(j, 1), :]
            cmax = s_scr[h][0:ck, :]
            for c in range(1, bs // ck):
                cmax = jnp.maximum(cmax, s_scr[h][c * ck:(c + 1) * ck, :])
            m = m_scr[h]
            m_new = jnp.maximum(m, jnp.max(cmax, axis=0, keepdims=True) + bias)
            alpha = jnp.exp2(m - m_new)
            shift = m_new - bias
            for c in range(bs // ck):
                p = jnp.exp2(s_scr[h][c * ck:(c + 1) * ck, :] - shift)
                p_scr[h][c * ck:(c + 1) * ck, :] = p.astype(BF16)
            m_scr[h] = m_new
            alphas.append(alpha)
        for h in range(nh):
            acc_scr[h][...] = (alphas[h] * acc_scr[h][...]
                               + _dot(vt_ref[j, h * vr:(h + 1) * vr, :], p_scr[h][...]))
        return carry

    lax.fori_loop(0, i, body, 0)
    for h in range(nh):
        acc = acc_scr[h][...]
        out_t = acc[0:hd] * (1.0 / acc[hd:hd + 1])
        o_ref[:, h * hd:(h + 1) * hd] = out_t.T.astype(o_ref.dtype)


MOBA_HEADS_PER_STEP = 8


def _moba_attn(qt, kn, vt, kmean):
    w, s = qt.shape
    bs = MOBA_BLOCK
    nb = s // bs
    nh = MOBA_HEADS_PER_STEP
    gw = nh * MOBA_HD
    resident = pl.Buffered(1)
    return pl.pallas_call(
        _moba_attn_kernel,
        grid=(MOBA_HEADS // nh, nb),
        in_specs=[pl.BlockSpec((gw, bs), lambda g, i: (g, i)),
                  pl.BlockSpec((s, gw), lambda g, i: (0, g), pipeline_mode=resident),
                  pl.BlockSpec((nb, nh * MOBA_VT_ROWS, bs), lambda g, i: (0, g, 0),
                               pipeline_mode=resident),
                  pl.BlockSpec((nb, 1, gw), lambda g, i: (0, 0, g))],
        out_specs=pl.BlockSpec((bs, gw), lambda g, i: (i, g)),
        out_shape=jax.ShapeDtypeStruct((s, w), BF16),
        scratch_shapes=[pltpu.VMEM((nh, nb, bs), F32),
                        pltpu.VMEM((nh, MOBA_HD, bs), BF16),
                        pltpu.VMEM((nh, 1, bs), F32)]
                       + [pltpu.VMEM((MOBA_VT_ROWS, bs), F32) for _ in range(nh)]
                       + [pltpu.VMEM((bs, bs), F32) for _ in range(nh)]
                       + [pltpu.VMEM((bs, bs), BF16) for _ in range(nh)],
        compiler_params=_params("parallel", "arbitrary"),
        name="moba_attn",
    )(qt, kn, vt, kmean)


def _mix_out_kernel(hm_ref, ha_ref, w_ref, x_ref, o_ref, wb_ref):
    half = hm_ref.shape[1]

    @pl.when(pl.program_id(0) == 0)
    def _():
        wb_ref[...] = w_ref[...].astype(BF16)

    o_ref[...] = (x_ref[...] + _dot(hm_ref[...], wb_ref[0:half, :])
                  + _dot(ha_ref[...], wb_ref[half:2 * half, :]))


def _mix_out(hm, ha, w, x, tm):
    s, d = x.shape
    half = hm.shape[1]
    return pl.pallas_call(
        _mix_out_kernel,
        grid=(s // tm,),
        in_specs=[pl.BlockSpec((tm, half), lambda i: (i, 0)),
                  pl.BlockSpec((tm, half), lambda i: (i, 0)),
                  pl.BlockSpec((2 * half, d), lambda i: (0, 0), pipeline_mode=pl.Buffered(1)),
                  pl.BlockSpec((tm, d), lambda i: (i, 0))],
        out_specs=pl.BlockSpec((tm, d), lambda i: (i, 0)),
        out_shape=jax.ShapeDtypeStruct((s, d), F32),
        scratch_shapes=[pltpu.VMEM((2 * half, d), BF16)],
        compiler_params=_params("arbitrary"),
        name="mix_out",
    )(hm, ha, w, x)


def _mem_kv_kernel(mem_ref, g_ref, w_ref, kg_ref, o_ref, mn_ref):
    j = pl.program_id(0)

    @pl.when(j == 0)
    def _():
        mn_ref[...] = _rms(mem_ref[...], g_ref[...]).astype(BF16)

    y = _dot(mn_ref[...], w_ref[...].astype(BF16))
    o_ref[...] = jnp.where(j < XA_HEADS, _rms(y, kg_ref[...]), y).astype(o_ref.dtype)


def _mem_kv(mem, g, wkv, k_g):
    m, d = mem.shape
    hd = d // XA_HEADS
    return pl.pallas_call(
        _mem_kv_kernel,
        grid=(2 * XA_HEADS,),
        in_specs=[pl.BlockSpec((m, d), lambda j: (0, 0)),
                  pl.BlockSpec((1, d), lambda j: (0, 0)),
                  pl.BlockSpec((d, hd), lambda j: (0, j)),
                  pl.BlockSpec((1, hd), lambda j: (0, 0))],
        out_specs=pl.BlockSpec((m, hd), lambda j: (0, j)),
        out_shape=jax.ShapeDtypeStruct((m, 2 * d), BF16),
        scratch_shapes=[pltpu.VMEM((m, d), BF16)],
        compiler_params=_params("arbitrary"),
        name="mem_kv",
    )(mem, g, wkv, k_g)


def _xattn_kernel(h_ref, g_ref, wq_hbm, kv_ref, qg_ref, wo_hbm, rg_ref, rw_ref, rb_ref,
                  o_ref, routed_ref, cnt_ref, wqb, wob, stage, o_all, carry, sem):
    d = h_ref.shape[1]
    hd = d // XA_HEADS
    n_slab = 2 * XA_HEADS

    @pl.when(pl.program_id(0) == 0)
    def _():
        def slab_copy(k):
            src = wq_hbm if k < XA_HEADS else wo_hbm
            cols = pl.ds((k % XA_HEADS) * hd, hd)
            return pltpu.make_async_copy(src.at[:, cols], stage.at[k % 2], sem.at[k % 2])

        slab_copy(0).start()
        for k in range(n_slab):
            if k + 1 < n_slab:
                slab_copy(k + 1).start()
            slab_copy(k).wait()
            dst = wqb if k < XA_HEADS else wob
            dst[:, (k % XA_HEADS) * hd:(k % XA_HEADS + 1) * hd] = stage[k % 2].astype(BF16)

    h = h_ref[...]
    q_all = _dot(_rms(h, g_ref[...]).astype(BF16), wqb[...])
    for j in range(XA_HEADS):
        hs = slice(j * hd, (j + 1) * hd)
        q = _rms(q_all[:, hs], qg_ref[...]).astype(BF16)
        sc = _dot_nt(q, kv_ref[:, hs]) * (hd ** -0.5)
        p = jnp.exp(sc - jnp.max(sc, axis=-1, keepdims=True))
        p = p * (1.0 / jnp.sum(p, axis=-1, keepdims=True))
        o_all[:, hs] = _dot(p.astype(BF16), kv_ref[:, d + j * hd:d + (j + 1) * hd]).astype(BF16)
    h_out = h + _dot(o_all[...], wob[...])
    o_ref[...] = h_out
    _route_rows(h_out, rg_ref, rw_ref, rb_ref, routed_ref, cnt_ref, carry)


def _xattn_route(h, g, wq, kv, q_g, wo, route_g, route_w, route_b, tm):
    s, d = h.shape
    m = kv.shape[0]
    hd = d // XA_HEADS
    return pl.pallas_call(
        _xattn_kernel,
        grid=(s // tm,),
        in_specs=[pl.BlockSpec((tm, d), lambda i: (i, 0)),
                  pl.BlockSpec((1, d), lambda i: (0, 0)),
                  pl.BlockSpec(memory_space=pl.ANY),
                  pl.BlockSpec((m, 2 * d), lambda i: (0, 0)),
                  pl.BlockSpec((1, hd), lambda i: (0, 0)),
                  pl.BlockSpec(memory_space=pl.ANY),
                  pl.BlockSpec((1, d), lambda i: (0, 0)),
                  pl.BlockSpec((2, d, LANES), lambda i: (0, 0, 0)),
                  pl.BlockSpec((1, LANES), lambda i: (0, 0))],
        out_specs=[pl.BlockSpec((tm, d), lambda i: (i, 0)),
                   pl.BlockSpec((tm, LANES), lambda i: (i, 0)),
                   pl.BlockSpec((SUBLANES, LANES), lambda i: (0, 0))],
        out_shape=[jax.ShapeDtypeStruct((s, d), F32),
                   jax.ShapeDtypeStruct((s, LANES), F32),
                   jax.ShapeDtypeStruct((SUBLANES, LANES), F32)],
        scratch_shapes=[pltpu.VMEM((d, d), BF16), pltpu.VMEM((d, d), BF16),
                        pltpu.VMEM((2, d, hd), F32), pltpu.VMEM((tm, d), BF16),
                        pltpu.VMEM((SUBLANES, LANES), F32),
                        pltpu.SemaphoreType.DMA((2,))],
        compiler_params=_params("arbitrary"),
        name="xattn_route",
    )(h, g, wq, kv, q_g, wo, route_g, route_w, route_b)


ROUTE_EID = 0
ROUTE_RANK = 2
ROUTE_WT = 4


def _route_rows(h, g_ref, w_ref, b_ref, out_ref, cnt_ref, carry):
    tm = h.shape[0]

    @pl.when(pl.program_id(0) == 0)
    def _():
        carry[...] = jnp.zeros_like(carry)

    hn = _rms(h, g_ref[...])
    hi = hn.astype(BF16)
    lo = (hn - hi.astype(F32)).astype(BF16)
    logits = (_dot(hi, w_ref[0]) + _dot(hi, w_ref[1]) + _dot(lo, w_ref[0])
              + b_ref[...])
    lane = lax.broadcasted_iota(I32, (tm, LANES), 1)

    def first_lane(mask):
        return jnp.min(jnp.where(mask, lane.astype(F32), float(LANES)), axis=-1,
                       keepdims=True).astype(I32)

    is_g = lane < MOE_GROUPS
    gmax = jnp.max(jnp.where(is_g, logits, -jnp.inf), axis=-1, keepdims=True)
    gsum = jnp.sum(jnp.where(is_g, jnp.exp(logits - gmax), 0.0), axis=-1, keepdims=True)
    gsel = first_lane(is_g & (logits == gmax))
    pg = 1.0 / gsum

    grp_lo = MOE_GROUPS + MOE_PER_GROUP * gsel
    in_grp = (lane >= grp_lo) & (lane < grp_lo + MOE_PER_GROUP)
    emax = jnp.max(jnp.where(in_grp, logits, -jnp.inf), axis=-1, keepdims=True)
    eexp = jnp.where(in_grp, jnp.exp(logits - emax), 0.0)
    eprob = eexp / jnp.sum(eexp, axis=-1, keepdims=True)
    p1 = jnp.max(jnp.where(in_grp, eprob, -1.0), axis=-1, keepdims=True)
    l1 = first_lane(in_grp & (eprob == p1))
    rest = in_grp & (lane != l1)
    p2 = jnp.max(jnp.where(rest, eprob, -1.0), axis=-1, keepdims=True)
    l2 = first_lane(rest & (eprob == p2))
    psum = p1 + p2
    w1 = pg * p1 / psum
    w2 = pg * p2 / psum

    oh1 = lane == l1
    oh2 = lane == l2
    onehot = jnp.where(oh1 | oh2, 1.0, 0.0)
    r = lax.broadcasted_iota(I32, (tm, tm), 0)
    c = lax.broadcasted_iota(I32, (tm, tm), 1)
    strict = jnp.where(c < r, 1.0, 0.0).astype(BF16)
    before = _dot(strict, onehot.astype(BF16)) + carry[0:1, :]
    rank1 = jnp.sum(jnp.where(oh1, before, 0.0), axis=-1, keepdims=True)
    rank2 = jnp.sum(jnp.where(oh2, before, 0.0), axis=-1, keepdims=True)
    total = carry[0:1, :] + jnp.sum(onehot, axis=0, keepdims=True)
    carry[...] = jnp.broadcast_to(total, carry.shape)
    cnt_ref[...] = jnp.broadcast_to(total, cnt_ref.shape)

    out = jnp.where(lane == ROUTE_EID, (l1 - MOE_GROUPS).astype(F32), 0.0)
    out = jnp.where(lane == ROUTE_EID + 1, (l2 - MOE_GROUPS).astype(F32), out)
    out = jnp.where(lane == ROUTE_RANK, rank1, out)
    out = jnp.where(lane == ROUTE_RANK + 1, rank2, out)
    out = jnp.where(lane == ROUTE_WT, w1, out)
    out = jnp.where(lane == ROUTE_WT + 1, w2, out)
    out_ref[...] = out


def _row_tiles(x):
    return x.astype(BF16).reshape(x.shape[0], x.shape[1] // LANES, LANES)


def _from_row_tiles(x3):
    return x3.reshape(x3.shape[0], x3.shape[1] * LANES)


def _dispatch_kernel(d0_ref, d1_ref, zb_ref, h_ref, g_ref, xg_ref, buf, zero, sem, zsem):
    tm = h_ref.shape[0]
    base = pl.program_id(0) * tm

    @pl.when(pl.program_id(0) == 0)
    def _():
        zero[...] = jnp.zeros_like(zero)
        n_blk = xg_ref.shape[0] // MOE_ROWS

        def zero_copy(b):
            rows = pl.ds(pl.multiple_of(b * MOE_ROWS, MOE_ROWS), MOE_ROWS)
            return pltpu.make_async_copy(zero, xg_ref.at[rows], zsem)

        def start(b, carry):
            @pl.when(zb_ref[b] != 0)
            def _():
                zero_copy(b).start()
            return carry

        def finish(b, carry):
            @pl.when(zb_ref[b] != 0)
            def _():
                zero_copy(0).wait()
            return carry

        lax.fori_loop(0, n_blk, start, 0)
        lax.fori_loop(0, n_blk, finish, 0)

    buf[...] = _row_tiles(_rms(h_ref[...], g_ref[...]))

    def row_copy(t, d):
        return pltpu.make_async_copy(buf.at[pl.ds(t, 1)], xg_ref.at[pl.ds(d, 1)], sem)

    def issue(t, carry):
        row_copy(t, d0_ref[base + t]).start()
        row_copy(t, d1_ref[base + t]).start()
        return carry

    lax.fori_loop(0, tm, issue, 0, unroll=ROW_DMA_UNROLL)
    whole = pltpu.make_async_copy(buf, xg_ref.at[pl.ds(0, tm)], sem)
    whole.wait()
    whole.wait()


def _dispatch(dest, zero_blk, h, g, tm):
    s, d = h.shape
    n_rows = zero_blk.shape[0] * MOE_ROWS
    grid_spec = pltpu.PrefetchScalarGridSpec(
        num_scalar_prefetch=3,
        grid=(s // tm,),
        in_specs=[pl.BlockSpec((tm, d), lambda i, d0, d1, zb: (i, 0)),
                  pl.BlockSpec((1, d), lambda i, d0, d1, zb: (0, 0))],
        out_specs=pl.BlockSpec(memory_space=pl.ANY),
        scratch_shapes=[pltpu.VMEM((tm, d // LANES, LANES), BF16),
                        pltpu.VMEM((MOE_ROWS, d // LANES, LANES), BF16),
                        pltpu.SemaphoreType.DMA(()), pltpu.SemaphoreType.DMA(())],
    )
    return pl.pallas_call(
        _dispatch_kernel,
        grid_spec=grid_spec,
        out_shape=jax.ShapeDtypeStruct((n_rows, d // LANES, LANES), BF16),
        compiler_params=_params("arbitrary"),
        name="moe_dispatch",
    )(dest[0], dest[1], zero_blk, h, g)


def _moe_ffn_kernel(be_ref, nu_ref, nx_ref, x_ref, wg_hbm, wu_hbm, wd_hbm, y_ref,
                    wgf, wuf, wdf, wgb, wub, wdb, slot_ref, sem):
    b = pl.program_id(0)
    e = be_ref[b]
    used = b < nu_ref[0]

    def weight_copies(expert, slot):
        return (pltpu.make_async_copy(wg_hbm.at[expert], wgf.at[slot], sem.at[slot, 0]),
                pltpu.make_async_copy(wu_hbm.at[expert], wuf.at[slot], sem.at[slot, 1]),
                pltpu.make_async_copy(wd_hbm.at[expert], wdf.at[slot], sem.at[slot, 2]))

    @pl.when(b == 0)
    def _():
        slot_ref[0] = 0
        for c in weight_copies(e, 0):
            c.start()

    @pl.when(used & ((b == 0) | (e != be_ref[jnp.maximum(b - 1, 0)])))
    def _():
        slot = slot_ref[0]
        for c in weight_copies(e, slot):
            c.wait()
        nxt = nx_ref[e]

        @pl.when(nxt >= 0)
        def _():
            for c in weight_copies(nxt, 1 - slot):
                c.start()

        wgb[...] = wgf[slot].astype(BF16)
        wub[...] = wuf[slot].astype(BF16)
        wdb[...] = wdf[slot].astype(BF16)
        slot_ref[0] = 1 - slot

    @pl.when(used)
    def _():
        xb = _from_row_tiles(x_ref[...])
        gate = _dot(xb, wgb[...])
        up = _dot(xb, wub[...])
        act = (gate * jax.nn.sigmoid(gate) * up).astype(BF16)
        y_ref[...] = _dot(act, wdb[...])

    @pl.when(jnp.logical_not(used))
    def _():
        y_ref[...] = jnp.zeros_like(y_ref)


def _moe_ffn(blk_e, n_used, next_e, xg, w_gate, w_up, w_down):
    n_rows, nt, _ = xg.shape
    d = nt * LANES
    ff = w_gate.shape[2]
    n_blk = n_rows // MOE_ROWS
    def row_blk(b, be, nu, nx):
        return jnp.minimum(b, nu[0] - 1), 0, 0

    grid_spec = pltpu.PrefetchScalarGridSpec(
        num_scalar_prefetch=3,
        grid=(n_blk,),
        in_specs=[pl.BlockSpec((MOE_ROWS, nt, LANES), row_blk),
                  pl.BlockSpec(memory_space=pl.ANY),
                  pl.BlockSpec(memory_space=pl.ANY),
                  pl.BlockSpec(memory_space=pl.ANY)],
        out_specs=pl.BlockSpec((MOE_ROWS, d), lambda b, be, nu, nx: (b, 0)),
        scratch_shapes=[pltpu.VMEM((2, d, ff), F32), pltpu.VMEM((2, d, ff), F32),
                        pltpu.VMEM((2, ff, d), F32),
                        pltpu.VMEM((d, ff), BF16), pltpu.VMEM((d, ff), BF16),
                        pltpu.VMEM((ff, d), BF16),
                        pltpu.SMEM((1,), I32), pltpu.SemaphoreType.DMA((2, 3))],
    )
    return pl.pallas_call(
        _moe_ffn_kernel,
        grid_spec=grid_spec,
        out_shape=jax.ShapeDtypeStruct((n_rows, d), F32),
        compiler_params=_params("arbitrary"),
        name="moe_ffn",
    )(blk_e, n_used, next_e, xg, w_gate, w_up, w_down)


def _combine_kernel(d0_ref, d1_ref, h_ref, r_ref, y_ref, o_ref, buf, sem):
    tm = h_ref.shape[0]
    base = pl.program_id(0) * tm

    def row_copy(k, t, d):
        return pltpu.make_async_copy(y_ref.at[pl.ds(d, 1), :], buf.at[k, pl.ds(t, 1), :], sem)

    def issue(t, carry):
        row_copy(0, t, d0_ref[base + t]).start()
        row_copy(1, t, d1_ref[base + t]).start()
        return carry

    lax.fori_loop(0, tm, issue, 0, unroll=ROW_DMA_UNROLL)
    for k in range(2):
        pltpu.make_async_copy(y_ref.at[pl.ds(0, tm), :], buf.at[k], sem).wait()
    r = r_ref[...]
    o_ref[...] = (h_ref[...] + r[:, ROUTE_WT:ROUTE_WT + 1] * buf[0]
                  + r[:, ROUTE_WT + 1:ROUTE_WT + 2] * buf[1])


def _combine(dest, h, routed, y, tm):
    s, d = h.shape
    grid_spec = pltpu.PrefetchScalarGridSpec(
        num_scalar_prefetch=2,
        grid=(s // tm,),
        in_specs=[pl.BlockSpec((tm, d), lambda i, d0, d1: (i, 0)),
                  pl.BlockSpec((tm, LANES), lambda i, d0, d1: (i, 0)),
                  pl.BlockSpec(memory_space=pl.ANY)],
        out_specs=pl.BlockSpec((tm, d), lambda i, d0, d1: (i, 0)),
        scratch_shapes=[pltpu.VMEM((2, tm, d), F32), pltpu.SemaphoreType.DMA(())],
    )
    return pl.pallas_call(
        _combine_kernel,
        grid_spec=grid_spec,
        out_shape=jax.ShapeDtypeStruct((s, d), F32),
        compiler_params=_params("arbitrary"),
        name="moe_combine",
    )(dest[0], dest[1], h, routed, y)


def _layer(h, mem, pos, norm_mix_g, w_in, gate_b, conv_w, conv_b, out_g, moba_q_g, moba_k_g, w_out,
           norm_cross_g, norm_mem_g, xa_wq, xa_wkv, xa_q_g, xa_k_g, xa_wo, norm_ffn_g,
           router_group_w, router_group_b, router_expert_w, router_expert_b,
           exp_w_gate, exp_w_up, exp_w_down):
    s, d = h.shape
    row = lambda v: v.reshape(1, -1)

    n_gate = 2 * MLSTM_HEADS
    gate_lo = 2 * MLSTM_HEADS * MLSTM_QK + 2 * MLSTM_HEADS * MLSTM_V
    proj_tn = 512
    w_a = w_in[:, :gate_lo].astype(BF16)
    w_b = w_in[:, gate_lo + n_gate:].astype(BF16)
    w_g = jnp.pad(w_in[:, gate_lo:gate_lo + n_gate], ((0, 0), (0, proj_tn - n_gate))).astype(BF16)
    proj = _proj_in(h, row(norm_mix_g), w_a, w_b, w_g, tm=min(s, 1024))

    gate_b_row = jnp.pad(gate_b, (0, LANES - n_gate)).reshape(1, LANES)
    hm = _mlstm(proj, w_a.shape[1] + w_b.shape[1], gate_b_row, conv_w, row(conv_b), row(out_g))
    qn, kn, vb, kmean = _moba_prep(proj, gate_lo, pos, row(moba_q_g), row(moba_k_g))
    ha = _moba_attn(qn, kn, vb, kmean)
    h = _mix_out(hm, ha, w_out, h, tm=512)

    kv = _mem_kv(mem, row(norm_mem_g), xa_wkv, row(xa_k_g))

    w_route = jnp.concatenate(
        [router_group_w, router_expert_w,
         jnp.zeros((d, LANES - MOE_GROUPS - MOE_EXPERTS), router_group_w.dtype)], axis=1)
    w_route_hi = w_route.astype(BF16)
    w_route = jnp.stack([w_route_hi, (w_route - w_route_hi.astype(F32)).astype(BF16)])
    b_route = jnp.pad(jnp.concatenate([router_group_b, router_expert_b]),
                      (0, LANES - MOE_GROUPS - MOE_EXPERTS)).reshape(1, LANES)
    h, routed, counts = _xattn_route(h, row(norm_cross_g), xa_wq, kv, row(xa_q_g), xa_wo,
                                     row(norm_ffn_g), w_route, b_route, tm=512)

    cnt = counts[0, MOE_GROUPS:MOE_GROUPS + MOE_EXPERTS].astype(I32)
    padded = (cnt + MOE_ROWS - 1) // MOE_ROWS * MOE_ROWS
    pend = jnp.cumsum(padded)
    seg_start = (pend - padded).astype(F32)[None, :]
    expert_ids = jnp.arange(MOE_EXPERTS, dtype=F32)[None, :]

    def dest_of(k):
        mine = routed[:, ROUTE_EID + k:ROUTE_EID + k + 1] == expert_ids
        return (jnp.sum(jnp.where(mine, seg_start, 0.0), axis=1) + routed[:, ROUTE_RANK + k]).astype(I32)

    dest = (dest_of(0), dest_of(1))
    n_blk = -(-2 * s // MOE_ROWS) + MOE_EXPERTS
    blk_start = jnp.arange(n_blk, dtype=I32) * MOE_ROWS
    blk_e = jnp.minimum(jnp.sum((pend[None, :] <= blk_start[:, None]).astype(I32), axis=1),
                        MOE_EXPERTS - 1)
    n_used = (pend[-1:] // MOE_ROWS).astype(I32)
    ids = jnp.arange(MOE_EXPERTS, dtype=I32)
    later_used = (ids[None, :] > ids[:, None]) & (cnt[None, :] > 0)
    next_e = jnp.min(jnp.where(later_used, ids[None, :], MOE_EXPERTS), axis=1)
    next_e = jnp.where(next_e < MOE_EXPERTS, next_e, -1).astype(I32)

    is_blk_e = blk_e[:, None] == ids[None, :]
    seg_end = jnp.sum(jnp.where(is_blk_e, (pend - padded + cnt)[None, :], 0), axis=1)
    zero_blk = (blk_start + MOE_ROWS > seg_end).astype(I32)
    xg = _dispatch(dest, zero_blk, h, row(norm_ffn_g), tm=256)
    y = _moe_ffn(blk_e, n_used, next_e, xg, exp_w_gate, exp_w_up, exp_w_down)
    return _combine(dest, h, routed, y, tm=256)


def kernel(x, mem, positions, norm_mix_g, w_in, mlstm_gate_b, mlstm_conv_w, mlstm_conv_b, mlstm_out_g, moba_q_g, moba_k_g, w_out, norm_cross_g, norm_mem_g, xa_wq, xa_wkv, xa_q_g, xa_k_g, xa_wo, norm_ffn_g, router_group_w, router_group_b, router_expert_w, router_expert_b, exp_w_gate, exp_w_up, exp_w_down):
    bsz, s, _ = x.shape
    assert bsz == 1, "single-sequence prefill only"
    per_layer = (norm_mix_g, w_in, mlstm_gate_b, mlstm_conv_w, mlstm_conv_b, mlstm_out_g, moba_q_g,
                 moba_k_g, w_out, norm_cross_g, norm_mem_g, xa_wq, xa_wkv, xa_q_g, xa_k_g, xa_wo,
                 norm_ffn_g, router_group_w, router_group_b, router_expert_w, router_expert_b,
                 exp_w_gate, exp_w_up, exp_w_down)
    h = x[0]
    pos = positions.reshape(s, 1)
    for l in range(norm_mix_g.shape[0]):
        h = _layer(h, mem[0], pos, *(p[l] for p in per_layer))
    return h[None]
```

```python
import functools
import math

import jax
import jax.numpy as jnp
from jax import lax
from jax.experimental import pallas as pl
from jax.experimental.pallas import tpu as pltpu

F32 = jnp.float32
BF16 = jnp.bfloat16
I32 = jnp.int32

EPS = 1e-6
LANES = 128
SUBLANES = 8
VMEM_LIMIT = 56 * 1024 * 1024

MLSTM_HEADS = 4
MLSTM_QK = 128
MLSTM_V = 256
MLSTM_CHUNK = 128
MLSTM_CONV = 4
MLSTM_CHUNKS_PER_STEP = 1
MOBA_HEADS = 8
MOBA_HD = 128
MOBA_BLOCK = 256
MOBA_TOPK = 3
MOBA_VT_ROWS = MOBA_HD + 16
ROPE_DIM = 32
ROPE_THETA = 500000.0
XA_HEADS = 4
MOE_GROUPS = 8
MOE_PER_GROUP = 8
MOE_EXPERTS = MOE_GROUPS * MOE_PER_GROUP
MOE_ROWS = 128
ROW_DMA_UNROLL = 8

NT_DIMS = (((1,), (1,)), ((), ()))


def _params(*sem):
    return pltpu.CompilerParams(dimension_semantics=sem, vmem_limit_bytes=VMEM_LIMIT)


def _rms(x, g):
    return x * lax.rsqrt(jnp.mean(x * x, axis=-1, keepdims=True) + EPS) * g


def _dot(a, b):
    return jnp.dot(a, b, preferred_element_type=F32)


def _dot_nt(a, b, precision=None):
    return lax.dot_general(a, b, NT_DIMS, precision=precision, preferred_element_type=F32)


def _proj_in_kernel(x_ref, g_ref, wa_ref, wb_ref, wg_ref, o_ref, xn_ref, *, na, nb):
    j = pl.program_id(1)

    @pl.when(j == 0)
    def _():
        xn_ref[...] = _rms(x_ref[...], g_ref[...]).astype(BF16)

    @pl.when(j < na)
    def _():
        o_ref[...] = _dot_nt(xn_ref[...], wa_ref[...])

    @pl.when((j >= na) & (j < na + nb))
    def _():
        o_ref[...] = _dot_nt(xn_ref[...], wb_ref[...])

    @pl.when(j == na + nb)
    def _():
        o_ref[...] = _dot_nt(xn_ref[...], wg_ref[...])


def _proj_in(x, g, w_a, w_b, w_g, tm):
    s, d = x.shape
    tn = w_g.shape[0]
    na, nb = w_a.shape[0] // tn, w_b.shape[0] // tn
    return pl.pallas_call(
        functools.partial(_proj_in_kernel, na=na, nb=nb),
        grid=(s // tm, na + nb + 1),
        in_specs=[pl.BlockSpec((tm, d), lambda i, j: (i, 0)),
                  pl.BlockSpec((1, d), lambda i, j: (0, 0)),
                  pl.BlockSpec((tn, d), lambda i, j: (jnp.minimum(j, na - 1), 0)),
                  pl.BlockSpec((tn, d), lambda i, j: (jnp.clip(j - na, 0, nb - 1), 0)),
                  pl.BlockSpec((tn, d), lambda i, j: (0, 0))],
        out_specs=pl.BlockSpec((tm, tn), lambda i, j: (i, j)),
        out_shape=jax.ShapeDtypeStruct((s, (na + nb + 1) * tn), F32),
        scratch_shapes=[pltpu.VMEM((tm, d), BF16)],
        compiler_params=_params("parallel", "arbitrary"),
        name="proj_in",
    )(x, g, w_a, w_b, w_g)


def _split3(x):
    hi = x.astype(BF16)
    r = x - hi.astype(F32)
    mid = r.astype(BF16)
    lo = (r - mid.astype(F32)).astype(BF16)
    return hi, mid, lo


def _log_sigmoid(x):
    return jnp.minimum(x, 0.0) - jnp.log(1.0 + jnp.exp(-jnp.abs(x)))


def _mlstm_kernel(q_ref, k_ref, v_ref, og_ref, gt_ref, gb_ref, cw_ref, cb_ref, outg_ref, hm_ref,
                  qext, kext, c_scr, n_scr, m_scr):
    L = MLSTM_CHUNK
    pad = SUBLANES
    qkw = MLSTM_HEADS * MLSTM_QK
    rows_blk = q_ref.shape[0]

    @pl.when(pl.program_id(0) == 0)
    def _():
        qext[0:pad, :] = jnp.zeros((pad, qkw), F32)
        kext[0:pad, :] = jnp.zeros((pad, qkw), F32)
        c_scr[...] = jnp.zeros_like(c_scr)
        n_scr[...] = jnp.zeros_like(n_scr)
        m_scr[...] = jnp.zeros_like(m_scr)

    qext[pad:pad + rows_blk, :] = q_ref[...]
    kext[pad:pad + rows_blk, :] = k_ref[...]

    def conv_silu(ext, lo):
        acc = jnp.broadcast_to(cb_ref[:, lo:lo + qkw], (rows_blk, qkw))
        for j in range(MLSTM_CONV):
            shift = MLSTM_CONV - 1 - j
            acc = acc + cw_ref[j:j + 1, lo:lo + qkw] * ext[pad - shift:pad - shift + rows_blk, :]
        return acc * jax.nn.sigmoid(acc)

    qc_all = conv_silu(qext, 0)
    kc_all = conv_silu(kext, qkw)
    qext[0:pad, :] = qext[rows_blk:rows_blk + pad, :]
    kext[0:pad, :] = kext[rows_blk:rows_blk + pad, :]

    row = lax.broadcasted_iota(I32, (L, L), 0)
    col = lax.broadcasted_iota(I32, (L, L), 1)
    causal = col <= row
    tri_lo = jnp.where(causal, 1.0, 0.0).astype(BF16)
    tri_up = jnp.where(row <= col, 1.0, 0.0).astype(BF16)
    for ck in range(rows_blk // L):
        rs = slice(ck * L, (ck + 1) * L)
        _mlstm_chunk(rs, qc_all[rs], kc_all[rs], v_ref, og_ref, gt_ref, gb_ref, outg_ref, hm_ref,
                     c_scr, n_scr, m_scr, causal, tri_lo, tri_up)


def _mlstm_chunk(rs, qc, kc, v_ref, og_ref, gt_ref, gb_ref, outg_ref, hm_ref, c_scr, n_scr, m_scr,
                 causal, tri_lo, tri_up):
    L = MLSTM_CHUNK
    gts = gt_ref[rs, :] + gb_ref[...]
    gtr = gts.T
    bc_all = sum(_dot(tri_lo, part) for part in _split3(_log_sigmoid(gts)))
    br_all = sum(_dot(part, tri_up) for part in _split3(_log_sigmoid(gtr)))

    for h in range(MLSTM_HEADS):
        fi = MLSTM_HEADS + h
        m_prev = m_scr[h, 0:1, 0:1]
        b_c = bc_all[:, fi:fi + 1]
        b_r = br_all[fi:fi + 1, :]
        i_c = gts[:, h:h + 1]
        i_r = gtr[h:h + 1, :]
        g = b_c[L - 1:L, :]
        d = jnp.where(causal, (b_c - b_r) + i_r, -jnp.inf)
        inter = b_c + m_prev
        m_t = jnp.maximum(inter, jnp.max(d, axis=-1, keepdims=True))
        w_inter = jnp.exp(inter - m_t)
        qh = qc[:, h * MLSTM_QK:(h + 1) * MLSTM_QK] * (MLSTM_QK ** -0.5)
        kh = kc[:, h * MLSTM_QK:(h + 1) * MLSTM_QK]
        vb = v_ref[rs, h * MLSTM_V:(h + 1) * MLSTM_V].astype(BF16)
        qb = qh.astype(BF16)
        p = jnp.exp(d - m_t) * _dot_nt(qb, kh.astype(BF16))
        c_prev = c_scr[h]
        n_prev = n_scr[h, 0:1, :]
        num = w_inter * _dot(qb, c_prev.astype(BF16)) + _dot(p.astype(BF16), vb)
        den = (w_inter * jnp.sum(qh * n_prev, axis=-1, keepdims=True)
               + jnp.sum(p, axis=-1, keepdims=True))
        hh = num * (1.0 / jnp.maximum(jnp.abs(den), jnp.exp(-m_t)))

        a = (g - b_c) + i_c
        m_new = jnp.maximum(g + m_prev, jnp.max(a, axis=0, keepdims=True))
        kw = kh * jnp.exp(a - m_new)
        decay = jnp.exp(g + m_prev - m_new)
        c_scr[h] = decay * c_prev + _dot(kw.T.astype(BF16), vb)
        n_scr[h, 0:1, :] = decay * n_prev + jnp.sum(kw, axis=0, keepdims=True)
        m_scr[h] = jnp.broadcast_to(m_new, (SUBLANES, LANES))

        hn = hh * lax.rsqrt(jnp.mean(hh * hh, axis=-1, keepdims=True) + EPS)
        vs = slice(h * MLSTM_V, (h + 1) * MLSTM_V)
        hn = hn * outg_ref[:, vs] * jax.nn.sigmoid(og_ref[rs, vs])
        hm_ref[rs, vs] = hn.astype(hm_ref.dtype)


def _mlstm(proj, gate_col, gate_b, conv_w, conv_b, out_g):
    s = proj.shape[0]
    L = MLSTM_CHUNKS_PER_STEP * MLSTM_CHUNK
    qkw = MLSTM_HEADS * MLSTM_QK
    vw = MLSTM_HEADS * MLSTM_V
    gate_blk = gate_col // LANES
    return pl.pallas_call(
        _mlstm_kernel,
        grid=(s // L,),
        in_specs=[pl.BlockSpec((L, qkw), lambda c: (c, 0)),
                  pl.BlockSpec((L, qkw), lambda c: (c, 1)),
                  pl.BlockSpec((L, vw), lambda c: (c, 1)),
                  pl.BlockSpec((L, vw), lambda c: (c, 2)),
                  pl.BlockSpec((L, LANES), lambda c: (c, gate_blk)),
                  pl.BlockSpec((1, LANES), lambda c: (0, 0)),
                  pl.BlockSpec((MLSTM_CONV, 2 * qkw), lambda c: (0, 0)),
                  pl.BlockSpec((1, 2 * qkw), lambda c: (0, 0)),
                  pl.BlockSpec((1, vw), lambda c: (0, 0))],
        out_specs=pl.BlockSpec((L, vw), lambda c: (c, 0)),
        out_shape=jax.ShapeDtypeStruct((s, vw), BF16),
        scratch_shapes=[pltpu.VMEM((L + SUBLANES, qkw), F32),
                        pltpu.VMEM((L + SUBLANES, qkw), F32),
                        pltpu.VMEM((MLSTM_HEADS, MLSTM_QK, MLSTM_V), F32),
                        pltpu.VMEM((MLSTM_HEADS, SUBLANES, LANES), F32),
                        pltpu.VMEM((MLSTM_HEADS, SUBLANES, LANES), F32)],
        compiler_params=_params("arbitrary"),
        name="mlstm",
    )(proj, proj, proj, proj, proj, gate_b, conv_w, conv_b, out_g)


def _moba_prep_kernel(q_ref, k_ref, v_ref, pos_ref, qg_ref, kg_ref, qt_ref, kn_ref, vt_ref, km_ref):
    rows = q_ref.shape[0]
    half = ROPE_DIM // 2
    lane = lax.broadcasted_iota(I32, (1, MOBA_HD), 1)
    inv_freq = jnp.exp((lane & (half - 1)).astype(F32) * (-(2.0 / ROPE_DIM) * math.log(ROPE_THETA)))
    ang = pos_ref[...].astype(F32) * inv_freq
    cos = jnp.where(lane < ROPE_DIM, jnp.cos(ang), 1.0)
    sin = jnp.sin(ang)
    sin = jnp.where(lane < half, -sin, jnp.where(lane < ROPE_DIM, sin, 0.0))

    def rope(x):
        partner = jnp.where(lane < half, pltpu.roll(x, MOBA_HD - half, 1), pltpu.roll(x, half, 1))
        return x * cos + partner * sin

    for h in range(MOBA_HEADS):
        hs = slice(h * MOBA_HD, (h + 1) * MOBA_HD)
        qt_ref[hs, :] = rope(_rms(q_ref[:, hs], qg_ref[...])).T
        kn = rope(_rms(k_ref[:, hs], kg_ref[...]))
        kn_ref[:, hs] = kn.astype(BF16)
        km_ref[0, :, hs] = jnp.sum(kn, axis=0, keepdims=True) * (1.0 / rows)
        vlo = h * MOBA_VT_ROWS
        vt_ref[0, vlo:vlo + MOBA_HD, :] = v_ref[:, hs].T.astype(BF16)
        vt_ref[0, vlo + MOBA_HD:vlo + MOBA_VT_ROWS, :] = jnp.ones((MOBA_VT_ROWS - MOBA_HD, rows), BF16)


def _moba_prep(proj, q_col, pos, q_g, k_g):
    s = proj.shape[0]
    w = MOBA_HEADS * MOBA_HD
    bs = MOBA_BLOCK
    nb = s // bs
    first = q_col // w
    return pl.pallas_call(
        _moba_prep_kernel,
        grid=(nb,),
        in_specs=[pl.BlockSpec((bs, w), lambda i: (i, first)),
                  pl.BlockSpec((bs, w), lambda i: (i, first + 1)),
                  pl.BlockSpec((bs, w), lambda i: (i, first + 2)),
                  pl.BlockSpec((bs, 1), lambda i: (i, 0)),
                  pl.BlockSpec((1, MOBA_HD), lambda i: (0, 0)),
                  pl.BlockSpec((1, MOBA_HD), lambda i: (0, 0))],
        out_specs=[pl.BlockSpec((w, bs), lambda i: (0, i)),
                   pl.BlockSpec((bs, w), lambda i: (i, 0)),
                   pl.BlockSpec((1, MOBA_HEADS * MOBA_VT_ROWS, bs), lambda i: (i, 0, 0)),
                   pl.BlockSpec((1, 1, w), lambda i: (i, 0, 0))],
        out_shape=[jax.ShapeDtypeStruct((w, s), F32),
                   jax.ShapeDtypeStruct((s, w), BF16),
                   jax.ShapeDtypeStruct((nb, MOBA_HEADS * MOBA_VT_ROWS, bs), BF16),
                   jax.ShapeDtypeStruct((nb, 1, w), F32)],
        compiler_params=_params("parallel"),
        name="moba_prep",
    )(proj, proj, proj, pos, q_g, k_g)


def _moba_attn_kernel(qt_ref, k_ref, vt_ref, km_ref, o_ref, bias_scr, qb_scr, m_scr, *tiles):
    i = pl.program_id(1)
    bs = MOBA_BLOCK
    hd = MOBA_HD
    nh = len(tiles) // 3
    vr = MOBA_VT_ROWS
    acc_scr = tiles[:nh]
    s_scr = tiles[nh:2 * nh]
    p_scr = tiles[2 * nh:]
    nb = km_ref.shape[0]
    c2 = (hd ** -0.5) * math.log2(math.e)
    blk = lax.broadcasted_iota(I32, (nb, bs), 0)
    blk_f = blk.astype(F32)
    valid = blk < i
    key = lax.broadcasted_iota(I32, (bs, bs), 0)
    qry = lax.broadcasted_iota(I32, (bs, bs), 1)
    own = pl.ds(pl.multiple_of(i * bs, bs), bs)

    for h in range(nh):
        hs = slice(h * hd, (h + 1) * hd)
        qt = qt_ref[hs, :]
        gate = jnp.dot(km_ref[:, 0, hs], qt, precision=lax.Precision.HIGHEST,
                       preferred_element_type=F32)
        cand = valid
        for _ in range(MOBA_TOPK):
            gmax = jnp.max(jnp.where(cand, gate, -jnp.inf), axis=0, keepdims=True)
            hit = cand & (gate == gmax)
            first = jnp.min(jnp.where(hit, blk_f, float(nb)), axis=0, keepdims=True)
            cand = cand & jnp.logical_not(hit & (blk_f == first))
        bias_scr[h] = jnp.where(valid & jnp.logical_not(cand), 0.0, -jnp.inf)

        qb = (qt * c2).astype(BF16)
        qb_scr[h] = qb
        s0 = jnp.where(key <= qry, _dot(k_ref[own, hs], qb), -jnp.inf)
        m0 = jnp.max(s0, axis=0, keepdims=True)
        p0 = jnp.exp2(s0 - m0)
        m_scr[h] = m0
        acc_scr[h][...] = _dot(vt_ref[i, h * vr:(h + 1) * vr, :], p0.astype(BF16))

    ck = 64

    def body(j, carry):
        rows = pl.ds(pl.multiple_of(j * bs, bs), bs)
        for h in range(nh):
            s_scr[h][...] = _dot(k_ref[rows, h * hd:(h + 1) * hd], qb_scr[h])
        alphas = []
        for h in range(nh):
            bias = bias_scr[h, pl.ds(j, 1), :]
            cmax = s_scr[h][0:ck, :]
            for c in range(1, bs // ck):
                cmax = jnp.maximum(cmax, s_scr[h][c * ck:(c + 1) * ck, :])
            m = m_scr[h]
            m_new = jnp.maximum(m, jnp.max(cmax, axis=0, keepdims=True) + bias)
            alpha = jnp.exp2(m - m_new)
            shift = m_new - bias
            for c in range(bs // ck):
                p = jnp.exp2(s_scr[h][c * ck:(c + 1) * ck, :] - shift)
                p_scr[h][c * ck:(c + 1) * ck, :] = p.astype(BF16)
            m_scr[h] = m_new
            alphas.append(alpha)
        for h in range(nh):
            acc_scr[h][...] = (alphas[h] * acc_scr[h][...]
                               + _dot(vt_ref[j, h * vr:(h + 1) * vr, :], p_scr[h][...]))
        return carry

    lax.fori_loop(0, i, body, 0)
    for h in range(nh):
        acc = acc_scr[h][...]
        out_t = acc[0:hd] * (1.0 / acc[hd:hd + 1])
        o_ref[:, h * hd:(h + 1) * hd] = out_t.T.astype(o_ref.dtype)


MOBA_HEADS_PER_STEP = 8


def _moba_attn(qt, kn, vt, kmean):
    w, s = qt.shape
    bs = MOBA_BLOCK
    nb = s // bs
    nh = MOBA_HEADS_PER_STEP
    gw = nh * MOBA_HD
    resident = pl.Buffered(1)
    return pl.pallas_call(
        _moba_attn_kernel,
        grid=(MOBA_HEADS // nh, nb),
        in_specs=[pl.BlockSpec((gw, bs), lambda g, i: (g, i)),
                  pl.BlockSpec((s, gw), lambda g, i: (0, g), pipeline_mode=resident),
                  pl.BlockSpec((nb, nh * MOBA_VT_ROWS, bs), lambda g, i: (0, g, 0),
                               pipeline_mode=resident),
                  pl.BlockSpec((nb, 1, gw), lambda g, i: (0, 0, g))],
        out_specs=pl.BlockSpec((bs, gw), lambda g, i: (i, g)),
        out_shape=jax.ShapeDtypeStruct((s, w), BF16),
        scratch_shapes=[pltpu.VMEM((nh, nb, bs), F32),
                        pltpu.VMEM((nh, MOBA_HD, bs), BF16),
                        pltpu.VMEM((nh, 1, bs), F32)]
                       + [pltpu.VMEM((MOBA_VT_ROWS, bs), F32) for _ in range(nh)]
                       + [pltpu.VMEM((bs, bs), F32) for _ in range(nh)]
                       + [pltpu.VMEM((bs, bs), BF16) for _ in range(nh)],
        compiler_params=_params("parallel", "arbitrary"),
        name="moba_attn",
    )(qt, kn, vt, kmean)


def _mix_out_kernel(hm_ref, ha_ref, w_ref, x_ref, o_ref, wb_ref):
    half = hm_ref.shape[1]

    @pl.when(pl.program_id(0) == 0)
    def _():
        wb_ref[...] = w_ref[...].astype(BF16)

    o_ref[...] = (x_ref[...] + _dot(hm_ref[...], wb_ref[0:half, :])
                  + _dot(ha_ref[...], wb_ref[half:2 * half, :]))


def _mix_out(hm, ha, w, x, tm):
    s, d = x.shape
    half = hm.shape[1]
    return pl.pallas_call(
        _mix_out_kernel,
        grid=(s // tm,),
        in_specs=[pl.BlockSpec((tm, half), lambda i: (i, 0)),
                  pl.BlockSpec((tm, half), lambda i: (i, 0)),
                  pl.BlockSpec((2 * half, d), lambda i: (0, 0), pipeline_mode=pl.Buffered(1)),
                  pl.BlockSpec((tm, d), lambda i: (i, 0))],
        out_specs=pl.BlockSpec((tm, d), lambda i: (i, 0)),
        out_shape=jax.ShapeDtypeStruct((s, d), F32),
        scratch_shapes=[pltpu.VMEM((2 * half, d), BF16)],
        compiler_params=_params("arbitrary"),
        name="mix_out",
    )(hm, ha, w, x)


def _mem_kv_kernel(mem_ref, g_ref, w_ref, kg_ref, o_ref, mn_ref):
    j = pl.program_id(0)

    @pl.when(j == 0)
    def _():
        mn_ref[...] = _rms(mem_ref[...], g_ref[...]).astype(BF16)

    y = _dot(mn_ref[...], w_ref[...].astype(BF16))
    o_ref[...] = jnp.where(j < XA_HEADS, _rms(y, kg_ref[...]), y).astype(o_ref.dtype)


def _mem_kv(mem, g, wkv, k_g):
    m, d = mem.shape
    hd = d // XA_HEADS
    return pl.pallas_call(
        _mem_kv_kernel,
        grid=(2 * XA_HEADS,),
        in_specs=[pl.BlockSpec((m, d), lambda j: (0, 0)),
                  pl.BlockSpec((1, d), lambda j: (0, 0)),
                  pl.BlockSpec((d, hd), lambda j: (0, j)),
                  pl.BlockSpec((1, hd), lambda j: (0, 0))],
        out_specs=pl.BlockSpec((m, hd), lambda j: (0, j)),
        out_shape=jax.ShapeDtypeStruct((m, 2 * d), BF16),
        scratch_shapes=[pltpu.VMEM((m, d), BF16)],
        compiler_params=_params("arbitrary"),
        name="mem_kv",
    )(mem, g, wkv, k_g)


def _xattn_kernel(h_ref, g_ref, wq_hbm, kv_ref, qg_ref, wo_hbm, rg_ref, rw_ref, rb_ref,
                  o_ref, routed_ref, cnt_ref, wqb, wob, stage, o_all, carry, sem):
    d = h_ref.shape[1]
    hd = d // XA_HEADS
    n_slab = 2 * XA_HEADS

    @pl.when(pl.program_id(0) == 0)
    def _():
        def slab_copy(k):
            src = wq_hbm if k < XA_HEADS else wo_hbm
            cols = pl.ds((k % XA_HEADS) * hd, hd)
            return pltpu.make_async_copy(src.at[:, cols], stage.at[k % 2], sem.at[k % 2])

        slab_copy(0).start()
        for k in range(n_slab):
            if k + 1 < n_slab:
                slab_copy(k + 1).start()
            slab_copy(k).wait()
            dst = wqb if k < XA_HEADS else wob
            dst[:, (k % XA_HEADS) * hd:(k % XA_HEADS + 1) * hd] = stage[k % 2].astype(BF16)

    h = h_ref[...]
    q_all = _dot(_rms(h, g_ref[...]).astype(BF16), wqb[...])
    for j in range(XA_HEADS):
        hs = slice(j * hd, (j + 1) * hd)
        q = _rms(q_all[:, hs], qg_ref[...]).astype(BF16)
        sc = _dot_nt(q, kv_ref[:, hs]) * (hd ** -0.5)
        p = jnp.exp(sc - jnp.max(sc, axis=-1, keepdims=True))
        p = p * (1.0 / jnp.sum(p, axis=-1, keepdims=True))
        o_all[:, hs] = _dot(p.astype(BF16), kv_ref[:, d + j * hd:d + (j + 1) * hd]).astype(BF16)
    h_out = h + _dot(o_all[...], wob[...])
    o_ref[...] = h_out
    _route_rows(h_out, rg_ref, rw_ref, rb_ref, routed_ref, cnt_ref, carry)


def _xattn_route(h, g, wq, kv, q_g, wo, route_g, route_w, route_b, tm):
    s, d = h.shape
    m = kv.shape[0]
    hd = d // XA_HEADS
    return pl.pallas_call(
        _xattn_kernel,
        grid=(s // tm,),
        in_specs=[pl.BlockSpec((tm, d), lambda i: (i, 0)),
                  pl.BlockSpec((1, d), lambda i: (0, 0)),
                  pl.BlockSpec(memory_space=pl.ANY),
                  pl.BlockSpec((m, 2 * d), lambda i: (0, 0)),
                  pl.BlockSpec((1, hd), lambda i: (0, 0)),
                  pl.BlockSpec(memory_space=pl.ANY),
                  pl.BlockSpec((1, d), lambda i: (0, 0)),
                  pl.BlockSpec((2, d, LANES), lambda i: (0, 0, 0)),
                  pl.BlockSpec((1, LANES), lambda i: (0, 0))],
        out_specs=[pl.BlockSpec((tm, d), lambda i: (i, 0)),
                   pl.BlockSpec((tm, LANES), lambda i: (i, 0)),
                   pl.BlockSpec((SUBLANES, LANES), lambda i: (0, 0))],
        out_shape=[jax.ShapeDtypeStruct((s, d), F32),
                   jax.ShapeDtypeStruct((s, LANES), F32),
                   jax.ShapeDtypeStruct((SUBLANES, LANES), F32)],
        scratch_shapes=[pltpu.VMEM((d, d), BF16), pltpu.VMEM((d, d), BF16),
                        pltpu.VMEM((2, d, hd), F32), pltpu.VMEM((tm, d), BF16),
                        pltpu.VMEM((SUBLANES, LANES), F32),
                        pltpu.SemaphoreType.DMA((2,))],
        compiler_params=_params("arbitrary"),
        name="xattn_route",
    )(h, g, wq, kv, q_g, wo, route_g, route_w, route_b)


ROUTE_EID = 0
ROUTE_RANK = 2
ROUTE_WT = 4


def _route_rows(h, g_ref, w_ref, b_ref, out_ref, cnt_ref, carry):
    tm = h.shape[0]

    @pl.when(pl.program_id(0) == 0)
    def _():
        carry[...] = jnp.zeros_like(carry)

    hn = _rms(h, g_ref[...])
    hi = hn.astype(BF16)
    lo = (hn - hi.astype(F32)).astype(BF16)
    logits = (_dot(hi, w_ref[0]) + _dot(hi, w_ref[1]) + _dot(lo, w_ref[0])
              + b_ref[...])
    lane = lax.broadcasted_iota(I32, (tm, LANES), 1)

    def first_lane(mask):
        return jnp.min(jnp.where(mask, lane.astype(F32), float(LANES)), axis=-1,
                       keepdims=True).astype(I32)

    is_g = lane < MOE_GROUPS
    gmax = jnp.max(jnp.where(is_g, logits, -jnp.inf), axis=-1, keepdims=True)
    gsum = jnp.sum(jnp.where(is_g, jnp.exp(logits - gmax), 0.0), axis=-1, keepdims=True)
    gsel = first_lane(is_g & (logits == gmax))
    pg = 1.0 / gsum

    grp_lo = MOE_GROUPS + MOE_PER_GROUP * gsel
    in_grp = (lane >= grp_lo) & (lane < grp_lo + MOE_PER_GROUP)
    emax = jnp.max(jnp.where(in_grp, logits, -jnp.inf), axis=-1, keepdims=True)
    eexp = jnp.where(in_grp, jnp.exp(logits - emax), 0.0)
    eprob = eexp / jnp.sum(eexp, axis=-1, keepdims=True)
    p1 = jnp.max(jnp.where(in_grp, eprob, -1.0), axis=-1, keepdims=True)
    l1 = first_lane(in_grp & (eprob == p1))
    rest = in_grp & (lane != l1)
    p2 = jnp.max(jnp.where(rest, eprob, -1.0), axis=-1, keepdims=True)
    l2 = first_lane(rest & (eprob == p2))
    psum = p1 + p2
    w1 = pg * p1 / psum
    w2 = pg * p2 / psum

    oh1 = lane == l1
    oh2 = lane == l2
    onehot = jnp.where(oh1 | oh2, 1.0, 0.0)
    r = lax.broadcasted_iota(I32, (tm, tm), 0)
    c = lax.broadcasted_iota(I32, (tm, tm), 1)
    strict = jnp.where(c < r, 1.0, 0.0).astype(BF16)
    before = _dot(strict, onehot.astype(BF16)) + carry[0:1, :]
    rank1 = jnp.sum(jnp.where(oh1, before, 0.0), axis=-1, keepdims=True)
    rank2 = jnp.sum(jnp.where(oh2, before, 0.0), axis=-1, keepdims=True)
    total = carry[0:1, :] + jnp.sum(onehot, axis=0, keepdims=True)
    carry[...] = jnp.broadcast_to(total, carry.shape)
    cnt_ref[...] = jnp.broadcast_to(total, cnt_ref.shape)

    out = jnp.where(lane == ROUTE_EID, (l1 - MOE_GROUPS).astype(F32), 0.0)
    out = jnp.where(lane == ROUTE_EID + 1, (l2 - MOE_GROUPS).astype(F32), out)
    out = jnp.where(lane == ROUTE_RANK, rank1, out)
    out = jnp.where(lane == ROUTE_RANK + 1, rank2, out)
    out = jnp.where(lane == ROUTE_WT, w1, out)
    out = jnp.where(lane == ROUTE_WT + 1, w2, out)
    out_ref[...] = out


def _row_tiles(x):
    return x.astype(BF16).reshape(x.shape[0], x.shape[1] // LANES, LANES)


def _from_row_tiles(x3):
    return x3.reshape(x3.shape[0], x3.shape[1] * LANES)


def _dispatch_kernel(d0_ref, d1_ref, zb_ref, h_ref, g_ref, xg_ref, buf, zero, sem, zsem):
    tm = h_ref.shape[0]
    base = pl.program_id(0) * tm

    @pl.when(pl.program_id(0) == 0)
    def _():
        zero[...] = jnp.zeros_like(zero)
        n_blk = xg_ref.shape[0] // MOE_ROWS

        def zero_copy(b):
            rows = pl.ds(pl.multiple_of(b * MOE_ROWS, MOE_ROWS), MOE_ROWS)
            return pltpu.make_async_copy(zero, xg_ref.at[rows], zsem)

        def start(b, carry):
            @pl.when(zb_ref[b] != 0)
            def _():
                zero_copy(b).start()
            return carry

        def finish(b, carry):
            @pl.when(zb_ref[b] != 0)
            def _():
                zero_copy(0).wait()
            return carry

        lax.fori_loop(0, n_blk, start, 0)
        lax.fori_loop(0, n_blk, finish, 0)

    buf[...] = _row_tiles(_rms(h_ref[...], g_ref[...]))

    def row_copy(t, d):
        return pltpu.make_async_copy(buf.at[pl.ds(t, 1)], xg_ref.at[pl.ds(d, 1)], sem)

    def issue(t, carry):
        row_copy(t, d0_ref[base + t]).start()
        row_copy(t, d1_ref[base + t]).start()
        return carry

    lax.fori_loop(0, tm, issue, 0, unroll=ROW_DMA_UNROLL)
    whole = pltpu.make_async_copy(buf, xg_ref.at[pl.ds(0, tm)], sem)
    whole.wait()
    whole.wait()


def _dispatch(dest, zero_blk, h, g, tm):
    s, d = h.shape
    n_rows = zero_blk.shape[0] * MOE_ROWS
    grid_spec = pltpu.PrefetchScalarGridSpec(
        num_scalar_prefetch=3,
        grid=(s // tm,),
        in_specs=[pl.BlockSpec((tm, d), lambda i, d0, d1, zb: (i, 0)),
                  pl.BlockSpec((1, d), lambda i, d0, d1, zb: (0, 0))],
        out_specs=pl.BlockSpec(memory_space=pl.ANY),
        scratch_shapes=[pltpu.VMEM((tm, d // LANES, LANES), BF16),
                        pltpu.VMEM((MOE_ROWS, d // LANES, LANES), BF16),
                        pltpu.SemaphoreType.DMA(()), pltpu.SemaphoreType.DMA(())],
    )
    return pl.pallas_call(
        _dispatch_kernel,
        grid_spec=grid_spec,
        out_shape=jax.ShapeDtypeStruct((n_rows, d // LANES, LANES), BF16),
        compiler_params=_params("arbitrary"),
        name="moe_dispatch",
    )(dest[0], dest[1], zero_blk, h, g)


def _moe_ffn_kernel(be_ref, nu_ref, nx_ref, x_ref, wg_hbm, wu_hbm, wd_hbm, y_ref,
                    wgf, wuf, wdf, wgb, wub, wdb, slot_ref, sem):
    b = pl.program_id(0)
    e = be_ref[b]
    used = b < nu_ref[0]

    def weight_copies(expert, slot):
        return (pltpu.make_async_copy(wg_hbm.at[expert], wgf.at[slot], sem.at[slot, 0]),
                pltpu.make_async_copy(wu_hbm.at[expert], wuf.at[slot], sem.at[slot, 1]),
                pltpu.make_async_copy(wd_hbm.at[expert], wdf.at[slot], sem.at[slot, 2]))

    @pl.when(b == 0)
    def _():
        slot_ref[0] = 0
        for c in weight_copies(e, 0):
            c.start()

    @pl.when(used & ((b == 0) | (e != be_ref[jnp.maximum(b - 1, 0)])))
    def _():
        slot = slot_ref[0]
        for c in weight_copies(e, slot):
            c.wait()
        nxt = nx_ref[e]

        @pl.when(nxt >= 0)
        def _():
            for c in weight_copies(nxt, 1 - slot):
                c.start()

        wgb[...] = wgf[slot].astype(BF16)
        wub[...] = wuf[slot].astype(BF16)
        wdb[...] = wdf[slot].astype(BF16)
        slot_ref[0] = 1 - slot

    @pl.when(used)
    def _():
        xb = _from_row_tiles(x_ref[...])
        gate = _dot(xb, wgb[...])
        up = _dot(xb, wub[...])
        act = (gate * jax.nn.sigmoid(gate) * up).astype(BF16)
        y_ref[...] = _dot(act, wdb[...])

    @pl.when(jnp.logical_not(used))
    def _():
        y_ref[...] = jnp.zeros_like(y_ref)


def _moe_ffn(blk_e, n_used, next_e, xg, w_gate, w_up, w_down):
    n_rows, nt, _ = xg.shape
    d = nt * LANES
    ff = w_gate.shape[2]
    n_blk = n_rows // MOE_ROWS
    def row_blk(b, be, nu, nx):
        return jnp.minimum(b, nu[0] - 1), 0, 0

    grid_spec = pltpu.PrefetchScalarGridSpec(
        num_scalar_prefetch=3,
        grid=(n_blk,),
        in_specs=[pl.BlockSpec((MOE_ROWS, nt, LANES), row_blk),
                  pl.BlockSpec(memory_space=pl.ANY),
                  pl.BlockSpec(memory_space=pl.ANY),
                  pl.BlockSpec(memory_space=pl.ANY)],
        out_specs=pl.BlockSpec((MOE_ROWS, d), lambda b, be, nu, nx: (b, 0)),
        scratch_shapes=[pltpu.VMEM((2, d, ff), F32), pltpu.VMEM((2, d, ff), F32),
                        pltpu.VMEM((2, ff, d), F32),
                        pltpu.VMEM((d, ff), BF16), pltpu.VMEM((d, ff), BF16),
                        pltpu.VMEM((ff, d), BF16),
                        pltpu.SMEM((1,), I32), pltpu.SemaphoreType.DMA((2, 3))],
    )
    return pl.pallas_call(
        _moe_ffn_kernel,
        grid_spec=grid_spec,
        out_shape=jax.ShapeDtypeStruct((n_rows, d), F32),
        compiler_params=_params("arbitrary"),
        name="moe_ffn",
    )(blk_e, n_used, next_e, xg, w_gate, w_up, w_down)


def _combine_kernel(d0_ref, d1_ref, h_ref, r_ref, y_ref, o_ref, buf, sem):
    tm = h_ref.shape[0]
    base = pl.program_id(0) * tm

    def row_copy(k, t, d):
        return pltpu.make_async_copy(y_ref.at[pl.ds(d, 1), :], buf.at[k, pl.ds(t, 1), :], sem)

    def issue(t, carry):
        row_copy(0, t, d0_ref[base + t]).start()
        row_copy(1, t, d1_ref[base + t]).start()
        return carry

    lax.fori_loop(0, tm, issue, 0, unroll=ROW_DMA_UNROLL)
    for k in range(2):
        pltpu.make_async_copy(y_ref.at[pl.ds(0, tm), :], buf.at[k], sem).wait()
    r = r_ref[...]
    o_ref[...] = (h_ref[...] + r[:, ROUTE_WT:ROUTE_WT + 1] * buf[0]
                  + r[:, ROUTE_WT + 1:ROUTE_WT + 2] * buf[1])


def _combine(dest, h, routed, y, tm):
    s, d = h.shape
    grid_spec = pltpu.PrefetchScalarGridSpec(
        num_scalar_prefetch=2,
        grid=(s // tm,),
        in_specs=[pl.BlockSpec((tm, d), lambda i, d0, d1: (i, 0)),
                  pl.BlockSpec((tm, LANES), lambda i, d0, d1: (i, 0)),
                  pl.BlockSpec(memory_space=pl.ANY)],
        out_specs=pl.BlockSpec((tm, d), lambda i, d0, d1: (i, 0)),
        scratch_shapes=[pltpu.VMEM((2, tm, d), F32), pltpu.SemaphoreType.DMA(())],
    )
    return pl.pallas_call(
        _combine_kernel,
        grid_spec=grid_spec,
        out_shape=jax.ShapeDtypeStruct((s, d), F32),
        compiler_params=_params("arbitrary"),
        name="moe_combine",
    )(dest[0], dest[1], h, routed, y)


def _layer(h, mem, pos, norm_mix_g, w_in, gate_b, conv_w, conv_b, out_g, moba_q_g, moba_k_g, w_out,
           norm_cross_g, norm_mem_g, xa_wq, xa_wkv, xa_q_g, xa_k_g, xa_wo, norm_ffn_g,
           router_group_w, router_group_b, router_expert_w, router_expert_b,
           exp_w_gate, exp_w_up, exp_w_down):
    s, d = h.shape
    row = lambda v: v.reshape(1, -1)

    n_gate = 2 * MLSTM_HEADS
    gate_lo = 2 * MLSTM_HEADS * MLSTM_QK + 2 * MLSTM_HEADS * MLSTM_V
    proj_tn = 512
    w_t = w_in.T
    w_a = w_t[:gate_lo].astype(BF16)
    w_b = w_t[gate_lo + n_gate:].astype(BF16)
    w_g = jnp.pad(w_t[gate_lo:gate_lo + n_gate], ((0, proj_tn - n_gate), (0, 0))).astype(BF16)
    proj = _proj_in(h, row(norm_mix_g), w_a, w_b, w_g, tm=min(s, 1024))

    gate_b_row = jnp.pad(gate_b, (0, LANES - n_gate)).reshape(1, LANES)
    hm = _mlstm(proj, w_a.shape[0] + w_b.shape[0], gate_b_row, conv_w, row(conv_b), row(out_g))
    qn, kn, vb, kmean = _moba_prep(proj, gate_lo, pos, row(moba_q_g), row(moba_k_g))
    ha = _moba_attn(qn, kn, vb, kmean)
    h = _mix_out(hm, ha, w_out, h, tm=512)

    kv = _mem_kv(mem, row(norm_mem_g), xa_wkv, row(xa_k_g))

    w_route = jnp.concatenate(
        [router_group_w, router_expert_w,
         jnp.zeros((d, LANES - MOE_GROUPS - MOE_EXPERTS), router_group_w.dtype)], axis=1)
    w_route_hi = w_route.astype(BF16)
    w_route = jnp.stack([w_route_hi, (w_route - w_route_hi.astype(F32)).astype(BF16)])
    b_route = jnp.pad(jnp.concatenate([router_group_b, router_expert_b]),
                      (0, LANES - MOE_GROUPS - MOE_EXPERTS)).reshape(1, LANES)
    h, routed, counts = _xattn_route(h, row(norm_cross_g), xa_wq, kv, row(xa_q_g), xa_wo,
                                     row(norm_ffn_g), w_route, b_route, tm=512)

    cnt = counts[0, MOE_GROUPS:MOE_GROUPS + MOE_EXPERTS].astype(I32)
    padded = (cnt + MOE_ROWS - 1) // MOE_ROWS * MOE_ROWS
    pend = jnp.cumsum(padded)
    seg_start = (pend - padded).astype(F32)[None, :]
    expert_ids = jnp.arange(MOE_EXPERTS, dtype=F32)[None, :]

    def dest_of(k):
        mine = routed[:, ROUTE_EID + k:ROUTE_EID + k + 1] == expert_ids
        return (jnp.sum(jnp.where(mine, seg_start, 0.0), axis=1) + routed[:, ROUTE_RANK + k]).astype(I32)

    dest = (dest_of(0), dest_of(1))
    n_blk = -(-2 * s // MOE_ROWS) + MOE_EXPERTS
    blk_start = jnp.arange(n_blk, dtype=I32) * MOE_ROWS
    blk_e = jnp.minimum(jnp.sum((pend[None, :] <= blk_start[:, None]).astype(I32), axis=1),
                        MOE_EXPERTS - 1)
    n_used = (pend[-1:] // MOE_ROWS).astype(I32)
    ids = jnp.arange(MOE_EXPERTS, dtype=I32)
    later_used = (ids[None, :] > ids[:, None]) & (cnt[None, :] > 0)
    next_e = jnp.min(jnp.where(later_used, ids[None, :], MOE_EXPERTS), axis=1)
    next_e = jnp.where(next_e < MOE_EXPERTS, next_e, -1).astype(I32)

    is_blk_e = blk_e[:, None] == ids[None, :]
    seg_end = jnp.sum(jnp.where(is_blk_e, (pend - padded + cnt)[None, :], 0), axis=1)
    zero_blk = (blk_start + MOE_ROWS > seg_end).astype(I32)
    xg = _dispatch(dest, zero_blk, h, row(norm_ffn_g), tm=256)
    y = _moe_ffn(blk_e, n_used, next_e, xg, exp_w_gate, exp_w_up, exp_w_down)
    return _combine(dest, h, routed, y, tm=256)


def kernel(x, mem, positions, norm_mix_g, w_in, mlstm_gate_b, mlstm_conv_w, mlstm_conv_b, mlstm_out_g, moba_q_g, moba_k_g, w_out, norm_cross_g, norm_mem_g, xa_wq, xa_wkv, xa_q_g, xa_k_g, xa_wo, norm_ffn_g, router_group_w, router_group_b, router_expert_w, router_expert_b, exp_w_gate, exp_w_up, exp_w_down):
    bsz, s, _ = x.shape
    assert bsz == 1, "single-sequence prefill only"
    per_layer = (norm_mix_g, w_in, mlstm_gate_b, mlstm_conv_w, mlstm_conv_b, mlstm_out_g, moba_q_g,
                 moba_k_g, w_out, norm_cross_g, norm_mem_g, xa_wq, xa_wkv, xa_q_g, xa_k_g, xa_wo,
                 norm_ffn_g, router_group_w, router_group_b, router_expert_w, router_expert_b,
                 exp_w_gate, exp_w_up, exp_w_down)
    h = x[0]
    pos = positions.reshape(s, 1)
    for l in range(norm_mix_g.shape[0]):
        h = _layer(h, mem[0], pos, *(p[l] for p in per_layer))
    return h[None]
```

```python
import functools
import math

import jax
import jax.numpy as jnp
from jax import lax
from jax.experimental import pallas as pl
from jax.experimental.pallas import tpu as pltpu

F32 = jnp.float32
BF16 = jnp.bfloat16
I32 = jnp.int32

EPS = 1e-6
LANES = 128
SUBLANES = 8
VMEM_LIMIT = 56 * 1024 * 1024

MLSTM_HEADS = 4
MLSTM_QK = 128
MLSTM_V = 256
MLSTM_CHUNK = 128
MLSTM_CONV = 4
MOBA_HEADS = 8
MOBA_HD = 128
MOBA_BLOCK = 256
MOBA_TOPK = 3
MOBA_VT_ROWS = MOBA_HD + 16
ROPE_DIM = 32
ROPE_THETA = 500000.0
XA_HEADS = 4
MOE_GROUPS = 8
MOE_PER_GROUP = 8
MOE_EXPERTS = MOE_GROUPS * MOE_PER_GROUP
MOE_ROWS = 128
ROW_DMA_UNROLL = 8

PROJ_ROW_TILE = 1024
PROJ_COL_TILE = 512
MIX_ROW_TILE = 512
XATTN_ROW_TILE = 512
MOE_TOKEN_TILE = 256

NT_DIMS = (((1,), (1,)), ((), ()))


def _params(*sem):
    return pltpu.CompilerParams(dimension_semantics=sem, vmem_limit_bytes=VMEM_LIMIT)


def _rms(x, g):
    return x * lax.rsqrt(jnp.mean(x * x, axis=-1, keepdims=True) + EPS) * g


def _dot(a, b):
    return jnp.dot(a, b, preferred_element_type=F32)


def _dot_nt(a, b, precision=None):
    return lax.dot_general(a, b, NT_DIMS, precision=precision, preferred_element_type=F32)


def _proj_in_kernel(x_ref, g_ref, wa_ref, wb_ref, wg_ref, o_ref, xn_ref, *, na, nb):
    j = pl.program_id(1)

    @pl.when(j == 0)
    def _():
        xn_ref[...] = _rms(x_ref[...], g_ref[...]).astype(BF16)

    @pl.when(j < na)
    def _():
        o_ref[...] = _dot_nt(xn_ref[...], wa_ref[...])

    @pl.when((j >= na) & (j < na + nb))
    def _():
        o_ref[...] = _dot_nt(xn_ref[...], wb_ref[...])

    @pl.when(j == na + nb)
    def _():
        o_ref[...] = _dot_nt(xn_ref[...], wg_ref[...])


def _proj_in(x, g, w_a, w_b, w_g, tm):
    s, d = x.shape
    tn = w_g.shape[0]
    na, nb = w_a.shape[0] // tn, w_b.shape[0] // tn
    return pl.pallas_call(
        functools.partial(_proj_in_kernel, na=na, nb=nb),
        grid=(s // tm, na + nb + 1),
        in_specs=[pl.BlockSpec((tm, d), lambda i, j: (i, 0)),
                  pl.BlockSpec((1, d), lambda i, j: (0, 0)),
                  pl.BlockSpec((tn, d), lambda i, j: (jnp.minimum(j, na - 1), 0)),
                  pl.BlockSpec((tn, d), lambda i, j: (jnp.clip(j - na, 0, nb - 1), 0)),
                  pl.BlockSpec((tn, d), lambda i, j: (0, 0))],
        out_specs=pl.BlockSpec((tm, tn), lambda i, j: (i, j)),
        out_shape=jax.ShapeDtypeStruct((s, (na + nb + 1) * tn), F32),
        scratch_shapes=[pltpu.VMEM((tm, d), BF16)],
        compiler_params=_params("parallel", "arbitrary"),
        name="proj_in",
    )(x, g, w_a, w_b, w_g)


def _split3(x):
    hi = x.astype(BF16)
    r = x - hi.astype(F32)
    mid = r.astype(BF16)
    lo = (r - mid.astype(F32)).astype(BF16)
    return hi, mid, lo


def _log_sigmoid(x):
    return jnp.minimum(x, 0.0) - jnp.log(1.0 + jnp.exp(-jnp.abs(x)))


def _mlstm_kernel(q_ref, k_ref, v_ref, og_ref, gt_ref, gb_ref, cw_ref, cb_ref, outg_ref, hm_ref,
                  qext, kext, c_scr, n_scr, m_scr):
    L = MLSTM_CHUNK
    pad = SUBLANES
    qkw = MLSTM_HEADS * MLSTM_QK
    rows_blk = q_ref.shape[0]

    @pl.when(pl.program_id(0) == 0)
    def _():
        qext[0:pad, :] = jnp.zeros((pad, qkw), F32)
        kext[0:pad, :] = jnp.zeros((pad, qkw), F32)
        c_scr[...] = jnp.zeros_like(c_scr)
        n_scr[...] = jnp.zeros_like(n_scr)
        m_scr[...] = jnp.zeros_like(m_scr)

    qext[pad:pad + rows_blk, :] = q_ref[...]
    kext[pad:pad + rows_blk, :] = k_ref[...]

    def conv_silu(ext, lo):
        acc = jnp.broadcast_to(cb_ref[:, lo:lo + qkw], (rows_blk, qkw))
        for j in range(MLSTM_CONV):
            shift = MLSTM_CONV - 1 - j
            acc = acc + cw_ref[j:j + 1, lo:lo + qkw] * ext[pad - shift:pad - shift + rows_blk, :]
        return acc * jax.nn.sigmoid(acc)

    qc_all = conv_silu(qext, 0)
    kc_all = conv_silu(kext, qkw)
    qext[0:pad, :] = qext[rows_blk:rows_blk + pad, :]
    kext[0:pad, :] = kext[rows_blk:rows_blk + pad, :]

    row = lax.broadcasted_iota(I32, (L, L), 0)
    col = lax.broadcasted_iota(I32, (L, L), 1)
    causal = col <= row
    tri_lo = jnp.where(causal, 1.0, 0.0).astype(BF16)
    tri_up = jnp.where(row <= col, 1.0, 0.0).astype(BF16)
    for ck in range(rows_blk // L):
        rs = slice(ck * L, (ck + 1) * L)
        _mlstm_chunk(rs, qc_all[rs], kc_all[rs], v_ref, og_ref, gt_ref, gb_ref, outg_ref, hm_ref,
                     c_scr, n_scr, m_scr, causal, tri_lo, tri_up)


def _mlstm_chunk(rs, qc, kc, v_ref, og_ref, gt_ref, gb_ref, outg_ref, hm_ref, c_scr, n_scr, m_scr,
                 causal, tri_lo, tri_up):
    L = MLSTM_CHUNK
    gts = gt_ref[rs, :] + gb_ref[...]
    gtr = gts.T
    bc_all = sum(_dot(tri_lo, part) for part in _split3(_log_sigmoid(gts)))
    br_all = sum(_dot(part, tri_up) for part in _split3(_log_sigmoid(gtr)))

    for h in range(MLSTM_HEADS):
        fi = MLSTM_HEADS + h
        m_prev = m_scr[h, 0:1, 0:1]
        b_c = bc_all[:, fi:fi + 1]
        b_r = br_all[fi:fi + 1, :]
        i_c = gts[:, h:h + 1]
        i_r = gtr[h:h + 1, :]
        g = b_c[L - 1:L, :]
        d = jnp.where(causal, (b_c - b_r) + i_r, -jnp.inf)
        inter = b_c + m_prev
        m_t = jnp.maximum(inter, jnp.max(d, axis=-1, keepdims=True))
        w_inter = jnp.exp(inter - m_t)
        qh = qc[:, h * MLSTM_QK:(h + 1) * MLSTM_QK] * (MLSTM_QK ** -0.5)
        kh = kc[:, h * MLSTM_QK:(h + 1) * MLSTM_QK]
        vb = v_ref[rs, h * MLSTM_V:(h + 1) * MLSTM_V].astype(BF16)
        qb = qh.astype(BF16)
        p = jnp.exp(d - m_t) * _dot_nt(qb, kh.astype(BF16))
        c_prev = c_scr[h]
        n_prev = n_scr[h, 0:1, :]
        num = w_inter * _dot(qb, c_prev.astype(BF16)) + _dot(p.astype(BF16), vb)
        den = (w_inter * jnp.sum(qh * n_prev, axis=-1, keepdims=True)
               + jnp.sum(p, axis=-1, keepdims=True))
        hh = num * (1.0 / jnp.maximum(jnp.abs(den), jnp.exp(-m_t)))

        a = (g - b_c) + i_c
        m_new = jnp.maximum(g + m_prev, jnp.max(a, axis=0, keepdims=True))
        kw = kh * jnp.exp(a - m_new)
        decay = jnp.exp(g + m_prev - m_new)
        c_scr[h] = decay * c_prev + _dot(kw.T.astype(BF16), vb)
        n_scr[h, 0:1, :] = decay * n_prev + jnp.sum(kw, axis=0, keepdims=True)
        m_scr[h] = jnp.broadcast_to(m_new, (SUBLANES, LANES))

        hn = hh * lax.rsqrt(jnp.mean(hh * hh, axis=-1, keepdims=True) + EPS)
        vs = slice(h * MLSTM_V, (h + 1) * MLSTM_V)
        hn = hn * outg_ref[:, vs] * jax.nn.sigmoid(og_ref[rs, vs])
        hm_ref[rs, vs] = hn.astype(hm_ref.dtype)


def _mlstm(proj, gate_col, gate_b, conv_w, conv_b, out_g):
    s = proj.shape[0]
    L = MLSTM_CHUNK
    qkw = MLSTM_HEADS * MLSTM_QK
    vw = MLSTM_HEADS * MLSTM_V
    gate_blk = gate_col // LANES
    return pl.pallas_call(
        _mlstm_kernel,
        grid=(s // L,),
        in_specs=[pl.BlockSpec((L, qkw), lambda c: (c, 0)),
                  pl.BlockSpec((L, qkw), lambda c: (c, 1)),
                  pl.BlockSpec((L, vw), lambda c: (c, 1)),
                  pl.BlockSpec((L, vw), lambda c: (c, 2)),
                  pl.BlockSpec((L, LANES), lambda c: (c, gate_blk)),
                  pl.BlockSpec((1, LANES), lambda c: (0, 0)),
                  pl.BlockSpec((MLSTM_CONV, 2 * qkw), lambda c: (0, 0)),
                  pl.BlockSpec((1, 2 * qkw), lambda c: (0, 0)),
                  pl.BlockSpec((1, vw), lambda c: (0, 0))],
        out_specs=pl.BlockSpec((L, vw), lambda c: (c, 0)),
        out_shape=jax.ShapeDtypeStruct((s, vw), BF16),
        scratch_shapes=[pltpu.VMEM((L + SUBLANES, qkw), F32),
                        pltpu.VMEM((L + SUBLANES, qkw), F32),
                        pltpu.VMEM((MLSTM_HEADS, MLSTM_QK, MLSTM_V), F32),
                        pltpu.VMEM((MLSTM_HEADS, SUBLANES, LANES), F32),
                        pltpu.VMEM((MLSTM_HEADS, SUBLANES, LANES), F32)],
        compiler_params=_params("arbitrary"),
        name="mlstm",
    )(proj, proj, proj, proj, proj, gate_b, conv_w, conv_b, out_g)


def _moba_prep_kernel(q_ref, k_ref, v_ref, pos_ref, qg_ref, kg_ref, qt_ref, kn_ref, vt_ref, km_ref):
    rows = q_ref.shape[0]
    half = ROPE_DIM // 2
    lane = lax.broadcasted_iota(I32, (1, MOBA_HD), 1)
    inv_freq = jnp.exp((lane & (half - 1)).astype(F32) * (-(2.0 / ROPE_DIM) * math.log(ROPE_THETA)))
    ang = pos_ref[...].astype(F32) * inv_freq
    cos = jnp.where(lane < ROPE_DIM, jnp.cos(ang), 1.0)
    sin = jnp.sin(ang)
    sin = jnp.where(lane < half, -sin, jnp.where(lane < ROPE_DIM, sin, 0.0))

    def rope(x):
        partner = jnp.where(lane < half, pltpu.roll(x, MOBA_HD - half, 1), pltpu.roll(x, half, 1))
        return x * cos + partner * sin

    for h in range(MOBA_HEADS):
        hs = slice(h * MOBA_HD, (h + 1) * MOBA_HD)
        qt_ref[hs, :] = rope(_rms(q_ref[:, hs], qg_ref[...])).T
        kn = rope(_rms(k_ref[:, hs], kg_ref[...]))
        kn_ref[:, hs] = kn.astype(BF16)
        km_ref[0, :, hs] = jnp.sum(kn, axis=0, keepdims=True) * (1.0 / rows)
        vlo = h * MOBA_VT_ROWS
        vt_ref[0, vlo:vlo + MOBA_HD, :] = v_ref[:, hs].T.astype(BF16)
        vt_ref[0, vlo + MOBA_HD:vlo + MOBA_VT_ROWS, :] = jnp.ones((MOBA_VT_ROWS - MOBA_HD, rows), BF16)


def _moba_prep(proj, q_col, pos, q_g, k_g):
    s = proj.shape[0]
    w = MOBA_HEADS * MOBA_HD
    bs = MOBA_BLOCK
    nb = s // bs
    first = q_col // w
    return pl.pallas_call(
        _moba_prep_kernel,
        grid=(nb,),
        in_specs=[pl.BlockSpec((bs, w), lambda i: (i, first)),
                  pl.BlockSpec((bs, w), lambda i: (i, first + 1)),
                  pl.BlockSpec((bs, w), lambda i: (i, first + 2)),
                  pl.BlockSpec((bs, 1), lambda i: (i, 0)),
                  pl.BlockSpec((1, MOBA_HD), lambda i: (0, 0)),
                  pl.BlockSpec((1, MOBA_HD), lambda i: (0, 0))],
        out_specs=[pl.BlockSpec((w, bs), lambda i: (0, i)),
                   pl.BlockSpec((bs, w), lambda i: (i, 0)),
                   pl.BlockSpec((1, MOBA_HEADS * MOBA_VT_ROWS, bs), lambda i: (i, 0, 0)),
                   pl.BlockSpec((1, 1, w), lambda i: (i, 0, 0))],
        out_shape=[jax.ShapeDtypeStruct((w, s), F32),
                   jax.ShapeDtypeStruct((s, w), BF16),
                   jax.ShapeDtypeStruct((nb, MOBA_HEADS * MOBA_VT_ROWS, bs), BF16),
                   jax.ShapeDtypeStruct((nb, 1, w), F32)],
        compiler_params=_params("parallel"),
        name="moba_prep",
    )(proj, proj, proj, pos, q_g, k_g)


def _moba_attn_kernel(qt_ref, k_ref, vt_ref, km_ref, o_ref, bias_scr, qb_scr, m_scr, *tiles):
    i = pl.program_id(1)
    bs = MOBA_BLOCK
    hd = MOBA_HD
    nh = len(tiles) // 3
    vr = MOBA_VT_ROWS
    acc_scr = tiles[:nh]
    s_scr = tiles[nh:2 * nh]
    p_scr = tiles[2 * nh:]
    nb = km_ref.shape[0]
    c2 = (hd ** -0.5) * math.log2(math.e)
    blk = lax.broadcasted_iota(I32, (nb, bs), 0)
    blk_f = blk.astype(F32)
    valid = blk < i
    key = lax.broadcasted_iota(I32, (bs, bs), 0)
    qry = lax.broadcasted_iota(I32, (bs, bs), 1)
    own = pl.ds(pl.multiple_of(i * bs, bs), bs)

    for h in range(nh):
        hs = slice(h * hd, (h + 1) * hd)
        qt = qt_ref[hs, :]
        gate = jnp.dot(km_ref[:, 0, hs], qt, precision=lax.Precision.HIGHEST,
                       preferred_element_type=F32)
        cand = valid
        for _ in range(MOBA_TOPK):
            gmax = jnp.max(jnp.where(cand, gate, -jnp.inf), axis=0, keepdims=True)
            hit = cand & (gate == gmax)
            first = jnp.min(jnp.where(hit, blk_f, float(nb)), axis=0, keepdims=True)
            cand = cand & jnp.logical_not(hit & (blk_f == first))
        bias_scr[h] = jnp.where(valid & jnp.logical_not(cand), 0.0, -jnp.inf)

        qb = (qt * c2).astype(BF16)
        qb_scr[h] = qb
        s0 = jnp.where(key <= qry, _dot(k_ref[own, hs], qb), -jnp.inf)
        m0 = jnp.max(s0, axis=0, keepdims=True)
        p0 = jnp.exp2(s0 - m0)
        m_scr[h] = m0
        acc_scr[h][...] = _dot(vt_ref[i, h * vr:(h + 1) * vr, :], p0.astype(BF16))

    ck = 64

    def body(j, carry):
        rows = pl.ds(pl.multiple_of(j * bs, bs), bs)
        for h in range(nh):
            s_scr[h][...] = _dot(k_ref[rows, h * hd:(h + 1) * hd], qb_scr[h])
        alphas = []
        for h in range(nh):
            bias = bias_scr[h, pl.ds(j, 1), :]
            cmax = s_scr[h][0:ck, :]
            for c in range(1, bs // ck):
                cmax = jnp.maximum(cmax, s_scr[h][c * ck:(c + 1) * ck, :])
            m = m_scr[h]
            m_new = jnp.maximum(m, jnp.max(cmax, axis=0, keepdims=True) + bias)
            alpha = jnp.exp2(m - m_new)
            shift = m_new - bias
            for c in range(bs // ck):
                p = jnp.exp2(s_scr[h][c * ck:(c + 1) * ck, :] - shift)
                p_scr[h][c * ck:(c + 1) * ck, :] = p.astype(BF16)
            m_scr[h] = m_new
            alphas.append(alpha)
        for h in range(nh):
            acc_scr[h][...] = (alphas[h] * acc_scr[h][...]
                               + _dot(vt_ref[j, h * vr:(h + 1) * vr, :], p_scr[h][...]))
        return carry

    lax.fori_loop(0, i, body, 0)
    for h in range(nh):
        acc = acc_scr[h][...]
        out_t = acc[0:hd] * (1.0 / acc[hd:hd + 1])
        o_ref[:, h * hd:(h + 1) * hd] = out_t.T.astype(o_ref.dtype)


MOBA_HEADS_PER_STEP = 8


def _moba_attn(qt, kn, vt, kmean):
    w, s = qt.shape
    bs = MOBA_BLOCK
    nb = s // bs
    nh = MOBA_HEADS_PER_STEP
    gw = nh * MOBA_HD
    resident = pl.Buffered(1)
    return pl.pallas_call(
        _moba_attn_kernel,
        grid=(MOBA_HEADS // nh, nb),
        in_specs=[pl.BlockSpec((gw, bs), lambda g, i: (g, i)),
                  pl.BlockSpec((s, gw), lambda g, i: (0, g), pipeline_mode=resident),
                  pl.BlockSpec((nb, nh * MOBA_VT_ROWS, bs), lambda g, i: (0, g, 0),
                               pipeline_mode=resident),
                  pl.BlockSpec((nb, 1, gw), lambda g, i: (0, 0, g))],
        out_specs=pl.BlockSpec((bs, gw), lambda g, i: (i, g)),
        out_shape=jax.ShapeDtypeStruct((s, w), BF16),
        scratch_shapes=[pltpu.VMEM((nh, nb, bs), F32),
                        pltpu.VMEM((nh, MOBA_HD, bs), BF16),
                        pltpu.VMEM((nh, 1, bs), F32)]
                       + [pltpu.VMEM((MOBA_VT_ROWS, bs), F32) for _ in range(nh)]
                       + [pltpu.VMEM((bs, bs), F32) for _ in range(nh)]
                       + [pltpu.VMEM((bs, bs), BF16) for _ in range(nh)],
        compiler_params=_params("parallel", "arbitrary"),
        name="moba_attn",
    )(qt, kn, vt, kmean)


def _mix_out_kernel(hm_ref, ha_ref, w_ref, x_ref, o_ref, wb_ref):
    half = hm_ref.shape[1]

    @pl.when(pl.program_id(0) == 0)
    def _():
        wb_ref[...] = w_ref[...].astype(BF16)

    o_ref[...] = (x_ref[...] + _dot(hm_ref[...], wb_ref[0:half, :])
                  + _dot(ha_ref[...], wb_ref[half:2 * half, :]))


def _mix_out(hm, ha, w, x, tm):
    s, d = x.shape
    half = hm.shape[1]
    return pl.pallas_call(
        _mix_out_kernel,
        grid=(s // tm,),
        in_specs=[pl.BlockSpec((tm, half), lambda i: (i, 0)),
                  pl.BlockSpec((tm, half), lambda i: (i, 0)),
                  pl.BlockSpec((2 * half, d), lambda i: (0, 0), pipeline_mode=pl.Buffered(1)),
                  pl.BlockSpec((tm, d), lambda i: (i, 0))],
        out_specs=pl.BlockSpec((tm, d), lambda i: (i, 0)),
        out_shape=jax.ShapeDtypeStruct((s, d), F32),
        scratch_shapes=[pltpu.VMEM((2 * half, d), BF16)],
        compiler_params=_params("arbitrary"),
        name="mix_out",
    )(hm, ha, w, x)


def _mem_kv_kernel(mem_ref, g_ref, w_ref, kg_ref, o_ref, mn_ref):
    j = pl.program_id(0)

    @pl.when(j == 0)
    def _():
        mn_ref[...] = _rms(mem_ref[...], g_ref[...]).astype(BF16)

    y = _dot(mn_ref[...], w_ref[...].astype(BF16))
    o_ref[...] = jnp.where(j < XA_HEADS, _rms(y, kg_ref[...]), y).astype(o_ref.dtype)


def _mem_kv(mem, g, wkv, k_g):
    m, d = mem.shape
    hd = d // XA_HEADS
    return pl.pallas_call(
        _mem_kv_kernel,
        grid=(2 * XA_HEADS,),
        in_specs=[pl.BlockSpec((m, d), lambda j: (0, 0)),
                  pl.BlockSpec((1, d), lambda j: (0, 0)),
                  pl.BlockSpec((d, hd), lambda j: (0, j)),
                  pl.BlockSpec((1, hd), lambda j: (0, 0))],
        out_specs=pl.BlockSpec((m, hd), lambda j: (0, j)),
        out_shape=jax.ShapeDtypeStruct((m, 2 * d), BF16),
        scratch_shapes=[pltpu.VMEM((m, d), BF16)],
        compiler_params=_params("arbitrary"),
        name="mem_kv",
    )(mem, g, wkv, k_g)


def _xattn_kernel(h_ref, g_ref, wq_hbm, kv_ref, qg_ref, wo_hbm, rg_ref, rw_ref, rb_ref,
                  o_ref, routed_ref, cnt_ref, wqb, wob, stage, o_all, carry, sem):
    d = h_ref.shape[1]
    hd = d // XA_HEADS
    n_slab = 2 * XA_HEADS

    @pl.when(pl.program_id(0) == 0)
    def _():
        def slab_copy(k):
            src = wq_hbm if k < XA_HEADS else wo_hbm
            cols = pl.ds((k % XA_HEADS) * hd, hd)
            return pltpu.make_async_copy(src.at[:, cols], stage.at[k % 2], sem.at[k % 2])

        slab_copy(0).start()
        for k in range(n_slab):
            if k + 1 < n_slab:
                slab_copy(k + 1).start()
            slab_copy(k).wait()
            dst = wqb if k < XA_HEADS else wob
            dst[:, (k % XA_HEADS) * hd:(k % XA_HEADS + 1) * hd] = stage[k % 2].astype(BF16)

    h = h_ref[...]
    q_all = _dot(_rms(h, g_ref[...]).astype(BF16), wqb[...])
    for j in range(XA_HEADS):
        hs = slice(j * hd, (j + 1) * hd)
        q = _rms(q_all[:, hs], qg_ref[...]).astype(BF16)
        sc = _dot_nt(q, kv_ref[:, hs]) * (hd ** -0.5)
        p = jnp.exp(sc - jnp.max(sc, axis=-1, keepdims=True))
        p = p * (1.0 / jnp.sum(p, axis=-1, keepdims=True))
        o_all[:, hs] = _dot(p.astype(BF16), kv_ref[:, d + j * hd:d + (j + 1) * hd]).astype(BF16)
    h_out = h + _dot(o_all[...], wob[...])
    o_ref[...] = h_out
    _route_rows(h_out, rg_ref, rw_ref, rb_ref, routed_ref, cnt_ref, carry)


def _xattn_route(h, g, wq, kv, q_g, wo, route_g, route_w, route_b, tm):
    s, d = h.shape
    m = kv.shape[0]
    hd = d // XA_HEADS
    return pl.pallas_call(
        _xattn_kernel,
        grid=(s // tm,),
        in_specs=[pl.BlockSpec((tm, d), lambda i: (i, 0)),
                  pl.BlockSpec((1, d), lambda i: (0, 0)),
                  pl.BlockSpec(memory_space=pl.ANY),
                  pl.BlockSpec((m, 2 * d), lambda i: (0, 0)),
                  pl.BlockSpec((1, hd), lambda i: (0, 0)),
                  pl.BlockSpec(memory_space=pl.ANY),
                  pl.BlockSpec((1, d), lambda i: (0, 0)),
                  pl.BlockSpec((2, d, LANES), lambda i: (0, 0, 0)),
                  pl.BlockSpec((1, LANES), lambda i: (0, 0))],
        out_specs=[pl.BlockSpec((tm, d), lambda i: (i, 0)),
                   pl.BlockSpec((tm, LANES), lambda i: (i, 0)),
                   pl.BlockSpec((SUBLANES, LANES), lambda i: (0, 0))],
        out_shape=[jax.ShapeDtypeStruct((s, d), F32),
                   jax.ShapeDtypeStruct((s, LANES), F32),
                   jax.ShapeDtypeStruct((SUBLANES, LANES), F32)],
        scratch_shapes=[pltpu.VMEM((d, d), BF16), pltpu.VMEM((d, d), BF16),
                        pltpu.VMEM((2, d, hd), F32), pltpu.VMEM((tm, d), BF16),
                        pltpu.VMEM((SUBLANES, LANES), F32),
                        pltpu.SemaphoreType.DMA((2,))],
        compiler_params=_params("arbitrary"),
        name="xattn_route",
    )(h, g, wq, kv, q_g, wo, route_g, route_w, route_b)


ROUTE_EID = 0
ROUTE_RANK = 2
ROUTE_WT = 4


def _route_rows(h, g_ref, w_ref, b_ref, out_ref, cnt_ref, carry):
    tm = h.shape[0]

    @pl.when(pl.program_id(0) == 0)
    def _():
        carry[...] = jnp.zeros_like(carry)

    hn = _rms(h, g_ref[...])
    hi = hn.astype(BF16)
    lo = (hn - hi.astype(F32)).astype(BF16)
    logits = (_dot(hi, w_ref[0]) + _dot(hi, w_ref[1]) + _dot(lo, w_ref[0])
              + b_ref[...])
    lane = lax.broadcasted_iota(I32, (tm, LANES), 1)

    def first_lane(mask):
        return jnp.min(jnp.where(mask, lane.astype(F32), float(LANES)), axis=-1,
                       keepdims=True).astype(I32)

    is_g = lane < MOE_GROUPS
    gmax = jnp.max(jnp.where(is_g, logits, -jnp.inf), axis=-1, keepdims=True)
    gsum = jnp.sum(jnp.where(is_g, jnp.exp(logits - gmax), 0.0), axis=-1, keepdims=True)
    gsel = first_lane(is_g & (logits == gmax))
    pg = 1.0 / gsum

    grp_lo = MOE_GROUPS + MOE_PER_GROUP * gsel
    in_grp = (lane >= grp_lo) & (lane < grp_lo + MOE_PER_GROUP)
    emax = jnp.max(jnp.where(in_grp, logits, -jnp.inf), axis=-1, keepdims=True)
    eexp = jnp.where(in_grp, jnp.exp(logits - emax), 0.0)
    eprob = eexp / jnp.sum(eexp, axis=-1, keepdims=True)
    p1 = jnp.max(jnp.where(in_grp, eprob, -1.0), axis=-1, keepdims=True)
    l1 = first_lane(in_grp & (eprob == p1))
    rest = in_grp & (lane != l1)
    p2 = jnp.max(jnp.where(rest, eprob, -1.0), axis=-1, keepdims=True)
    l2 = first_lane(rest & (eprob == p2))
    psum = p1 + p2
    w1 = pg * p1 / psum
    w2 = pg * p2 / psum

    oh1 = lane == l1
    oh2 = lane == l2
    onehot = jnp.where(oh1 | oh2, 1.0, 0.0)
    r = lax.broadcasted_iota(I32, (tm, tm), 0)
    c = lax.broadcasted_iota(I32, (tm, tm), 1)
    strict = jnp.where(c < r, 1.0, 0.0).astype(BF16)
    before = _dot(strict, onehot.astype(BF16)) + carry[0:1, :]
    rank1 = jnp.sum(jnp.where(oh1, before, 0.0), axis=-1, keepdims=True)
    rank2 = jnp.sum(jnp.where(oh2, before, 0.0), axis=-1, keepdims=True)
    total = carry[0:1, :] + jnp.sum(onehot, axis=0, keepdims=True)
    carry[...] = jnp.broadcast_to(total, carry.shape)
    cnt_ref[...] = jnp.broadcast_to(total, cnt_ref.shape)

    out = jnp.where(lane == ROUTE_EID, (l1 - MOE_GROUPS).astype(F32), 0.0)
    out = jnp.where(lane == ROUTE_EID + 1, (l2 - MOE_GROUPS).astype(F32), out)
    out = jnp.where(lane == ROUTE_RANK, rank1, out)
    out = jnp.where(lane == ROUTE_RANK + 1, rank2, out)
    out = jnp.where(lane == ROUTE_WT, w1, out)
    out = jnp.where(lane == ROUTE_WT + 1, w2, out)
    out_ref[...] = out


def _row_tiles(x):
    return x.astype(BF16).reshape(x.shape[0], x.shape[1] // LANES, LANES)


def _from_row_tiles(x3):
    return x3.reshape(x3.shape[0], x3.shape[1] * LANES)


def _dispatch_kernel(d0_ref, d1_ref, zb_ref, h_ref, g_ref, xg_ref, buf, zero, sem, zsem):
    tm = h_ref.shape[0]
    base = pl.program_id(0) * tm

    @pl.when(pl.program_id(0) == 0)
    def _():
        zero[...] = jnp.zeros_like(zero)
        n_blk = xg_ref.shape[0] // MOE_ROWS

        def zero_copy(b):
            rows = pl.ds(pl.multiple_of(b * MOE_ROWS, MOE_ROWS), MOE_ROWS)
            return pltpu.make_async_copy(zero, xg_ref.at[rows], zsem)

        def start(b, carry):
            @pl.when(zb_ref[b] != 0)
            def _():
                zero_copy(b).start()
            return carry

        def finish(b, carry):
            @pl.when(zb_ref[b] != 0)
            def _():
                zero_copy(0).wait()
            return carry

        lax.fori_loop(0, n_blk, start, 0)
        lax.fori_loop(0, n_blk, finish, 0)

    step = pl.program_id(0)
    slot = step % 2
    buf[slot] = _row_tiles(_rms(h_ref[...], g_ref[...]))

    def row_copy(t, d):
        return pltpu.make_async_copy(buf.at[slot, pl.ds(t, 1)], xg_ref.at[pl.ds(d, 1)], sem.at[slot])

    def issue(t, carry):
        row_copy(t, d0_ref[base + t]).start()
        row_copy(t, d1_ref[base + t]).start()
        return carry

    lax.fori_loop(0, tm, issue, 0, unroll=ROW_DMA_UNROLL)

    def drain(s):
        whole = pltpu.make_async_copy(buf.at[s], xg_ref.at[pl.ds(0, tm)], sem.at[s])
        whole.wait()
        whole.wait()

    @pl.when(step > 0)
    def _():
        drain(1 - slot)

    @pl.when(step == pl.num_programs(0) - 1)
    def _():
        drain(slot)


def _dispatch(dest, zero_blk, h, g, tm):
    s, d = h.shape
    n_rows = zero_blk.shape[0] * MOE_ROWS
    grid_spec = pltpu.PrefetchScalarGridSpec(
        num_scalar_prefetch=3,
        grid=(s // tm,),
        in_specs=[pl.BlockSpec((tm, d), lambda i, d0, d1, zb: (i, 0)),
                  pl.BlockSpec((1, d), lambda i, d0, d1, zb: (0, 0))],
        out_specs=pl.BlockSpec(memory_space=pl.ANY),
        scratch_shapes=[pltpu.VMEM((2, tm, d // LANES, LANES), BF16),
                        pltpu.VMEM((MOE_ROWS, d // LANES, LANES), BF16),
                        pltpu.SemaphoreType.DMA((2,)), pltpu.SemaphoreType.DMA(())],
    )
    return pl.pallas_call(
        _dispatch_kernel,
        grid_spec=grid_spec,
        out_shape=jax.ShapeDtypeStruct((n_rows, d // LANES, LANES), BF16),
        compiler_params=_params("arbitrary"),
        name="moe_dispatch",
    )(dest[0], dest[1], zero_blk, h, g)


def _moe_ffn_kernel(be_ref, nu_ref, nx_ref, x_ref, wg_hbm, wu_hbm, wd_hbm, y_ref,
                    wgf, wuf, wdf, wgb, wub, wdb, slot_ref, sem):
    b = pl.program_id(0)
    e = be_ref[b]
    used = b < nu_ref[0]

    def weight_copies(expert, slot):
        return (pltpu.make_async_copy(wg_hbm.at[expert], wgf.at[slot], sem.at[slot, 0]),
                pltpu.make_async_copy(wu_hbm.at[expert], wuf.at[slot], sem.at[slot, 1]),
                pltpu.make_async_copy(wd_hbm.at[expert], wdf.at[slot], sem.at[slot, 2]))

    @pl.when(b == 0)
    def _():
        slot_ref[0] = 0
        for c in weight_copies(e, 0):
            c.start()

    @pl.when(used & ((b == 0) | (e != be_ref[jnp.maximum(b - 1, 0)])))
    def _():
        slot = slot_ref[0]
        for c in weight_copies(e, slot):
            c.wait()
        nxt = nx_ref[e]

        @pl.when(nxt >= 0)
        def _():
            for c in weight_copies(nxt, 1 - slot):
                c.start()

        wgb[...] = wgf[slot].astype(BF16)
        wub[...] = wuf[slot].astype(BF16)
        wdb[...] = wdf[slot].astype(BF16)
        slot_ref[0] = 1 - slot

    @pl.when(used)
    def _():
        xb = _from_row_tiles(x_ref[...])
        gate = _dot(xb, wgb[...])
        up = _dot(xb, wub[...])
        act = (gate * jax.nn.sigmoid(gate) * up).astype(BF16)
        y_ref[...] = _dot(act, wdb[...])

    @pl.when(jnp.logical_not(used))
    def _():
        y_ref[...] = jnp.zeros_like(y_ref)


def _moe_ffn(blk_e, n_used, next_e, xg, w_gate, w_up, w_down):
    n_rows, nt, _ = xg.shape
    d = nt * LANES
    ff = w_gate.shape[2]
    n_blk = n_rows // MOE_ROWS
    def row_blk(b, be, nu, nx):
        return jnp.minimum(b, nu[0] - 1), 0, 0

    grid_spec = pltpu.PrefetchScalarGridSpec(
        num_scalar_prefetch=3,
        grid=(n_blk,),
        in_specs=[pl.BlockSpec((MOE_ROWS, nt, LANES), row_blk),
                  pl.BlockSpec(memory_space=pl.ANY),
                  pl.BlockSpec(memory_space=pl.ANY),
                  pl.BlockSpec(memory_space=pl.ANY)],
        out_specs=pl.BlockSpec((MOE_ROWS, d), lambda b, be, nu, nx: (b, 0)),
        scratch_shapes=[pltpu.VMEM((2, d, ff), F32), pltpu.VMEM((2, d, ff), F32),
                        pltpu.VMEM((2, ff, d), F32),
                        pltpu.VMEM((d, ff), BF16), pltpu.VMEM((d, ff), BF16),
                        pltpu.VMEM((ff, d), BF16),
                        pltpu.SMEM((1,), I32), pltpu.SemaphoreType.DMA((2, 3))],
    )
    return pl.pallas_call(
        _moe_ffn_kernel,
        grid_spec=grid_spec,
        out_shape=jax.ShapeDtypeStruct((n_rows, d), F32),
        compiler_params=_params("arbitrary"),
        name="moe_ffn",
    )(blk_e, n_used, next_e, xg, w_gate, w_up, w_down)


def _combine_kernel(d0_ref, d1_ref, h_ref, r_ref, y_ref, o_ref, buf, sem):
    tm = h_ref.shape[0]
    step = pl.program_id(0)
    slot = step % 2

    def gather(tile, s):
        base = tile * tm

        def issue(t, carry):
            for k, dk in enumerate((d0_ref, d1_ref)):
                pltpu.make_async_copy(y_ref.at[pl.ds(dk[base + t], 1), :],
                                      buf.at[s, k, pl.ds(t, 1), :], sem.at[s]).start()
            return carry

        lax.fori_loop(0, tm, issue, 0, unroll=ROW_DMA_UNROLL)

    @pl.when(step == 0)
    def _():
        gather(step, slot)

    @pl.when(step + 1 < pl.num_programs(0))
    def _():
        gather(step + 1, 1 - slot)

    for k in range(2):
        pltpu.make_async_copy(y_ref.at[pl.ds(0, tm), :], buf.at[slot, k], sem.at[slot]).wait()
    r = r_ref[...]
    o_ref[...] = (h_ref[...] + r[:, ROUTE_WT:ROUTE_WT + 1] * buf[slot, 0]
                  + r[:, ROUTE_WT + 1:ROUTE_WT + 2] * buf[slot, 1])


def _combine(dest, h, routed, y, tm):
    s, d = h.shape
    grid_spec = pltpu.PrefetchScalarGridSpec(
        num_scalar_prefetch=2,
        grid=(s // tm,),
        in_specs=[pl.BlockSpec((tm, d), lambda i, d0, d1: (i, 0)),
                  pl.BlockSpec((tm, LANES), lambda i, d0, d1: (i, 0)),
                  pl.BlockSpec(memory_space=pl.ANY)],
        out_specs=pl.BlockSpec((tm, d), lambda i, d0, d1: (i, 0)),
        scratch_shapes=[pltpu.VMEM((2, 2, tm, d), F32), pltpu.SemaphoreType.DMA((2,))],
    )
    return pl.pallas_call(
        _combine_kernel,
        grid_spec=grid_spec,
        out_shape=jax.ShapeDtypeStruct((s, d), F32),
        compiler_params=_params("arbitrary"),
        name="moe_combine",
    )(dest[0], dest[1], h, routed, y)


def _layer(h, mem, pos, norm_mix_g, w_in, gate_b, conv_w, conv_b, out_g, moba_q_g, moba_k_g, w_out,
           norm_cross_g, norm_mem_g, xa_wq, xa_wkv, xa_q_g, xa_k_g, xa_wo, norm_ffn_g,
           router_group_w, router_group_b, router_expert_w, router_expert_b,
           exp_w_gate, exp_w_up, exp_w_down):
    s, d = h.shape
    row = lambda v: v.reshape(1, -1)

    n_gate = 2 * MLSTM_HEADS
    gate_lo = 2 * MLSTM_HEADS * MLSTM_QK + 2 * MLSTM_HEADS * MLSTM_V
    w_t = w_in.T
    w_a = w_t[:gate_lo].astype(BF16)
    w_b = w_t[gate_lo + n_gate:].astype(BF16)
    w_g = jnp.pad(w_t[gate_lo:gate_lo + n_gate], ((0, PROJ_COL_TILE - n_gate), (0, 0))).astype(BF16)
    proj = _proj_in(h, row(norm_mix_g), w_a, w_b, w_g, tm=min(s, PROJ_ROW_TILE))

    gate_b_row = jnp.pad(gate_b, (0, LANES - n_gate)).reshape(1, LANES)
    hm = _mlstm(proj, w_a.shape[0] + w_b.shape[0], gate_b_row, conv_w, row(conv_b), row(out_g))
    qn, kn, vb, kmean = _moba_prep(proj, gate_lo, pos, row(moba_q_g), row(moba_k_g))
    ha = _moba_attn(qn, kn, vb, kmean)
    h = _mix_out(hm, ha, w_out, h, tm=MIX_ROW_TILE)

    kv = _mem_kv(mem, row(norm_mem_g), xa_wkv, row(xa_k_g))

    w_route = jnp.concatenate(
        [router_group_w, router_expert_w,
         jnp.zeros((d, LANES - MOE_GROUPS - MOE_EXPERTS), router_group_w.dtype)], axis=1)
    w_route_hi = w_route.astype(BF16)
    w_route = jnp.stack([w_route_hi, (w_route - w_route_hi.astype(F32)).astype(BF16)])
    b_route = jnp.pad(jnp.concatenate([router_group_b, router_expert_b]),
                      (0, LANES - MOE_GROUPS - MOE_EXPERTS)).reshape(1, LANES)
    h, routed, counts = _xattn_route(h, row(norm_cross_g), xa_wq, kv, row(xa_q_g), xa_wo,
                                     row(norm_ffn_g), w_route, b_route, tm=XATTN_ROW_TILE)

    cnt = counts[0, MOE_GROUPS:MOE_GROUPS + MOE_EXPERTS].astype(I32)
    padded = (cnt + MOE_ROWS - 1) // MOE_ROWS * MOE_ROWS
    pend = jnp.cumsum(padded)
    seg_start = (pend - padded).astype(F32)[None, :]
    expert_ids = jnp.arange(MOE_EXPERTS, dtype=F32)[None, :]

    def dest_of(k):
        mine = routed[:, ROUTE_EID + k:ROUTE_EID + k + 1] == expert_ids
        return (jnp.sum(jnp.where(mine, seg_start, 0.0), axis=1) + routed[:, ROUTE_RANK + k]).astype(I32)

    dest = (dest_of(0), dest_of(1))
    n_blk = -(-2 * s // MOE_ROWS) + MOE_EXPERTS
    blk_start = jnp.arange(n_blk, dtype=I32) * MOE_ROWS
    blk_e = jnp.minimum(jnp.sum((pend[None, :] <= blk_start[:, None]).astype(I32), axis=1),
                        MOE_EXPERTS - 1)
    n_used = (pend[-1:] // MOE_ROWS).astype(I32)
    ids = jnp.arange(MOE_EXPERTS, dtype=I32)
    later_used = (ids[None, :] > ids[:, None]) & (cnt[None, :] > 0)
    next_e = jnp.min(jnp.where(later_used, ids[None, :], MOE_EXPERTS), axis=1)
    next_e = jnp.where(next_e < MOE_EXPERTS, next_e, -1).astype(I32)

    is_blk_e = blk_e[:, None] == ids[None, :]
    seg_end = jnp.sum(jnp.where(is_blk_e, (pend - padded + cnt)[None, :], 0), axis=1)
    zero_blk = (blk_start + MOE_ROWS > seg_end).astype(I32)
    xg = _dispatch(dest, zero_blk, h, row(norm_ffn_g), tm=MOE_TOKEN_TILE)
    y = _moe_ffn(blk_e, n_used, next_e, xg, exp_w_gate, exp_w_up, exp_w_down)
    return _combine(dest, h, routed, y, tm=MOE_TOKEN_TILE)


def kernel(x, mem, positions, norm_mix_g, w_in, mlstm_gate_b, mlstm_conv_w, mlstm_conv_b, mlstm_out_g, moba_q_g, moba_k_g, w_out, norm_cross_g, norm_mem_g, xa_wq, xa_wkv, xa_q_g, xa_k_g, xa_wo, norm_ffn_g, router_group_w, router_group_b, router_expert_w, router_expert_b, exp_w_gate, exp_w_up, exp_w_down):
    bsz, s, _ = x.shape
    assert bsz == 1, "single-sequence prefill only"
    per_layer = (norm_mix_g, w_in, mlstm_gate_b, mlstm_conv_w, mlstm_conv_b, mlstm_out_g, moba_q_g,
                 moba_k_g, w_out, norm_cross_g, norm_mem_g, xa_wq, xa_wkv, xa_q_g, xa_k_g, xa_wo,
                 norm_ffn_g, router_group_w, router_group_b, router_expert_w, router_expert_b,
                 exp_w_gate, exp_w_up, exp_w_down)
    h = x[0]
    pos = positions.reshape(s, 1)
    for l in range(norm_mix_g.shape[0]):
        h = _layer(h, mem[0], pos, *(p[l] for p in per_layer))
    return h[None]
```

```python
import functools
import math

import jax
import jax.numpy as jnp
from jax import lax
from jax.experimental import pallas as pl
from jax.experimental.pallas import tpu as pltpu

F32 = jnp.float32
BF16 = jnp.bfloat16
I32 = jnp.int32

EPS = 1e-6
LANES = 128
SUBLANES = 8
VMEM_LIMIT = 56 * 1024 * 1024

MLSTM_HEADS = 4
MLSTM_QK = 128
MLSTM_V = 256
MLSTM_CHUNK = 128
MLSTM_CONV = 4
MOBA_HEADS = 8
MOBA_HD = 128
MOBA_BLOCK = 256
MOBA_TOPK = 3
MOBA_VT_ROWS = MOBA_HD + 16
ROPE_DIM = 32
ROPE_THETA = 500000.0
XA_HEADS = 4
MOE_GROUPS = 8
MOE_PER_GROUP = 8
MOE_EXPERTS = MOE_GROUPS * MOE_PER_GROUP
MOE_ROWS = 128
ROW_DMA_UNROLL = 8
WEIGHT_DMA_SPLIT = 4

PROJ_ROW_TILE = 1024
PROJ_COL_TILE = 512
MIX_ROW_TILE = 512
XATTN_ROW_TILE = 512
MOE_TOKEN_TILE = 256

NT_DIMS = (((1,), (1,)), ((), ()))


def _params(*sem):
    return pltpu.CompilerParams(dimension_semantics=sem, vmem_limit_bytes=VMEM_LIMIT)


def _rms(x, g):
    return x * lax.rsqrt(jnp.mean(x * x, axis=-1, keepdims=True) + EPS) * g


def _dot(a, b):
    return jnp.dot(a, b, preferred_element_type=F32)


def _dot_nt(a, b, precision=None):
    return lax.dot_general(a, b, NT_DIMS, precision=precision, preferred_element_type=F32)


def _proj_in_kernel(x_ref, g_ref, wa_ref, wb_ref, wg_ref, o_ref, xn_ref, *, na, nb):
    j = pl.program_id(1)

    @pl.when(j == 0)
    def _():
        xn_ref[...] = _rms(x_ref[...], g_ref[...]).astype(BF16)

    @pl.when(j < na)
    def _():
        o_ref[...] = _dot_nt(xn_ref[...], wa_ref[...])

    @pl.when((j >= na) & (j < na + nb))
    def _():
        o_ref[...] = _dot_nt(xn_ref[...], wb_ref[...])

    @pl.when(j == na + nb)
    def _():
        o_ref[...] = _dot_nt(xn_ref[...], wg_ref[...])


def _proj_in(x, g, w_a, w_b, w_g, tm):
    s, d = x.shape
    tn = w_g.shape[0]
    na, nb = w_a.shape[0] // tn, w_b.shape[0] // tn
    return pl.pallas_call(
        functools.partial(_proj_in_kernel, na=na, nb=nb),
        grid=(s // tm, na + nb + 1),
        in_specs=[pl.BlockSpec((tm, d), lambda i, j: (i, 0)),
                  pl.BlockSpec((1, d), lambda i, j: (0, 0)),
                  pl.BlockSpec((tn, d), lambda i, j: (jnp.minimum(j, na - 1), 0)),
                  pl.BlockSpec((tn, d), lambda i, j: (jnp.clip(j - na, 0, nb - 1), 0)),
                  pl.BlockSpec((tn, d), lambda i, j: (0, 0))],
        out_specs=pl.BlockSpec((tm, tn), lambda i, j: (i, j)),
        out_shape=jax.ShapeDtypeStruct((s, (na + nb + 1) * tn), F32),
        scratch_shapes=[pltpu.VMEM((tm, d), BF16)],
        compiler_params=_params("parallel", "arbitrary"),
        name="proj_in",
    )(x, g, w_a, w_b, w_g)


def _split3(x):
    hi = x.astype(BF16)
    r = x - hi.astype(F32)
    mid = r.astype(BF16)
    lo = (r - mid.astype(F32)).astype(BF16)
    return hi, mid, lo


def _log_sigmoid(x):
    return jnp.minimum(x, 0.0) - jnp.log(1.0 + jnp.exp(-jnp.abs(x)))


def _mlstm_kernel(q_ref, k_ref, v_ref, og_ref, gt_ref, gb_ref, cw_ref, cb_ref, outg_ref, hm_ref,
                  qext, kext, c_scr, n_scr, m_scr):
    L = MLSTM_CHUNK
    pad = SUBLANES
    qkw = MLSTM_HEADS * MLSTM_QK
    rows_blk = q_ref.shape[0]

    @pl.when(pl.program_id(0) == 0)
    def _():
        qext[0:pad, :] = jnp.zeros((pad, qkw), F32)
        kext[0:pad, :] = jnp.zeros((pad, qkw), F32)
        c_scr[...] = jnp.zeros_like(c_scr)
        n_scr[...] = jnp.zeros_like(n_scr)
        m_scr[...] = jnp.zeros_like(m_scr)

    qext[pad:pad + rows_blk, :] = q_ref[...]
    kext[pad:pad + rows_blk, :] = k_ref[...]

    def conv_silu(ext, lo):
        acc = jnp.broadcast_to(cb_ref[:, lo:lo + qkw], (rows_blk, qkw))
        for j in range(MLSTM_CONV):
            shift = MLSTM_CONV - 1 - j
            acc = acc + cw_ref[j:j + 1, lo:lo + qkw] * ext[pad - shift:pad - shift + rows_blk, :]
        return acc * jax.nn.sigmoid(acc)

    qc_all = conv_silu(qext, 0)
    kc_all = conv_silu(kext, qkw)
    qext[0:pad, :] = qext[rows_blk:rows_blk + pad, :]
    kext[0:pad, :] = kext[rows_blk:rows_blk + pad, :]

    row = lax.broadcasted_iota(I32, (L, L), 0)
    col = lax.broadcasted_iota(I32, (L, L), 1)
    causal = col <= row
    tri_lo = jnp.where(causal, 1.0, 0.0).astype(BF16)
    tri_up = jnp.where(row <= col, 1.0, 0.0).astype(BF16)
    for ck in range(rows_blk // L):
        rs = slice(ck * L, (ck + 1) * L)
        _mlstm_chunk(rs, qc_all[rs], kc_all[rs], v_ref, og_ref, gt_ref, gb_ref, outg_ref, hm_ref,
                     c_scr, n_scr, m_scr, causal, tri_lo, tri_up)


def _mlstm_chunk(rs, qc, kc, v_ref, og_ref, gt_ref, gb_ref, outg_ref, hm_ref, c_scr, n_scr, m_scr,
                 causal, tri_lo, tri_up):
    L = MLSTM_CHUNK
    gts = gt_ref[rs, :] + gb_ref[...]
    gtr = gts.T
    bc_all = sum(_dot(tri_lo, part) for part in _split3(_log_sigmoid(gts)))
    br_all = sum(_dot(part, tri_up) for part in _split3(_log_sigmoid(gtr)))

    for h in range(MLSTM_HEADS):
        fi = MLSTM_HEADS + h
        m_prev = m_scr[h, 0:1, 0:1]
        b_c = bc_all[:, fi:fi + 1]
        b_r = br_all[fi:fi + 1, :]
        i_c = gts[:, h:h + 1]
        i_r = gtr[h:h + 1, :]
        g = b_c[L - 1:L, :]
        d = jnp.where(causal, (b_c - b_r) + i_r, -jnp.inf)
        inter = b_c + m_prev
        m_t = jnp.maximum(inter, jnp.max(d, axis=-1, keepdims=True))
        w_inter = jnp.exp(inter - m_t)
        qh = qc[:, h * MLSTM_QK:(h + 1) * MLSTM_QK] * (MLSTM_QK ** -0.5)
        kh = kc[:, h * MLSTM_QK:(h + 1) * MLSTM_QK]
        vb = v_ref[rs, h * MLSTM_V:(h + 1) * MLSTM_V].astype(BF16)
        qb = qh.astype(BF16)
        p = jnp.exp(d - m_t) * _dot_nt(qb, kh.astype(BF16))
        c_prev = c_scr[h]
        n_prev = n_scr[h, 0:1, :]
        num = w_inter * _dot(qb, c_prev.astype(BF16)) + _dot(p.astype(BF16), vb)
        den = (w_inter * jnp.sum(qh * n_prev, axis=-1, keepdims=True)
               + jnp.sum(p, axis=-1, keepdims=True))
        hh = num * (1.0 / jnp.maximum(jnp.abs(den), jnp.exp(-m_t)))

        a = (g - b_c) + i_c
        m_new = jnp.maximum(g + m_prev, jnp.max(a, axis=0, keepdims=True))
        kw = kh * jnp.exp(a - m_new)
        decay = jnp.exp(g + m_prev - m_new)
        c_scr[h] = decay * c_prev + _dot(kw.T.astype(BF16), vb)
        n_scr[h, 0:1, :] = decay * n_prev + jnp.sum(kw, axis=0, keepdims=True)
        m_scr[h] = jnp.broadcast_to(m_new, (SUBLANES, LANES))

        hn = hh * lax.rsqrt(jnp.mean(hh * hh, axis=-1, keepdims=True) + EPS)
        vs = slice(h * MLSTM_V, (h + 1) * MLSTM_V)
        hn = hn * outg_ref[:, vs] * jax.nn.sigmoid(og_ref[rs, vs])
        hm_ref[rs, vs] = hn.astype(hm_ref.dtype)


def _mlstm(proj, gate_col, gate_b, conv_w, conv_b, out_g):
    s = proj.shape[0]
    L = MLSTM_CHUNK
    qkw = MLSTM_HEADS * MLSTM_QK
    vw = MLSTM_HEADS * MLSTM_V
    gate_blk = gate_col // LANES
    return pl.pallas_call(
        _mlstm_kernel,
        grid=(s // L,),
        in_specs=[pl.BlockSpec((L, qkw), lambda c: (c, 0)),
                  pl.BlockSpec((L, qkw), lambda c: (c, 1)),
                  pl.BlockSpec((L, vw), lambda c: (c, 1)),
                  pl.BlockSpec((L, vw), lambda c: (c, 2)),
                  pl.BlockSpec((L, LANES), lambda c: (c, gate_blk)),
                  pl.BlockSpec((1, LANES), lambda c: (0, 0)),
                  pl.BlockSpec((MLSTM_CONV, 2 * qkw), lambda c: (0, 0)),
                  pl.BlockSpec((1, 2 * qkw), lambda c: (0, 0)),
                  pl.BlockSpec((1, vw), lambda c: (0, 0))],
        out_specs=pl.BlockSpec((L, vw), lambda c: (c, 0)),
        out_shape=jax.ShapeDtypeStruct((s, vw), BF16),
        scratch_shapes=[pltpu.VMEM((L + SUBLANES, qkw), F32),
                        pltpu.VMEM((L + SUBLANES, qkw), F32),
                        pltpu.VMEM((MLSTM_HEADS, MLSTM_QK, MLSTM_V), F32),
                        pltpu.VMEM((MLSTM_HEADS, SUBLANES, LANES), F32),
                        pltpu.VMEM((MLSTM_HEADS, SUBLANES, LANES), F32)],
        compiler_params=_params("arbitrary"),
        name="mlstm",
    )(proj, proj, proj, proj, proj, gate_b, conv_w, conv_b, out_g)


def _moba_prep_kernel(q_ref, k_ref, v_ref, pos_ref, qg_ref, kg_ref, qt_ref, kn_ref, vt_ref, km_ref):
    rows = q_ref.shape[0]
    half = ROPE_DIM // 2
    lane = lax.broadcasted_iota(I32, (1, MOBA_HD), 1)
    inv_freq = jnp.exp((lane & (half - 1)).astype(F32) * (-(2.0 / ROPE_DIM) * math.log(ROPE_THETA)))
    ang = pos_ref[...].astype(F32) * inv_freq
    cos = jnp.where(lane < ROPE_DIM, jnp.cos(ang), 1.0)
    sin = jnp.sin(ang)
    sin = jnp.where(lane < half, -sin, jnp.where(lane < ROPE_DIM, sin, 0.0))

    def rope(x):
        partner = jnp.where(lane < half, pltpu.roll(x, MOBA_HD - half, 1), pltpu.roll(x, half, 1))
        return x * cos + partner * sin

    for h in range(MOBA_HEADS):
        hs = slice(h * MOBA_HD, (h + 1) * MOBA_HD)
        qt_ref[hs, :] = rope(_rms(q_ref[:, hs], qg_ref[...])).T
        kn = rope(_rms(k_ref[:, hs], kg_ref[...]))
        kn_ref[:, hs] = kn.astype(BF16)
        km_ref[0, :, hs] = jnp.sum(kn, axis=0, keepdims=True) * (1.0 / rows)
        vlo = h * MOBA_VT_ROWS
        vt_ref[0, vlo:vlo + MOBA_HD, :] = v_ref[:, hs].T.astype(BF16)
        vt_ref[0, vlo + MOBA_HD:vlo + MOBA_VT_ROWS, :] = jnp.ones((MOBA_VT_ROWS - MOBA_HD, rows), BF16)


def _moba_prep(proj, q_col, pos, q_g, k_g):
    s = proj.shape[0]
    w = MOBA_HEADS * MOBA_HD
    bs = MOBA_BLOCK
    nb = s // bs
    first = q_col // w
    return pl.pallas_call(
        _moba_prep_kernel,
        grid=(nb,),
        in_specs=[pl.BlockSpec((bs, w), lambda i: (i, first)),
                  pl.BlockSpec((bs, w), lambda i: (i, first + 1)),
                  pl.BlockSpec((bs, w), lambda i: (i, first + 2)),
                  pl.BlockSpec((bs, 1), lambda i: (i, 0)),
                  pl.BlockSpec((1, MOBA_HD), lambda i: (0, 0)),
                  pl.BlockSpec((1, MOBA_HD), lambda i: (0, 0))],
        out_specs=[pl.BlockSpec((w, bs), lambda i: (0, i)),
                   pl.BlockSpec((bs, w), lambda i: (i, 0)),
                   pl.BlockSpec((1, MOBA_HEADS * MOBA_VT_ROWS, bs), lambda i: (i, 0, 0)),
                   pl.BlockSpec((1, 1, w), lambda i: (i, 0, 0))],
        out_shape=[jax.ShapeDtypeStruct((w, s), F32),
                   jax.ShapeDtypeStruct((s, w), BF16),
                   jax.ShapeDtypeStruct((nb, MOBA_HEADS * MOBA_VT_ROWS, bs), BF16),
                   jax.ShapeDtypeStruct((nb, 1, w), F32)],
        compiler_params=_params("parallel"),
        name="moba_prep",
    )(proj, proj, proj, pos, q_g, k_g)


def _moba_attn_kernel(qt_ref, k_ref, vt_ref, km_ref, o_ref, bias_scr, qb_scr, m_scr, *tiles):
    i = pl.program_id(1)
    bs = MOBA_BLOCK
    hd = MOBA_HD
    nh = len(tiles) // 3
    vr = MOBA_VT_ROWS
    acc_scr = tiles[:nh]
    s_scr = tiles[nh:2 * nh]
    p_scr = tiles[2 * nh:]
    nb = km_ref.shape[0]
    c2 = (hd ** -0.5) * math.log2(math.e)
    blk = lax.broadcasted_iota(I32, (nb, bs), 0)
    blk_f = blk.astype(F32)
    valid = blk < i
    key = lax.broadcasted_iota(I32, (bs, bs), 0)
    qry = lax.broadcasted_iota(I32, (bs, bs), 1)
    own = pl.ds(pl.multiple_of(i * bs, bs), bs)

    for h in range(nh):
        hs = slice(h * hd, (h + 1) * hd)
        qt = qt_ref[hs, :]
        gate = jnp.dot(km_ref[:, 0, hs], qt, precision=lax.Precision.HIGHEST,
                       preferred_element_type=F32)
        cand = valid
        for _ in range(MOBA_TOPK):
            gmax = jnp.max(jnp.where(cand, gate, -jnp.inf), axis=0, keepdims=True)
            hit = cand & (gate == gmax)
            first = jnp.min(jnp.where(hit, blk_f, float(nb)), axis=0, keepdims=True)
            cand = cand & jnp.logical_not(hit & (blk_f == first))
        bias_scr[h] = jnp.where(valid & jnp.logical_not(cand), 0.0, -jnp.inf)

        qb = (qt * c2).astype(BF16)
        qb_scr[h] = qb
        s0 = jnp.where(key <= qry, _dot(k_ref[own, hs], qb), -jnp.inf)
        m0 = jnp.max(s0, axis=0, keepdims=True)
        p0 = jnp.exp2(s0 - m0)
        m_scr[h] = m0
        acc_scr[h][...] = _dot(vt_ref[i, h * vr:(h + 1) * vr, :], p0.astype(BF16))

    ck = 64

    def body(j, carry):
        rows = pl.ds(pl.multiple_of(j * bs, bs), bs)
        for h in range(nh):
            s_scr[h][...] = _dot(k_ref[rows, h * hd:(h + 1) * hd], qb_scr[h])
        alphas = []
        for h in range(nh):
            bias = bias_scr[h, pl.ds(j, 1), :]
            cmax = s_scr[h][0:ck, :]
            for c in range(1, bs // ck):
                cmax = jnp.maximum(cmax, s_scr[h][c * ck:(c + 1) * ck, :])
            m = m_scr[h]
            m_new = jnp.maximum(m, jnp.max(cmax, axis=0, keepdims=True) + bias)
            alpha = jnp.exp2(m - m_new)
            shift = m_new - bias
            for c in range(bs // ck):
                p = jnp.exp2(s_scr[h][c * ck:(c + 1) * ck, :] - shift)
                p_scr[h][c * ck:(c + 1) * ck, :] = p.astype(BF16)
            m_scr[h] = m_new
            alphas.append(alpha)
        for h in range(nh):
            acc_scr[h][...] = (alphas[h] * acc_scr[h][...]
                               + _dot(vt_ref[j, h * vr:(h + 1) * vr, :], p_scr[h][...]))
        return carry

    lax.fori_loop(0, i, body, 0)
    for h in range(nh):
        acc = acc_scr[h][...]
        out_t = acc[0:hd] * (1.0 / acc[hd:hd + 1])
        o_ref[:, h * hd:(h + 1) * hd] = out_t.T.astype(o_ref.dtype)


MOBA_HEADS_PER_STEP = 8


def _moba_attn(qt, kn, vt, kmean):
    w, s = qt.shape
    bs = MOBA_BLOCK
    nb = s // bs
    nh = MOBA_HEADS_PER_STEP
    gw = nh * MOBA_HD
    resident = pl.Buffered(1)
    return pl.pallas_call(
        _moba_attn_kernel,
        grid=(MOBA_HEADS // nh, nb),
        in_specs=[pl.BlockSpec((gw, bs), lambda g, i: (g, i)),
                  pl.BlockSpec((s, gw), lambda g, i: (0, g), pipeline_mode=resident),
                  pl.BlockSpec((nb, nh * MOBA_VT_ROWS, bs), lambda g, i: (0, g, 0),
                               pipeline_mode=resident),
                  pl.BlockSpec((nb, 1, gw), lambda g, i: (0, 0, g))],
        out_specs=pl.BlockSpec((bs, gw), lambda g, i: (i, g)),
        out_shape=jax.ShapeDtypeStruct((s, w), BF16),
        scratch_shapes=[pltpu.VMEM((nh, nb, bs), F32),
                        pltpu.VMEM((nh, MOBA_HD, bs), BF16),
                        pltpu.VMEM((nh, 1, bs), F32)]
                       + [pltpu.VMEM((MOBA_VT_ROWS, bs), F32) for _ in range(nh)]
                       + [pltpu.VMEM((bs, bs), F32) for _ in range(nh)]
                       + [pltpu.VMEM((bs, bs), BF16) for _ in range(nh)],
        compiler_params=_params("parallel", "arbitrary"),
        name="moba_attn",
    )(qt, kn, vt, kmean)


def _mix_out_kernel(hm_ref, ha_ref, w_ref, x_ref, o_ref, wb_ref):
    half = hm_ref.shape[1]

    @pl.when(pl.program_id(0) == 0)
    def _():
        wb_ref[...] = w_ref[...].astype(BF16)

    o_ref[...] = (x_ref[...] + _dot(hm_ref[...], wb_ref[0:half, :])
                  + _dot(ha_ref[...], wb_ref[half:2 * half, :]))


def _mix_out(hm, ha, w, x, tm):
    s, d = x.shape
    half = hm.shape[1]
    return pl.pallas_call(
        _mix_out_kernel,
        grid=(s // tm,),
        in_specs=[pl.BlockSpec((tm, half), lambda i: (i, 0)),
                  pl.BlockSpec((tm, half), lambda i: (i, 0)),
                  pl.BlockSpec((2 * half, d), lambda i: (0, 0), pipeline_mode=pl.Buffered(1)),
                  pl.BlockSpec((tm, d), lambda i: (i, 0))],
        out_specs=pl.BlockSpec((tm, d), lambda i: (i, 0)),
        out_shape=jax.ShapeDtypeStruct((s, d), F32),
        scratch_shapes=[pltpu.VMEM((2 * half, d), BF16)],
        compiler_params=_params("arbitrary"),
        name="mix_out",
    )(hm, ha, w, x)


def _mem_kv_kernel(mem_ref, g_ref, w_ref, kg_ref, o_ref, mn_ref):
    j = pl.program_id(0)

    @pl.when(j == 0)
    def _():
        mn_ref[...] = _rms(mem_ref[...], g_ref[...]).astype(BF16)

    y = _dot(mn_ref[...], w_ref[...].astype(BF16))
    o_ref[...] = jnp.where(j < XA_HEADS, _rms(y, kg_ref[...]), y).astype(o_ref.dtype)


def _mem_kv(mem, g, wkv, k_g):
    m, d = mem.shape
    hd = d // XA_HEADS
    return pl.pallas_call(
        _mem_kv_kernel,
        grid=(2 * XA_HEADS,),
        in_specs=[pl.BlockSpec((m, d), lambda j: (0, 0)),
                  pl.BlockSpec((1, d), lambda j: (0, 0)),
                  pl.BlockSpec((d, hd), lambda j: (0, j)),
                  pl.BlockSpec((1, hd), lambda j: (0, 0))],
        out_specs=pl.BlockSpec((m, hd), lambda j: (0, j)),
        out_shape=jax.ShapeDtypeStruct((m, 2 * d), BF16),
        scratch_shapes=[pltpu.VMEM((m, d), BF16)],
        compiler_params=_params("arbitrary"),
        name="mem_kv",
    )(mem, g, wkv, k_g)


def _xattn_kernel(h_ref, g_ref, wq_hbm, kv_ref, qg_ref, wo_hbm, rg_ref, rw_ref, rb_ref,
                  o_ref, routed_ref, cnt_ref, wqb, wob, stage, o_all, carry, sem):
    d = h_ref.shape[1]
    hd = d // XA_HEADS
    n_slab = 2 * XA_HEADS

    @pl.when(pl.program_id(0) == 0)
    def _():
        def slab_copy(k):
            src = wq_hbm if k < XA_HEADS else wo_hbm
            cols = pl.ds((k % XA_HEADS) * hd, hd)
            return pltpu.make_async_copy(src.at[:, cols], stage.at[k % 2], sem.at[k % 2])

        slab_copy(0).start()
        for k in range(n_slab):
            if k + 1 < n_slab:
                slab_copy(k + 1).start()
            slab_copy(k).wait()
            dst = wqb if k < XA_HEADS else wob
            dst[:, (k % XA_HEADS) * hd:(k % XA_HEADS + 1) * hd] = stage[k % 2].astype(BF16)

    h = h_ref[...]
    q_all = _dot(_rms(h, g_ref[...]).astype(BF16), wqb[...])
    for j in range(XA_HEADS):
        hs = slice(j * hd, (j + 1) * hd)
        q = _rms(q_all[:, hs], qg_ref[...]).astype(BF16)
        sc = _dot_nt(q, kv_ref[:, hs]) * (hd ** -0.5)
        p = jnp.exp(sc - jnp.max(sc, axis=-1, keepdims=True))
        p = p * (1.0 / jnp.sum(p, axis=-1, keepdims=True))
        o_all[:, hs] = _dot(p.astype(BF16), kv_ref[:, d + j * hd:d + (j + 1) * hd]).astype(BF16)
    h_out = h + _dot(o_all[...], wob[...])
    o_ref[...] = h_out
    _route_rows(h_out, rg_ref, rw_ref, rb_ref, routed_ref, cnt_ref, carry)


def _xattn_route(h, g, wq, kv, q_g, wo, route_g, route_w, route_b, tm):
    s, d = h.shape
    m = kv.shape[0]
    hd = d // XA_HEADS
    return pl.pallas_call(
        _xattn_kernel,
        grid=(s // tm,),
        in_specs=[pl.BlockSpec((tm, d), lambda i: (i, 0)),
                  pl.BlockSpec((1, d), lambda i: (0, 0)),
                  pl.BlockSpec(memory_space=pl.ANY),
                  pl.BlockSpec((m, 2 * d), lambda i: (0, 0)),
                  pl.BlockSpec((1, hd), lambda i: (0, 0)),
                  pl.BlockSpec(memory_space=pl.ANY),
                  pl.BlockSpec((1, d), lambda i: (0, 0)),
                  pl.BlockSpec((2, d, LANES), lambda i: (0, 0, 0)),
                  pl.BlockSpec((1, LANES), lambda i: (0, 0))],
        out_specs=[pl.BlockSpec((tm, d), lambda i: (i, 0)),
                   pl.BlockSpec((tm, LANES), lambda i: (i, 0)),
                   pl.BlockSpec((SUBLANES, LANES), lambda i: (0, 0))],
        out_shape=[jax.ShapeDtypeStruct((s, d), F32),
                   jax.ShapeDtypeStruct((s, LANES), F32),
                   jax.ShapeDtypeStruct((SUBLANES, LANES), F32)],
        scratch_shapes=[pltpu.VMEM((d, d), BF16), pltpu.VMEM((d, d), BF16),
                        pltpu.VMEM((2, d, hd), F32), pltpu.VMEM((tm, d), BF16),
                        pltpu.VMEM((SUBLANES, LANES), F32),
                        pltpu.SemaphoreType.DMA((2,))],
        compiler_params=_params("arbitrary"),
        name="xattn_route",
    )(h, g, wq, kv, q_g, wo, route_g, route_w, route_b)


ROUTE_EID = 0
ROUTE_RANK = 2
ROUTE_WT = 4


def _route_rows(h, g_ref, w_ref, b_ref, out_ref, cnt_ref, carry):
    tm = h.shape[0]

    @pl.when(pl.program_id(0) == 0)
    def _():
        carry[...] = jnp.zeros_like(carry)

    hn = _rms(h, g_ref[...])
    hi = hn.astype(BF16)
    lo = (hn - hi.astype(F32)).astype(BF16)
    logits = (_dot(hi, w_ref[0]) + _dot(hi, w_ref[1]) + _dot(lo, w_ref[0])
              + b_ref[...])
    lane = lax.broadcasted_iota(I32, (tm, LANES), 1)

    def first_lane(mask):
        return jnp.min(jnp.where(mask, lane.astype(F32), float(LANES)), axis=-1,
                       keepdims=True).astype(I32)

    is_g = lane < MOE_GROUPS
    gmax = jnp.max(jnp.where(is_g, logits, -jnp.inf), axis=-1, keepdims=True)
    gsum = jnp.sum(jnp.where(is_g, jnp.exp(logits - gmax), 0.0), axis=-1, keepdims=True)
    gsel = first_lane(is_g & (logits == gmax))
    pg = 1.0 / gsum

    grp_lo = MOE_GROUPS + MOE_PER_GROUP * gsel
    in_grp = (lane >= grp_lo) & (lane < grp_lo + MOE_PER_GROUP)
    emax = jnp.max(jnp.where(in_grp, logits, -jnp.inf), axis=-1, keepdims=True)
    eexp = jnp.where(in_grp, jnp.exp(logits - emax), 0.0)
    eprob = eexp / jnp.sum(eexp, axis=-1, keepdims=True)
    p1 = jnp.max(jnp.where(in_grp, eprob, -1.0), axis=-1, keepdims=True)
    l1 = first_lane(in_grp & (eprob == p1))
    rest = in_grp & (lane != l1)
    p2 = jnp.max(jnp.where(rest, eprob, -1.0), axis=-1, keepdims=True)
    l2 = first_lane(rest & (eprob == p2))
    psum = p1 + p2
    w1 = pg * p1 / psum
    w2 = pg * p2 / psum

    oh1 = lane == l1
    oh2 = lane == l2
    onehot = jnp.where(oh1 | oh2, 1.0, 0.0)
    r = lax.broadcasted_iota(I32, (tm, tm), 0)
    c = lax.broadcasted_iota(I32, (tm, tm), 1)
    strict = jnp.where(c < r, 1.0, 0.0).astype(BF16)
    before = _dot(strict, onehot.astype(BF16)) + carry[0:1, :]
    rank1 = jnp.sum(jnp.where(oh1, before, 0.0), axis=-1, keepdims=True)
    rank2 = jnp.sum(jnp.where(oh2, before, 0.0), axis=-1, keepdims=True)
    total = carry[0:1, :] + jnp.sum(onehot, axis=0, keepdims=True)
    carry[...] = jnp.broadcast_to(total, carry.shape)
    cnt_ref[...] = jnp.broadcast_to(total, cnt_ref.shape)

    out = jnp.where(lane == ROUTE_EID, (l1 - MOE_GROUPS).astype(F32), 0.0)
    out = jnp.where(lane == ROUTE_EID + 1, (l2 - MOE_GROUPS).astype(F32), out)
    out = jnp.where(lane == ROUTE_RANK, rank1, out)
    out = jnp.where(lane == ROUTE_RANK + 1, rank2, out)
    out = jnp.where(lane == ROUTE_WT, w1, out)
    out = jnp.where(lane == ROUTE_WT + 1, w2, out)
    out_ref[...] = out


def _row_tiles(x):
    return x.astype(BF16).reshape(x.shape[0], x.shape[1] // LANES, LANES)


def _from_row_tiles(x3):
    return x3.reshape(x3.shape[0], x3.shape[1] * LANES)


def _dispatch_kernel(d0_ref, d1_ref, zb_ref, h_ref, g_ref, xg_ref, buf, zero, sem, zsem):
    tm = h_ref.shape[0]
    base = pl.program_id(0) * tm

    @pl.when(pl.program_id(0) == 0)
    def _():
        zero[...] = jnp.zeros_like(zero)
        n_blk = xg_ref.shape[0] // MOE_ROWS

        def zero_copy(b):
            rows = pl.ds(pl.multiple_of(b * MOE_ROWS, MOE_ROWS), MOE_ROWS)
            return pltpu.make_async_copy(zero, xg_ref.at[rows], zsem)

        def start(b, carry):
            @pl.when(zb_ref[b] != 0)
            def _():
                zero_copy(b).start()
            return carry

        def finish(b, carry):
            @pl.when(zb_ref[b] != 0)
            def _():
                zero_copy(0).wait()
            return carry

        lax.fori_loop(0, n_blk, start, 0)
        lax.fori_loop(0, n_blk, finish, 0)

    step = pl.program_id(0)
    slot = step % 2
    buf[slot] = _row_tiles(_rms(h_ref[...], g_ref[...]))

    def row_copy(t, d):
        return pltpu.make_async_copy(buf.at[slot, pl.ds(t, 1)], xg_ref.at[pl.ds(d, 1)], sem.at[slot])

    def issue(t, carry):
        row_copy(t, d0_ref[base + t]).start()
        row_copy(t, d1_ref[base + t]).start()
        return carry

    lax.fori_loop(0, tm, issue, 0, unroll=ROW_DMA_UNROLL)

    def drain(s):
        whole = pltpu.make_async_copy(buf.at[s], xg_ref.at[pl.ds(0, tm)], sem.at[s])
        whole.wait()
        whole.wait()

    @pl.when(step > 0)
    def _():
        drain(1 - slot)

    @pl.when(step == pl.num_programs(0) - 1)
    def _():
        drain(slot)


def _dispatch(dest, zero_blk, h, g, tm):
    s, d = h.shape
    n_rows = zero_blk.shape[0] * MOE_ROWS
    grid_spec = pltpu.PrefetchScalarGridSpec(
        num_scalar_prefetch=3,
        grid=(s // tm,),
        in_specs=[pl.BlockSpec((tm, d), lambda i, d0, d1, zb: (i, 0)),
                  pl.BlockSpec((1, d), lambda i, d0, d1, zb: (0, 0))],
        out_specs=pl.BlockSpec(memory_space=pl.ANY),
        scratch_shapes=[pltpu.VMEM((2, tm, d // LANES, LANES), BF16),
                        pltpu.VMEM((MOE_ROWS, d // LANES, LANES), BF16),
                        pltpu.SemaphoreType.DMA((2,)), pltpu.SemaphoreType.DMA(())],
    )
    return pl.pallas_call(
        _dispatch_kernel,
        grid_spec=grid_spec,
        out_shape=jax.ShapeDtypeStruct((n_rows, d // LANES, LANES), BF16),
        compiler_params=_params("arbitrary"),
        name="moe_dispatch",
    )(dest[0], dest[1], zero_blk, h, g)


def _moe_ffn_kernel(be_ref, nu_ref, nx_ref, x_ref, wg_hbm, wu_hbm, wd_hbm, y_ref,
                    wgf, wuf, wdf, wgb, wub, wdb, slot_ref, sem):
    b = pl.program_id(0)
    e = be_ref[b]
    used = b < nu_ref[0]

    def weight_copies(expert, slot):
        copies = []
        for m, (src, dst) in enumerate(((wg_hbm, wgf), (wu_hbm, wuf), (wd_hbm, wdf))):
            slab = dst.shape[1] // WEIGHT_DMA_SPLIT
            for c in range(WEIGHT_DMA_SPLIT):
                rows = pl.ds(c * slab, slab)
                copies.append(pltpu.make_async_copy(src.at[expert, rows], dst.at[slot, rows],
                                                    sem.at[slot, m * WEIGHT_DMA_SPLIT + c]))
        return copies

    @pl.when(b == 0)
    def _():
        slot_ref[0] = 0
        for c in weight_copies(e, 0):
            c.start()

    @pl.when(used & ((b == 0) | (e != be_ref[jnp.maximum(b - 1, 0)])))
    def _():
        slot = slot_ref[0]
        for c in weight_copies(e, slot):
            c.wait()
        nxt = nx_ref[e]

        @pl.when(nxt >= 0)
        def _():
            for c in weight_copies(nxt, 1 - slot):
                c.start()

        wgb[...] = wgf[slot].astype(BF16)
        wub[...] = wuf[slot].astype(BF16)
        wdb[...] = wdf[slot].astype(BF16)
        slot_ref[0] = 1 - slot

    @pl.when(used)
    def _():
        xb = _from_row_tiles(x_ref[...])
        gate = _dot(xb, wgb[...])
        up = _dot(xb, wub[...])
        act = (gate * jax.nn.sigmoid(gate) * up).astype(BF16)
        y_ref[...] = _row_tiles(_dot(act, wdb[...]))

    @pl.when(jnp.logical_not(used))
    def _():
        y_ref[...] = jnp.zeros_like(y_ref)


def _moe_ffn(blk_e, n_used, next_e, xg, w_gate, w_up, w_down):
    n_rows, nt, _ = xg.shape
    d = nt * LANES
    ff = w_gate.shape[2]
    n_blk = n_rows // MOE_ROWS
    def row_blk(b, be, nu, nx):
        return jnp.minimum(b, nu[0] - 1), 0, 0

    grid_spec = pltpu.PrefetchScalarGridSpec(
        num_scalar_prefetch=3,
        grid=(n_blk,),
        in_specs=[pl.BlockSpec((MOE_ROWS, nt, LANES), row_blk),
                  pl.BlockSpec(memory_space=pl.ANY),
                  pl.BlockSpec(memory_space=pl.ANY),
                  pl.BlockSpec(memory_space=pl.ANY)],
        out_specs=pl.BlockSpec((MOE_ROWS, nt, LANES), lambda b, be, nu, nx: (b, 0, 0)),
        scratch_shapes=[pltpu.VMEM((2, d, ff), F32), pltpu.VMEM((2, d, ff), F32),
                        pltpu.VMEM((2, ff, d), F32),
                        pltpu.VMEM((d, ff), BF16), pltpu.VMEM((d, ff), BF16),
                        pltpu.VMEM((ff, d), BF16),
                        pltpu.SMEM((1,), I32),
                        pltpu.SemaphoreType.DMA((2, 3 * WEIGHT_DMA_SPLIT))],
    )
    return pl.pallas_call(
        _moe_ffn_kernel,
        grid_spec=grid_spec,
        out_shape=jax.ShapeDtypeStruct((n_rows, nt, LANES), BF16),
        compiler_params=_params("arbitrary"),
        name="moe_ffn",
    )(blk_e, n_used, next_e, xg, w_gate, w_up, w_down)


def _combine_kernel(d0_ref, d1_ref, h_ref, r_ref, y_ref, o_ref, buf, sem):
    tm = h_ref.shape[0]
    step = pl.program_id(0)
    slot = step % 2

    def gather(tile, s):
        base = tile * tm

        def issue(t, carry):
            for k, dk in enumerate((d0_ref, d1_ref)):
                pltpu.make_async_copy(y_ref.at[pl.ds(dk[base + t], 1)],
                                      buf.at[s, k, pl.ds(t, 1)], sem.at[s]).start()
            return carry

        lax.fori_loop(0, tm, issue, 0, unroll=ROW_DMA_UNROLL)

    @pl.when(step == 0)
    def _():
        gather(step, slot)

    @pl.when(step + 1 < pl.num_programs(0))
    def _():
        gather(step + 1, 1 - slot)

    for k in range(2):
        pltpu.make_async_copy(y_ref.at[pl.ds(0, tm)], buf.at[slot, k], sem.at[slot]).wait()
    r = r_ref[...]
    o_ref[...] = (h_ref[...]
                  + r[:, ROUTE_WT:ROUTE_WT + 1] * _from_row_tiles(buf[slot, 0]).astype(F32)
                  + r[:, ROUTE_WT + 1:ROUTE_WT + 2] * _from_row_tiles(buf[slot, 1]).astype(F32))


def _combine(dest, h, routed, y, tm):
    s, d = h.shape
    grid_spec = pltpu.PrefetchScalarGridSpec(
        num_scalar_prefetch=2,
        grid=(s // tm,),
        in_specs=[pl.BlockSpec((tm, d), lambda i, d0, d1: (i, 0)),
                  pl.BlockSpec((tm, LANES), lambda i, d0, d1: (i, 0)),
                  pl.BlockSpec(memory_space=pl.ANY)],
        out_specs=pl.BlockSpec((tm, d), lambda i, d0, d1: (i, 0)),
        scratch_shapes=[pltpu.VMEM((2, 2, tm, d // LANES, LANES), BF16),
                        pltpu.SemaphoreType.DMA((2,))],
    )
    return pl.pallas_call(
        _combine_kernel,
        grid_spec=grid_spec,
        out_shape=jax.ShapeDtypeStruct((s, d), F32),
        compiler_params=_params("arbitrary"),
        name="moe_combine",
    )(dest[0], dest[1], h, routed, y)


def _layer(h, mem, pos, norm_mix_g, w_in, gate_b, conv_w, conv_b, out_g, moba_q_g, moba_k_g, w_out,
           norm_cross_g, norm_mem_g, xa_wq, xa_wkv, xa_q_g, xa_k_g, xa_wo, norm_ffn_g,
           router_group_w, router_group_b, router_expert_w, router_expert_b,
           exp_w_gate, exp_w_up, exp_w_down):
    s, d = h.shape
    row = lambda v: v.reshape(1, -1)

    n_gate = 2 * MLSTM_HEADS
    gate_lo = 2 * MLSTM_HEADS * MLSTM_QK + 2 * MLSTM_HEADS * MLSTM_V
    w_t = w_in.T
    w_a = w_t[:gate_lo].astype(BF16)
    w_b = w_t[gate_lo + n_gate:].astype(BF16)
    w_g = jnp.pad(w_t[gate_lo:gate_lo + n_gate], ((0, PROJ_COL_TILE - n_gate), (0, 0))).astype(BF16)
    proj = _proj_in(h, row(norm_mix_g), w_a, w_b, w_g, tm=min(s, PROJ_ROW_TILE))

    gate_b_row = jnp.pad(gate_b, (0, LANES - n_gate)).reshape(1, LANES)
    hm = _mlstm(proj, w_a.shape[0] + w_b.shape[0], gate_b_row, conv_w, row(conv_b), row(out_g))
    qn, kn, vb, kmean = _moba_prep(proj, gate_lo, pos, row(moba_q_g), row(moba_k_g))
    ha = _moba_attn(qn, kn, vb, kmean)
    h = _mix_out(hm, ha, w_out, h, tm=MIX_ROW_TILE)

    kv = _mem_kv(mem, row(norm_mem_g), xa_wkv, row(xa_k_g))

    w_route = jnp.concatenate(
        [router_group_w, router_expert_w,
         jnp.zeros((d, LANES - MOE_GROUPS - MOE_EXPERTS), router_group_w.dtype)], axis=1)
    w_route_hi = w_route.astype(BF16)
    w_route = jnp.stack([w_route_hi, (w_route - w_route_hi.astype(F32)).astype(BF16)])
    b_route = jnp.pad(jnp.concatenate([router_group_b, router_expert_b]),
                      (0, LANES - MOE_GROUPS - MOE_EXPERTS)).reshape(1, LANES)
    h, routed, counts = _xattn_route(h, row(norm_cross_g), xa_wq, kv, row(xa_q_g), xa_wo,
                                     row(norm_ffn_g), w_route, b_route, tm=XATTN_ROW_TILE)

    cnt = counts[0, MOE_GROUPS:MOE_GROUPS + MOE_EXPERTS].astype(I32)
    padded = (cnt + MOE_ROWS - 1) // MOE_ROWS * MOE_ROWS
    pend = jnp.cumsum(padded)
    seg_start = (pend - padded).astype(F32)[None, :]
    expert_ids = jnp.arange(MOE_EXPERTS, dtype=F32)[None, :]

    def dest_of(k):
        mine = routed[:, ROUTE_EID + k:ROUTE_EID + k + 1] == expert_ids
        return (jnp.sum(jnp.where(mine, seg_start, 0.0), axis=1) + routed[:, ROUTE_RANK + k]).astype(I32)

    dest = (dest_of(0), dest_of(1))
    n_blk = -(-2 * s // MOE_ROWS) + MOE_EXPERTS
    blk_start = jnp.arange(n_blk, dtype=I32) * MOE_ROWS
    blk_e = jnp.minimum(jnp.sum((pend[None, :] <= blk_start[:, None]).astype(I32), axis=1),
                        MOE_EXPERTS - 1)
    n_used = (pend[-1:] // MOE_ROWS).astype(I32)
    ids = jnp.arange(MOE_EXPERTS, dtype=I32)
    later_used = (ids[None, :] > ids[:, None]) & (cnt[None, :] > 0)
    next_e = jnp.min(jnp.where(later_used, ids[None, :], MOE_EXPERTS), axis=1)
    next_e = jnp.where(next_e < MOE_EXPERTS, next_e, -1).astype(I32)

    is_blk_e = blk_e[:, None] == ids[None, :]
    seg_end = jnp.sum(jnp.where(is_blk_e, (pend - padded + cnt)[None, :], 0), axis=1)
    zero_blk = (blk_start + MOE_ROWS > seg_end).astype(I32)
    xg = _dispatch(dest, zero_blk, h, row(norm_ffn_g), tm=MOE_TOKEN_TILE)
    y = _moe_ffn(blk_e, n_used, next_e, xg, exp_w_gate, exp_w_up, exp_w_down)
    return _combine(dest, h, routed, y, tm=MOE_TOKEN_TILE)


def kernel(x, mem, positions, norm_mix_g, w_in, mlstm_gate_b, mlstm_conv_w, mlstm_conv_b, mlstm_out_g, moba_q_g, moba_k_g, w_out, norm_cross_g, norm_mem_g, xa_wq, xa_wkv, xa_q_g, xa_k_g, xa_wo, norm_ffn_g, router_group_w, router_group_b, router_expert_w, router_expert_b, exp_w_gate, exp_w_up, exp_w_down):
    bsz, s, _ = x.shape
    assert bsz == 1, "single-sequence prefill only"
    per_layer = (norm_mix_g, w_in, mlstm_gate_b, mlstm_conv_w, mlstm_conv_b, mlstm_out_g, moba_q_g,
                 moba_k_g, w_out, norm_cross_g, norm_mem_g, xa_wq, xa_wkv, xa_q_g, xa_k_g, xa_wo,
                 norm_ffn_g, router_group_w, router_group_b, router_expert_w, router_expert_b,
                 exp_w_gate, exp_w_up, exp_w_down)
    h = x[0]
    pos = positions.reshape(s, 1)
    for l in range(norm_mix_g.shape[0]):
        h = _layer(h, mem[0], pos, *(p[l] for p in per_layer))
    return h[None]
```

```python
import functools
import math

import jax
import jax.numpy as jnp
from jax import lax
from jax.experimental import pallas as pl
from jax.experimental.pallas import tpu as pltpu

F32 = jnp.float32
BF16 = jnp.bfloat16
I32 = jnp.int32

EPS = 1e-6
LANES = 128
SUBLANES = 8
VMEM_LIMIT = 56 * 1024 * 1024

MLSTM_HEADS = 4
MLSTM_QK = 128
MLSTM_V = 256
MLSTM_CHUNK = 128
MLSTM_CONV = 4
MOBA_HEADS = 8
MOBA_HD = 128
MOBA_BLOCK = 256
MOBA_TOPK = 3
MOBA_VT_ROWS = MOBA_HD + 16
ROPE_DIM = 32
ROPE_THETA = 500000.0
XA_HEADS = 4
MOE_GROUPS = 8
MOE_PER_GROUP = 8
MOE_EXPERTS = MOE_GROUPS * MOE_PER_GROUP
MOE_ROWS = 128
ROW_DMA_UNROLL = 8

PROJ_ROW_TILE = 1024
PROJ_COL_TILE = 512
MIX_ROW_TILE = 512
XATTN_ROW_TILE = 512
MOE_TOKEN_TILE = 256

NT_DIMS = (((1,), (1,)), ((), ()))


def _params(*sem):
    return pltpu.CompilerParams(dimension_semantics=sem, vmem_limit_bytes=VMEM_LIMIT)


def _rms(x, g):
    return x * lax.rsqrt(jnp.mean(x * x, axis=-1, keepdims=True) + EPS) * g


def _dot(a, b):
    return jnp.dot(a, b, preferred_element_type=F32)


def _dot_nt(a, b, precision=None):
    return lax.dot_general(a, b, NT_DIMS, precision=precision, preferred_element_type=F32)


def _proj_in_kernel(x_ref, g_ref, wa_ref, wb_ref, wg_ref, o_ref, xn_ref, *, na, nb):
    j = pl.program_id(1)

    @pl.when(j == 0)
    def _():
        xn_ref[...] = _rms(x_ref[...], g_ref[...]).astype(BF16)

    @pl.when(j < na)
    def _():
        o_ref[...] = _dot_nt(xn_ref[...], wa_ref[...])

    @pl.when((j >= na) & (j < na + nb))
    def _():
        o_ref[...] = _dot_nt(xn_ref[...], wb_ref[...])

    @pl.when(j == na + nb)
    def _():
        o_ref[...] = _dot_nt(xn_ref[...], wg_ref[...])


def _proj_in(x, g, w_a, w_b, w_g, tm):
    s, d = x.shape
    tn = w_g.shape[0]
    na, nb = w_a.shape[0] // tn, w_b.shape[0] // tn
    return pl.pallas_call(
        functools.partial(_proj_in_kernel, na=na, nb=nb),
        grid=(s // tm, na + nb + 1),
        in_specs=[pl.BlockSpec((tm, d), lambda i, j: (i, 0)),
                  pl.BlockSpec((1, d), lambda i, j: (0, 0)),
                  pl.BlockSpec((tn, d), lambda i, j: (jnp.minimum(j, na - 1), 0)),
                  pl.BlockSpec((tn, d), lambda i, j: (jnp.clip(j - na, 0, nb - 1), 0)),
                  pl.BlockSpec((tn, d), lambda i, j: (0, 0))],
        out_specs=pl.BlockSpec((tm, tn), lambda i, j: (i, j)),
        out_shape=jax.ShapeDtypeStruct((s, (na + nb + 1) * tn), F32),
        scratch_shapes=[pltpu.VMEM((tm, d), BF16)],
        compiler_params=_params("parallel", "arbitrary"),
        name="proj_in",
    )(x, g, w_a, w_b, w_g)


def _split3(x):
    hi = x.astype(BF16)
    r = x - hi.astype(F32)
    mid = r.astype(BF16)
    lo = (r - mid.astype(F32)).astype(BF16)
    return hi, mid, lo


def _log_sigmoid(x):
    return jnp.minimum(x, 0.0) - jnp.log(1.0 + jnp.exp(-jnp.abs(x)))


def _mlstm_kernel(q_ref, k_ref, v_ref, og_ref, gt_ref, gb_ref, cw_ref, cb_ref, outg_ref, hm_ref,
                  qext, kext, c_scr, n_scr, m_scr):
    L = MLSTM_CHUNK
    pad = SUBLANES
    qkw = MLSTM_HEADS * MLSTM_QK
    rows_blk = q_ref.shape[0]

    @pl.when(pl.program_id(0) == 0)
    def _():
        qext[0:pad, :] = jnp.zeros((pad, qkw), F32)
        kext[0:pad, :] = jnp.zeros((pad, qkw), F32)
        c_scr[...] = jnp.zeros_like(c_scr)
        n_scr[...] = jnp.zeros_like(n_scr)
        m_scr[...] = jnp.zeros_like(m_scr)

    qext[pad:pad + rows_blk, :] = q_ref[...]
    kext[pad:pad + rows_blk, :] = k_ref[...]

    def conv_silu(ext, lo):
        acc = jnp.broadcast_to(cb_ref[:, lo:lo + qkw], (rows_blk, qkw))
        for j in range(MLSTM_CONV):
            shift = MLSTM_CONV - 1 - j
            acc = acc + cw_ref[j:j + 1, lo:lo + qkw] * ext[pad - shift:pad - shift + rows_blk, :]
        return acc * jax.nn.sigmoid(acc)

    qc_all = conv_silu(qext, 0)
    kc_all = conv_silu(kext, qkw)
    qext[0:pad, :] = qext[rows_blk:rows_blk + pad, :]
    kext[0:pad, :] = kext[rows_blk:rows_blk + pad, :]

    row = lax.broadcasted_iota(I32, (L, L), 0)
    col = lax.broadcasted_iota(I32, (L, L), 1)
    causal = col <= row
    tri_lo = jnp.where(causal, 1.0, 0.0).astype(BF16)
    tri_up = jnp.where(row <= col, 1.0, 0.0).astype(BF16)
    for ck in range(rows_blk // L):
        rs = slice(ck * L, (ck + 1) * L)
        _mlstm_chunk(rs, qc_all[rs], kc_all[rs], v_ref, og_ref, gt_ref, gb_ref, outg_ref, hm_ref,
                     c_scr, n_scr, m_scr, causal, tri_lo, tri_up)


def _mlstm_chunk(rs, qc, kc, v_ref, og_ref, gt_ref, gb_ref, outg_ref, hm_ref, c_scr, n_scr, m_scr,
                 causal, tri_lo, tri_up):
    L = MLSTM_CHUNK
    gts = gt_ref[rs, :] + gb_ref[...]
    gtr = gts.T
    bc_all = sum(_dot(tri_lo, part) for part in _split3(_log_sigmoid(gts)))
    br_all = sum(_dot(part, tri_up) for part in _split3(_log_sigmoid(gtr)))

    for h in range(MLSTM_HEADS):
        fi = MLSTM_HEADS + h
        m_prev = m_scr[h, 0:1, 0:1]
        b_c = bc_all[:, fi:fi + 1]
        b_r = br_all[fi:fi + 1, :]
        i_c = gts[:, h:h + 1]
        i_r = gtr[h:h + 1, :]
        g = b_c[L - 1:L, :]
        d = jnp.where(causal, (b_c - b_r) + i_r, -jnp.inf)
        inter = b_c + m_prev
        m_t = jnp.maximum(inter, jnp.max(d, axis=-1, keepdims=True))
        w_inter = jnp.exp(inter - m_t)
        qh = qc[:, h * MLSTM_QK:(h + 1) * MLSTM_QK] * (MLSTM_QK ** -0.5)
        kh = kc[:, h * MLSTM_QK:(h + 1) * MLSTM_QK]
        vb = v_ref[rs, h * MLSTM_V:(h + 1) * MLSTM_V].astype(BF16)
        qb = qh.astype(BF16)
        p = jnp.exp(d - m_t) * _dot_nt(qb, kh.astype(BF16))
        c_prev = c_scr[h]
        n_prev = n_scr[h, 0:1, :]
        num = w_inter * _dot(qb, c_prev.astype(BF16)) + _dot(p.astype(BF16), vb)
        den = (w_inter * jnp.sum(qh * n_prev, axis=-1, keepdims=True)
               + jnp.sum(p, axis=-1, keepdims=True))
        hh = num * (1.0 / jnp.maximum(jnp.abs(den), jnp.exp(-m_t)))

        a = (g - b_c) + i_c
        m_new = jnp.maximum(g + m_prev, jnp.max(a, axis=0, keepdims=True))
        kw = kh * jnp.exp(a - m_new)
        decay = jnp.exp(g + m_prev - m_new)
        c_scr[h] = decay * c_prev + _dot(kw.T.astype(BF16), vb)
        n_scr[h, 0:1, :] = decay * n_prev + jnp.sum(kw, axis=0, keepdims=True)
        m_scr[h] = jnp.broadcast_to(m_new, (SUBLANES, LANES))

        hn = hh * lax.rsqrt(jnp.mean(hh * hh, axis=-1, keepdims=True) + EPS)
        vs = slice(h * MLSTM_V, (h + 1) * MLSTM_V)
        hn = hn * outg_ref[:, vs] * jax.nn.sigmoid(og_ref[rs, vs])
        hm_ref[rs, vs] = hn.astype(hm_ref.dtype)


def _mlstm(proj, gate_col, gate_b, conv_w, conv_b, out_g):
    s = proj.shape[0]
    L = MLSTM_CHUNK
    qkw = MLSTM_HEADS * MLSTM_QK
    vw = MLSTM_HEADS * MLSTM_V
    gate_blk = gate_col // LANES
    return pl.pallas_call(
        _mlstm_kernel,
        grid=(s // L,),
        in_specs=[pl.BlockSpec((L, qkw), lambda c: (c, 0)),
                  pl.BlockSpec((L, qkw), lambda c: (c, 1)),
                  pl.BlockSpec((L, vw), lambda c: (c, 1)),
                  pl.BlockSpec((L, vw), lambda c: (c, 2)),
                  pl.BlockSpec((L, LANES), lambda c: (c, gate_blk)),
                  pl.BlockSpec((1, LANES), lambda c: (0, 0)),
                  pl.BlockSpec((MLSTM_CONV, 2 * qkw), lambda c: (0, 0)),
                  pl.BlockSpec((1, 2 * qkw), lambda c: (0, 0)),
                  pl.BlockSpec((1, vw), lambda c: (0, 0))],
        out_specs=pl.BlockSpec((L, vw), lambda c: (c, 0)),
        out_shape=jax.ShapeDtypeStruct((s, vw), BF16),
        scratch_shapes=[pltpu.VMEM((L + SUBLANES, qkw), F32),
                        pltpu.VMEM((L + SUBLANES, qkw), F32),
                        pltpu.VMEM((MLSTM_HEADS, MLSTM_QK, MLSTM_V), F32),
                        pltpu.VMEM((MLSTM_HEADS, SUBLANES, LANES), F32),
                        pltpu.VMEM((MLSTM_HEADS, SUBLANES, LANES), F32)],
        compiler_params=_params("arbitrary"),
        name="mlstm",
    )(proj, proj, proj, proj, proj, gate_b, conv_w, conv_b, out_g)


def _moba_prep_kernel(q_ref, k_ref, v_ref, pos_ref, qg_ref, kg_ref, qt_ref, kn_ref, vt_ref, km_ref):
    rows = q_ref.shape[0]
    half = ROPE_DIM // 2
    lane = lax.broadcasted_iota(I32, (1, MOBA_HD), 1)
    inv_freq = jnp.exp((lane & (half - 1)).astype(F32) * (-(2.0 / ROPE_DIM) * math.log(ROPE_THETA)))
    ang = pos_ref[...].astype(F32) * inv_freq
    cos = jnp.where(lane < ROPE_DIM, jnp.cos(ang), 1.0)
    sin = jnp.sin(ang)
    sin = jnp.where(lane < half, -sin, jnp.where(lane < ROPE_DIM, sin, 0.0))

    def rope(x):
        partner = jnp.where(lane < half, pltpu.roll(x, MOBA_HD - half, 1), pltpu.roll(x, half, 1))
        return x * cos + partner * sin

    for h in range(MOBA_HEADS):
        hs = slice(h * MOBA_HD, (h + 1) * MOBA_HD)
        qt_ref[hs, :] = rope(_rms(q_ref[:, hs], qg_ref[...])).T
        kn = rope(_rms(k_ref[:, hs], kg_ref[...]))
        kn_ref[:, hs] = kn.astype(BF16)
        km_ref[0, :, hs] = jnp.sum(kn, axis=0, keepdims=True) * (1.0 / rows)
        vlo = h * MOBA_VT_ROWS
        vt_ref[0, vlo:vlo + MOBA_HD, :] = v_ref[:, hs].T.astype(BF16)
        vt_ref[0, vlo + MOBA_HD:vlo + MOBA_VT_ROWS, :] = jnp.ones((MOBA_VT_ROWS - MOBA_HD, rows), BF16)


def _moba_prep(proj, q_col, pos, q_g, k_g):
    s = proj.shape[0]
    w = MOBA_HEADS * MOBA_HD
    bs = MOBA_BLOCK
    nb = s // bs
    first = q_col // w
    return pl.pallas_call(
        _moba_prep_kernel,
        grid=(nb,),
        in_specs=[pl.BlockSpec((bs, w), lambda i: (i, first)),
                  pl.BlockSpec((bs, w), lambda i: (i, first + 1)),
                  pl.BlockSpec((bs, w), lambda i: (i, first + 2)),
                  pl.BlockSpec((bs, 1), lambda i: (i, 0)),
                  pl.BlockSpec((1, MOBA_HD), lambda i: (0, 0)),
                  pl.BlockSpec((1, MOBA_HD), lambda i: (0, 0))],
        out_specs=[pl.BlockSpec((w, bs), lambda i: (0, i)),
                   pl.BlockSpec((bs, w), lambda i: (i, 0)),
                   pl.BlockSpec((1, MOBA_HEADS * MOBA_VT_ROWS, bs), lambda i: (i, 0, 0)),
                   pl.BlockSpec((1, 1, w), lambda i: (i, 0, 0))],
        out_shape=[jax.ShapeDtypeStruct((w, s), F32),
                   jax.ShapeDtypeStruct((s, w), BF16),
                   jax.ShapeDtypeStruct((nb, MOBA_HEADS * MOBA_VT_ROWS, bs), BF16),
                   jax.ShapeDtypeStruct((nb, 1, w), F32)],
        compiler_params=_params("parallel"),
        name="moba_prep",
    )(proj, proj, proj, pos, q_g, k_g)


def _moba_attn_kernel(qt_ref, k_ref, vt_ref, km_ref, o_ref, bias_scr, qb_scr, m_scr, *tiles):
    i = pl.program_id(1)
    bs = MOBA_BLOCK
    hd = MOBA_HD
    nh = len(tiles) // 3
    vr = MOBA_VT_ROWS
    acc_scr = tiles[:nh]
    s_scr = tiles[nh:2 * nh]
    p_scr = tiles[2 * nh:]
    nb = km_ref.shape[0]
    c2 = (hd ** -0.5) * math.log2(math.e)
    blk = lax.broadcasted_iota(I32, (nb, bs), 0)
    blk_f = blk.astype(F32)
    valid = blk < i
    key = lax.broadcasted_iota(I32, (bs, bs), 0)
    qry = lax.broadcasted_iota(I32, (bs, bs), 1)
    own = pl.ds(pl.multiple_of(i * bs, bs), bs)

    for h in range(nh):
        hs = slice(h * hd, (h + 1) * hd)
        qt = qt_ref[hs, :]
        km = km_ref[:, 0, hs]
        km_hi, q_hi = km.astype(BF16), qt.astype(BF16)
        km_lo = (km - km_hi.astype(F32)).astype(BF16)
        q_lo = (qt - q_hi.astype(F32)).astype(BF16)
        gate = _dot(km_hi, q_hi) + _dot(km_hi, q_lo) + _dot(km_lo, q_hi)
        cand = valid
        for _ in range(MOBA_TOPK):
            gmax = jnp.max(jnp.where(cand, gate, -jnp.inf), axis=0, keepdims=True)
            hit = cand & (gate == gmax)
            first = jnp.min(jnp.where(hit, blk_f, float(nb)), axis=0, keepdims=True)
            cand = cand & jnp.logical_not(hit & (blk_f == first))
        bias_scr[h] = jnp.where(valid & jnp.logical_not(cand), 0.0, -jnp.inf)

        qb = (qt * c2).astype(BF16)
        qb_scr[h] = qb
        s0 = jnp.where(key <= qry, _dot(k_ref[own, hs], qb), -jnp.inf)
        m0 = jnp.max(s0, axis=0, keepdims=True)
        p0 = jnp.exp2(s0 - m0)
        m_scr[h] = m0
        acc_scr[h][...] = _dot(vt_ref[i, h * vr:(h + 1) * vr, :], p0.astype(BF16))

    ck = 64

    def body(j, carry):
        rows = pl.ds(pl.multiple_of(j * bs, bs), bs)
        for h in range(nh):
            s_scr[h][...] = _dot(k_ref[rows, h * hd:(h + 1) * hd], qb_scr[h])
        alphas = []
        for h in range(nh):
            bias = bias_scr[h, pl.ds(j, 1), :]
            cmax = s_scr[h][0:ck, :]
            for c in range(1, bs // ck):
                cmax = jnp.maximum(cmax, s_scr[h][c * ck:(c + 1) * ck, :])
            m = m_scr[h]
            m_new = jnp.maximum(m, jnp.max(cmax, axis=0, keepdims=True) + bias)
            alpha = jnp.exp2(m - m_new)
            shift = m_new - bias
            for c in range(bs // ck):
                p = jnp.exp2(s_scr[h][c * ck:(c + 1) * ck, :] - shift)
                p_scr[h][c * ck:(c + 1) * ck, :] = p.astype(BF16)
            m_scr[h] = m_new
            alphas.append(alpha)
        for h in range(nh):
            acc_scr[h][...] = (alphas[h] * acc_scr[h][...]
                               + _dot(vt_ref[j, h * vr:(h + 1) * vr, :], p_scr[h][...]))
        return carry

    lax.fori_loop(0, i, body, 0)
    for h in range(nh):
        acc = acc_scr[h][...]
        out_t = acc[0:hd] * (1.0 / acc[hd:hd + 1])
        o_ref[:, h * hd:(h + 1) * hd] = out_t.T.astype(o_ref.dtype)


MOBA_HEADS_PER_STEP = 8


def _moba_attn(qt, kn, vt, kmean):
    w, s = qt.shape
    bs = MOBA_BLOCK
    nb = s // bs
    nh = MOBA_HEADS_PER_STEP
    gw = nh * MOBA_HD
    resident = pl.Buffered(1)
    return pl.pallas_call(
        _moba_attn_kernel,
        grid=(MOBA_HEADS // nh, nb),
        in_specs=[pl.BlockSpec((gw, bs), lambda g, i: (g, i)),
                  pl.BlockSpec((s, gw), lambda g, i: (0, g), pipeline_mode=resident),
                  pl.BlockSpec((nb, nh * MOBA_VT_ROWS, bs), lambda g, i: (0, g, 0),
                               pipeline_mode=resident),
                  pl.BlockSpec((nb, 1, gw), lambda g, i: (0, 0, g))],
        out_specs=pl.BlockSpec((bs, gw), lambda g, i: (i, g)),
        out_shape=jax.ShapeDtypeStruct((s, w), BF16),
        scratch_shapes=[pltpu.VMEM((nh, nb, bs), F32),
                        pltpu.VMEM((nh, MOBA_HD, bs), BF16),
                        pltpu.VMEM((nh, 1, bs), F32)]
                       + [pltpu.VMEM((MOBA_VT_ROWS, bs), F32) for _ in range(nh)]
                       + [pltpu.VMEM((bs, bs), F32) for _ in range(nh)]
                       + [pltpu.VMEM((bs, bs), BF16) for _ in range(nh)],
        compiler_params=_params("parallel", "arbitrary"),
        name="moba_attn",
    )(qt, kn, vt, kmean)


def _mix_out_kernel(hm_ref, ha_ref, w_ref, x_ref, o_ref, wb_ref):
    half = hm_ref.shape[1]

    @pl.when(pl.program_id(0) == 0)
    def _():
        wb_ref[...] = w_ref[...].astype(BF16)

    o_ref[...] = (x_ref[...] + _dot(hm_ref[...], wb_ref[0:half, :])
                  + _dot(ha_ref[...], wb_ref[half:2 * half, :]))


def _mix_out(hm, ha, w, x, tm):
    s, d = x.shape
    half = hm.shape[1]
    return pl.pallas_call(
        _mix_out_kernel,
        grid=(s // tm,),
        in_specs=[pl.BlockSpec((tm, half), lambda i: (i, 0)),
                  pl.BlockSpec((tm, half), lambda i: (i, 0)),
                  pl.BlockSpec((2 * half, d), lambda i: (0, 0), pipeline_mode=pl.Buffered(1)),
                  pl.BlockSpec((tm, d), lambda i: (i, 0))],
        out_specs=pl.BlockSpec((tm, d), lambda i: (i, 0)),
        out_shape=jax.ShapeDtypeStruct((s, d), F32),
        scratch_shapes=[pltpu.VMEM((2 * half, d), BF16)],
        compiler_params=_params("arbitrary"),
        name="mix_out",
    )(hm, ha, w, x)


def _mem_kv_kernel(mem_ref, g_ref, w_ref, kg_ref, o_ref, mn_ref):
    j = pl.program_id(0)

    @pl.when(j == 0)
    def _():
        mn_ref[...] = _rms(mem_ref[...], g_ref[...]).astype(BF16)

    y = _dot(mn_ref[...], w_ref[...].astype(BF16))
    o_ref[...] = jnp.where(j < XA_HEADS, _rms(y, kg_ref[...]), y).astype(o_ref.dtype)


def _mem_kv(mem, g, wkv, k_g):
    m, d = mem.shape
    hd = d // XA_HEADS
    return pl.pallas_call(
        _mem_kv_kernel,
        grid=(2 * XA_HEADS,),
        in_specs=[pl.BlockSpec((m, d), lambda j: (0, 0)),
                  pl.BlockSpec((1, d), lambda j: (0, 0)),
                  pl.BlockSpec((d, hd), lambda j: (0, j)),
                  pl.BlockSpec((1, hd), lambda j: (0, 0))],
        out_specs=pl.BlockSpec((m, hd), lambda j: (0, j)),
        out_shape=jax.ShapeDtypeStruct((m, 2 * d), BF16),
        scratch_shapes=[pltpu.VMEM((m, d), BF16)],
        compiler_params=_params("arbitrary"),
        name="mem_kv",
    )(mem, g, wkv, k_g)


def _xattn_kernel(h_ref, g_ref, wq_hbm, kv_ref, qg_ref, wo_hbm, rg_ref, rw_ref, rb_ref,
                  o_ref, routed_ref, cnt_ref, wqb, wob, stage, o_all, carry, sem):
    d = h_ref.shape[1]
    hd = d // XA_HEADS
    n_slab = 2 * XA_HEADS

    @pl.when(pl.program_id(0) == 0)
    def _():
        def slab_copy(k):
            src = wq_hbm if k < XA_HEADS else wo_hbm
            cols = pl.ds((k % XA_HEADS) * hd, hd)
            return pltpu.make_async_copy(src.at[:, cols], stage.at[k % 2], sem.at[k % 2])

        slab_copy(0).start()
        for k in range(n_slab):
            if k + 1 < n_slab:
                slab_copy(k + 1).start()
            slab_copy(k).wait()
            dst = wqb if k < XA_HEADS else wob
            dst[:, (k % XA_HEADS) * hd:(k % XA_HEADS + 1) * hd] = stage[k % 2].astype(BF16)

    h = h_ref[...]
    q_all = _dot(_rms(h, g_ref[...]).astype(BF16), wqb[...])
    for j in range(XA_HEADS):
        hs = slice(j * hd, (j + 1) * hd)
        q = _rms(q_all[:, hs], qg_ref[...]).astype(BF16)
        sc = _dot_nt(q, kv_ref[:, hs]) * (hd ** -0.5)
        p = jnp.exp(sc - jnp.max(sc, axis=-1, keepdims=True))
        p = p * (1.0 / jnp.sum(p, axis=-1, keepdims=True))
        o_all[:, hs] = _dot(p.astype(BF16), kv_ref[:, d + j * hd:d + (j + 1) * hd]).astype(BF16)
    h_out = h + _dot(o_all[...], wob[...])
    o_ref[...] = h_out
    _route_rows(h_out, rg_ref, rw_ref, rb_ref, routed_ref, cnt_ref, carry)


def _xattn_route(h, g, wq, kv, q_g, wo, route_g, route_w, route_b, tm):
    s, d = h.shape
    m = kv.shape[0]
    hd = d // XA_HEADS
    return pl.pallas_call(
        _xattn_kernel,
        grid=(s // tm,),
        in_specs=[pl.BlockSpec((tm, d), lambda i: (i, 0)),
                  pl.BlockSpec((1, d), lambda i: (0, 0)),
                  pl.BlockSpec(memory_space=pl.ANY),
                  pl.BlockSpec((m, 2 * d), lambda i: (0, 0)),
                  pl.BlockSpec((1, hd), lambda i: (0, 0)),
                  pl.BlockSpec(memory_space=pl.ANY),
                  pl.BlockSpec((1, d), lambda i: (0, 0)),
                  pl.BlockSpec((2, d, LANES), lambda i: (0, 0, 0)),
                  pl.BlockSpec((1, LANES), lambda i: (0, 0))],
        out_specs=[pl.BlockSpec((tm, d), lambda i: (i, 0)),
                   pl.BlockSpec((tm, LANES), lambda i: (i, 0)),
                   pl.BlockSpec((SUBLANES, LANES), lambda i: (0, 0))],
        out_shape=[jax.ShapeDtypeStruct((s, d), F32),
                   jax.ShapeDtypeStruct((s, LANES), F32),
                   jax.ShapeDtypeStruct((SUBLANES, LANES), F32)],
        scratch_shapes=[pltpu.VMEM((d, d), BF16), pltpu.VMEM((d, d), BF16),
                        pltpu.VMEM((2, d, hd), F32), pltpu.VMEM((tm, d), BF16),
                        pltpu.VMEM((SUBLANES, LANES), F32),
                        pltpu.SemaphoreType.DMA((2,))],
        compiler_params=_params("arbitrary"),
        name="xattn_route",
    )(h, g, wq, kv, q_g, wo, route_g, route_w, route_b)


ROUTE_EID = 0
ROUTE_RANK = 2
ROUTE_WT = 4


def _route_rows(h, g_ref, w_ref, b_ref, out_ref, cnt_ref, carry):
    tm = h.shape[0]

    @pl.when(pl.program_id(0) == 0)
    def _():
        carry[...] = jnp.zeros_like(carry)

    hn = _rms(h, g_ref[...])
    hi = hn.astype(BF16)
    lo = (hn - hi.astype(F32)).astype(BF16)
    logits = (_dot(hi, w_ref[0]) + _dot(hi, w_ref[1]) + _dot(lo, w_ref[0])
              + b_ref[...])
    lane = lax.broadcasted_iota(I32, (tm, LANES), 1)

    def first_lane(mask):
        return jnp.min(jnp.where(mask, lane.astype(F32), float(LANES)), axis=-1,
                       keepdims=True).astype(I32)

    is_g = lane < MOE_GROUPS
    gmax = jnp.max(jnp.where(is_g, logits, -jnp.inf), axis=-1, keepdims=True)
    gsum = jnp.sum(jnp.where(is_g, jnp.exp(logits - gmax), 0.0), axis=-1, keepdims=True)
    gsel = first_lane(is_g & (logits == gmax))
    pg = 1.0 / gsum

    grp_lo = MOE_GROUPS + MOE_PER_GROUP * gsel
    in_grp = (lane >= grp_lo) & (lane < grp_lo + MOE_PER_GROUP)
    emax = jnp.max(jnp.where(in_grp, logits, -jnp.inf), axis=-1, keepdims=True)
    eexp = jnp.where(in_grp, jnp.exp(logits - emax), 0.0)
    eprob = eexp / jnp.sum(eexp, axis=-1, keepdims=True)
    p1 = jnp.max(jnp.where(in_grp, eprob, -1.0), axis=-1, keepdims=True)
    l1 = first_lane(in_grp & (eprob == p1))
    rest = in_grp & (lane != l1)
    p2 = jnp.max(jnp.where(rest, eprob, -1.0), axis=-1, keepdims=True)
    l2 = first_lane(rest & (eprob == p2))
    psum = p1 + p2
    w1 = pg * p1 / psum
    w2 = pg * p2 / psum

    oh1 = lane == l1
    oh2 = lane == l2
    onehot = jnp.where(oh1 | oh2, 1.0, 0.0)
    r = lax.broadcasted_iota(I32, (tm, tm), 0)
    c = lax.broadcasted_iota(I32, (tm, tm), 1)
    strict = jnp.where(c < r, 1.0, 0.0).astype(BF16)
    before = _dot(strict, onehot.astype(BF16)) + carry[0:1, :]
    rank1 = jnp.sum(jnp.where(oh1, before, 0.0), axis=-1, keepdims=True)
    rank2 = jnp.sum(jnp.where(oh2, before, 0.0), axis=-1, keepdims=True)
    total = carry[0:1, :] + jnp.sum(onehot, axis=0, keepdims=True)
    carry[...] = jnp.broadcast_to(total, carry.shape)
    cnt_ref[...] = jnp.broadcast_to(total, cnt_ref.shape)

    out = jnp.where(lane == ROUTE_EID, (l1 - MOE_GROUPS).astype(F32), 0.0)
    out = jnp.where(lane == ROUTE_EID + 1, (l2 - MOE_GROUPS).astype(F32), out)
    out = jnp.where(lane == ROUTE_RANK, rank1, out)
    out = jnp.where(lane == ROUTE_RANK + 1, rank2, out)
    out = jnp.where(lane == ROUTE_WT, w1, out)
    out = jnp.where(lane == ROUTE_WT + 1, w2, out)
    out_ref[...] = out


def _row_tiles(x):
    return x.astype(BF16).reshape(x.shape[0], x.shape[1] // LANES, LANES)


def _from_row_tiles(x3):
    return x3.reshape(x3.shape[0], x3.shape[1] * LANES)


def _dispatch_kernel(d0_ref, d1_ref, zb_ref, h_ref, g_ref, xg_ref, buf, zero, sem, zsem):
    tm = h_ref.shape[0]
    base = pl.program_id(0) * tm

    n_blk = xg_ref.shape[0] // MOE_ROWS

    def zero_copy(b, kind):
        rows = pl.ds(pl.multiple_of(b * MOE_ROWS, MOE_ROWS), MOE_ROWS)
        return pltpu.make_async_copy(zero, xg_ref.at[rows], zsem.at[kind - 1])

    def finish(kind):
        def body(b, carry):
            @pl.when(zb_ref[b] == kind)
            def _():
                zero_copy(0, kind).wait()
            return carry
        lax.fori_loop(0, n_blk, body, 0)

    @pl.when(pl.program_id(0) == 0)
    def _():
        zero[...] = jnp.zeros_like(zero)

        def start(b, carry):
            for kind in (1, 2):
                @pl.when(zb_ref[b] == kind)
                def _():
                    zero_copy(b, kind).start()
            return carry

        lax.fori_loop(0, n_blk, start, 0)
        finish(1)

    step = pl.program_id(0)
    slot = step % 2
    buf[slot] = _row_tiles(_rms(h_ref[...], g_ref[...]))

    def row_copy(t, d):
        return pltpu.make_async_copy(buf.at[slot, pl.ds(t, 1)], xg_ref.at[pl.ds(d, 1)], sem.at[slot])

    def issue(t, carry):
        row_copy(t, d0_ref[base + t]).start()
        row_copy(t, d1_ref[base + t]).start()
        return carry

    lax.fori_loop(0, tm, issue, 0, unroll=ROW_DMA_UNROLL)

    def drain(s):
        whole = pltpu.make_async_copy(buf.at[s], xg_ref.at[pl.ds(0, tm)], sem.at[s])
        whole.wait()
        whole.wait()

    @pl.when(step > 0)
    def _():
        drain(1 - slot)

    @pl.when(step == pl.num_programs(0) - 1)
    def _():
        drain(slot)
        finish(2)


def _dispatch(dest, zero_blk, h, g, tm):
    s, d = h.shape
    n_rows = zero_blk.shape[0] * MOE_ROWS
    grid_spec = pltpu.PrefetchScalarGridSpec(
        num_scalar_prefetch=3,
        grid=(s // tm,),
        in_specs=[pl.BlockSpec((tm, d), lambda i, d0, d1, zb: (i, 0)),
                  pl.BlockSpec((1, d), lambda i, d0, d1, zb: (0, 0))],
        out_specs=pl.BlockSpec(memory_space=pl.ANY),
        scratch_shapes=[pltpu.VMEM((2, tm, d // LANES, LANES), BF16),
                        pltpu.VMEM((MOE_ROWS, d // LANES, LANES), BF16),
                        pltpu.SemaphoreType.DMA((2,)), pltpu.SemaphoreType.DMA((2,))],
    )
    return pl.pallas_call(
        _dispatch_kernel,
        grid_spec=grid_spec,
        out_shape=jax.ShapeDtypeStruct((n_rows, d // LANES, LANES), BF16),
        compiler_params=_params("arbitrary"),
        name="moe_dispatch",
    )(dest[0], dest[1], zero_blk, h, g)


def _moe_ffn_kernel(be_ref, nu_ref, nx_ref, x_ref, wg_hbm, wu_hbm, wd_hbm, y_ref,
                    wgf, wuf, wdf, wgb, wub, wdb, slot_ref, sem):
    b = pl.program_id(0)
    e = be_ref[b]
    used = b < nu_ref[0]

    def weight_copies(expert, slot):
        return (pltpu.make_async_copy(wg_hbm.at[expert], wgf.at[slot], sem.at[slot, 0]),
                pltpu.make_async_copy(wu_hbm.at[expert], wuf.at[slot], sem.at[slot, 1]),
                pltpu.make_async_copy(wd_hbm.at[expert], wdf.at[slot], sem.at[slot, 2]))

    @pl.when(b == 0)
    def _():
        slot_ref[0] = 0
        for c in weight_copies(e, 0):
            c.start()

    @pl.when(used & ((b == 0) | (e != be_ref[jnp.maximum(b - 1, 0)])))
    def _():
        slot = slot_ref[0]
        for c in weight_copies(e, slot):
            c.wait()
        nxt = nx_ref[e]

        @pl.when(nxt >= 0)
        def _():
            for c in weight_copies(nxt, 1 - slot):
                c.start()

        wgb[...] = wgf[slot].astype(BF16)
        wub[...] = wuf[slot].astype(BF16)
        wdb[...] = wdf[slot].astype(BF16)
        slot_ref[0] = 1 - slot

    @pl.when(used)
    def _():
        xb = _from_row_tiles(x_ref[...])
        gate = _dot(xb, wgb[...])
        up = _dot(xb, wub[...])
        act = (gate * jax.nn.sigmoid(gate) * up).astype(BF16)
        y_ref[...] = _row_tiles(_dot(act, wdb[...]))

    @pl.when(jnp.logical_not(used))
    def _():
        y_ref[...] = jnp.zeros_like(y_ref)


def _moe_ffn(blk_e, n_used, next_e, xg, w_gate, w_up, w_down):
    n_rows, nt, _ = xg.shape
    d = nt * LANES
    ff = w_gate.shape[2]
    n_blk = n_rows // MOE_ROWS
    def row_blk(b, be, nu, nx):
        return jnp.minimum(b, nu[0] - 1), 0, 0

    grid_spec = pltpu.PrefetchScalarGridSpec(
        num_scalar_prefetch=3,
        grid=(n_blk,),
        in_specs=[pl.BlockSpec((MOE_ROWS, nt, LANES), row_blk),
                  pl.BlockSpec(memory_space=pl.ANY),
                  pl.BlockSpec(memory_space=pl.ANY),
                  pl.BlockSpec(memory_space=pl.ANY)],
        out_specs=pl.BlockSpec((MOE_ROWS, nt, LANES), lambda b, be, nu, nx: (b, 0, 0)),
        scratch_shapes=[pltpu.VMEM((2, d, ff), F32), pltpu.VMEM((2, d, ff), F32),
                        pltpu.VMEM((2, ff, d), F32),
                        pltpu.VMEM((d, ff), BF16), pltpu.VMEM((d, ff), BF16),
                        pltpu.VMEM((ff, d), BF16),
                        pltpu.SMEM((1,), I32), pltpu.SemaphoreType.DMA((2, 3))],
    )
    return pl.pallas_call(
        _moe_ffn_kernel,
        grid_spec=grid_spec,
        out_shape=jax.ShapeDtypeStruct((n_rows, nt, LANES), BF16),
        compiler_params=_params("arbitrary"),
        name="moe_ffn",
    )(blk_e, n_used, next_e, xg, w_gate, w_up, w_down)


def _combine_kernel(d0_ref, d1_ref, h_ref, r_ref, y_ref, o_ref, buf, sem):
    tm = h_ref.shape[0]
    step = pl.program_id(0)
    slot = step % 2

    def gather(tile, s):
        base = tile * tm

        def issue(t, carry):
            for k, dk in enumerate((d0_ref, d1_ref)):
                pltpu.make_async_copy(y_ref.at[pl.ds(dk[base + t], 1)],
                                      buf.at[s, k, pl.ds(t, 1)], sem.at[s]).start()
            return carry

        lax.fori_loop(0, tm, issue, 0, unroll=ROW_DMA_UNROLL)

    @pl.when(step == 0)
    def _():
        gather(step, slot)

    @pl.when(step + 1 < pl.num_programs(0))
    def _():
        gather(step + 1, 1 - slot)

    for k in range(2):
        pltpu.make_async_copy(y_ref.at[pl.ds(0, tm)], buf.at[slot, k], sem.at[slot]).wait()
    r = r_ref[...]
    o_ref[...] = (h_ref[...]
                  + r[:, ROUTE_WT:ROUTE_WT + 1] * _from_row_tiles(buf[slot, 0]).astype(F32)
                  + r[:, ROUTE_WT + 1:ROUTE_WT + 2] * _from_row_tiles(buf[slot, 1]).astype(F32))


def _combine(dest, h, routed, y, tm):
    s, d = h.shape
    grid_spec = pltpu.PrefetchScalarGridSpec(
        num_scalar_prefetch=2,
        grid=(s // tm,),
        in_specs=[pl.BlockSpec((tm, d), lambda i, d0, d1: (i, 0)),
                  pl.BlockSpec((tm, LANES), lambda i, d0, d1: (i, 0)),
                  pl.BlockSpec(memory_space=pl.ANY)],
        out_specs=pl.BlockSpec((tm, d), lambda i, d0, d1: (i, 0)),
        scratch_shapes=[pltpu.VMEM((2, 2, tm, d // LANES, LANES), BF16),
                        pltpu.SemaphoreType.DMA((2,))],
    )
    return pl.pallas_call(
        _combine_kernel,
        grid_spec=grid_spec,
        out_shape=jax.ShapeDtypeStruct((s, d), F32),
        compiler_params=_params("arbitrary"),
        name="moe_combine",
    )(dest[0], dest[1], h, routed, y)


def _layer(h, mem, pos, norm_mix_g, w_in, gate_b, conv_w, conv_b, out_g, moba_q_g, moba_k_g, w_out,
           norm_cross_g, norm_mem_g, xa_wq, xa_wkv, xa_q_g, xa_k_g, xa_wo, norm_ffn_g,
           router_group_w, router_group_b, router_expert_w, router_expert_b,
           exp_w_gate, exp_w_up, exp_w_down):
    s, d = h.shape
    row = lambda v: v.reshape(1, -1)

    n_gate = 2 * MLSTM_HEADS
    gate_lo = 2 * MLSTM_HEADS * MLSTM_QK + 2 * MLSTM_HEADS * MLSTM_V
    w_t = w_in.T
    w_a = w_t[:gate_lo].astype(BF16)
    w_b = w_t[gate_lo + n_gate:].astype(BF16)
    w_g = jnp.pad(w_t[gate_lo:gate_lo + n_gate], ((0, PROJ_COL_TILE - n_gate), (0, 0))).astype(BF16)
    proj = _proj_in(h, row(norm_mix_g), w_a, w_b, w_g, tm=min(s, PROJ_ROW_TILE))

    gate_b_row = jnp.pad(gate_b, (0, LANES - n_gate)).reshape(1, LANES)
    hm = _mlstm(proj, w_a.shape[0] + w_b.shape[0], gate_b_row, conv_w, row(conv_b), row(out_g))
    qn, kn, vb, kmean = _moba_prep(proj, gate_lo, pos, row(moba_q_g), row(moba_k_g))
    ha = _moba_attn(qn, kn, vb, kmean)
    h = _mix_out(hm, ha, w_out, h, tm=MIX_ROW_TILE)

    kv = _mem_kv(mem, row(norm_mem_g), xa_wkv, row(xa_k_g))

    w_route = jnp.concatenate(
        [router_group_w, router_expert_w,
         jnp.zeros((d, LANES - MOE_GROUPS - MOE_EXPERTS), router_group_w.dtype)], axis=1)
    w_route_hi = w_route.astype(BF16)
    w_route = jnp.stack([w_route_hi, (w_route - w_route_hi.astype(F32)).astype(BF16)])
    b_route = jnp.pad(jnp.concatenate([router_group_b, router_expert_b]),
                      (0, LANES - MOE_GROUPS - MOE_EXPERTS)).reshape(1, LANES)
    h, routed, counts = _xattn_route(h, row(norm_cross_g), xa_wq, kv, row(xa_q_g), xa_wo,
                                     row(norm_ffn_g), w_route, b_route, tm=XATTN_ROW_TILE)

    cnt = counts[0, MOE_GROUPS:MOE_GROUPS + MOE_EXPERTS].astype(I32)
    padded = (cnt + MOE_ROWS - 1) // MOE_ROWS * MOE_ROWS
    pend = jnp.cumsum(padded)
    seg_start = (pend - padded).astype(F32)[None, :]
    expert_ids = jnp.arange(MOE_EXPERTS, dtype=F32)[None, :]

    def dest_of(k):
        mine = routed[:, ROUTE_EID + k:ROUTE_EID + k + 1] == expert_ids
        return (jnp.sum(jnp.where(mine, seg_start, 0.0), axis=1) + routed[:, ROUTE_RANK + k]).astype(I32)

    dest = (dest_of(0), dest_of(1))
    n_blk = -(-2 * s // MOE_ROWS) + MOE_EXPERTS
    blk_start = jnp.arange(n_blk, dtype=I32) * MOE_ROWS
    blk_e = jnp.minimum(jnp.sum((pend[None, :] <= blk_start[:, None]).astype(I32), axis=1),
                        MOE_EXPERTS - 1)
    n_used = (pend[-1:] // MOE_ROWS).astype(I32)
    ids = jnp.arange(MOE_EXPERTS, dtype=I32)
    later_used = (ids[None, :] > ids[:, None]) & (cnt[None, :] > 0)
    next_e = jnp.min(jnp.where(later_used, ids[None, :], MOE_EXPERTS), axis=1)
    next_e = jnp.where(next_e < MOE_EXPERTS, next_e, -1).astype(I32)

    is_blk_e = blk_e[:, None] == ids[None, :]
    seg_end = jnp.sum(jnp.where(is_blk_e, (pend - padded + cnt)[None, :], 0), axis=1)
    zero_blk = jnp.where(blk_start >= pend[-1], 2, (blk_start + MOE_ROWS > seg_end).astype(I32))
    xg = _dispatch(dest, zero_blk, h, row(norm_ffn_g), tm=MOE_TOKEN_TILE)
    y = _moe_ffn(blk_e, n_used, next_e, xg, exp_w_gate, exp_w_up, exp_w_down)
    return _combine(dest, h, routed, y, tm=MOE_TOKEN_TILE)


def kernel(x, mem, positions, norm_mix_g, w_in, mlstm_gate_b, mlstm_conv_w, mlstm_conv_b, mlstm_out_g, moba_q_g, moba_k_g, w_out, norm_cross_g, norm_mem_g, xa_wq, xa_wkv, xa_q_g, xa_k_g, xa_wo, norm_ffn_g, router_group_w, router_group_b, router_expert_w, router_expert_b, exp_w_gate, exp_w_up, exp_w_down):
    bsz, s, _ = x.shape
    assert bsz == 1, "single-sequence prefill only"
    per_layer = (norm_mix_g, w_in, mlstm_gate_b, mlstm_conv_w, mlstm_conv_b, mlstm_out_g, moba_q_g,
                 moba_k_g, w_out, norm_cross_g, norm_mem_g, xa_wq, xa_wkv, xa_q_g, xa_k_g, xa_wo,
                 norm_ffn_g, router_group_w, router_group_b, router_expert_w, router_expert_b,
                 exp_w_gate, exp_w_up, exp_w_down)
    h = x[0]
    pos = positions.reshape(s, 1)
    for l in range(norm_mix_g.shape[0]):
        h = _layer(h, mem[0], pos, *(p[l] for p in per_layer))
    return h[None]
```

```python
import functools
import math

import jax
import jax.numpy as jnp
from jax import lax
from jax.experimental import pallas as pl
from jax.experimental.pallas import tpu as pltpu

F32 = jnp.float32
BF16 = jnp.bfloat16
I32 = jnp.int32

EPS = 1e-6
LANES = 128
SUBLANES = 8
VMEM_LIMIT = 56 * 1024 * 1024

MLSTM_HEADS = 4
MLSTM_QK = 128
MLSTM_V = 256
MLSTM_CHUNK = 128
MLSTM_CONV = 4
MOBA_HEADS = 8
MOBA_HD = 128
MOBA_BLOCK = 256
MOBA_TOPK = 3
MOBA_VT_ROWS = MOBA_HD + 16
ROPE_DIM = 32
ROPE_THETA = 500000.0
XA_HEADS = 4
MOE_GROUPS = 8
MOE_PER_GROUP = 8
MOE_EXPERTS = MOE_GROUPS * MOE_PER_GROUP
MOE_ROWS = 128
ROW_DMA_UNROLL = 8

PROJ_ROW_TILE = 1024
PROJ_COL_TILE = 512
MIX_ROW_TILE = 512
XATTN_ROW_TILE = 512
MOE_TOKEN_TILE = 256

NT_DIMS = (((1,), (1,)), ((), ()))


def _params(*sem):
    return pltpu.CompilerParams(dimension_semantics=sem, vmem_limit_bytes=VMEM_LIMIT)


def _rms(x, g):
    return x * lax.rsqrt(jnp.mean(x * x, axis=-1, keepdims=True) + EPS) * g


def _dot(a, b):
    return jnp.dot(a, b, preferred_element_type=F32)


def _dot_nt(a, b, precision=None):
    return lax.dot_general(a, b, NT_DIMS, precision=precision, preferred_element_type=F32)


def _proj_in_kernel(x_ref, g_ref, wa_ref, wb_ref, wg_ref, o_ref, xn_ref, *, na, nb):
    j = pl.program_id(1)

    @pl.when(j == 0)
    def _():
        xn_ref[...] = _rms(x_ref[...], g_ref[...]).astype(BF16)

    @pl.when(j < na)
    def _():
        o_ref[...] = _dot_nt(xn_ref[...], wa_ref[...])

    @pl.when((j >= na) & (j < na + nb))
    def _():
        o_ref[...] = _dot_nt(xn_ref[...], wb_ref[...])

    @pl.when(j == na + nb)
    def _():
        o_ref[...] = _dot_nt(xn_ref[...], wg_ref[...])


def _proj_in(x, g, w_a, w_b, w_g, tm):
    s, d = x.shape
    tn = w_g.shape[0]
    na, nb = w_a.shape[0] // tn, w_b.shape[0] // tn
    return pl.pallas_call(
        functools.partial(_proj_in_kernel, na=na, nb=nb),
        grid=(s // tm, na + nb + 1),
        in_specs=[pl.BlockSpec((tm, d), lambda i, j: (i, 0)),
                  pl.BlockSpec((1, d), lambda i, j: (0, 0)),
                  pl.BlockSpec((tn, d), lambda i, j: (jnp.minimum(j, na - 1), 0)),
                  pl.BlockSpec((tn, d), lambda i, j: (jnp.clip(j - na, 0, nb - 1), 0)),
                  pl.BlockSpec((tn, d), lambda i, j: (0, 0))],
        out_specs=pl.BlockSpec((tm, tn), lambda i, j: (i, j)),
        out_shape=jax.ShapeDtypeStruct((s, (na + nb + 1) * tn), F32),
        scratch_shapes=[pltpu.VMEM((tm, d), BF16)],
        compiler_params=_params("parallel", "arbitrary"),
        name="proj_in",
    )(x, g, w_a, w_b, w_g)


def _split3(x):
    hi = x.astype(BF16)
    r = x - hi.astype(F32)
    mid = r.astype(BF16)
    lo = (r - mid.astype(F32)).astype(BF16)
    return hi, mid, lo


def _log_sigmoid(x):
    return jnp.minimum(x, 0.0) - jnp.log(1.0 + jnp.exp(-jnp.abs(x)))


def _mlstm_kernel(q_ref, k_ref, v_ref, og_ref, gt_ref, gb_ref, cw_ref, cb_ref, outg_ref, hm_ref,
                  qext, kext, c_scr, n_scr, m_scr):
    L = MLSTM_CHUNK
    pad = SUBLANES
    qkw = MLSTM_HEADS * MLSTM_QK
    rows_blk = q_ref.shape[0]

    @pl.when(pl.program_id(0) == 0)
    def _():
        qext[0:pad, :] = jnp.zeros((pad, qkw), F32)
        kext[0:pad, :] = jnp.zeros((pad, qkw), F32)
        c_scr[...] = jnp.zeros_like(c_scr)
        n_scr[...] = jnp.zeros_like(n_scr)
        m_scr[...] = jnp.zeros_like(m_scr)

    qext[pad:pad + rows_blk, :] = q_ref[...]
    kext[pad:pad + rows_blk, :] = k_ref[...]

    def conv_silu(ext, lo):
        acc = jnp.broadcast_to(cb_ref[:, lo:lo + qkw], (rows_blk, qkw))
        for j in range(MLSTM_CONV):
            shift = MLSTM_CONV - 1 - j
            acc = acc + cw_ref[j:j + 1, lo:lo + qkw] * ext[pad - shift:pad - shift + rows_blk, :]
        return acc * jax.nn.sigmoid(acc)

    qc_all = conv_silu(qext, 0)
    kc_all = conv_silu(kext, qkw)
    qext[0:pad, :] = qext[rows_blk:rows_blk + pad, :]
    kext[0:pad, :] = kext[rows_blk:rows_blk + pad, :]

    row = lax.broadcasted_iota(I32, (L, L), 0)
    col = lax.broadcasted_iota(I32, (L, L), 1)
    causal = col <= row
    tri_lo = jnp.where(causal, 1.0, 0.0).astype(BF16)
    tri_up = jnp.where(row <= col, 1.0, 0.0).astype(BF16)
    for ck in range(rows_blk // L):
        rs = slice(ck * L, (ck + 1) * L)
        _mlstm_chunk(rs, qc_all[rs], kc_all[rs], v_ref, og_ref, gt_ref, gb_ref, outg_ref, hm_ref,
                     c_scr, n_scr, m_scr, causal, tri_lo, tri_up)


def _mlstm_chunk(rs, qc, kc, v_ref, og_ref, gt_ref, gb_ref, outg_ref, hm_ref, c_scr, n_scr, m_scr,
                 causal, tri_lo, tri_up):
    L = MLSTM_CHUNK
    gts = gt_ref[rs, :] + gb_ref[...]
    gtr = gts.T
    bc_all = sum(_dot(tri_lo, part) for part in _split3(_log_sigmoid(gts)))
    br_all = sum(_dot(part, tri_up) for part in _split3(_log_sigmoid(gtr)))

    for h in range(MLSTM_HEADS):
        fi = MLSTM_HEADS + h
        m_prev = m_scr[h, 0:1, 0:1]
        b_c = bc_all[:, fi:fi + 1]
        b_r = br_all[fi:fi + 1, :]
        i_c = gts[:, h:h + 1]
        i_r = gtr[h:h + 1, :]
        g = b_c[L - 1:L, :]
        d = jnp.where(causal, (b_c - b_r) + i_r, -jnp.inf)
        inter = b_c + m_prev
        m_t = jnp.maximum(inter, jnp.max(d, axis=-1, keepdims=True))
        w_inter = jnp.exp(inter - m_t)
        qh = qc[:, h * MLSTM_QK:(h + 1) * MLSTM_QK] * (MLSTM_QK ** -0.5)
        kh = kc[:, h * MLSTM_QK:(h + 1) * MLSTM_QK]
        vb = v_ref[rs, h * MLSTM_V:(h + 1) * MLSTM_V].astype(BF16)
        qb = qh.astype(BF16)
        p = jnp.exp(d - m_t) * _dot_nt(qb, kh.astype(BF16))
        c_prev = c_scr[h]
        n_prev = n_scr[h, 0:1, :]
        num = w_inter * _dot(qb, c_prev.astype(BF16)) + _dot(p.astype(BF16), vb)
        den = (w_inter * jnp.sum(qh * n_prev, axis=-1, keepdims=True)
               + jnp.sum(p, axis=-1, keepdims=True))
        hh = num * (1.0 / jnp.maximum(jnp.abs(den), jnp.exp(-m_t)))

        a = (g - b_c) + i_c
        m_new = jnp.maximum(g + m_prev, jnp.max(a, axis=0, keepdims=True))
        kw = kh * jnp.exp(a - m_new)
        decay = jnp.exp(g + m_prev - m_new)
        c_scr[h] = decay * c_prev + _dot(kw.T.astype(BF16), vb)
        n_scr[h, 0:1, :] = decay * n_prev + jnp.sum(kw, axis=0, keepdims=True)
        m_scr[h] = jnp.broadcast_to(m_new, (SUBLANES, LANES))

        hn = hh * lax.rsqrt(jnp.mean(hh * hh, axis=-1, keepdims=True) + EPS)
        vs = slice(h * MLSTM_V, (h + 1) * MLSTM_V)
        hn = hn * outg_ref[:, vs] * jax.nn.sigmoid(og_ref[rs, vs])
        hm_ref[rs, vs] = hn.astype(hm_ref.dtype)


def _mlstm(proj, gate_col, gate_b, conv_w, conv_b, out_g):
    s = proj.shape[0]
    L = MLSTM_CHUNK
    qkw = MLSTM_HEADS * MLSTM_QK
    vw = MLSTM_HEADS * MLSTM_V
    gate_blk = gate_col // LANES
    return pl.pallas_call(
        _mlstm_kernel,
        grid=(s // L,),
        in_specs=[pl.BlockSpec((L, qkw), lambda c: (c, 0)),
                  pl.BlockSpec((L, qkw), lambda c: (c, 1)),
                  pl.BlockSpec((L, vw), lambda c: (c, 1)),
                  pl.BlockSpec((L, vw), lambda c: (c, 2)),
                  pl.BlockSpec((L, LANES), lambda c: (c, gate_blk)),
                  pl.BlockSpec((1, LANES), lambda c: (0, 0)),
                  pl.BlockSpec((MLSTM_CONV, 2 * qkw), lambda c: (0, 0)),
                  pl.BlockSpec((1, 2 * qkw), lambda c: (0, 0)),
                  pl.BlockSpec((1, vw), lambda c: (0, 0))],
        out_specs=pl.BlockSpec((L, vw), lambda c: (c, 0)),
        out_shape=jax.ShapeDtypeStruct((s, vw), BF16),
        scratch_shapes=[pltpu.VMEM((L + SUBLANES, qkw), F32),
                        pltpu.VMEM((L + SUBLANES, qkw), F32),
                        pltpu.VMEM((MLSTM_HEADS, MLSTM_QK, MLSTM_V), F32),
                        pltpu.VMEM((MLSTM_HEADS, SUBLANES, LANES), F32),
                        pltpu.VMEM((MLSTM_HEADS, SUBLANES, LANES), F32)],
        compiler_params=_params("arbitrary"),
        name="mlstm",
    )(proj, proj, proj, proj, proj, gate_b, conv_w, conv_b, out_g)


def _moba_prep_kernel(q_ref, k_ref, v_ref, pos_ref, qg_ref, kg_ref, qt_ref, kn_ref, vt_ref, km_ref):
    rows = q_ref.shape[0]
    half = ROPE_DIM // 2
    lane = lax.broadcasted_iota(I32, (1, MOBA_HD), 1)
    inv_freq = jnp.exp((lane & (half - 1)).astype(F32) * (-(2.0 / ROPE_DIM) * math.log(ROPE_THETA)))
    ang = pos_ref[...].astype(F32) * inv_freq
    cos = jnp.where(lane < ROPE_DIM, jnp.cos(ang), 1.0)
    sin = jnp.sin(ang)
    sin = jnp.where(lane < half, -sin, jnp.where(lane < ROPE_DIM, sin, 0.0))

    def rope(x):
        partner = jnp.where(lane < half, pltpu.roll(x, MOBA_HD - half, 1), pltpu.roll(x, half, 1))
        return x * cos + partner * sin

    for h in range(MOBA_HEADS):
        hs = slice(h * MOBA_HD, (h + 1) * MOBA_HD)
        qt_ref[hs, :] = rope(_rms(q_ref[:, hs], qg_ref[...])).T
        kn = rope(_rms(k_ref[:, hs], kg_ref[...]))
        kn_ref[:, hs] = kn.astype(BF16)
        km_ref[0, :, hs] = jnp.sum(kn, axis=0, keepdims=True) * (1.0 / rows)
        vlo = h * MOBA_VT_ROWS
        vt_ref[0, vlo:vlo + MOBA_HD, :] = v_ref[:, hs].T.astype(BF16)
        vt_ref[0, vlo + MOBA_HD:vlo + MOBA_VT_ROWS, :] = jnp.ones((MOBA_VT_ROWS - MOBA_HD, rows), BF16)


def _moba_prep(proj, q_col, pos, q_g, k_g):
    s = proj.shape[0]
    w = MOBA_HEADS * MOBA_HD
    bs = MOBA_BLOCK
    nb = s // bs
    first = q_col // w
    return pl.pallas_call(
        _moba_prep_kernel,
        grid=(nb,),
        in_specs=[pl.BlockSpec((bs, w), lambda i: (i, first)),
                  pl.BlockSpec((bs, w), lambda i: (i, first + 1)),
                  pl.BlockSpec((bs, w), lambda i: (i, first + 2)),
                  pl.BlockSpec((bs, 1), lambda i: (i, 0)),
                  pl.BlockSpec((1, MOBA_HD), lambda i: (0, 0)),
                  pl.BlockSpec((1, MOBA_HD), lambda i: (0, 0))],
        out_specs=[pl.BlockSpec((w, bs), lambda i: (0, i)),
                   pl.BlockSpec((bs, w), lambda i: (i, 0)),
                   pl.BlockSpec((1, MOBA_HEADS * MOBA_VT_ROWS, bs), lambda i: (i, 0, 0)),
                   pl.BlockSpec((1, 1, w), lambda i: (i, 0, 0))],
        out_shape=[jax.ShapeDtypeStruct((w, s), F32),
                   jax.ShapeDtypeStruct((s, w), BF16),
                   jax.ShapeDtypeStruct((nb, MOBA_HEADS * MOBA_VT_ROWS, bs), BF16),
                   jax.ShapeDtypeStruct((nb, 1, w), F32)],
        compiler_params=_params("parallel"),
        name="moba_prep",
    )(proj, proj, proj, pos, q_g, k_g)


def _moba_attn_kernel(qt_ref, k_ref, vt_ref, km_ref, o_ref, bias_scr, qb_scr, m_scr, a_scr, *tiles):
    i = pl.program_id(1)
    bs = MOBA_BLOCK
    hd = MOBA_HD
    nh = len(tiles) // 5
    vr = MOBA_VT_ROWS
    acc_scr = tiles[:nh]
    s_scr = tiles[nh:3 * nh]
    p_scr = tiles[3 * nh:]
    nb = km_ref.shape[0]
    c2 = (hd ** -0.5) * math.log2(math.e)
    blk = lax.broadcasted_iota(I32, (nb, bs), 0)
    blk_f = blk.astype(F32)
    valid = blk < i
    key = lax.broadcasted_iota(I32, (bs, bs), 0)
    qry = lax.broadcasted_iota(I32, (bs, bs), 1)
    own = pl.ds(pl.multiple_of(i * bs, bs), bs)

    for h in range(nh):
        hs = slice(h * hd, (h + 1) * hd)
        qt = qt_ref[hs, :]
        km = km_ref[:, 0, hs]
        km_hi, q_hi = km.astype(BF16), qt.astype(BF16)
        km_lo = (km - km_hi.astype(F32)).astype(BF16)
        q_lo = (qt - q_hi.astype(F32)).astype(BF16)
        gate = _dot(km_hi, q_hi) + _dot(km_hi, q_lo) + _dot(km_lo, q_hi)
        cand = valid
        for _ in range(MOBA_TOPK):
            gmax = jnp.max(jnp.where(cand, gate, -jnp.inf), axis=0, keepdims=True)
            hit = cand & (gate == gmax)
            first = jnp.min(jnp.where(hit, blk_f, float(nb)), axis=0, keepdims=True)
            cand = cand & jnp.logical_not(hit & (blk_f == first))
        bias_scr[h] = jnp.where(valid & jnp.logical_not(cand), 0.0, -jnp.inf)

        qb = (qt * c2).astype(BF16)
        qb_scr[h] = qb
        s0 = jnp.where(key <= qry, _dot(k_ref[own, hs], qb), -jnp.inf)
        m0 = jnp.max(s0, axis=0, keepdims=True)
        p0 = jnp.exp2(s0 - m0)
        m_scr[h] = m0
        acc_scr[h][...] = _dot(vt_ref[i, h * vr:(h + 1) * vr, :], p0.astype(BF16))

    ck = 64

    def body(j, carry):
        rows = pl.ds(pl.multiple_of(j * bs, bs), bs)
        for h in range(nh):
            s_scr[h][...] = _dot(k_ref[rows, h * hd:(h + 1) * hd], qb_scr[h])
        alphas = []
        for h in range(nh):
            bias = bias_scr[h, pl.ds(j, 1), :]
            cmax = s_scr[h][0:ck, :]
            for c in range(1, bs // ck):
                cmax = jnp.maximum(cmax, s_scr[h][c * ck:(c + 1) * ck, :])
            m = m_scr[h]
            m_new = jnp.maximum(m, jnp.max(cmax, axis=0, keepdims=True) + bias)
            alpha = jnp.exp2(m - m_new)
            shift = m_new - bias
            for c in range(bs // ck):
                p = jnp.exp2(s_scr[h][c * ck:(c + 1) * ck, :] - shift)
                p_scr[h][c * ck:(c + 1) * ck, :] = p.astype(BF16)
            m_scr[h] = m_new
            alphas.append(alpha)
        for h in range(nh):
            acc_scr[h][...] = (alphas[h] * acc_scr[h][...]
                               + _dot(vt_ref[j, h * vr:(h + 1) * vr, :], p_scr[h][...]))
        return carry

    def scores(t, par):
        rows = pl.ds(pl.multiple_of(jnp.clip(t, 0, nb - 1) * bs, bs), bs)
        for h in range(nh):
            s_scr[par * nh + h][...] = _dot(k_ref[rows, h * hd:(h + 1) * hd], qb_scr[h])

    def softmax(t, par):
        jb = jnp.where((t >= 0) & (t < i), t, nb - 1)
        for h in range(nh):
            s_h = s_scr[par * nh + h]
            bias = bias_scr[h, pl.ds(jb, 1), :]
            cmax = s_h[0:ck, :]
            for c in range(1, bs // ck):
                cmax = jnp.maximum(cmax, s_h[c * ck:(c + 1) * ck, :])
            m = m_scr[h]
            m_new = jnp.maximum(m, jnp.max(cmax, axis=0, keepdims=True) + bias)
            a_scr[par, h] = jnp.exp2(m - m_new)
            shift = m_new - bias
            for c in range(bs // ck):
                p = jnp.exp2(s_h[c * ck:(c + 1) * ck, :] - shift)
                p_scr[par * nh + h][c * ck:(c + 1) * ck, :] = p.astype(BF16)
            m_scr[h] = m_new

    def accumulate(t, par):
        jv = jnp.clip(t, 0, nb - 1)
        for h in range(nh):
            acc_scr[h][...] = (a_scr[par, h] * acc_scr[h][...]
                               + _dot(vt_ref[jv, h * vr:(h + 1) * vr, :], p_scr[par * nh + h][...]))

    def pair(u, carry):
        t0 = 2 * u
        scores(t0, 0)
        softmax(t0 - 1, 1)
        accumulate(t0 - 2, 0)
        scores(t0 + 1, 1)
        softmax(t0, 0)
        accumulate(t0 - 1, 1)
        return carry

    @pl.when(i < MOBA_PIPELINE_FROM)
    def _():
        lax.fori_loop(0, i, body, 0)

    @pl.when(i >= MOBA_PIPELINE_FROM)
    def _():
        for h in range(nh):
            s_scr[nh + h][...] = jnp.zeros((bs, bs), F32)
            for par in range(2):
                p_scr[par * nh + h][...] = jnp.zeros((bs, bs), BF16)
        a_scr[...] = jnp.ones(a_scr.shape, F32)
        lax.fori_loop(0, (i + 3) // 2, pair, 0)

    for h in range(nh):
        acc = acc_scr[h][...]
        out_t = acc[0:hd] * (1.0 / acc[hd:hd + 1])
        o_ref[:, h * hd:(h + 1) * hd] = out_t.T.astype(o_ref.dtype)


MOBA_HEADS_PER_STEP = 8
MOBA_PIPELINE_FROM = 10


def _moba_attn(qt, kn, vt, kmean):
    w, s = qt.shape
    bs = MOBA_BLOCK
    nb = s // bs
    nh = MOBA_HEADS_PER_STEP
    gw = nh * MOBA_HD
    resident = pl.Buffered(1)
    return pl.pallas_call(
        _moba_attn_kernel,
        grid=(MOBA_HEADS // nh, nb),
        in_specs=[pl.BlockSpec((gw, bs), lambda g, i: (g, i)),
                  pl.BlockSpec((s, gw), lambda g, i: (0, g), pipeline_mode=resident),
                  pl.BlockSpec((nb, nh * MOBA_VT_ROWS, bs), lambda g, i: (0, g, 0),
                               pipeline_mode=resident),
                  pl.BlockSpec((nb, 1, gw), lambda g, i: (0, 0, g))],
        out_specs=pl.BlockSpec((bs, gw), lambda g, i: (i, g)),
        out_shape=jax.ShapeDtypeStruct((s, w), BF16),
        scratch_shapes=[pltpu.VMEM((nh, nb, bs), F32),
                        pltpu.VMEM((nh, MOBA_HD, bs), BF16),
                        pltpu.VMEM((nh, 1, bs), F32),
                        pltpu.VMEM((2, nh, 1, bs), F32)]
                       + [pltpu.VMEM((MOBA_VT_ROWS, bs), F32) for _ in range(nh)]
                       + [pltpu.VMEM((bs, bs), F32) for _ in range(2 * nh)]
                       + [pltpu.VMEM((bs, bs), BF16) for _ in range(2 * nh)],
        compiler_params=_params("parallel", "arbitrary"),
        name="moba_attn",
    )(qt, kn, vt, kmean)


def _mix_out_kernel(hm_ref, ha_ref, w_ref, x_ref, o_ref, wb_ref):
    half = hm_ref.shape[1]

    @pl.when(pl.program_id(0) == 0)
    def _():
        wb_ref[...] = w_ref[...].astype(BF16)

    o_ref[...] = (x_ref[...] + _dot(hm_ref[...], wb_ref[0:half, :])
                  + _dot(ha_ref[...], wb_ref[half:2 * half, :]))


def _mix_out(hm, ha, w, x, tm):
    s, d = x.shape
    half = hm.shape[1]
    return pl.pallas_call(
        _mix_out_kernel,
        grid=(s // tm,),
        in_specs=[pl.BlockSpec((tm, half), lambda i: (i, 0)),
                  pl.BlockSpec((tm, half), lambda i: (i, 0)),
                  pl.BlockSpec((2 * half, d), lambda i: (0, 0), pipeline_mode=pl.Buffered(1)),
                  pl.BlockSpec((tm, d), lambda i: (i, 0))],
        out_specs=pl.BlockSpec((tm, d), lambda i: (i, 0)),
        out_shape=jax.ShapeDtypeStruct((s, d), F32),
        scratch_shapes=[pltpu.VMEM((2 * half, d), BF16)],
        compiler_params=_params("arbitrary"),
        name="mix_out",
    )(hm, ha, w, x)


def _mem_kv_kernel(mem_ref, g_ref, w_ref, kg_ref, o_ref, mn_ref):
    j = pl.program_id(0)

    @pl.when(j == 0)
    def _():
        mn_ref[...] = _rms(mem_ref[...], g_ref[...]).astype(BF16)

    y = _dot(mn_ref[...], w_ref[...].astype(BF16))
    o_ref[...] = jnp.where(j < XA_HEADS, _rms(y, kg_ref[...]), y).astype(o_ref.dtype)


def _mem_kv(mem, g, wkv, k_g):
    m, d = mem.shape
    hd = d // XA_HEADS
    return pl.pallas_call(
        _mem_kv_kernel,
        grid=(2 * XA_HEADS,),
        in_specs=[pl.BlockSpec((m, d), lambda j: (0, 0)),
                  pl.BlockSpec((1, d), lambda j: (0, 0)),
                  pl.BlockSpec((d, hd), lambda j: (0, j)),
                  pl.BlockSpec((1, hd), lambda j: (0, 0))],
        out_specs=pl.BlockSpec((m, hd), lambda j: (0, j)),
        out_shape=jax.ShapeDtypeStruct((m, 2 * d), BF16),
        scratch_shapes=[pltpu.VMEM((m, d), BF16)],
        compiler_params=_params("arbitrary"),
        name="mem_kv",
    )(mem, g, wkv, k_g)


def _xattn_kernel(h_ref, g_ref, wq_hbm, kv_ref, qg_ref, wo_hbm, rg_ref, rw_ref, rb_ref,
                  o_ref, routed_ref, cnt_ref, wqb, wob, stage, o_all, carry, sem):
    d = h_ref.shape[1]
    hd = d // XA_HEADS
    n_slab = 2 * XA_HEADS

    @pl.when(pl.program_id(0) == 0)
    def _():
        def slab_copy(k):
            src = wq_hbm if k < XA_HEADS else wo_hbm
            cols = pl.ds((k % XA_HEADS) * hd, hd)
            return pltpu.make_async_copy(src.at[:, cols], stage.at[k % 2], sem.at[k % 2])

        slab_copy(0).start()
        for k in range(n_slab):
            if k + 1 < n_slab:
                slab_copy(k + 1).start()
            slab_copy(k).wait()
            dst = wqb if k < XA_HEADS else wob
            dst[:, (k % XA_HEADS) * hd:(k % XA_HEADS + 1) * hd] = stage[k % 2].astype(BF16)

    h = h_ref[...]
    q_all = _dot(_rms(h, g_ref[...]).astype(BF16), wqb[...])
    for j in range(XA_HEADS):
        hs = slice(j * hd, (j + 1) * hd)
        q = _rms(q_all[:, hs], qg_ref[...]).astype(BF16)
        sc = _dot_nt(q, kv_ref[:, hs]) * (hd ** -0.5)
        p = jnp.exp(sc - jnp.max(sc, axis=-1, keepdims=True))
        p = p * (1.0 / jnp.sum(p, axis=-1, keepdims=True))
        o_all[:, hs] = _dot(p.astype(BF16), kv_ref[:, d + j * hd:d + (j + 1) * hd]).astype(BF16)
    h_out = h + _dot(o_all[...], wob[...])
    o_ref[...] = h_out
    _route_rows(h_out, rg_ref, rw_ref, rb_ref, routed_ref, cnt_ref, carry)


def _xattn_route(h, g, wq, kv, q_g, wo, route_g, route_w, route_b, tm):
    s, d = h.shape
    m = kv.shape[0]
    hd = d // XA_HEADS
    return pl.pallas_call(
        _xattn_kernel,
        grid=(s // tm,),
        in_specs=[pl.BlockSpec((tm, d), lambda i: (i, 0)),
                  pl.BlockSpec((1, d), lambda i: (0, 0)),
                  pl.BlockSpec(memory_space=pl.ANY),
                  pl.BlockSpec((m, 2 * d), lambda i: (0, 0)),
                  pl.BlockSpec((1, hd), lambda i: (0, 0)),
                  pl.BlockSpec(memory_space=pl.ANY),
                  pl.BlockSpec((1, d), lambda i: (0, 0)),
                  pl.BlockSpec((2, d, LANES), lambda i: (0, 0, 0)),
                  pl.BlockSpec((1, LANES), lambda i: (0, 0))],
        out_specs=[pl.BlockSpec((tm, d), lambda i: (i, 0)),
                   pl.BlockSpec((tm, LANES), lambda i: (i, 0)),
                   pl.BlockSpec((SUBLANES, LANES), lambda i: (0, 0))],
        out_shape=[jax.ShapeDtypeStruct((s, d), F32),
                   jax.ShapeDtypeStruct((s, LANES), F32),
                   jax.ShapeDtypeStruct((SUBLANES, LANES), F32)],
        scratch_shapes=[pltpu.VMEM((d, d), BF16), pltpu.VMEM((d, d), BF16),
                        pltpu.VMEM((2, d, hd), F32), pltpu.VMEM((tm, d), BF16),
                        pltpu.VMEM((SUBLANES, LANES), F32),
                        pltpu.SemaphoreType.DMA((2,))],
        compiler_params=_params("arbitrary"),
        name="xattn_route",
    )(h, g, wq, kv, q_g, wo, route_g, route_w, route_b)


ROUTE_EID = 0
ROUTE_RANK = 2
ROUTE_WT = 4


def _route_rows(h, g_ref, w_ref, b_ref, out_ref, cnt_ref, carry):
    tm = h.shape[0]

    @pl.when(pl.program_id(0) == 0)
    def _():
        carry[...] = jnp.zeros_like(carry)

    hn = _rms(h, g_ref[...])
    hi = hn.astype(BF16)
    lo = (hn - hi.astype(F32)).astype(BF16)
    logits = (_dot(hi, w_ref[0]) + _dot(hi, w_ref[1]) + _dot(lo, w_ref[0])
              + b_ref[...])
    lane = lax.broadcasted_iota(I32, (tm, LANES), 1)

    def first_lane(mask):
        return jnp.min(jnp.where(mask, lane.astype(F32), float(LANES)), axis=-1,
                       keepdims=True).astype(I32)

    is_g = lane < MOE_GROUPS
    gmax = jnp.max(jnp.where(is_g, logits, -jnp.inf), axis=-1, keepdims=True)
    gsum = jnp.sum(jnp.where(is_g, jnp.exp(logits - gmax), 0.0), axis=-1, keepdims=True)
    gsel = first_lane(is_g & (logits == gmax))
    pg = 1.0 / gsum

    grp_lo = MOE_GROUPS + MOE_PER_GROUP * gsel
    in_grp = (lane >= grp_lo) & (lane < grp_lo + MOE_PER_GROUP)
    emax = jnp.max(jnp.where(in_grp, logits, -jnp.inf), axis=-1, keepdims=True)
    eexp = jnp.where(in_grp, jnp.exp(logits - emax), 0.0)
    eprob = eexp / jnp.sum(eexp, axis=-1, keepdims=True)
    p1 = jnp.max(jnp.where(in_grp, eprob, -1.0), axis=-1, keepdims=True)
    l1 = first_lane(in_grp & (eprob == p1))
    rest = in_grp & (lane != l1)
    p2 = jnp.max(jnp.where(rest, eprob, -1.0), axis=-1, keepdims=True)
    l2 = first_lane(rest & (eprob == p2))
    psum = p1 + p2
    w1 = pg * p1 / psum
    w2 = pg * p2 / psum

    oh1 = lane == l1
    oh2 = lane == l2
    onehot = jnp.where(oh1 | oh2, 1.0, 0.0)
    r = lax.broadcasted_iota(I32, (tm, tm), 0)
    c = lax.broadcasted_iota(I32, (tm, tm), 1)
    strict = jnp.where(c < r, 1.0, 0.0).astype(BF16)
    before = _dot(strict, onehot.astype(BF16)) + carry[0:1, :]
    rank1 = jnp.sum(jnp.where(oh1, before, 0.0), axis=-1, keepdims=True)
    rank2 = jnp.sum(jnp.where(oh2, before, 0.0), axis=-1, keepdims=True)
    total = carry[0:1, :] + jnp.sum(onehot, axis=0, keepdims=True)
    carry[...] = jnp.broadcast_to(total, carry.shape)
    cnt_ref[...] = jnp.broadcast_to(total, cnt_ref.shape)

    out = jnp.where(lane == ROUTE_EID, (l1 - MOE_GROUPS).astype(F32), 0.0)
    out = jnp.where(lane == ROUTE_EID + 1, (l2 - MOE_GROUPS).astype(F32), out)
    out = jnp.where(lane == ROUTE_RANK, rank1, out)
    out = jnp.where(lane == ROUTE_RANK + 1, rank2, out)
    out = jnp.where(lane == ROUTE_WT, w1, out)
    out = jnp.where(lane == ROUTE_WT + 1, w2, out)
    out_ref[...] = out


def _row_tiles(x):
    return x.astype(BF16).reshape(x.shape[0], x.shape[1] // LANES, LANES)


def _from_row_tiles(x3):
    return x3.reshape(x3.shape[0], x3.shape[1] * LANES)


def _dispatch_kernel(d0_ref, d1_ref, zb_ref, h_ref, g_ref, xg_ref, buf, zero, sem, zsem):
    tm = h_ref.shape[0]
    base = pl.program_id(0) * tm

    n_blk = xg_ref.shape[0] // MOE_ROWS

    def zero_copy(b, kind):
        rows = pl.ds(pl.multiple_of(b * MOE_ROWS, MOE_ROWS), MOE_ROWS)
        return pltpu.make_async_copy(zero, xg_ref.at[rows], zsem.at[kind - 1])

    def finish(kind):
        def body(b, carry):
            @pl.when(zb_ref[b] == kind)
            def _():
                zero_copy(0, kind).wait()
            return carry
        lax.fori_loop(0, n_blk, body, 0)

    @pl.when(pl.program_id(0) == 0)
    def _():
        zero[...] = jnp.zeros_like(zero)

        def start(b, carry):
            for kind in (1, 2):
                @pl.when(zb_ref[b] == kind)
                def _():
                    zero_copy(b, kind).start()
            return carry

        lax.fori_loop(0, n_blk, start, 0)
        finish(1)

    step = pl.program_id(0)
    slot = step % 2
    buf[slot] = _row_tiles(_rms(h_ref[...], g_ref[...]))

    def row_copy(t, d):
        return pltpu.make_async_copy(buf.at[slot, pl.ds(t, 1)], xg_ref.at[pl.ds(d, 1)], sem.at[slot])

    def issue(t, carry):
        row_copy(t, d0_ref[base + t]).start()
        row_copy(t, d1_ref[base + t]).start()
        return carry

    lax.fori_loop(0, tm, issue, 0, unroll=ROW_DMA_UNROLL)

    def drain(s):
        whole = pltpu.make_async_copy(buf.at[s], xg_ref.at[pl.ds(0, tm)], sem.at[s])
        whole.wait()
        whole.wait()

    @pl.when(step > 0)
    def _():
        drain(1 - slot)

    @pl.when(step == pl.num_programs(0) - 1)
    def _():
        drain(slot)
        finish(2)


def _dispatch(dest, zero_blk, h, g, tm):
    s, d = h.shape
    n_rows = zero_blk.shape[0] * MOE_ROWS
    grid_spec = pltpu.PrefetchScalarGridSpec(
        num_scalar_prefetch=3,
        grid=(s // tm,),
        in_specs=[pl.BlockSpec((tm, d), lambda i, d0, d1, zb: (i, 0)),
                  pl.BlockSpec((1, d), lambda i, d0, d1, zb: (0, 0))],
        out_specs=pl.BlockSpec(memory_space=pl.ANY),
        scratch_shapes=[pltpu.VMEM((2, tm, d // LANES, LANES), BF16),
                        pltpu.VMEM((MOE_ROWS, d // LANES, LANES), BF16),
                        pltpu.SemaphoreType.DMA((2,)), pltpu.SemaphoreType.DMA((2,))],
    )
    return pl.pallas_call(
        _dispatch_kernel,
        grid_spec=grid_spec,
        out_shape=jax.ShapeDtypeStruct((n_rows, d // LANES, LANES), BF16),
        compiler_params=_params("arbitrary"),
        name="moe_dispatch",
    )(dest[0], dest[1], zero_blk, h, g)


def _moe_ffn_kernel(be_ref, nu_ref, nx_ref, x_ref, wg_hbm, wu_hbm, wd_hbm, y_ref,
                    wgf, wuf, wdf, wgb, wub, wdb, slot_ref, sem):
    b = pl.program_id(0)
    e = be_ref[b]
    used = b < nu_ref[0]

    def weight_copies(expert, slot):
        return (pltpu.make_async_copy(wg_hbm.at[expert], wgf.at[slot], sem.at[slot, 0]),
                pltpu.make_async_copy(wu_hbm.at[expert], wuf.at[slot], sem.at[slot, 1]),
                pltpu.make_async_copy(wd_hbm.at[expert], wdf.at[slot], sem.at[slot, 2]))

    @pl.when(b == 0)
    def _():
        slot_ref[0] = 0
        for c in weight_copies(e, 0):
            c.start()

    @pl.when(used & ((b == 0) | (e != be_ref[jnp.maximum(b - 1, 0)])))
    def _():
        slot = slot_ref[0]
        for c in weight_copies(e, slot):
            c.wait()
        nxt = nx_ref[e]

        @pl.when(nxt >= 0)
        def _():
            for c in weight_copies(nxt, 1 - slot):
                c.start()

        wgb[...] = wgf[slot].astype(BF16)
        wub[...] = wuf[slot].astype(BF16)
        wdb[...] = wdf[slot].astype(BF16)
        slot_ref[0] = 1 - slot

    @pl.when(used)
    def _():
        xb = _from_row_tiles(x_ref[...])
        gate = _dot(xb, wgb[...])
        up = _dot(xb, wub[...])
        act = (gate * jax.nn.sigmoid(gate) * up).astype(BF16)
        y_ref[...] = _row_tiles(_dot(act, wdb[...]))

    @pl.when(jnp.logical_not(used))
    def _():
        y_ref[...] = jnp.zeros_like(y_ref)


def _moe_ffn(blk_e, n_used, next_e, xg, w_gate, w_up, w_down):
    n_rows, nt, _ = xg.shape
    d = nt * LANES
    ff = w_gate.shape[2]
    n_blk = n_rows // MOE_ROWS
    def row_blk(b, be, nu, nx):
        return jnp.minimum(b, nu[0] - 1), 0, 0

    grid_spec = pltpu.PrefetchScalarGridSpec(
        num_scalar_prefetch=3,
        grid=(n_blk,),
        in_specs=[pl.BlockSpec((MOE_ROWS, nt, LANES), row_blk),
                  pl.BlockSpec(memory_space=pl.ANY),
                  pl.BlockSpec(memory_space=pl.ANY),
                  pl.BlockSpec(memory_space=pl.ANY)],
        out_specs=pl.BlockSpec((MOE_ROWS, nt, LANES), lambda b, be, nu, nx: (b, 0, 0)),
        scratch_shapes=[pltpu.VMEM((2, d, ff), F32), pltpu.VMEM((2, d, ff), F32),
                        pltpu.VMEM((2, ff, d), F32),
                        pltpu.VMEM((d, ff), BF16), pltpu.VMEM((d, ff), BF16),
                        pltpu.VMEM((ff, d), BF16),
                        pltpu.SMEM((1,), I32), pltpu.SemaphoreType.DMA((2, 3))],
    )
    return pl.pallas_call(
        _moe_ffn_kernel,
        grid_spec=grid_spec,
        out_shape=jax.ShapeDtypeStruct((n_rows, nt, LANES), BF16),
        compiler_params=_params("arbitrary"),
        name="moe_ffn",
    )(blk_e, n_used, next_e, xg, w_gate, w_up, w_down)


def _combine_kernel(d0_ref, d1_ref, h_ref, r_ref, y_ref, o_ref, buf, sem):
    tm = h_ref.shape[0]
    step = pl.program_id(0)
    slot = step % 2

    def gather(tile, s):
        base = tile * tm

        def issue(t, carry):
            for k, dk in enumerate((d0_ref, d1_ref)):
                pltpu.make_async_copy(y_ref.at[pl.ds(dk[base + t], 1)],
                                      buf.at[s, k, pl.ds(t, 1)], sem.at[s]).start()
            return carry

        lax.fori_loop(0, tm, issue, 0, unroll=ROW_DMA_UNROLL)

    @pl.when(step == 0)
    def _():
        gather(step, slot)

    @pl.when(step + 1 < pl.num_programs(0))
    def _():
        gather(step + 1, 1 - slot)

    for k in range(2):
        pltpu.make_async_copy(y_ref.at[pl.ds(0, tm)], buf.at[slot, k], sem.at[slot]).wait()
    r = r_ref[...]
    o_ref[...] = (h_ref[...]
                  + r[:, ROUTE_WT:ROUTE_WT + 1] * _from_row_tiles(buf[slot, 0]).astype(F32)
                  + r[:, ROUTE_WT + 1:ROUTE_WT + 2] * _from_row_tiles(buf[slot, 1]).astype(F32))


def _combine(dest, h, routed, y, tm):
    s, d = h.shape
    grid_spec = pltpu.PrefetchScalarGridSpec(
        num_scalar_prefetch=2,
        grid=(s // tm,),
        in_specs=[pl.BlockSpec((tm, d), lambda i, d0, d1: (i, 0)),
                  pl.BlockSpec((tm, LANES), lambda i, d0, d1: (i, 0)),
                  pl.BlockSpec(memory_space=pl.ANY)],
        out_specs=pl.BlockSpec((tm, d), lambda i, d0, d1: (i, 0)),
        scratch_shapes=[pltpu.VMEM((2, 2, tm, d // LANES, LANES), BF16),
                        pltpu.SemaphoreType.DMA((2,))],
    )
    return pl.pallas_call(
        _combine_kernel,
        grid_spec=grid_spec,
        out_shape=jax.ShapeDtypeStruct((s, d), F32),
        compiler_params=_params("arbitrary"),
        name="moe_combine",
    )(dest[0], dest[1], h, routed, y)


def _layer(h, mem, pos, norm_mix_g, w_in, gate_b, conv_w, conv_b, out_g, moba_q_g, moba_k_g, w_out,
           norm_cross_g, norm_mem_g, xa_wq, xa_wkv, xa_q_g, xa_k_g, xa_wo, norm_ffn_g,
           router_group_w, router_group_b, router_expert_w, router_expert_b,
           exp_w_gate, exp_w_up, exp_w_down):
    s, d = h.shape
    row = lambda v: v.reshape(1, -1)

    n_gate = 2 * MLSTM_HEADS
    gate_lo = 2 * MLSTM_HEADS * MLSTM_QK + 2 * MLSTM_HEADS * MLSTM_V
    w_t = w_in.T
    w_a = w_t[:gate_lo].astype(BF16)
    w_b = w_t[gate_lo + n_gate:].astype(BF16)
    w_g = jnp.pad(w_t[gate_lo:gate_lo + n_gate], ((0, PROJ_COL_TILE - n_gate), (0, 0))).astype(BF16)
    proj = _proj_in(h, row(norm_mix_g), w_a, w_b, w_g, tm=min(s, PROJ_ROW_TILE))

    gate_b_row = jnp.pad(gate_b, (0, LANES - n_gate)).reshape(1, LANES)
    hm = _mlstm(proj, w_a.shape[0] + w_b.shape[0], gate_b_row, conv_w, row(conv_b), row(out_g))
    qn, kn, vb, kmean = _moba_prep(proj, gate_lo, pos, row(moba_q_g), row(moba_k_g))
    ha = _moba_attn(qn, kn, vb, kmean)
    h = _mix_out(hm, ha, w_out, h, tm=MIX_ROW_TILE)

    kv = _mem_kv(mem, row(norm_mem_g), xa_wkv, row(xa_k_g))

    w_route = jnp.concatenate(
        [router_group_w, router_expert_w,
         jnp.zeros((d, LANES - MOE_GROUPS - MOE_EXPERTS), router_group_w.dtype)], axis=1)
    w_route_hi = w_route.astype(BF16)
    w_route = jnp.stack([w_route_hi, (w_route - w_route_hi.astype(F32)).astype(BF16)])
    b_route = jnp.pad(jnp.concatenate([router_group_b, router_expert_b]),
                      (0, LANES - MOE_GROUPS - MOE_EXPERTS)).reshape(1, LANES)
    h, routed, counts = _xattn_route(h, row(norm_cross_g), xa_wq, kv, row(xa_q_g), xa_wo,
                                     row(norm_ffn_g), w_route, b_route, tm=XATTN_ROW_TILE)

    cnt = counts[0, MOE_GROUPS:MOE_GROUPS + MOE_EXPERTS].astype(I32)
    padded = (cnt + MOE_ROWS - 1) // MOE_ROWS * MOE_ROWS
    pend = jnp.cumsum(padded)
    seg_start = (pend - padded).astype(F32)[None, :]
    expert_ids = jnp.arange(MOE_EXPERTS, dtype=F32)[None, :]

    def dest_of(k):
        mine = routed[:, ROUTE_EID + k:ROUTE_EID + k + 1] == expert_ids
        return (jnp.sum(jnp.where(mine, seg_start, 0.0), axis=1) + routed[:, ROUTE_RANK + k]).astype(I32)

    dest = (dest_of(0), dest_of(1))
    n_blk = -(-2 * s // MOE_ROWS) + MOE_EXPERTS
    blk_start = jnp.arange(n_blk, dtype=I32) * MOE_ROWS
    blk_e = jnp.minimum(jnp.sum((pend[None, :] <= blk_start[:, None]).astype(I32), axis=1),
                        MOE_EXPERTS - 1)
    n_used = (pend[-1:] // MOE_ROWS).astype(I32)
    ids = jnp.arange(MOE_EXPERTS, dtype=I32)
    later_used = (ids[None, :] > ids[:, None]) & (cnt[None, :] > 0)
    next_e = jnp.min(jnp.where(later_used, ids[None, :], MOE_EXPERTS), axis=1)
    next_e = jnp.where(next_e < MOE_EXPERTS, next_e, -1).astype(I32)

    is_blk_e = blk_e[:, None] == ids[None, :]
    seg_end = jnp.sum(jnp.where(is_blk_e, (pend - padded + cnt)[None, :], 0), axis=1)
    zero_blk = jnp.where(blk_start >= pend[-1], 2, (blk_start + MOE_ROWS > seg_end).astype(I32))
    xg = _dispatch(dest, zero_blk, h, row(norm_ffn_g), tm=MOE_TOKEN_TILE)
    y = _moe_ffn(blk_e, n_used, next_e, xg, exp_w_gate, exp_w_up, exp_w_down)
    return _combine(dest, h, routed, y, tm=MOE_TOKEN_TILE)


def kernel(x, mem, positions, norm_mix_g, w_in, mlstm_gate_b, mlstm_conv_w, mlstm_conv_b, mlstm_out_g, moba_q_g, moba_k_g, w_out, norm_cross_g, norm_mem_g, xa_wq, xa_wkv, xa_q_g, xa_k_g, xa_wo, norm_ffn_g, router_group_w, router_group_b, router_expert_w, router_expert_b, exp_w_gate, exp_w_up, exp_w_down):
    bsz, s, _ = x.shape
    assert bsz == 1, "single-sequence prefill only"
    per_layer = (norm_mix_g, w_in, mlstm_gate_b, mlstm_conv_w, mlstm_conv_b, mlstm_out_g, moba_q_g,
                 moba_k_g, w_out, norm_cross_g, norm_mem_g, xa_wq, xa_wkv, xa_q_g, xa_k_g, xa_wo,
                 norm_ffn_g, router_group_w, router_group_b, router_expert_w, router_expert_b,
                 exp_w_gate, exp_w_up, exp_w_down)
    h = x[0]
    pos = positions.reshape(s, 1)
    for l in range(norm_mix_g.shape[0]):
        h = _layer(h, mem[0], pos, *(p[l] for p in per_layer))
    return h[None]
```
